```python
import math
import jax, jax.numpy as jnp
from jax import lax
import numpy as np

D_MODEL = 1024
BATCH = 2
SEQ = 8192
DEPTH = 2

D_MIX = D_MODEL
A_WIDTH = D_MIX // 2
A_HEADS = 8
A_HEAD_DIM = A_WIDTH // A_HEADS
CHUNK = 128
B_HEADS = 8
B_HEAD_DIM = (D_MIX - A_WIDTH) // B_HEADS
B_KV_GROUPS = 2
B_HPG = B_HEADS // B_KV_GROUPS
CMP_BLOCK = 32
CMP_STRIDE = 16
SEL_BLOCK = 64
SEL_TOP = 16
WINDOW = 512
Q_BLOCK = 128
ROPE_THETA = 10000.0
KV_W = B_KV_GROUPS * B_HEAD_DIM
P_IN = 2 * A_WIDTH + B_HEADS * B_HEAD_DIM + 6 * KV_W + 3 * B_HEADS
N_EXPERTS = 32
TOP_K = 4
D_EXPERT = D_MODEL
SWIGLU_LIMIT = 7.0
SWIGLU_ALPHA = 1.702
EXPERT_ROWS = 256
NORM_EPS = 1e-6
ADA_SCALE = 0.5

kernel_name = "hybrid_sgu_nsa_moe_adaln"


def rms_norm(x, g):
    xf = x.astype(jnp.float32)
    y = xf * lax.rsqrt(jnp.mean(xf * xf, axis=-1, keepdims=True) + NORM_EPS)
    return (y * g.astype(jnp.float32)).astype(x.dtype)


def masked_softmax(s, mask):
    s = jnp.where(mask, s.astype(jnp.float32), -jnp.inf)
    m = jnp.max(s, axis=-1, keepdims=True)
    m = jnp.where(jnp.isfinite(m), m, 0.0)
    e = jnp.exp(s - m)
    d = jnp.sum(e, axis=-1, keepdims=True)
    return e / jnp.where(d > 0, d, 1.0)


def rope(x, pos):
    half = B_HEAD_DIM // 2
    inv = ROPE_THETA ** (-jnp.arange(half, dtype=jnp.float32) / half)
    ang = pos.astype(jnp.float32)[..., None] * inv
    cos = jnp.cos(ang)[:, :, None, :]
    sin = jnp.sin(ang)[:, :, None, :]
    xf = x.astype(jnp.float32)
    x1, x2 = xf[..., :half], xf[..., half:]
    return jnp.concatenate([x1 * cos - x2 * sin, x2 * cos + x1 * sin], axis=-1).astype(x.dtype)


def spatial_gating(u, v, ln_g, ln_b, w_s, b_s):
    b, s, _ = u.shape
    u = jax.nn.gelu(u)
    vf = jax.nn.gelu(v).astype(jnp.float32)
    mu = jnp.mean(vf, axis=-1, keepdims=True)
    var = jnp.mean(jnp.square(vf - mu), axis=-1, keepdims=True)
    v = ((vf - mu) * lax.rsqrt(var + NORM_EPS) * ln_g + ln_b).astype(u.dtype)
    v = v.reshape(b, s // CHUNK, CHUNK, A_HEADS, A_HEAD_DIM)
    causal = jnp.tril(jnp.ones((CHUNK, CHUNK), dtype=bool))
    w = jnp.where(causal[None], w_s, 0.0)
    mixed = jnp.einsum('hts,bcshd->bcthd', w, v) + b_s.T[:, :, None]
    return u * mixed.reshape(b, s, A_WIDTH)


def compress_tokens(tok, pe, w1, w2):
    b, s = tok.shape[:2]
    n_cmp = (s - CMP_BLOCK) // CMP_STRIDE + 1
    idx = CMP_STRIDE * jnp.arange(n_cmp)[:, None] + jnp.arange(CMP_BLOCK)[None, :]
    blk = tok[:, idx] + pe[:, None, :]
    blk = blk.transpose(0, 1, 3, 2, 4).reshape(b, n_cmp, B_KV_GROUPS, CMP_BLOCK * B_HEAD_DIM)
    return jax.nn.gelu(blk @ w1) @ w2


def cmp_to_sel_matrix(n_cmp, n_sel):
    cs = CMP_STRIDE * np.arange(n_cmp)[:, None]
    ce = cs + CMP_BLOCK
    ss = SEL_BLOCK * np.arange(n_sel)[None, :]
    se = ss + SEL_BLOCK
    ov = np.clip(np.minimum(ce, se) - np.maximum(cs, ss), 0, None) / CMP_STRIDE
    return jnp.asarray(ov, dtype=jnp.float32)


def native_sparse_attention(q, kc, vc, ks, vs, kw, vw, gate_logits, positions,
                            pe_k, pe_v, w1k, w2k, w1v, w2v):
    b, s = q.shape[:2]
    scale = B_HEAD_DIM ** -0.5
    n_cmp = (s - CMP_BLOCK) // CMP_STRIDE + 1
    n_sel = s // SEL_BLOCK
    n_top = min(SEL_TOP, n_sel)
    cmp_end = CMP_STRIDE * jnp.arange(n_cmp) + CMP_BLOCK - 1

    q = rope(q, positions)
    q5 = q.reshape(b, s, B_KV_GROUPS, B_HPG, B_HEAD_DIM)
    k_cmp = rope(compress_tokens(kc, pe_k, w1k, w2k), positions[:, cmp_end])
    v_cmp = compress_tokens(vc, pe_v, w1v, w2v)
    ks = rope(ks, positions)
    kw = rope(kw, positions)
    ks_blk = ks.reshape(b, n_sel, SEL_BLOCK, B_KV_GROUPS, B_HEAD_DIM).transpose(0, 3, 1, 2, 4)
    vs_blk = vs.reshape(b, n_sel, SEL_BLOCK, B_KV_GROUPS, B_HEAD_DIM).transpose(0, 3, 1, 2, 4)
    pad = ((0, 0), (WINDOW, 0), (0, 0), (0, 0))
    kw_pad = jnp.pad(kw, pad)
    vw_pad = jnp.pad(vw, pad)
    sel_map = cmp_to_sel_matrix(n_cmp, n_sel)
    gates = jax.nn.sigmoid(gate_logits.astype(jnp.float32))
    bi = jnp.arange(b)[:, None, None, None]
    gi = jnp.arange(B_KV_GROUPS)[None, :, None, None]

    def query_block(qb):
        q0 = qb * Q_BLOCK
        t = q0 + jnp.arange(Q_BLOCK)
        qblk = lax.dynamic_slice_in_dim(q5, q0, Q_BLOCK, axis=1)
        s1 = jnp.einsum('bqghd,bngd->bghqn', qblk, k_cmp) * scale
        m1 = (cmp_end[None, :] <= t[:, None])[None, None, None]
        p1 = masked_softmax(s1, m1)
        o_c = jnp.einsum('bghqn,bngd->bqghd', p1, v_cmp.astype(jnp.float32))
        imp = jnp.einsum('bghqn,nj->bgqj', p1, sel_map)
        cur = t // SEL_BLOCK
        j = jnp.arange(n_sel)
        valid = j[None, :] <= cur[:, None]
        forced = (j[None, :] == 0) | (j[None, :] == cur[:, None]) | (j[None, :] == cur[:, None] - 1)
        imp = jnp.where(valid, jnp.where(forced, jnp.inf, imp), -jnp.inf)
        _, sel = lax.top_k(imp, n_top)
        kg = ks_blk[bi, gi, sel]
        vg = vs_blk[bi, gi, sel]
        s2 = jnp.einsum('bqghd,bgqnkd->bghqnk', qblk, kg) * scale
        kpos = sel[..., None] * SEL_BLOCK + jnp.arange(SEL_BLOCK)
        m2 = (kpos <= t[None, None, :, None, None]).reshape(b, B_KV_GROUPS, 1, Q_BLOCK, n_top * SEL_BLOCK)
        p2 = masked_softmax(s2.reshape(b, B_KV_GROUPS, B_HPG, Q_BLOCK, n_top * SEL_BLOCK), m2)
        p2 = p2.reshape(b, B_KV_GROUPS, B_HPG, Q_BLOCK, n_top, SEL_BLOCK)
        o_s = jnp.einsum('bghqnk,bgqnkd->bqghd', p2, vg.astype(jnp.float32))
        kwb = lax.dynamic_slice_in_dim(kw_pad, q0, Q_BLOCK + WINDOW, axis=1)
        vwb = lax.dynamic_slice_in_dim(vw_pad, q0, Q_BLOCK + WINDOW, axis=1)
        spos = q0 - WINDOW + jnp.arange(Q_BLOCK + WINDOW)
        dlt = t[:, None] - spos[None, :]
        m3 = ((dlt >= 0) & (dlt < WINDOW) & (spos[None, :] >= 0))[None, None, None]
        s3 = jnp.einsum('bqghd,bkgd->bghqk', qblk, kwb) * scale
        p3 = masked_softmax(s3, m3)
        o_w = jnp.einsum('bghqk,bkgd->bqghd', p3, vwb.astype(jnp.float32))
        g = lax.dynamic_slice_in_dim(gates, q0, Q_BLOCK, axis=1).reshape(b, Q_BLOCK, B_KV_GROUPS, B_HPG, 3)
        o = g[..., 0:1] * o_c + g[..., 1:2] * o_s + g[..., 2:3] * o_w
        return o.reshape(b, Q_BLOCK, B_HEADS * B_HEAD_DIM).astype(q.dtype)

    out = lax.map(query_block, jnp.arange(s // Q_BLOCK))
    return jnp.moveaxis(out, 0, 1).reshape(b, s, B_HEADS * B_HEAD_DIM)


def clamped_swiglu(hdn):
    glu, lin = jnp.split(hdn, 2, axis=-1)
    glu = jnp.minimum(glu, SWIGLU_LIMIT)
    lin = jnp.clip(lin, -SWIGLU_LIMIT, SWIGLU_LIMIT)
    return glu * jax.nn.sigmoid(SWIGLU_ALPHA * glu) * (lin + 1.0)


def moe_ffn(h, router_w, router_b, w1, b1, w2, b2):
    b, s, d = h.shape
    n_tok = b * s
    xf = h.reshape(n_tok, d)
    logits = (xf @ router_w + router_b).astype(jnp.float32)
    top_val, top_idx = lax.top_k(logits, TOP_K)
    gates = jax.nn.softmax(top_val, axis=-1).astype(h.dtype)
    e_flat = top_idx.reshape(-1)
    tok_flat = jnp.repeat(jnp.arange(n_tok, dtype=jnp.int32), TOP_K)
    g_flat = gates.reshape(-1)
    n_assign = n_tok * TOP_K
    order = jnp.argsort(e_flat, stable=True)
    e_sorted = e_flat[order]
    counts = jnp.bincount(e_flat, length=N_EXPERTS)
    padded = ((counts + EXPERT_ROWS - 1) // EXPERT_ROWS) * EXPERT_ROWS
    pad_end = jnp.cumsum(padded)
    pad_start = pad_end - padded
    start = jnp.cumsum(counts) - counts
    dest = pad_start[e_sorted] + (jnp.arange(n_assign) - start[e_sorted])
    n_blocks = -(-n_assign // EXPERT_ROWS) + N_EXPERTS
    n_rows = n_blocks * EXPERT_ROWS
    row_tok = jnp.full((n_rows,), n_tok, dtype=jnp.int32).at[dest].set(tok_flat[order])
    row_gate = jnp.zeros((n_rows,), dtype=h.dtype).at[dest].set(g_flat[order])
    block_exp = jnp.minimum(jnp.searchsorted(pad_end, jnp.arange(n_blocks) * EXPERT_ROWS, side='right'),
                            N_EXPERTS - 1)
    x_pad = jnp.concatenate([xf, jnp.zeros((1, d), dtype=xf.dtype)], axis=0)
    rows = x_pad[row_tok].reshape(n_blocks, EXPERT_ROWS, d)

    def expert_block(args):
        xb, e = args
        return clamped_swiglu(xb @ w1[e] + b1[e]) @ w2[e] + b2[e]

    out = lax.map(expert_block, (rows, block_exp)).reshape(n_rows, d)
    y = jax.ops.segment_sum(out * row_gate[:, None], row_tok, num_segments=n_tok + 1)[:n_tok]
    return y.reshape(b, s, d)


def setup_inputs(seed: int = 0) -> dict:
    key = jax.random.key(seed)
    ks = jax.random.split(key, 32)
    nrm = jax.random.normal
    f32 = jnp.float32
    L, D, E, F, DK = DEPTH, D_MODEL, N_EXPERTS, D_EXPERT, B_HEAD_DIM
    offset = jax.random.randint(ks[2], (BATCH, 1), 0, 1024, dtype=jnp.int32)
    return {
        "x": nrm(ks[0], (BATCH, SEQ, D), f32),
        "c": nrm(ks[1], (BATCH, D), f32),
        "positions": offset + jnp.arange(SEQ, dtype=jnp.int32)[None, :],
        "ada_w": nrm(ks[3], (L, D, 6 * D), f32) * (ADA_SCALE * D ** -0.5),
        "ada_b": nrm(ks[4], (L, 6 * D), f32) * 0.02,
        "norm1_g": 1.0 + 0.02 * nrm(ks[5], (L, D), f32),
        "norm2_g": 1.0 + 0.02 * nrm(ks[6], (L, D), f32),
        "w_in": nrm(ks[7], (L, D, P_IN), f32) * D ** -0.5,
        "w_out": nrm(ks[8], (L, D_MIX, D), f32) * D_MIX ** -0.5,
        "sg_ln_g": 1.0 + 0.02 * nrm(ks[9], (L, A_WIDTH), f32),
        "sg_ln_b": 0.02 * nrm(ks[10], (L, A_WIDTH), f32),
        "sg_w": nrm(ks[11], (L, A_HEADS, CHUNK, CHUNK), f32) * CHUNK ** -0.5,
        "sg_b": 1.0 + 0.02 * nrm(ks[12], (L, A_HEADS, CHUNK), f32),
        "cmp_pe_k": 0.02 * nrm(ks[13], (L, CMP_BLOCK, DK), f32),
        "cmp_pe_v": 0.02 * nrm(ks[14], (L, CMP_BLOCK, DK), f32),
        "cmp_w1_k": nrm(ks[15], (L, CMP_BLOCK * DK, DK), f32) * (CMP_BLOCK * DK) ** -0.5,
        "cmp_w2_k": nrm(ks[16], (L, DK, DK), f32) * DK ** -0.5,
        "cmp_w1_v": nrm(ks[17], (L, CMP_BLOCK * DK, DK), f32) * (CMP_BLOCK * DK) ** -0.5,
        "cmp_w2_v": nrm(ks[18], (L, DK, DK), f32) * DK ** -0.5,
        "router_w": nrm(ks[19], (L, D, E), f32) * D ** -0.5,
        "router_b": 0.01 * nrm(ks[20], (L, E), f32),
        "exp_w1": nrm(ks[21], (L, E, D, 2 * F), f32) * D ** -0.5,
        "exp_b1": 0.02 * nrm(ks[22], (L, E, 2 * F), f32),
        "exp_w2": nrm(ks[23], (L, E, F, D), f32) * F ** -0.5,
        "exp_b2": 0.02 * nrm(ks[24], (L, E, D), f32),
        "final_g": 1.0 + 0.02 * nrm(ks[25], (D,), f32),
    }


def reference(x, c, positions, ada_w, ada_b, norm1_g, norm2_g, w_in, w_out,
              sg_ln_g, sg_ln_b, sg_w, sg_b, cmp_pe_k, cmp_pe_v, cmp_w1_k, cmp_w2_k,
              cmp_w1_v, cmp_w2_v, router_w, router_b, exp_w1, exp_b1, exp_w2, exp_b2, final_g):
    b, s, _ = x.shape
    offs = np.cumsum([A_WIDTH, A_WIDTH, B_HEADS * B_HEAD_DIM] + [KV_W] * 6).tolist()
    cond = jax.nn.silu(c)
    for l in range(DEPTH):
        mod = cond @ ada_w[l] + ada_b[l]
        sh1, sc1, g1, sh2, sc2, g2 = jnp.split(mod, 6, axis=-1)
        h = rms_norm(x, norm1_g[l]) * (1.0 + sc1[:, None, :]) + sh1[:, None, :]
        proj = h @ w_in[l]
        u, v, q, kc, vc, ksl, vsl, kwn, vwn, gl = jnp.split(proj, offs, axis=-1)
        kv = lambda t: t.reshape(b, s, B_KV_GROUPS, B_HEAD_DIM)
        y_a = spatial_gating(u, v, sg_ln_g[l], sg_ln_b[l], sg_w[l], sg_b[l])
        y_b = native_sparse_attention(
            q.reshape(b, s, B_HEADS, B_HEAD_DIM), kv(kc), kv(vc), kv(ksl), kv(vsl), kv(kwn), kv(vwn),
            gl.reshape(b, s, B_HEADS, 3), positions,
            cmp_pe_k[l], cmp_pe_v[l], cmp_w1_k[l], cmp_w2_k[l], cmp_w1_v[l], cmp_w2_v[l])
        mixed = jnp.concatenate([y_a, y_b], axis=-1) @ w_out[l]
        x = x + g1[:, None, :] * mixed
        h2 = rms_norm(x, norm2_g[l]) * (1.0 + sc2[:, None, :]) + sh2[:, None, :]
        x = x + g2[:, None, :] * moe_ffn(h2, router_w[l], router_b[l], exp_w1[l], exp_b1[l],
                                         exp_w2[l], exp_b2[l])
    return rms_norm(x, final_g)
```

```python
import functools

import numpy as np
import jax
import jax.numpy as jnp
from jax import lax
from jax.experimental import pallas as pl
from jax.experimental.pallas import tpu as pltpu

F32 = jnp.float32
BF16 = jnp.bfloat16
HIGHEST = lax.Precision.HIGHEST

D_MODEL = 1024
A_WIDTH = 512
A_HEADS = 8
CHUNK = 128
B_HEADS = 8
DK = 64
KV_GROUPS = 2
HPG = B_HEADS // KV_GROUPS
CMP_BLOCK = 32
CMP_STRIDE = 16
SEL_BLOCK = 64
SEL_TOP = 16
WINDOW = 512
ROPE_THETA = 10000.0
N_EXPERTS = 32
TOP_K = 4
D_EXPERT = 1024
SWIGLU_LIMIT = 7.0
SWIGLU_ALPHA = 1.702
NORM_EPS = 1e-6

LANES = 128
GW = HPG * DK
N_FORCED = 3
MASK_BIAS = -32768.0
VMEM_LIMIT = 56 * 1024 * 1024

C_U, C_V, C_Q = 0, 512, 1024
C_KC, C_VC = 1536, 1664
C_KS, C_VS, C_KW, C_VW = 1792, 2048, 2304, 2560
C_GL = 2816
P_WIDE = 3072


def _params(sem):
    return pltpu.CompilerParams(dimension_semantics=sem, vmem_limit_bytes=VMEM_LIMIT)


def _ada_kernel(c_ref, w_ref, b_ref, o_ref):
    c = c_ref[...]
    cond = c * jax.nn.sigmoid(c)
    o_ref[0] = jnp.dot(cond, w_ref[0], precision=HIGHEST, preferred_element_type=F32) + b_ref[0]


def _ada_mod(c, ada_w, ada_b):
    n_layers, d, d6 = ada_w.shape
    b = c.shape[0]
    rows = 8
    c_pad = jnp.zeros((rows, d), F32).at[:b].set(c)
    out = pl.pallas_call(
        _ada_kernel,
        grid=(n_layers, d6 // d),
        in_specs=[
            pl.BlockSpec((rows, d), lambda l, j: (0, 0)),
            pl.BlockSpec((1, d, d), lambda l, j: (l, 0, j)),
            pl.BlockSpec((1, 1, d), lambda l, j: (l, 0, j)),
        ],
        out_specs=pl.BlockSpec((1, rows, d), lambda l, j: (l, 0, j)),
        out_shape=jax.ShapeDtypeStruct((n_layers, rows, d6), F32),
        compiler_params=_params(("arbitrary", "arbitrary")),
        name="ada",
    )(c_pad, ada_w, ada_b.reshape(n_layers, 1, d6))
    return out[:, :b]


def _rope_slab(t, cos, sin_signed, lo):
    partner = jnp.where(lo, pltpu.roll(t, LANES - DK // 2, 1), pltpu.roll(t, DK // 2, 1))
    return t * cos + partner * sin_signed


def _rope(t, cos, sin_signed):
    lane = lax.broadcasted_iota(jnp.int32, (1, LANES), 1)
    lo = (lane % DK) < (DK // 2)
    slabs = [_rope_slab(t[:, s * LANES:(s + 1) * LANES], cos, sin_signed, lo)
             for s in range(t.shape[1] // LANES)]
    return slabs[0] if len(slabs) == 1 else jnp.concatenate(slabs, axis=1)


def _pre_kernel(has_res, *refs):
    if has_res:
        (x_ref, y_ref, g2_ref, ng_ref, sc_ref, sh_ref, w_ref, cos_ref, sin_ref,
         xo_ref, uv_ref, q_ref, kc_ref, vc_ref, ksd_ref, vsd_ref, kwd_ref, vwd_ref, gl_ref) = refs
        x = x_ref[0] + g2_ref[0] * y_ref[0]
        xo_ref[0] = x
    else:
        (x_ref, ng_ref, sc_ref, sh_ref, w_ref, cos_ref, sin_ref,
         uv_ref, q_ref, kc_ref, vc_ref, ksd_ref, vsd_ref, kwd_ref, vwd_ref, gl_ref) = refs
        x = x_ref[0]
    ms = jnp.mean(x * x, axis=-1, keepdims=True)
    h = x * lax.rsqrt(ms + NORM_EPS) * ng_ref[...]
    h = h * (1.0 + sc_ref[0]) + sh_ref[0]
    proj = jnp.dot(h.astype(BF16), w_ref[...], preferred_element_type=F32)
    cos = cos_ref[0]
    sin = sin_ref[0]
    uv_ref[0] = proj[:, C_U:C_Q]
    q_ref[0] = _rope(proj[:, C_Q:C_KC], cos, sin) * (DK ** -0.5)
    kc_ref[0] = proj[:, C_KC:C_VC]
    vc_ref[0] = proj[:, C_VC:C_KS]
    ksd_ref[0] = _rope(proj[:, C_KS:C_VS], cos, sin).astype(BF16)
    vsd_ref[0] = proj[:, C_VS:C_KW].astype(BF16)
    kwd_ref[0] = _rope(proj[:, C_KW:C_VW], cos, sin).astype(BF16)
    vwd_ref[0] = proj[:, C_VW:C_GL].astype(BF16)
    gl_ref[0] = proj[:, C_GL:P_WIDE]


def _pre(x, res, norm_g, sc, sh, w_wide, cos, sin, tm=512):
    b, s, d = x.shape
    tm = min(tm, s)
    row = lambda w: pl.BlockSpec((1, tm, w), lambda i, j: (i, j, 0))
    vec = pl.BlockSpec((1, 1, d), lambda i, j: (i, 0, 0))
    in_specs = [row(d)]
    args = [x]
    if res is not None:
        y, g2 = res
        in_specs += [row(d), vec]
        args += [y, g2]
    in_specs += [pl.BlockSpec((1, d), lambda i, j: (0, 0)), vec, vec,
                 pl.BlockSpec((d, P_WIDE), lambda i, j: (0, 0)), row(LANES), row(LANES)]
    args += [norm_g.reshape(1, d), sc, sh, w_wide, cos, sin]
    shapes = [(d, F32)] if res is not None else []
    shapes += [(2 * A_WIDTH, F32), (B_HEADS * DK, F32), (LANES, F32), (LANES, F32),
               (2 * LANES, BF16), (2 * LANES, BF16), (2 * LANES, BF16), (2 * LANES, BF16),
               (2 * LANES, F32)]
    outs = pl.pallas_call(
        functools.partial(_pre_kernel, res is not None),
        grid=(b, s // tm),
        in_specs=in_specs,
        out_specs=[row(w) for w, _ in shapes],
        out_shape=[jax.ShapeDtypeStruct((b, s, w), dt) for w, dt in shapes],
        compiler_params=_params(("parallel", "parallel")),
        name="pre",
    )(*args)
    if res is None:
        outs = [x] + list(outs)
    return outs


def _sgu_kernel(uv_ref, lng_ref, lnb_ref, w_ref, bias_ref, o_ref):
    rows = uv_ref.shape[1]
    uv = uv_ref[0]
    gu = jax.nn.gelu(uv[:, :A_WIDTH])
    gv = jax.nn.gelu(uv[:, A_WIDTH:])
    mu = jnp.mean(gv, axis=-1, keepdims=True)
    var = jnp.mean(jnp.square(gv - mu), axis=-1, keepdims=True)
    vn = ((gv - mu) * lax.rsqrt(var + NORM_EPS) * lng_ref[...] + lnb_ref[...]).astype(BF16)
    r = lax.broadcasted_iota(jnp.int32, (CHUNK, CHUNK), 0)
    c = lax.broadcasted_iota(jnp.int32, (CHUNK, CHUNK), 1)
    causal = c <= r
    lane_lo = lax.broadcasted_iota(jnp.int32, (CHUNK, LANES), 1) < DK
    for p in range(A_HEADS // 2):
        w0 = jnp.where(causal, w_ref[2 * p], 0.0).astype(BF16)
        w1 = jnp.where(causal, w_ref[2 * p + 1], 0.0).astype(BF16)
        bias = bias_ref[:, p * LANES:(p + 1) * LANES]
        for ch in range(rows // CHUNK):
            rs = slice(ch * CHUNK, (ch + 1) * CHUNK)
            cs = slice(p * LANES, (p + 1) * LANES)
            vp = vn[rs, cs]
            m0 = jnp.dot(w0, vp, preferred_element_type=F32)
            m1 = jnp.dot(w1, vp, preferred_element_type=F32)
            mixed = jnp.where(lane_lo, m0, m1) + bias
            o_ref[0, rs, cs] = (gu[rs, cs] * mixed).astype(o_ref.dtype)


def _sgu(uv, ln_g, ln_b, w_s, b_s, tm=512):
    b, s, _ = uv.shape
    tm = min(tm, s)
    bias = jnp.repeat(b_s.T, A_WIDTH // A_HEADS, axis=1)
    return pl.pallas_call(
        _sgu_kernel,
        grid=(b, s // tm),
        in_specs=[
            pl.BlockSpec((1, tm, 2 * A_WIDTH), lambda i, j: (i, j, 0)),
            pl.BlockSpec((1, A_WIDTH), lambda i, j: (0, 0)),
            pl.BlockSpec((1, A_WIDTH), lambda i, j: (0, 0)),
            pl.BlockSpec((A_HEADS, CHUNK, CHUNK), lambda i, j: (0, 0, 0)),
            pl.BlockSpec((CHUNK, A_WIDTH), lambda i, j: (0, 0)),
        ],
        out_specs=pl.BlockSpec((1, tm, A_WIDTH), lambda i, j: (i, j, 0)),
        out_shape=jax.ShapeDtypeStruct((b, s, A_WIDTH), BF16),
        compiler_params=_params(("parallel", "parallel")),
        name="sgu",
    )(uv, ln_g.reshape(1, -1), ln_b.reshape(1, -1), w_s, bias)


def _compress_kernel(kr_ref, vr_ref, pek_ref, pev_ref, w1k_ref, w1v_ref, w2k_ref, w2v_ref,
                     cos_ref, sin_ref, ko_ref, vo_ref):
    nc = kr_ref.shape[1]

    def mlp(r, pe_ref, w1_ref, w2_ref):
        top = jnp.dot(r + pe_ref[0:1], w1_ref[0], precision=HIGHEST, preferred_element_type=F32)
        bot = jnp.dot(r + pe_ref[1:2], w1_ref[1], precision=HIGHEST, preferred_element_type=F32)
        pre = top + pltpu.roll(bot, nc - 1, 0)
        return jnp.dot(jax.nn.gelu(pre), w2_ref[...], precision=HIGHEST, preferred_element_type=F32)

    kc = _rope(mlp(kr_ref[0], pek_ref, w1k_ref, w2k_ref), cos_ref[0], sin_ref[0])
    vc = mlp(vr_ref[0], pev_ref, w1v_ref, w2v_ref)
    lo = lax.broadcasted_iota(jnp.int32, (nc, LANES), 1) < DK
    for t, o_ref in ((kc, ko_ref), (vc, vo_ref)):
        rolled = pltpu.roll(t, DK, 1)
        o_ref[0, 0, 0] = jnp.where(lo, t, 0.0)
        o_ref[0, 0, 1] = jnp.where(lo, 0.0, rolled)
        o_ref[0, 1, 0] = jnp.where(lo, rolled, 0.0)
        o_ref[0, 1, 1] = jnp.where(lo, 0.0, t)


def _compress_weights(pe, w1, w2):
    half = CMP_BLOCK // 2
    eye = jnp.eye(KV_GROUPS, dtype=F32)
    w1r = w1.reshape(CMP_BLOCK, DK, DK)
    wfull = jnp.einsum('lde,gh->lgdhe', w1r, eye)
    w1s = wfull.reshape(2, half * KV_GROUPS * DK, KV_GROUPS * DK)
    pes = jnp.broadcast_to(pe.reshape(2, half, 1, DK), (2, half, KV_GROUPS, DK)).reshape(2, -1)
    w2bd = jnp.einsum('de,gh->gdhe', w2, eye).reshape(KV_GROUPS * DK, KV_GROUPS * DK)
    return pes, w1s, w2bd


def _compress(kc, vc, pe_k, pe_v, w1_k, w2_k, w1_v, w2_v, cos_c, sin_c):
    b, s, _ = kc.shape
    nc = s // CMP_STRIDE
    rw = CMP_STRIDE * LANES
    pek, w1ks, w2kb = _compress_weights(pe_k, w1_k, w2_k)
    pev, w1vs, w2vb = _compress_weights(pe_v, w1_v, w2_v)
    full = lambda shape: pl.BlockSpec(shape, lambda i: (0,) * len(shape))
    out_spec = pl.BlockSpec((1, KV_GROUPS, 2, nc, LANES), lambda i: (i, 0, 0, 0, 0))
    out_shape = jax.ShapeDtypeStruct((b, KV_GROUPS, 2, nc, LANES), F32)
    return pl.pallas_call(
        _compress_kernel,
        grid=(b,),
        in_specs=[
            pl.BlockSpec((1, nc, rw), lambda i: (i, 0, 0)),
            pl.BlockSpec((1, nc, rw), lambda i: (i, 0, 0)),
            full((2, rw)), full((2, rw)),
            full((2, rw, LANES)), full((2, rw, LANES)),
            full((LANES, LANES)), full((LANES, LANES)),
            pl.BlockSpec((1, nc, LANES), lambda i: (i, 0, 0)),
            pl.BlockSpec((1, nc, LANES), lambda i: (i, 0, 0)),
        ],
        out_specs=[out_spec, out_spec],
        out_shape=[out_shape, out_shape],
        compiler_params=_params(("parallel",)),
        name="compress",
    )(kc.reshape(b, nc, rw), vc.reshape(b, nc, rw), pek, pev, w1ks, w1vs, w2kb, w2vb, cos_c, sin_c)


def _cmp_sel_kernel(n_top, q_ref, k_ref, v_ref, map_ref, oc_ref, sel_ref):
    tq = q_ref.shape[1]
    nc = k_ref.shape[3]
    nsel = map_ref.shape[0]
    q0 = pl.program_id(2) * tq
    q2 = q_ref[0]
    n_idx = lax.broadcasted_iota(jnp.int32, (nc, tq), 0)
    t_idx = q0 + lax.broadcasted_iota(jnp.int32, (nc, tq), 1)
    visible = (CMP_STRIDE * n_idx + CMP_BLOCK - 1) <= t_idx
    nt = (((1,), (1,)), ((), ()))
    tn = (((0,), (0,)), ((), ()))
    probs = []
    for h in range(HPG):
        qp = q2[:, (h // 2) * LANES:(h // 2 + 1) * LANES]
        s = lax.dot_general(k_ref[0, 0, h % 2], qp, nt, precision=HIGHEST,
                            preferred_element_type=F32)
        s = jnp.where(visible, s, -jnp.inf)
        m = jnp.max(s, axis=0, keepdims=True)
        m = jnp.where(m == -jnp.inf, 0.0, m)
        e = jnp.exp(s - m)
        d = jnp.sum(e, axis=0, keepdims=True)
        probs.append(e / jnp.where(d > 0, d, 1.0))
    for p in range(HPG // 2):
        o = lax.dot_general(probs[2 * p].astype(BF16), v_ref[0, 0, 0].astype(BF16), tn,
                            preferred_element_type=F32)
        o = o + lax.dot_general(probs[2 * p + 1].astype(BF16), v_ref[0, 0, 1].astype(BF16), tn,
                                preferred_element_type=F32)
        oc_ref[0, :, p * LANES:(p + 1) * LANES] = o
    psum = probs[0] + probs[1] + probs[2] + probs[3]
    imp = jnp.dot(map_ref[...], psum, precision=HIGHEST, preferred_element_type=F32)
    j = lax.broadcasted_iota(jnp.int32, (nsel, tq), 0)
    cur = (q0 + lax.broadcasted_iota(jnp.int32, (nsel, tq), 1)) // SEL_BLOCK
    valid = j <= cur
    forced = (j == 0) | (j == cur) | (j == cur - 1)
    keep = forced | (valid & (cur < n_top))
    vals = jnp.where(valid & jnp.logical_not(forced), imp, -jnp.inf)
    sel = jnp.where(keep, 1.0, 0.0)
    for _ in range(n_top - N_FORCED):
        m = jnp.max(vals, axis=0, keepdims=True)
        first = jnp.min(jnp.where(vals == m, j, nsel), axis=0, keepdims=True)
        pick = (j == first) & (m > -jnp.inf)
        sel = jnp.where(pick, 1.0, sel)
        vals = jnp.where(pick, -jnp.inf, vals)
    sel_ref[0, 0] = sel.T


def _sel_map_t(s):
    nc = s // CMP_STRIDE
    n_cmp = (s - CMP_BLOCK) // CMP_STRIDE + 1
    n_sel = s // SEL_BLOCK
    cs = CMP_STRIDE * np.arange(n_cmp)[:, None]
    ce = cs + CMP_BLOCK
    ss = SEL_BLOCK * np.arange(n_sel)[None, :]
    se = ss + SEL_BLOCK
    ov = np.clip(np.minimum(ce, se) - np.maximum(cs, ss), 0, None) / CMP_STRIDE
    out = np.zeros((n_sel, nc), np.float32)
    out[:, :n_cmp] = ov.T
    return jnp.asarray(out)


def _cmp_sel(q, kcmp, vcmp, tq=128):
    b, s, _ = q.shape
    nc = s // CMP_STRIDE
    nsel = s // SEL_BLOCK
    n_top = min(SEL_TOP, nsel)
    kv_spec = pl.BlockSpec((1, 1, 2, nc, LANES), lambda i, g, j: (i, g, 0, 0, 0))
    return pl.pallas_call(
        functools.partial(_cmp_sel_kernel, n_top),
        grid=(b, KV_GROUPS, s // tq),
        in_specs=[
            pl.BlockSpec((1, tq, GW), lambda i, g, j: (i, j, g)),
            kv_spec, kv_spec,
            pl.BlockSpec((nsel, nc), lambda i, g, j: (0, 0)),
        ],
        out_specs=[
            pl.BlockSpec((1, tq, GW), lambda i, g, j: (i, j, g)),
            pl.BlockSpec((1, 1, tq, nsel), lambda i, g, j: (i, g, j, 0)),
        ],
        out_shape=[
            jax.ShapeDtypeStruct((b, s, B_HEADS * DK), F32),
            jax.ShapeDtypeStruct((b, KV_GROUPS, s, nsel), F32),
        ],
        compiler_params=_params(("parallel", "parallel", "parallel")),
        name="cmp_sel",
    )(q, kcmp, vcmp, _sel_map_t(s))


def _attn_kernel(kt, q_ref, ks_ref, vs_ref, kw_ref, vw_ref, sel_ref, oh_ref, oc_ref, gl_ref,
                 o_ref, m_ref, l_ref, acc_ref):
    tq = q_ref.shape[1]
    s_len = ks_ref.shape[1]
    q0 = pl.program_id(2) * tq
    q2 = q_ref[0]
    lane_lo = lax.broadcasted_iota(jnp.int32, (tq, LANES), 1) < DK
    parts = []
    for p in range(HPG // 2):
        qp = q2[:, p * LANES:(p + 1) * LANES]
        parts += [jnp.where(lane_lo, qp, 0.0), jnp.where(lane_lo, 0.0, qp)]
    q4 = jnp.concatenate(parts, axis=0).astype(BF16)
    bias = jnp.where(sel_ref[0, 0] > 0.5, 0.0, MASK_BIAS).astype(BF16)
    q4aug = jnp.concatenate([q4, jnp.concatenate([bias] * HPG, axis=0)], axis=1)
    rows = HPG * tq
    nt = (((1,), (1,)), ((), ()))
    t_row = q0 + lax.broadcasted_iota(jnp.int32, (rows, 1), 0) % tq

    m_ref[...] = jnp.full(m_ref.shape, -jnp.inf, F32)
    l_ref[...] = jnp.zeros(l_ref.shape, F32)
    acc_ref[...] = jnp.zeros(acc_ref.shape, F32)

    def body(c, carry):
        k0 = pl.multiple_of(c * kt, kt)
        kaug = jnp.concatenate([ks_ref[0, pl.ds(k0, kt), :], oh_ref[pl.ds(k0, kt), :]], axis=1)
        sc = lax.dot_general(q4aug, kaug, nt, preferred_element_type=F32)
        kpos = k0 + lax.broadcasted_iota(jnp.int32, (rows, kt), 1)
        sc = jnp.where(kpos <= t_row, sc, -jnp.inf)
        m_old = m_ref[...]
        m_new = jnp.maximum(m_old, jnp.max(sc, axis=-1, keepdims=True))
        alpha = jnp.exp(m_old - m_new)
        pr = jnp.exp(sc - m_new)
        l_ref[...] = alpha * l_ref[...] + jnp.sum(pr, axis=-1, keepdims=True)
        acc_ref[...] = alpha * acc_ref[...] + jnp.dot(pr.astype(BF16), vs_ref[0, pl.ds(k0, kt), :],
                                                      preferred_element_type=F32)
        m_ref[...] = m_new
        return carry

    lax.fori_loop(0, (q0 + tq + kt - 1) // kt, body, 0)
    o_sel = acc_ref[...] / l_ref[...]

    wlen = WINDOW + tq
    w0 = pl.multiple_of(jnp.maximum(q0 - WINDOW, 0), tq)
    sw = lax.dot_general(q4, kw_ref[0, pl.ds(w0, wlen), :], nt, preferred_element_type=F32)
    dlt = t_row - (w0 + lax.broadcasted_iota(jnp.int32, (rows, wlen), 1))
    sw = jnp.where((dlt >= 0) & (dlt < WINDOW), sw, -jnp.inf)
    mw = jnp.max(sw, axis=-1, keepdims=True)
    pw = jnp.exp(sw - mw)
    lw = jnp.sum(pw, axis=-1, keepdims=True)
    o_win = jnp.dot(pw.astype(BF16), vw_ref[0, pl.ds(w0, wlen), :], preferred_element_type=F32) / lw

    gates = jax.nn.sigmoid(gl_ref[0])
    oc = oc_ref[0]
    for p in range(HPG // 2):
        def pair(t4):
            return jnp.where(lane_lo, t4[(2 * p) * tq:(2 * p + 1) * tq], t4[(2 * p + 1) * tq:(2 * p + 2) * tq])

        def gate(branch):
            c0 = (2 * p) * 3 + branch
            c1 = (2 * p + 1) * 3 + branch
            return jnp.where(lane_lo, jnp.broadcast_to(gates[:, c0:c0 + 1], (tq, LANES)),
                             jnp.broadcast_to(gates[:, c1:c1 + 1], (tq, LANES)))

        out = (gate(0) * oc[:, p * LANES:(p + 1) * LANES] + gate(1) * pair(o_sel) + gate(2) * pair(o_win))
        o_ref[0, :, p * LANES:(p + 1) * LANES] = out.astype(o_ref.dtype)


def _attn(q, ksd, vsd, kwd, vwd, sel, o_cmp, gl, tq=128, kt=512):
    b, s, _ = q.shape
    nsel = s // SEL_BLOCK
    kt = min(kt, s)
    onehot = jnp.asarray((np.arange(s)[:, None] // SEL_BLOCK == np.arange(nsel)[None, :]), dtype=BF16)
    kv_spec = pl.BlockSpec((1, s, LANES), lambda i, g, j: (i, 0, g))
    q_spec = pl.BlockSpec((1, tq, GW), lambda i, g, j: (i, j, g))
    rows = HPG * tq
    return pl.pallas_call(
        functools.partial(_attn_kernel, kt),
        grid=(b, KV_GROUPS, s // tq),
        in_specs=[
            q_spec, kv_spec, kv_spec, kv_spec, kv_spec,
            pl.BlockSpec((1, 1, tq, nsel), lambda i, g, j: (i, g, j, 0)),
            pl.BlockSpec((s, nsel), lambda i, g, j: (0, 0)),
            q_spec,
            pl.BlockSpec((1, tq, LANES), lambda i, g, j: (i, j, g)),
        ],
        out_specs=q_spec,
        out_shape=jax.ShapeDtypeStruct((b, s, B_HEADS * DK), BF16),
        scratch_shapes=[pltpu.VMEM((rows, 1), F32), pltpu.VMEM((rows, 1), F32),
                        pltpu.VMEM((rows, LANES), F32)],
        compiler_params=_params(("parallel", "parallel", "arbitrary")),
        name="attn",
    )(q, ksd, vsd, kwd, vwd, sel, onehot, o_cmp, gl)


def _post_kernel(ya_ref, yb_ref, wo_ref, x_ref, g1_ref, ng_ref, sc_ref, sh_ref, rw_ref, rb_ref,
                 xo_ref, h_ref, gt_ref):
    mixed = jnp.dot(ya_ref[0], wo_ref[:A_WIDTH], preferred_element_type=F32)
    mixed = mixed + jnp.dot(yb_ref[0], wo_ref[A_WIDTH:], preferred_element_type=F32)
    x = x_ref[0] + g1_ref[0] * mixed
    xo_ref[0] = x
    ms = jnp.mean(x * x, axis=-1, keepdims=True)
    h = x * lax.rsqrt(ms + NORM_EPS) * ng_ref[...]
    h = h * (1.0 + sc_ref[0]) + sh_ref[0]
    h_ref[0] = h.astype(BF16)
    logits = jnp.dot(h, rw_ref[...], precision=HIGHEST, preferred_element_type=F32) + rb_ref[...]
    lane = lax.broadcasted_iota(jnp.int32, logits.shape, 1)
    vals = logits
    top = jnp.max(vals, axis=-1, keepdims=True)
    expv = jnp.zeros_like(logits)
    for _ in range(TOP_K):
        m = jnp.max(vals, axis=-1, keepdims=True)
        first = jnp.min(jnp.where(vals == m, lane, LANES), axis=-1, keepdims=True)
        pick = lane == first
        expv = jnp.where(pick, jnp.exp(m - top), expv)
        vals = jnp.where(pick, -jnp.inf, vals)
    gates = expv / jnp.sum(expv, axis=-1, keepdims=True)
    gt_ref[...] = gates.T[:N_EXPERTS]


def _post(y_a, y_b, w_out, x, g1, norm_g, sc, sh, router_w, router_b, tm=512):
    b, s, d = x.shape
    tm = min(tm, s)
    nt = s // tm
    rw = jnp.zeros((d, LANES), F32).at[:, :N_EXPERTS].set(router_w)
    rb = jnp.full((1, LANES), -1e30, F32).at[0, :N_EXPERTS].set(router_b)
    vec = pl.BlockSpec((1, 1, d), lambda i, j: (i, 0, 0))
    row = lambda w: pl.BlockSpec((1, tm, w), lambda i, j: (i, j, 0))
    return pl.pallas_call(
        _post_kernel,
        grid=(b, nt),
        in_specs=[
            row(A_WIDTH), row(B_HEADS * DK),
            pl.BlockSpec((d, d), lambda i, j: (0, 0)),
            row(d), vec,
            pl.BlockSpec((1, d), lambda i, j: (0, 0)), vec, vec,
            pl.BlockSpec((d, LANES), lambda i, j: (0, 0)),
            pl.BlockSpec((1, LANES), lambda i, j: (0, 0)),
        ],
        out_specs=[row(d), row(d), pl.BlockSpec((N_EXPERTS, tm), lambda i, j: (0, i * nt + j))],
        out_shape=[
            jax.ShapeDtypeStruct((b, s, d), F32),
            jax.ShapeDtypeStruct((b, s, d), BF16),
            jax.ShapeDtypeStruct((N_EXPERTS, b * s), F32),
        ],
        compiler_params=_params(("parallel", "parallel")),
        name="post",
    )(y_a, y_b, w_out.astype(BF16), x, g1, norm_g.reshape(1, d), sc, sh, rw, rb)


def _moe_kernel(rows, h_ref, gt_ref, w1_ref, b1_ref, w2_ref, b2_ref, y_ref, pos_ref):
    e = pl.program_id(1)
    tile = h_ref.shape[0]
    sub = 256

    @pl.when(e == 0)
    def _():
        y_ref[...] = jnp.zeros(y_ref.shape, F32)
        r = lax.broadcasted_iota(jnp.int32, (sub, sub), 0)
        c = lax.broadcasted_iota(jnp.int32, (sub, sub), 1)
        before = jnp.where(r < c, 1.0, 0.0).astype(BF16)
        carry = jnp.zeros((N_EXPERTS, 1), F32)
        for k in range(tile // sub):
            routed = jnp.where(gt_ref[:, k * sub:(k + 1) * sub] > 0, 1.0, 0.0)
            pos_ref[:, k * sub:(k + 1) * sub] = carry + jnp.dot(routed.astype(BF16), before,
                                                                 preferred_element_type=F32)
            carry = carry + jnp.sum(routed, axis=-1, keepdims=True)

    gate_row = gt_ref[pl.ds(e, 1), :]
    routed_row = gate_row > 0
    pos_row = pos_ref[pl.ds(e, 1), :]
    count = jnp.sum(jnp.where(routed_row, 1, 0))
    tn = (((0,), (0,)), ((), ()))

    def body(c, carry):
        slot = lax.broadcasted_iota(jnp.int32, (rows, tile), 0).astype(F32) + (c * rows).astype(F32)
        hit = (pos_row == slot) & routed_row
        onehot = jnp.where(hit, 1.0, 0.0).astype(BF16)
        xe = jnp.dot(onehot, h_ref[...], preferred_element_type=F32).astype(BF16)
        hdn = jnp.dot(xe, w1_ref[0], preferred_element_type=F32) + b1_ref[0]
        glu = jnp.minimum(hdn[:, :D_EXPERT], SWIGLU_LIMIT)
        lin = jnp.clip(hdn[:, D_EXPERT:], -SWIGLU_LIMIT, SWIGLU_LIMIT)
        act = glu * jax.nn.sigmoid(SWIGLU_ALPHA * glu) * (lin + 1.0)
        out = jnp.dot(act.astype(BF16), w2_ref[0], preferred_element_type=F32) + b2_ref[0]
        row_gate = jnp.sum(jnp.where(hit, gate_row, 0.0), axis=-1, keepdims=True)
        scaled = (out * row_gate).astype(BF16)
        y_ref[...] += lax.dot_general(onehot, scaled, tn, preferred_element_type=F32)
        return carry

    lax.fori_loop(0, (count + rows - 1) // rows, body, 0)


def _moe(h2, gates_t, w1, b1, w2, b2, tile=2048, rows=128):
    t, d = h2.shape
    tile = min(tile, t)
    n_e, _, f2 = w1.shape
    return pl.pallas_call(
        functools.partial(_moe_kernel, rows),
        grid=(t // tile, n_e),
        in_specs=[
            pl.BlockSpec((tile, d), lambda i, e: (i, 0)),
            pl.BlockSpec((n_e, tile), lambda i, e: (0, i)),
            pl.BlockSpec((1, d, f2), lambda i, e: (e, 0, 0)),
            pl.BlockSpec((1, 1, f2), lambda i, e: (e, 0, 0)),
            pl.BlockSpec((1, f2 // 2, d), lambda i, e: (e, 0, 0)),
            pl.BlockSpec((1, 1, d), lambda i, e: (e, 0, 0)),
        ],
        out_specs=pl.BlockSpec((tile, d), lambda i, e: (i, 0)),
        out_shape=jax.ShapeDtypeStruct((t, d), F32),
        scratch_shapes=[pltpu.VMEM((n_e, tile), F32)],
        compiler_params=_params(("parallel", "arbitrary")),
        name="moe",
    )(h2, gates_t, w1.astype(BF16), b1.reshape(n_e, 1, f2), w2.astype(BF16), b2.reshape(n_e, 1, d))


def _final_kernel(x_ref, y_ref, g2_ref, fg_ref, o_ref):
    x = x_ref[0] + g2_ref[0] * y_ref[0]
    ms = jnp.mean(x * x, axis=-1, keepdims=True)
    o_ref[0] = x * lax.rsqrt(ms + NORM_EPS) * fg_ref[...]


def _final(x, y, g2, final_g, tm=512):
    b, s, d = x.shape
    tm = min(tm, s)
    row = pl.BlockSpec((1, tm, d), lambda i, j: (i, j, 0))
    return pl.pallas_call(
        _final_kernel,
        grid=(b, s // tm),
        in_specs=[row, row, pl.BlockSpec((1, 1, d), lambda i, j: (i, 0, 0)),
                  pl.BlockSpec((1, d), lambda i, j: (0, 0))],
        out_specs=row,
        out_shape=jax.ShapeDtypeStruct((b, s, d), F32),
        compiler_params=_params(("parallel", "parallel")),
        name="final",
    )(x, y, g2, final_g.reshape(1, d))


def _widen_w_in(w):
    offs = np.cumsum([A_WIDTH, A_WIDTH, B_HEADS * DK] + [KV_GROUPS * DK] * 6).tolist()
    u, v, q, kc, vc, ks, vs, kw, vw, gl = jnp.split(w, offs, axis=1)

    def dup(t):
        return jnp.concatenate([t[:, :DK], t[:, :DK], t[:, DK:], t[:, DK:]], axis=1)

    per_group = HPG * 3
    gl_groups = [jnp.pad(gl[:, g * per_group:(g + 1) * per_group], ((0, 0), (0, LANES - per_group)))
                 for g in range(KV_GROUPS)]
    wide = jnp.concatenate([u, v, q, kc, vc, dup(ks), dup(vs), dup(kw), dup(vw)] + gl_groups, axis=1)
    assert wide.shape[1] == P_WIDE
    return wide.astype(BF16)


def _rope_tables(pos):
    half = DK // 2
    inv = ROPE_THETA ** (-jnp.arange(half, dtype=F32) / half)
    ang = pos.astype(F32)[..., None] * inv
    cos = jnp.cos(ang)
    sin = jnp.sin(ang)
    reps = LANES // DK
    return (jnp.concatenate([cos, cos] * reps, axis=-1),
            jnp.concatenate([-sin, sin] * reps, axis=-1))


def kernel(x, c, positions, ada_w, ada_b, norm1_g, norm2_g, w_in, w_out, sg_ln_g, sg_ln_b, sg_w, sg_b,
           cmp_pe_k, cmp_pe_v, cmp_w1_k, cmp_w2_k, cmp_w1_v, cmp_w2_v, router_w, router_b,
           exp_w1, exp_b1, exp_w2, exp_b2, final_g):
    b, s, d = x.shape
    n_layers = ada_w.shape[0]
    mod = _ada_mod(c, ada_w, ada_b)
    cos, sin = _rope_tables(positions)
    cmp_end = jnp.minimum(CMP_STRIDE * jnp.arange(s // CMP_STRIDE) + CMP_BLOCK - 1, s - 1)
    cos_c, sin_c = _rope_tables(positions[:, cmp_end])
    res = None
    for l in range(n_layers):
        sh1, sc1, g1, sh2, sc2, g2 = [m.reshape(b, 1, d) for m in jnp.split(mod[l], 6, axis=-1)]
        x, uv, q, kc, vc, ksd, vsd, kwd, vwd, gl = _pre(x, res, norm1_g[l], sc1, sh1,
                                                        _widen_w_in(w_in[l]), cos, sin)
        y_a = _sgu(uv, sg_ln_g[l], sg_ln_b[l], sg_w[l], sg_b[l])
        kcmp, vcmp = _compress(kc, vc, cmp_pe_k[l], cmp_pe_v[l], cmp_w1_k[l], cmp_w2_k[l],
                               cmp_w1_v[l], cmp_w2_v[l], cos_c, sin_c)
        o_cmp, sel = _cmp_sel(q, kcmp, vcmp)
        y_b = _attn(q, ksd, vsd, kwd, vwd, sel, o_cmp, gl)
        x, h2, gates_t = _post(y_a, y_b, w_out[l], x, g1, norm2_g[l], sc2, sh2, router_w[l], router_b[l])
        y = _moe(h2.reshape(b * s, d), gates_t, exp_w1[l], exp_b1[l], exp_w2[l], exp_b2[l])
        res = (y.reshape(b, s, d), g2)
    return _final(x, res[0], res[1], final_g)
```

```python
import functools

import numpy as np
import jax
import jax.numpy as jnp
from jax import lax
from jax.experimental import pallas as pl
from jax.experimental.pallas import tpu as pltpu

F32 = jnp.float32
BF16 = jnp.bfloat16
HIGHEST = lax.Precision.HIGHEST

D_MODEL = 1024
A_WIDTH = 512
A_HEADS = 8
CHUNK = 128
B_HEADS = 8
DK = 64
KV_GROUPS = 2
HPG = B_HEADS // KV_GROUPS
CMP_BLOCK = 32
CMP_STRIDE = 16
SEL_BLOCK = 64
SEL_TOP = 16
WINDOW = 512
ROPE_THETA = 10000.0
N_EXPERTS = 32
TOP_K = 4
D_EXPERT = 1024
SWIGLU_LIMIT = 7.0
SWIGLU_ALPHA = 1.702
NORM_EPS = 1e-6

LANES = 128
SUBLANES = 8
GW = HPG * DK
N_FORCED = 3
MASK_BIAS = -32768.0
VMEM_LIMIT = 56 * 1024 * 1024

C_U, C_V, C_Q = 0, 512, 1024
C_KC, C_VC, C_KS, C_VS, C_KW, C_VW = 1536, 1664, 1792, 1920, 2048, 2176
C_GL = 2304
P_WIDE = 2560


def _params(sem):
    return pltpu.CompilerParams(dimension_semantics=sem, vmem_limit_bytes=VMEM_LIMIT)


def _split_bf16(x, parts):
    out = []
    for _ in range(parts):
        piece = x.astype(BF16)
        out.append(piece)
        x = x - piece.astype(F32)
    return out


def _ada_kernel(c_ref, w_ref, b_ref, o_ref):
    c = c_ref[...]
    cond = c * jax.nn.sigmoid(c)
    o_ref[0] = jnp.dot(cond, w_ref[0], precision=HIGHEST, preferred_element_type=F32) + b_ref[0]


def _ada_mod(c, ada_w, ada_b):
    n_layers, d, d6 = ada_w.shape
    b = c.shape[0]
    rows = SUBLANES
    c_pad = jnp.zeros((rows, d), F32).at[:b].set(c)
    out = pl.pallas_call(
        _ada_kernel,
        grid=(n_layers, d6 // d),
        in_specs=[
            pl.BlockSpec((rows, d), lambda l, j: (0, 0)),
            pl.BlockSpec((1, d, d), lambda l, j: (l, 0, j)),
            pl.BlockSpec((1, 1, d), lambda l, j: (l, 0, j)),
        ],
        out_specs=pl.BlockSpec((1, rows, d), lambda l, j: (l, 0, j)),
        out_shape=jax.ShapeDtypeStruct((n_layers, rows, d6), F32),
        compiler_params=_params(("arbitrary", "arbitrary")),
        name="ada",
    )(c_pad, ada_w, ada_b.reshape(n_layers, 1, d6))
    return out[:, :b]


def _rope_slab(t, cos, sin_signed, lo):
    partner = jnp.where(lo, pltpu.roll(t, LANES - DK // 2, 1), pltpu.roll(t, DK // 2, 1))
    return t * cos + partner * sin_signed


def _rope(t, cos, sin_signed):
    lane = lax.broadcasted_iota(jnp.int32, (1, LANES), 1)
    lo = (lane % DK) < (DK // 2)
    slabs = [_rope_slab(t[:, s * LANES:(s + 1) * LANES], cos, sin_signed, lo)
             for s in range(t.shape[1] // LANES)]
    return slabs[0] if len(slabs) == 1 else jnp.concatenate(slabs, axis=1)


def _pre_kernel(has_res, *refs):
    if has_res:
        (x_ref, y_ref, g2_ref, ng_ref, sc_ref, sh_ref, w_ref, cos_ref, sin_ref,
         xo_ref, uv_ref, qt_ref, kc_ref, vc_ref, ks_ref, vst_ref, kw_ref, vwt_ref, glt_ref) = refs
        x = x_ref[0] + g2_ref[0] * y_ref[0]
        xo_ref[0] = x
    else:
        (x_ref, ng_ref, sc_ref, sh_ref, w_ref, cos_ref, sin_ref,
         uv_ref, qt_ref, kc_ref, vc_ref, ks_ref, vst_ref, kw_ref, vwt_ref, glt_ref) = refs
        x = x_ref[0]
    ms = jnp.mean(x * x, axis=-1, keepdims=True)
    h = x * lax.rsqrt(ms + NORM_EPS) * ng_ref[...]
    h = h * (1.0 + sc_ref[0]) + sh_ref[0]
    proj = jnp.dot(h.astype(BF16), w_ref[...], preferred_element_type=F32)
    cos = cos_ref[0]
    sin = sin_ref[0]
    uv_ref[0] = proj[:, C_U:C_Q]
    qt_ref[0] = (_rope(proj[:, C_Q:C_KC], cos, sin) * (DK ** -0.5)).T
    kc_ref[0] = proj[:, C_KC:C_VC]
    vc_ref[0] = proj[:, C_VC:C_KS]
    ks_ref[0] = _rope(proj[:, C_KS:C_VS], cos, sin).astype(BF16)
    vst_ref[0] = proj[:, C_VS:C_KW].T.astype(BF16)
    kw_ref[0] = _rope(proj[:, C_KW:C_VW], cos, sin).astype(BF16)
    vwt_ref[0] = proj[:, C_VW:C_GL].T.astype(BF16)
    glt_ref[0] = proj[:, C_GL:P_WIDE].T


def _pre(x, res, norm_g, sc, sh, w_wide, cos, sin, tm=512):
    b, s, d = x.shape
    tm = min(tm, s)
    row = lambda w: pl.BlockSpec((1, tm, w), lambda i, j: (i, j, 0))
    col = lambda w: pl.BlockSpec((1, w, tm), lambda i, j: (i, 0, j))
    vec = pl.BlockSpec((1, 1, d), lambda i, j: (i, 0, 0))
    in_specs = [row(d)]
    args = [x]
    if res is not None:
        y, g2 = res
        in_specs += [row(d), vec]
        args += [y, g2]
    in_specs += [pl.BlockSpec((1, d), lambda i, j: (0, 0)), vec, vec,
                 pl.BlockSpec((d, P_WIDE), lambda i, j: (0, 0)), row(LANES), row(LANES)]
    args += [norm_g.reshape(1, d), sc, sh, w_wide, cos, sin]
    kv = KV_GROUPS * DK
    outs = [(d, F32, False)] if res is not None else []
    outs += [(2 * A_WIDTH, F32, False), (B_HEADS * DK, F32, True), (kv, F32, False), (kv, F32, False),
             (kv, BF16, False), (kv, BF16, True), (kv, BF16, False), (kv, BF16, True),
             (KV_GROUPS * LANES, F32, True)]
    res_out = pl.pallas_call(
        functools.partial(_pre_kernel, res is not None),
        grid=(b, s // tm),
        in_specs=in_specs,
        out_specs=[col(w) if t else row(w) for w, _, t in outs],
        out_shape=[jax.ShapeDtypeStruct((b, w, s) if t else (b, s, w), dt) for w, dt, t in outs],
        compiler_params=_params(("parallel", "parallel")),
        name="pre",
    )(*args)
    if res is None:
        res_out = [x] + list(res_out)
    return res_out


def _sgu_kernel(uv_ref, lng_ref, lnb_ref, w_ref, bias_ref, o_ref):
    rows = uv_ref.shape[1]
    uv = uv_ref[0]
    gu = jax.nn.gelu(uv[:, :A_WIDTH])
    gv = jax.nn.gelu(uv[:, A_WIDTH:])
    mu = jnp.mean(gv, axis=-1, keepdims=True)
    var = jnp.mean(jnp.square(gv - mu), axis=-1, keepdims=True)
    vn = ((gv - mu) * lax.rsqrt(var + NORM_EPS) * lng_ref[...] + lnb_ref[...]).astype(BF16)
    r = lax.broadcasted_iota(jnp.int32, (CHUNK, CHUNK), 0)
    c = lax.broadcasted_iota(jnp.int32, (CHUNK, CHUNK), 1)
    causal = c <= r
    lane_lo = lax.broadcasted_iota(jnp.int32, (CHUNK, LANES), 1) < DK
    for p in range(A_HEADS // 2):
        w0 = jnp.where(causal, w_ref[2 * p], 0.0).astype(BF16)
        w1 = jnp.where(causal, w_ref[2 * p + 1], 0.0).astype(BF16)
        bias = bias_ref[:, p * LANES:(p + 1) * LANES]
        for ch in range(rows // CHUNK):
            rs = slice(ch * CHUNK, (ch + 1) * CHUNK)
            cs = slice(p * LANES, (p + 1) * LANES)
            vp = vn[rs, cs]
            m0 = jnp.dot(w0, vp, preferred_element_type=F32)
            m1 = jnp.dot(w1, vp, preferred_element_type=F32)
            mixed = jnp.where(lane_lo, m0, m1) + bias
            o_ref[0, rs, cs] = (gu[rs, cs] * mixed).astype(o_ref.dtype)


def _sgu(uv, ln_g, ln_b, w_s, b_s, tm=512):
    b, s, _ = uv.shape
    tm = min(tm, s)
    bias = jnp.repeat(b_s.T, A_WIDTH // A_HEADS, axis=1)
    return pl.pallas_call(
        _sgu_kernel,
        grid=(b, s // tm),
        in_specs=[
            pl.BlockSpec((1, tm, 2 * A_WIDTH), lambda i, j: (i, j, 0)),
            pl.BlockSpec((1, A_WIDTH), lambda i, j: (0, 0)),
            pl.BlockSpec((1, A_WIDTH), lambda i, j: (0, 0)),
            pl.BlockSpec((A_HEADS, CHUNK, CHUNK), lambda i, j: (0, 0, 0)),
            pl.BlockSpec((CHUNK, A_WIDTH), lambda i, j: (0, 0)),
        ],
        out_specs=pl.BlockSpec((1, tm, A_WIDTH), lambda i, j: (i, j, 0)),
        out_shape=jax.ShapeDtypeStruct((b, s, A_WIDTH), BF16),
        compiler_params=_params(("parallel", "parallel")),
        name="sgu",
    )(uv, ln_g.reshape(1, -1), ln_b.reshape(1, -1), w_s, bias)


def _compress_kernel(kr_ref, vr_ref, pek_ref, pev_ref, w1k_ref, w1v_ref, w2k_ref, w2v_ref,
                     cos_ref, sin_ref, ko_ref, vt_ref):
    nc = kr_ref.shape[1]

    def mlp(r, pe_ref, w1_ref, w2_ref):
        top = jnp.dot(r + pe_ref[0:1], w1_ref[0], precision=HIGHEST, preferred_element_type=F32)
        bot = jnp.dot(r + pe_ref[1:2], w1_ref[1], precision=HIGHEST, preferred_element_type=F32)
        pre = top + pltpu.roll(bot, nc - 1, 0)
        return jnp.dot(jax.nn.gelu(pre), w2_ref[...], precision=HIGHEST, preferred_element_type=F32)

    kc = _rope(mlp(kr_ref[0], pek_ref, w1k_ref, w2k_ref), cos_ref[0], sin_ref[0])
    vc = mlp(vr_ref[0], pev_ref, w1v_ref, w2v_ref)
    lo = lax.broadcasted_iota(jnp.int32, (nc, LANES), 1) < DK
    rolled = pltpu.roll(kc, DK, 1)
    for g in range(KV_GROUPS):
        dup = jnp.where(lo, kc, rolled) if g == 0 else jnp.where(lo, rolled, kc)
        hi, low = _split_bf16(dup, 2)
        ko_ref[0, g, :, 0:LANES] = hi
        ko_ref[0, g, :, LANES:2 * LANES] = jnp.where(lo, low, jnp.zeros_like(low))
    vt_ref[0] = vc.T.astype(BF16)


def _compress_weights(pe, w1, w2):
    half = CMP_BLOCK // 2
    eye = jnp.eye(KV_GROUPS, dtype=F32)
    w1r = w1.reshape(CMP_BLOCK, DK, DK)
    wfull = jnp.einsum('lde,gh->lgdhe', w1r, eye)
    w1s = wfull.reshape(2, half * KV_GROUPS * DK, KV_GROUPS * DK)
    pes = jnp.broadcast_to(pe.reshape(2, half, 1, DK), (2, half, KV_GROUPS, DK)).reshape(2, -1)
    w2bd = jnp.einsum('de,gh->gdhe', w2, eye).reshape(KV_GROUPS * DK, KV_GROUPS * DK)
    return pes, w1s, w2bd


def _compress(kc, vc, pe_k, pe_v, w1_k, w2_k, w1_v, w2_v, cos_c, sin_c):
    b, s, _ = kc.shape
    nc = s // CMP_STRIDE
    rw = CMP_STRIDE * LANES
    pek, w1ks, w2kb = _compress_weights(pe_k, w1_k, w2_k)
    pev, w1vs, w2vb = _compress_weights(pe_v, w1_v, w2_v)
    full = lambda shape: pl.BlockSpec(shape, lambda i: (0,) * len(shape))
    return pl.pallas_call(
        _compress_kernel,
        grid=(b,),
        in_specs=[
            pl.BlockSpec((1, nc, rw), lambda i: (i, 0, 0)),
            pl.BlockSpec((1, nc, rw), lambda i: (i, 0, 0)),
            full((2, rw)), full((2, rw)),
            full((2, rw, LANES)), full((2, rw, LANES)),
            full((LANES, LANES)), full((LANES, LANES)),
            pl.BlockSpec((1, nc, LANES), lambda i: (i, 0, 0)),
            pl.BlockSpec((1, nc, LANES), lambda i: (i, 0, 0)),
        ],
        out_specs=[pl.BlockSpec((1, KV_GROUPS, nc, 2 * LANES), lambda i: (i, 0, 0, 0)),
                   pl.BlockSpec((1, KV_GROUPS * DK, nc), lambda i: (i, 0, 0))],
        out_shape=[jax.ShapeDtypeStruct((b, KV_GROUPS, nc, 2 * LANES), BF16),
                   jax.ShapeDtypeStruct((b, KV_GROUPS * DK, nc), BF16)],
        compiler_params=_params(("parallel",)),
        name="compress",
    )(kc.reshape(b, nc, rw), vc.reshape(b, nc, rw), pek, pev, w1ks, w1vs, w2kb, w2vb, cos_c, sin_c)


def _heads_on_lanes(qt):
    return jnp.concatenate([qt[h * DK:(h + 1) * DK] for h in range(HPG)], axis=1)


def _cmp_sel_kernel(n_top, qt_ref, k_ref, vt_ref, map_ref, oc_ref, bias_ref):
    tq = qt_ref.shape[2]
    cols = HPG * tq
    nc = k_ref.shape[2]
    nsel = map_ref.shape[0]
    q0 = pl.program_id(2) * tq
    q_hi, q_lo = _split_bf16(_heads_on_lanes(qt_ref[0]), 2)
    q3t = jnp.concatenate([q_hi, q_lo, q_hi, jnp.zeros_like(q_hi)], axis=0)
    s = jnp.dot(k_ref[0, 0], q3t, preferred_element_type=F32)
    n_idx = lax.broadcasted_iota(jnp.int32, (nc, 1), 0)
    t_col = q0 + lax.broadcasted_iota(jnp.int32, (1, cols), 1) % tq
    s = jnp.where((CMP_STRIDE * n_idx + CMP_BLOCK - 1) <= t_col, s, -jnp.inf)
    m = jnp.max(s, axis=0, keepdims=True)
    m = jnp.where(m == -jnp.inf, 0.0, m)
    e = jnp.exp(s - m)
    d = jnp.sum(e, axis=0, keepdims=True)
    p = e / jnp.where(d > 0, d, 1.0)
    oc_ref[0, 0] = jnp.dot(vt_ref[0], p.astype(BF16), preferred_element_type=F32)
    psum = p[:, 0:tq]
    for h in range(1, HPG):
        psum = psum + p[:, h * tq:(h + 1) * tq]
    imp = jnp.dot(map_ref[...], jnp.concatenate(_split_bf16(psum, 3), axis=0),
                  preferred_element_type=F32)
    j = lax.broadcasted_iota(jnp.int32, (nsel, tq), 0)
    cur = (q0 + lax.broadcasted_iota(jnp.int32, (nsel, tq), 1)) // SEL_BLOCK
    valid = j <= cur
    forced = (j == 0) | (j == cur) | (j == cur - 1)
    keep = forced | (valid & (cur < n_top))
    vals = jnp.where(valid & jnp.logical_not(forced), imp, -jnp.inf)
    bias = jnp.where(keep, 0.0, MASK_BIAS)
    for _ in range(n_top - N_FORCED):
        mx = jnp.max(vals, axis=0, keepdims=True)
        first = jnp.min(jnp.where(vals == mx, j, nsel), axis=0, keepdims=True)
        pick = (j == first) & (mx > -jnp.inf)
        bias = jnp.where(pick, 0.0, bias)
        vals = jnp.where(pick, -jnp.inf, vals)
    bias_ref[0, 0] = bias.astype(BF16)


def _sel_map_t(s):
    nc = s // CMP_STRIDE
    n_cmp = (s - CMP_BLOCK) // CMP_STRIDE + 1
    n_sel = s // SEL_BLOCK
    cs = CMP_STRIDE * np.arange(n_cmp)[:, None]
    ce = cs + CMP_BLOCK
    ss = SEL_BLOCK * np.arange(n_sel)[None, :]
    se = ss + SEL_BLOCK
    ov = np.clip(np.minimum(ce, se) - np.maximum(cs, ss), 0, None) / CMP_STRIDE
    out = np.zeros((n_sel, nc), np.float32)
    out[:, :n_cmp] = ov.T
    return out


def _cmp_sel(qt, kcmp, vcmp_t, tq=128):
    b, _, s = qt.shape
    nc = s // CMP_STRIDE
    nsel = s // SEL_BLOCK
    n_top = min(SEL_TOP, nsel)
    sel_map = jnp.asarray(np.tile(_sel_map_t(s), (1, 3)), dtype=BF16)
    return pl.pallas_call(
        functools.partial(_cmp_sel_kernel, n_top),
        grid=(b, KV_GROUPS, s // tq),
        in_specs=[
            pl.BlockSpec((1, GW, tq), lambda i, g, j: (i, g, j)),
            pl.BlockSpec((1, 1, nc, 2 * LANES), lambda i, g, j: (i, g, 0, 0)),
            pl.BlockSpec((1, DK, nc), lambda i, g, j: (i, g, 0)),
            pl.BlockSpec((nsel, 3 * nc), lambda i, g, j: (0, 0)),
        ],
        out_specs=[
            pl.BlockSpec((1, 1, DK, HPG * tq), lambda i, g, j: (i, g, 0, j)),
            pl.BlockSpec((1, 1, nsel, tq), lambda i, g, j: (i, g, 0, j)),
        ],
        out_shape=[
            jax.ShapeDtypeStruct((b, KV_GROUPS, DK, HPG * s), F32),
            jax.ShapeDtypeStruct((b, KV_GROUPS, nsel, s), BF16),
        ],
        compiler_params=_params(("parallel", "parallel", "parallel")),
        name="cmp_sel",
    )(qt, kcmp, vcmp_t, sel_map)


def _attn_kernel(kt, qt_ref, ks_ref, kw_ref, vst_ref, vwt_ref, bias_ref, oh_ref, oc_ref, glt_ref,
                 o_ref, qa_ref, acc_ref):
    tq = qt_ref.shape[2]
    cols = HPG * tq
    nsel = bias_ref.shape[2]
    g = pl.program_id(1)
    q0 = pl.program_id(2) * tq

    q4t = _heads_on_lanes(qt_ref[0]).astype(BF16)
    zero = jnp.zeros_like(q4t)
    qa_ref[0:DK] = jnp.where(g == 0, q4t, zero)
    qa_ref[DK:2 * DK] = jnp.where(g == 1, q4t, zero)
    qa_ref[LANES:LANES + nsel] = jnp.concatenate([bias_ref[0, 0]] * HPG, axis=1)
    qaug = qa_ref[...]
    t_col = q0 + lax.broadcasted_iota(jnp.int32, (1, cols), 1) % tq

    def scores(c, diagonal):
        k0 = pl.multiple_of(c * kt, kt)
        kaug = jnp.concatenate([ks_ref[0, pl.ds(k0, kt), :], oh_ref[pl.ds(k0, kt), :]], axis=1)
        sc = jnp.dot(kaug, qaug, preferred_element_type=F32)
        if diagonal:
            kpos = k0 + lax.broadcasted_iota(jnp.int32, (kt, 1), 0)
            sc = jnp.where(kpos <= t_col, sc, -jnp.inf)
        return sc

    def fold(x, op):
        return op(x.reshape(x.shape[0] // SUBLANES, SUBLANES, cols), axis=0)

    n_full = q0 // kt
    m8 = lax.fori_loop(0, n_full, lambda c, m: jnp.maximum(m, fold(scores(c, False), jnp.max)),
                       jnp.full((SUBLANES, cols), -jnp.inf, F32))
    m8 = jnp.maximum(m8, fold(scores(n_full, True), jnp.max))
    m = jnp.max(m8, axis=0, keepdims=True)

    acc_ref[...] = jnp.zeros(acc_ref.shape, F32)

    def accumulate(c, sc, l8):
        k0 = pl.multiple_of(c * kt, kt)
        pr = jnp.exp(sc - m)
        acc_ref[...] += jnp.dot(vst_ref[0, :, pl.ds(k0, kt)], pr.astype(BF16), preferred_element_type=F32)
        return l8 + fold(pr, jnp.sum)

    l8 = lax.fori_loop(0, n_full, lambda c, l: accumulate(c, scores(c, False), l),
                       jnp.zeros((SUBLANES, cols), F32))
    l8 = accumulate(n_full, scores(n_full, True), l8)
    o_sel = acc_ref[...] / jnp.sum(l8, axis=0, keepdims=True)

    wlen = WINDOW + tq
    w0 = pl.multiple_of(jnp.maximum(q0 - WINDOW, 0), tq)
    sw = jnp.dot(kw_ref[0, pl.ds(w0, wlen), :], qaug[0:LANES], preferred_element_type=F32)
    dlt = t_col - (w0 + lax.broadcasted_iota(jnp.int32, (wlen, 1), 0))
    sw = jnp.where((dlt >= 0) & (dlt < WINDOW), sw, -jnp.inf)
    pw = jnp.exp(sw - jnp.max(sw, axis=0, keepdims=True))
    o_win = jnp.dot(vwt_ref[0, :, pl.ds(w0, wlen)], pw.astype(BF16), preferred_element_type=F32)
    o_win = o_win / jnp.sum(pw, axis=0, keepdims=True)

    gates = jax.nn.sigmoid(glt_ref[0])
    oc = oc_ref[0, 0]
    for p in range(HPG // 2):
        halves = []
        for hh in (2 * p, 2 * p + 1):
            cs = slice(hh * tq, (hh + 1) * tq)
            halves.append(gates[3 * hh:3 * hh + 1] * oc[:, cs] + gates[3 * hh + 1:3 * hh + 2] * o_sel[:, cs]
                          + gates[3 * hh + 2:3 * hh + 3] * o_win[:, cs])
        o_ref[0, :, p * LANES:(p + 1) * LANES] = jnp.concatenate(halves, axis=0).T.astype(o_ref.dtype)


def _attn(qt, ks, kw, vs_t, vw_t, bias_t, oc_t, gl_t, tq=128, kt=512):
    b, _, s = qt.shape
    nsel = s // SEL_BLOCK
    kt = min(kt, s)
    onehot = jnp.asarray((np.arange(s)[:, None] // SEL_BLOCK == np.arange(nsel)[None, :]), dtype=BF16)
    k_spec = pl.BlockSpec((1, s, KV_GROUPS * DK), lambda i, g, j: (i, 0, 0))
    vt_spec = pl.BlockSpec((1, DK, s), lambda i, g, j: (i, g, 0))
    cols = HPG * tq
    return pl.pallas_call(
        functools.partial(_attn_kernel, kt),
        grid=(b, KV_GROUPS, s // tq),
        in_specs=[
            pl.BlockSpec((1, GW, tq), lambda i, g, j: (i, g, j)),
            k_spec, k_spec, vt_spec, vt_spec,
            pl.BlockSpec((1, 1, nsel, tq), lambda i, g, j: (i, g, 0, j)),
            pl.BlockSpec((s, nsel), lambda i, g, j: (0, 0)),
            pl.BlockSpec((1, 1, DK, cols), lambda i, g, j: (i, g, 0, j)),
            pl.BlockSpec((1, LANES, tq), lambda i, g, j: (i, g, j)),
        ],
        out_specs=pl.BlockSpec((1, tq, GW), lambda i, g, j: (i, j, g)),
        out_shape=jax.ShapeDtypeStruct((b, s, B_HEADS * DK), BF16),
        scratch_shapes=[pltpu.VMEM((LANES + nsel, cols), BF16), pltpu.VMEM((DK, cols), F32)],
        compiler_params=_params(("parallel", "parallel", "arbitrary")),
        name="attn",
    )(qt, ks, kw, vs_t, vw_t, bias_t, onehot, oc_t, gl_t)


def _post_kernel(ya_ref, yb_ref, wo_ref, x_ref, g1_ref, ng_ref, sc_ref, sh_ref, rw_ref, rb_ref,
                 xo_ref, h_ref, gt_ref):
    mixed = jnp.dot(ya_ref[0], wo_ref[:A_WIDTH], preferred_element_type=F32)
    mixed = mixed + jnp.dot(yb_ref[0], wo_ref[A_WIDTH:], preferred_element_type=F32)
    x = x_ref[0] + g1_ref[0] * mixed
    xo_ref[0] = x
    ms = jnp.mean(x * x, axis=-1, keepdims=True)
    h = x * lax.rsqrt(ms + NORM_EPS) * ng_ref[...]
    h = h * (1.0 + sc_ref[0]) + sh_ref[0]
    h_ref[0] = h.astype(BF16)
    logits = jnp.dot(h, rw_ref[...], precision=HIGHEST, preferred_element_type=F32) + rb_ref[...]
    lane = lax.broadcasted_iota(jnp.int32, logits.shape, 1)
    vals = logits
    top = jnp.max(vals, axis=-1, keepdims=True)
    expv = jnp.zeros_like(logits)
    for _ in range(TOP_K):
        m = jnp.max(vals, axis=-1, keepdims=True)
        first = jnp.min(jnp.where(vals == m, lane, LANES), axis=-1, keepdims=True)
        pick = lane == first
        expv = jnp.where(pick, jnp.exp(m - top), expv)
        vals = jnp.where(pick, -jnp.inf, vals)
    gates = expv / jnp.sum(expv, axis=-1, keepdims=True)
    gt_ref[...] = gates.T[:N_EXPERTS]


def _post(y_a, y_b, w_out, x, g1, norm_g, sc, sh, router_w, router_b, tm=512):
    b, s, d = x.shape
    tm = min(tm, s)
    nt = s // tm
    rw = jnp.zeros((d, LANES), F32).at[:, :N_EXPERTS].set(router_w)
    rb = jnp.full((1, LANES), -1e30, F32).at[0, :N_EXPERTS].set(router_b)
    vec = pl.BlockSpec((1, 1, d), lambda i, j: (i, 0, 0))
    row = lambda w: pl.BlockSpec((1, tm, w), lambda i, j: (i, j, 0))
    return pl.pallas_call(
        _post_kernel,
        grid=(b, nt),
        in_specs=[
            row(A_WIDTH), row(B_HEADS * DK),
            pl.BlockSpec((d, d), lambda i, j: (0, 0)),
            row(d), vec,
            pl.BlockSpec((1, d), lambda i, j: (0, 0)), vec, vec,
            pl.BlockSpec((d, LANES), lambda i, j: (0, 0)),
            pl.BlockSpec((1, LANES), lambda i, j: (0, 0)),
        ],
        out_specs=[row(d), row(d), pl.BlockSpec((N_EXPERTS, tm), lambda i, j: (0, i * nt + j))],
        out_shape=[
            jax.ShapeDtypeStruct((b, s, d), F32),
            jax.ShapeDtypeStruct((b, s, d), BF16),
            jax.ShapeDtypeStruct((N_EXPERTS, b * s), F32),
        ],
        compiler_params=_params(("parallel", "parallel")),
        name="post",
    )(y_a, y_b, w_out.astype(BF16), x, g1, norm_g.reshape(1, d), sc, sh, rw, rb)


def _moe_kernel(rows, h_ref, gt_ref, w1_ref, b1_ref, w2_ref, b2_ref, y_ref, pos_ref):
    e = pl.program_id(1)
    tile = h_ref.shape[0]
    sub = 256

    @pl.when(e == 0)
    def _():
        y_ref[...] = jnp.zeros(y_ref.shape, F32)
        r = lax.broadcasted_iota(jnp.int32, (sub, sub), 0)
        c = lax.broadcasted_iota(jnp.int32, (sub, sub), 1)
        before = jnp.where(r < c, 1.0, 0.0).astype(BF16)
        carry = jnp.zeros((N_EXPERTS, 1), F32)
        for k in range(tile // sub):
            routed = jnp.where(gt_ref[:, k * sub:(k + 1) * sub] > 0, 1.0, 0.0)
            pos_ref[:, k * sub:(k + 1) * sub] = carry + jnp.dot(routed.astype(BF16), before,
                                                                 preferred_element_type=F32)
            carry = carry + jnp.sum(routed, axis=-1, keepdims=True)

    gate_row = gt_ref[pl.ds(e, 1), :]
    routed_row = gate_row > 0
    pos_row = pos_ref[pl.ds(e, 1), :]
    count = jnp.sum(jnp.where(routed_row, 1, 0))
    tn = (((0,), (0,)), ((), ()))

    def body(c, carry):
        slot = lax.broadcasted_iota(jnp.int32, (rows, tile), 0).astype(F32) + (c * rows).astype(F32)
        hit = (pos_row == slot) & routed_row
        onehot = jnp.where(hit, 1.0, 0.0).astype(BF16)
        xe = jnp.dot(onehot, h_ref[...], preferred_element_type=F32).astype(BF16)
        hdn = jnp.dot(xe, w1_ref[0], preferred_element_type=F32) + b1_ref[0]
        glu = jnp.minimum(hdn[:, :D_EXPERT], SWIGLU_LIMIT)
        lin = jnp.clip(hdn[:, D_EXPERT:], -SWIGLU_LIMIT, SWIGLU_LIMIT)
        act = glu * jax.nn.sigmoid(SWIGLU_ALPHA * glu) * (lin + 1.0)
        out = jnp.dot(act.astype(BF16), w2_ref[0], preferred_element_type=F32) + b2_ref[0]
        row_gate = jnp.sum(jnp.where(hit, gate_row, 0.0), axis=-1, keepdims=True)
        scaled = (out * row_gate).astype(BF16)
        y_ref[...] += lax.dot_general(onehot, scaled, tn, preferred_element_type=F32)
        return carry

    lax.fori_loop(0, (count + rows - 1) // rows, body, 0)


def _moe(h2, gates_t, w1, b1, w2, b2, tile=2048, rows=128):
    t, d = h2.shape
    tile = min(tile, t)
    n_e, _, f2 = w1.shape
    return pl.pallas_call(
        functools.partial(_moe_kernel, rows),
        grid=(t // tile, n_e),
        in_specs=[
            pl.BlockSpec((tile, d), lambda i, e: (i, 0)),
            pl.BlockSpec((n_e, tile), lambda i, e: (0, i)),
            pl.BlockSpec((1, d, f2), lambda i, e: (e, 0, 0)),
            pl.BlockSpec((1, 1, f2), lambda i, e: (e, 0, 0)),
            pl.BlockSpec((1, f2 // 2, d), lambda i, e: (e, 0, 0)),
            pl.BlockSpec((1, 1, d), lambda i, e: (e, 0, 0)),
        ],
        out_specs=pl.BlockSpec((tile, d), lambda i, e: (i, 0)),
        out_shape=jax.ShapeDtypeStruct((t, d), F32),
        scratch_shapes=[pltpu.VMEM((n_e, tile), F32)],
        compiler_params=_params(("parallel", "arbitrary")),
        name="moe",
    )(h2, gates_t, w1.astype(BF16), b1.reshape(n_e, 1, f2), w2.astype(BF16), b2.reshape(n_e, 1, d))


def _final_kernel(x_ref, y_ref, g2_ref, fg_ref, o_ref):
    x = x_ref[0] + g2_ref[0] * y_ref[0]
    ms = jnp.mean(x * x, axis=-1, keepdims=True)
    o_ref[0] = x * lax.rsqrt(ms + NORM_EPS) * fg_ref[...]


def _final(x, y, g2, final_g, tm=512):
    b, s, d = x.shape
    tm = min(tm, s)
    row = pl.BlockSpec((1, tm, d), lambda i, j: (i, j, 0))
    return pl.pallas_call(
        _final_kernel,
        grid=(b, s // tm),
        in_specs=[row, row, pl.BlockSpec((1, 1, d), lambda i, j: (i, 0, 0)),
                  pl.BlockSpec((1, d), lambda i, j: (0, 0))],
        out_specs=row,
        out_shape=jax.ShapeDtypeStruct((b, s, d), F32),
        compiler_params=_params(("parallel", "parallel")),
        name="final",
    )(x, y, g2, final_g.reshape(1, d))


def _widen_w_in(w):
    gl0 = 2 * A_WIDTH + B_HEADS * DK + 6 * KV_GROUPS * DK
    per_group = HPG * 3
    gl_groups = [jnp.pad(w[:, gl0 + g * per_group:gl0 + (g + 1) * per_group], ((0, 0), (0, LANES - per_group)))
                 for g in range(KV_GROUPS)]
    wide = jnp.concatenate([w[:, :gl0]] + gl_groups, axis=1)
    assert gl0 == C_GL and wide.shape[1] == P_WIDE
    return wide.astype(BF16)


def _rope_tables(pos):
    half = DK // 2
    inv = ROPE_THETA ** (-jnp.arange(half, dtype=F32) / half)
    ang = pos.astype(F32)[..., None] * inv
    cos = jnp.cos(ang)
    sin = jnp.sin(ang)
    reps = LANES // DK
    return (jnp.concatenate([cos, cos] * reps, axis=-1),
            jnp.concatenate([-sin, sin] * reps, axis=-1))


def kernel(x, c, positions, ada_w, ada_b, norm1_g, norm2_g, w_in, w_out, sg_ln_g, sg_ln_b, sg_w, sg_b,
           cmp_pe_k, cmp_pe_v, cmp_w1_k, cmp_w2_k, cmp_w1_v, cmp_w2_v, router_w, router_b,
           exp_w1, exp_b1, exp_w2, exp_b2, final_g):
    b, s, d = x.shape
    n_layers = ada_w.shape[0]
    mod = _ada_mod(c, ada_w, ada_b)
    cos, sin = _rope_tables(positions)
    cmp_end = jnp.minimum(CMP_STRIDE * jnp.arange(s // CMP_STRIDE) + CMP_BLOCK - 1, s - 1)
    cos_c, sin_c = _rope_tables(positions[:, cmp_end])
    res = None
    for l in range(n_layers):
        sh1, sc1, g1, sh2, sc2, g2 = [m.reshape(b, 1, d) for m in jnp.split(mod[l], 6, axis=-1)]
        x, uv, qt, kc, vc, ks, vs_t, kw, vw_t, gl_t = _pre(x, res, norm1_g[l], sc1, sh1,
                                                          _widen_w_in(w_in[l]), cos, sin)
        y_a = _sgu(uv, sg_ln_g[l], sg_ln_b[l], sg_w[l], sg_b[l])
        kcmp, vcmp_t = _compress(kc, vc, cmp_pe_k[l], cmp_pe_v[l], cmp_w1_k[l], cmp_w2_k[l],
                                 cmp_w1_v[l], cmp_w2_v[l], cos_c, sin_c)
        oc_t, bias_t = _cmp_sel(qt, kcmp, vcmp_t)
        y_b = _attn(qt, ks, kw, vs_t, vw_t, bias_t, oc_t, gl_t)
        x, h2, gates_t = _post(y_a, y_b, w_out[l], x, g1, norm2_g[l], sc2, sh2, router_w[l], router_b[l])
        y = _moe(h2.reshape(b * s, d), gates_t, exp_w1[l], exp_b1[l], exp_w2[l], exp_b2[l])
        res = (y.reshape(b, s, d), g2)
    return _final(x, res[0], res[1], final_g)
```

```python
import functools

import numpy as np
import jax
import jax.numpy as jnp
from jax import lax
from jax.experimental import pallas as pl
from jax.experimental.pallas import tpu as pltpu

F32 = jnp.float32
BF16 = jnp.bfloat16
HIGHEST = lax.Precision.HIGHEST

D_MODEL = 1024
A_WIDTH = 512
A_HEADS = 8
CHUNK = 128
B_HEADS = 8
DK = 64
KV_GROUPS = 2
HPG = B_HEADS // KV_GROUPS
CMP_BLOCK = 32
CMP_STRIDE = 16
SEL_BLOCK = 64
SEL_TOP = 16
WINDOW = 512
ROPE_THETA = 10000.0
N_EXPERTS = 32
TOP_K = 4
D_EXPERT = 1024
SWIGLU_LIMIT = 7.0
SWIGLU_ALPHA = 1.702
NORM_EPS = 1e-6

LANES = 128
SUBLANES = 8
GW = HPG * DK
N_FORCED = 3
MASK_BIAS = -32768.0
Q_SCALE = DK ** -0.5 * 1.4426950408889634
VMEM_LIMIT = 56 * 1024 * 1024

C_U, C_V, C_Q = 0, 512, 1024
C_KC, C_VC, C_KS, C_VS, C_KW, C_VW = 1536, 1664, 1792, 1920, 2048, 2176
C_GL = 2304
P_WIDE = 2560


def _params(sem):
    return pltpu.CompilerParams(dimension_semantics=sem, vmem_limit_bytes=VMEM_LIMIT)


def _split_bf16(x, parts):
    out = []
    for _ in range(parts):
        piece = x.astype(BF16)
        out.append(piece)
        x = x - piece.astype(F32)
    return out


def _ada_kernel(c_ref, w_ref, b_ref, o_ref):
    c = c_ref[...]
    cond = c * jax.nn.sigmoid(c)
    o_ref[0] = jnp.dot(cond, w_ref[0], precision=HIGHEST, preferred_element_type=F32) + b_ref[0]


def _ada_mod(c, ada_w, ada_b):
    n_layers, d, d6 = ada_w.shape
    b = c.shape[0]
    rows = SUBLANES
    c_pad = jnp.zeros((rows, d), F32).at[:b].set(c)
    out = pl.pallas_call(
        _ada_kernel,
        grid=(n_layers, d6 // d),
        in_specs=[
            pl.BlockSpec((rows, d), lambda l, j: (0, 0)),
            pl.BlockSpec((1, d, d), lambda l, j: (l, 0, j)),
            pl.BlockSpec((1, 1, d), lambda l, j: (l, 0, j)),
        ],
        out_specs=pl.BlockSpec((1, rows, d), lambda l, j: (l, 0, j)),
        out_shape=jax.ShapeDtypeStruct((n_layers, rows, d6), F32),
        compiler_params=_params(("arbitrary", "arbitrary")),
        name="ada",
    )(c_pad, ada_w, ada_b.reshape(n_layers, 1, d6))
    return out[:, :b]


def _rope_slab(t, cos, sin_signed, lo):
    partner = jnp.where(lo, pltpu.roll(t, LANES - DK // 2, 1), pltpu.roll(t, DK // 2, 1))
    return t * cos + partner * sin_signed


def _rope(t, cos, sin_signed):
    lane = lax.broadcasted_iota(jnp.int32, (1, LANES), 1)
    lo = (lane % DK) < (DK // 2)
    slabs = [_rope_slab(t[:, s * LANES:(s + 1) * LANES], cos, sin_signed, lo)
             for s in range(t.shape[1] // LANES)]
    return slabs[0] if len(slabs) == 1 else jnp.concatenate(slabs, axis=1)


def _pre_kernel(has_res, *refs):
    if has_res:
        (x_ref, y_ref, g2_ref, ng_ref, sc_ref, sh_ref, w_ref, cos_ref, sin_ref,
         xo_ref, uv_ref, qt_ref, kc_ref, vc_ref, ks_ref, vst_ref, kw_ref, vwt_ref, glt_ref) = refs
        x = x_ref[0] + g2_ref[0] * y_ref[0]
        xo_ref[0] = x
    else:
        (x_ref, ng_ref, sc_ref, sh_ref, w_ref, cos_ref, sin_ref,
         uv_ref, qt_ref, kc_ref, vc_ref, ks_ref, vst_ref, kw_ref, vwt_ref, glt_ref) = refs
        x = x_ref[0]
    ms = jnp.mean(x * x, axis=-1, keepdims=True)
    h = x * lax.rsqrt(ms + NORM_EPS) * ng_ref[...]
    h = h * (1.0 + sc_ref[0]) + sh_ref[0]
    proj = jnp.dot(h.astype(BF16), w_ref[...], preferred_element_type=F32)
    cos = cos_ref[0]
    sin = sin_ref[0]
    uv_ref[0] = proj[:, C_U:C_Q]
    qt_ref[0] = (_rope(proj[:, C_Q:C_KC], cos, sin) * Q_SCALE).T
    kc_ref[0] = proj[:, C_KC:C_VC]
    vc_ref[0] = proj[:, C_VC:C_KS]
    ks_ref[0] = _rope(proj[:, C_KS:C_VS], cos, sin).astype(BF16)
    vst_ref[0] = proj[:, C_VS:C_KW].T.astype(BF16)
    kw_ref[0] = _rope(proj[:, C_KW:C_VW], cos, sin).astype(BF16)
    vwt_ref[0] = proj[:, C_VW:C_GL].T.astype(BF16)
    glt_ref[0] = proj[:, C_GL:P_WIDE].T


def _pre(x, res, norm_g, sc, sh, w_wide, cos, sin, tm=512):
    b, s, d = x.shape
    tm = min(tm, s)
    row = lambda w: pl.BlockSpec((1, tm, w), lambda i, j: (i, j, 0))
    col = lambda w: pl.BlockSpec((1, w, tm), lambda i, j: (i, 0, j))
    vec = pl.BlockSpec((1, 1, d), lambda i, j: (i, 0, 0))
    in_specs = [row(d)]
    args = [x]
    if res is not None:
        y, g2 = res
        in_specs += [row(d), vec]
        args += [y, g2]
    in_specs += [pl.BlockSpec((1, d), lambda i, j: (0, 0)), vec, vec,
                 pl.BlockSpec((d, P_WIDE), lambda i, j: (0, 0)), row(LANES), row(LANES)]
    args += [norm_g.reshape(1, d), sc, sh, w_wide, cos, sin]
    kv = KV_GROUPS * DK
    outs = [(d, F32, False)] if res is not None else []
    outs += [(2 * A_WIDTH, F32, False), (B_HEADS * DK, F32, True), (kv, F32, False), (kv, F32, False),
             (kv, BF16, False), (kv, BF16, True), (kv, BF16, False), (kv, BF16, True),
             (KV_GROUPS * LANES, F32, True)]
    res_out = pl.pallas_call(
        functools.partial(_pre_kernel, res is not None),
        grid=(b, s // tm),
        in_specs=in_specs,
        out_specs=[col(w) if t else row(w) for w, _, t in outs],
        out_shape=[jax.ShapeDtypeStruct((b, w, s) if t else (b, s, w), dt) for w, dt, t in outs],
        compiler_params=_params(("parallel", "parallel")),
        name="pre",
    )(*args)
    if res is None:
        res_out = [x] + list(res_out)
    return res_out


def _sgu_kernel(uv_ref, lng_ref, lnb_ref, w_ref, bias_ref, o_ref):
    rows = uv_ref.shape[1]
    uv = uv_ref[0]
    gu = jax.nn.gelu(uv[:, :A_WIDTH])
    gv = jax.nn.gelu(uv[:, A_WIDTH:])
    mu = jnp.mean(gv, axis=-1, keepdims=True)
    var = jnp.mean(jnp.square(gv - mu), axis=-1, keepdims=True)
    vn = ((gv - mu) * lax.rsqrt(var + NORM_EPS) * lng_ref[...] + lnb_ref[...]).astype(BF16)
    r = lax.broadcasted_iota(jnp.int32, (CHUNK, CHUNK), 0)
    c = lax.broadcasted_iota(jnp.int32, (CHUNK, CHUNK), 1)
    causal = c <= r
    lane_lo = lax.broadcasted_iota(jnp.int32, (CHUNK, LANES), 1) < DK
    for p in range(A_HEADS // 2):
        w0 = jnp.where(causal, w_ref[2 * p], 0.0).astype(BF16)
        w1 = jnp.where(causal, w_ref[2 * p + 1], 0.0).astype(BF16)
        bias = bias_ref[:, p * LANES:(p + 1) * LANES]
        for ch in range(rows // CHUNK):
            rs = slice(ch * CHUNK, (ch + 1) * CHUNK)
            cs = slice(p * LANES, (p + 1) * LANES)
            vp = vn[rs, cs]
            m0 = jnp.dot(w0, vp, preferred_element_type=F32)
            m1 = jnp.dot(w1, vp, preferred_element_type=F32)
            mixed = jnp.where(lane_lo, m0, m1) + bias
            o_ref[0, rs, cs] = (gu[rs, cs] * mixed).astype(o_ref.dtype)


def _sgu(uv, ln_g, ln_b, w_s, b_s, tm=512):
    b, s, _ = uv.shape
    tm = min(tm, s)
    bias = jnp.repeat(b_s.T, A_WIDTH // A_HEADS, axis=1)
    return pl.pallas_call(
        _sgu_kernel,
        grid=(b, s // tm),
        in_specs=[
            pl.BlockSpec((1, tm, 2 * A_WIDTH), lambda i, j: (i, j, 0)),
            pl.BlockSpec((1, A_WIDTH), lambda i, j: (0, 0)),
            pl.BlockSpec((1, A_WIDTH), lambda i, j: (0, 0)),
            pl.BlockSpec((A_HEADS, CHUNK, CHUNK), lambda i, j: (0, 0, 0)),
            pl.BlockSpec((CHUNK, A_WIDTH), lambda i, j: (0, 0)),
        ],
        out_specs=pl.BlockSpec((1, tm, A_WIDTH), lambda i, j: (i, j, 0)),
        out_shape=jax.ShapeDtypeStruct((b, s, A_WIDTH), BF16),
        compiler_params=_params(("parallel", "parallel")),
        name="sgu",
    )(uv, ln_g.reshape(1, -1), ln_b.reshape(1, -1), w_s, bias)


def _compress_kernel(kr_ref, vr_ref, pek_ref, pev_ref, w1k_ref, w1v_ref, w2k_ref, w2v_ref,
                     cos_ref, sin_ref, ko_ref, vt_ref):
    nc = kr_ref.shape[1]

    def mlp(r, pe_ref, w1_ref, w2_ref):
        top = jnp.dot(r + pe_ref[0:1], w1_ref[0], precision=HIGHEST, preferred_element_type=F32)
        bot = jnp.dot(r + pe_ref[1:2], w1_ref[1], precision=HIGHEST, preferred_element_type=F32)
        pre = top + pltpu.roll(bot, nc - 1, 0)
        return jnp.dot(jax.nn.gelu(pre), w2_ref[...], precision=HIGHEST, preferred_element_type=F32)

    kc = _rope(mlp(kr_ref[0], pek_ref, w1k_ref, w2k_ref), cos_ref[0], sin_ref[0])
    vc = mlp(vr_ref[0], pev_ref, w1v_ref, w2v_ref)
    lo = lax.broadcasted_iota(jnp.int32, (nc, LANES), 1) < DK
    rolled = pltpu.roll(kc, DK, 1)
    for g in range(KV_GROUPS):
        dup = jnp.where(lo, kc, rolled) if g == 0 else jnp.where(lo, rolled, kc)
        hi, low = _split_bf16(dup, 2)
        ko_ref[0, g, :, 0:LANES] = hi
        ko_ref[0, g, :, LANES:2 * LANES] = jnp.where(lo, low, jnp.zeros_like(low))
    vt_ref[0] = vc.T.astype(BF16)


def _compress_weights(pe, w1, w2):
    half = CMP_BLOCK // 2
    eye = jnp.eye(KV_GROUPS, dtype=F32)
    w1r = w1.reshape(CMP_BLOCK, DK, DK)
    wfull = jnp.einsum('lde,gh->lgdhe', w1r, eye)
    w1s = wfull.reshape(2, half * KV_GROUPS * DK, KV_GROUPS * DK)
    pes = jnp.broadcast_to(pe.reshape(2, half, 1, DK), (2, half, KV_GROUPS, DK)).reshape(2, -1)
    w2bd = jnp.einsum('de,gh->gdhe', w2, eye).reshape(KV_GROUPS * DK, KV_GROUPS * DK)
    return pes, w1s, w2bd


def _compress(kc, vc, pe_k, pe_v, w1_k, w2_k, w1_v, w2_v, cos_c, sin_c):
    b, s, _ = kc.shape
    nc = s // CMP_STRIDE
    rw = CMP_STRIDE * LANES
    pek, w1ks, w2kb = _compress_weights(pe_k, w1_k, w2_k)
    pev, w1vs, w2vb = _compress_weights(pe_v, w1_v, w2_v)
    full = lambda shape: pl.BlockSpec(shape, lambda i: (0,) * len(shape))
    return pl.pallas_call(
        _compress_kernel,
        grid=(b,),
        in_specs=[
            pl.BlockSpec((1, nc, rw), lambda i: (i, 0, 0)),
            pl.BlockSpec((1, nc, rw), lambda i: (i, 0, 0)),
            full((2, rw)), full((2, rw)),
            full((2, rw, LANES)), full((2, rw, LANES)),
            full((LANES, LANES)), full((LANES, LANES)),
            pl.BlockSpec((1, nc, LANES), lambda i: (i, 0, 0)),
            pl.BlockSpec((1, nc, LANES), lambda i: (i, 0, 0)),
        ],
        out_specs=[pl.BlockSpec((1, KV_GROUPS, nc, 2 * LANES), lambda i: (i, 0, 0, 0)),
                   pl.BlockSpec((1, KV_GROUPS * DK, nc), lambda i: (i, 0, 0))],
        out_shape=[jax.ShapeDtypeStruct((b, KV_GROUPS, nc, 2 * LANES), BF16),
                   jax.ShapeDtypeStruct((b, KV_GROUPS * DK, nc), BF16)],
        compiler_params=_params(("parallel",)),
        name="compress",
    )(kc.reshape(b, nc, rw), vc.reshape(b, nc, rw), pek, pev, w1ks, w1vs, w2kb, w2vb, cos_c, sin_c)


def _heads_on_lanes(qt):
    return jnp.concatenate([qt[h * DK:(h + 1) * DK] for h in range(HPG)], axis=1)


def _cmp_sel_kernel(n_top, qt_ref, k_ref, vt_ref, map_ref, oc_ref, bias_ref):
    tq = qt_ref.shape[2]
    cols = HPG * tq
    nc = k_ref.shape[2]
    nsel = map_ref.shape[0]
    q0 = pl.program_id(2) * tq
    q_hi, q_lo = _split_bf16(_heads_on_lanes(qt_ref[0]), 2)
    q3t = jnp.concatenate([q_hi, q_lo, q_hi, jnp.zeros_like(q_hi)], axis=0)
    s = jnp.dot(k_ref[0, 0], q3t, preferred_element_type=F32)
    n_idx = lax.broadcasted_iota(jnp.int32, (nc, 1), 0)
    t_col = q0 + lax.broadcasted_iota(jnp.int32, (1, cols), 1) % tq
    s = jnp.where((CMP_STRIDE * n_idx + CMP_BLOCK - 1) <= t_col, s, -jnp.inf)
    m = jnp.max(s, axis=0, keepdims=True)
    m = jnp.where(m == -jnp.inf, 0.0, m)
    e = jnp.exp2(s - m)
    d = jnp.sum(e, axis=0, keepdims=True)
    p = e / jnp.where(d > 0, d, 1.0)
    oc = jnp.dot(vt_ref[0], p.astype(BF16), preferred_element_type=F32)
    oc_ref[0] = jnp.concatenate([oc[:, h * tq:(h + 1) * tq] for h in range(HPG)], axis=0)
    psum = p[:, 0:tq]
    for h in range(1, HPG):
        psum = psum + p[:, h * tq:(h + 1) * tq]
    imp = jnp.dot(map_ref[...], jnp.concatenate(_split_bf16(psum, 3), axis=0),
                  preferred_element_type=F32)
    j = lax.broadcasted_iota(jnp.int32, (nsel, tq), 0)
    cur = (q0 + lax.broadcasted_iota(jnp.int32, (nsel, tq), 1)) // SEL_BLOCK
    valid = j <= cur
    forced = (j == 0) | (j == cur) | (j == cur - 1)
    keep = forced | (valid & (cur < n_top))
    vals = jnp.where(valid & jnp.logical_not(forced), imp, -jnp.inf)
    bias = jnp.where(keep, 0.0, MASK_BIAS)
    for _ in range(n_top - N_FORCED):
        mx = jnp.max(vals, axis=0, keepdims=True)
        first = jnp.min(jnp.where(vals == mx, j, nsel), axis=0, keepdims=True)
        pick = (j == first) & (mx > -jnp.inf)
        bias = jnp.where(pick, 0.0, bias)
        vals = jnp.where(pick, -jnp.inf, vals)
    bias_ref[0, 0] = bias.astype(BF16)


def _sel_map_t(s):
    nc = s // CMP_STRIDE
    n_cmp = (s - CMP_BLOCK) // CMP_STRIDE + 1
    n_sel = s // SEL_BLOCK
    cs = CMP_STRIDE * np.arange(n_cmp)[:, None]
    ce = cs + CMP_BLOCK
    ss = SEL_BLOCK * np.arange(n_sel)[None, :]
    se = ss + SEL_BLOCK
    ov = np.clip(np.minimum(ce, se) - np.maximum(cs, ss), 0, None) / CMP_STRIDE
    out = np.zeros((n_sel, nc), np.float32)
    out[:, :n_cmp] = ov.T
    return out


def _cmp_sel(qt, kcmp, vcmp_t, tq=128):
    b, _, s = qt.shape
    nc = s // CMP_STRIDE
    nsel = s // SEL_BLOCK
    n_top = min(SEL_TOP, nsel)
    sel_map = jnp.asarray(np.tile(_sel_map_t(s), (1, 3)), dtype=BF16)
    return pl.pallas_call(
        functools.partial(_cmp_sel_kernel, n_top),
        grid=(b, KV_GROUPS, s // tq),
        in_specs=[
            pl.BlockSpec((1, GW, tq), lambda i, g, j: (i, g, j)),
            pl.BlockSpec((1, 1, nc, 2 * LANES), lambda i, g, j: (i, g, 0, 0)),
            pl.BlockSpec((1, DK, nc), lambda i, g, j: (i, g, 0)),
            pl.BlockSpec((nsel, 3 * nc), lambda i, g, j: (0, 0)),
        ],
        out_specs=[
            pl.BlockSpec((1, GW, tq), lambda i, g, j: (i, g, j)),
            pl.BlockSpec((1, 1, nsel, tq), lambda i, g, j: (i, g, 0, j)),
        ],
        out_shape=[
            jax.ShapeDtypeStruct((b, B_HEADS * DK, s), F32),
            jax.ShapeDtypeStruct((b, KV_GROUPS, nsel, s), BF16),
        ],
        compiler_params=_params(("parallel", "parallel", "parallel")),
        name="cmp_sel",
    )(qt, kcmp, vcmp_t, sel_map)


def _attn_kernel(kt, qt_ref, ks_ref, kw_ref, vst_ref, vwt_ref, bias_ref, oh_ref, oc_ref, glt_ref,
                 o_ref, qa_ref, s0_ref, s1_ref, m_ref, l_ref, acc_ref):
    tq = qt_ref.shape[2]
    cols = HPG * tq
    nsel = bias_ref.shape[2]
    g = pl.program_id(1)
    q0 = pl.program_id(2) * tq

    q4t = _heads_on_lanes(qt_ref[0]).astype(BF16)
    zero = jnp.zeros_like(q4t)
    qa_ref[0:DK] = jnp.where(g == 0, q4t, zero)
    qa_ref[DK:2 * DK] = jnp.where(g == 1, q4t, zero)
    qa_ref[LANES:LANES + nsel] = jnp.concatenate([bias_ref[0, 0]] * HPG, axis=1)
    qaug = qa_ref[...]
    t_col = q0 + lax.broadcasted_iota(jnp.int32, (1, cols), 1) % tq

    def fold(x, op):
        return op(x.reshape(x.shape[0] // SUBLANES, SUBLANES, cols), axis=0)

    def stage(c, s_ref):
        k0 = pl.multiple_of(c * kt, kt)
        kaug = jnp.concatenate([ks_ref[0, pl.ds(k0, kt), :], oh_ref[pl.ds(k0, kt), :]], axis=1)
        s_ref[...] = jnp.dot(kaug, qaug, preferred_element_type=F32)
        r0 = pl.multiple_of(jnp.clip(q0 - k0, 0, kt - tq), tq)
        kpos = k0 + r0 + lax.broadcasted_iota(jnp.int32, (tq, 1), 0)
        s_ref[pl.ds(r0, tq), :] = jnp.where(kpos <= t_col, s_ref[pl.ds(r0, tq), :], -jnp.inf)

    def consume(c, s_ref):
        k0 = pl.multiple_of(c * kt, kt)
        sc = s_ref[...]
        m_old = m_ref[...]
        m_new = jnp.maximum(m_old, jnp.max(fold(sc, jnp.max), axis=0, keepdims=True))
        alpha = jnp.exp2(m_old - m_new)
        pr = jnp.exp2(sc - m_new)
        l_ref[...] = alpha * l_ref[...] + fold(pr, jnp.sum)
        acc_ref[...] = alpha * acc_ref[...] + jnp.dot(vst_ref[0, :, pl.ds(k0, kt)], pr.astype(BF16),
                                                      preferred_element_type=F32)
        m_ref[...] = m_new

    m_ref[...] = jnp.full(m_ref.shape, -jnp.inf, F32)
    l_ref[...] = jnp.zeros(l_ref.shape, F32)
    acc_ref[...] = jnp.zeros(acc_ref.shape, F32)
    n_full = q0 // kt
    n_pairs = n_full // 2
    odd = n_full % 2 == 1
    stage(0, s0_ref)

    def pair(i, carry):
        stage(2 * i + 1, s1_ref)
        consume(2 * i, s0_ref)
        stage(2 * i + 2, s0_ref)
        consume(2 * i + 1, s1_ref)
        return carry

    lax.fori_loop(0, n_pairs, pair, 0)

    @pl.when(odd)
    def _():
        stage(n_full, s1_ref)

    consume(2 * n_pairs, s0_ref)

    @pl.when(odd)
    def _():
        consume(n_full, s1_ref)

    o_sel = acc_ref[...] / jnp.sum(l_ref[...], axis=0, keepdims=True)

    wlen = WINDOW + tq
    w0 = pl.multiple_of(jnp.maximum(q0 - WINDOW, 0), tq)
    sw = jnp.dot(kw_ref[0, pl.ds(w0, wlen), :], qaug[0:LANES], preferred_element_type=F32)
    dlt = t_col - (w0 + lax.broadcasted_iota(jnp.int32, (wlen, 1), 0))
    sw = jnp.where((dlt >= 0) & (dlt < WINDOW), sw, -jnp.inf)
    pw = jnp.exp2(sw - jnp.max(sw, axis=0, keepdims=True))
    o_win = jnp.dot(vwt_ref[0, :, pl.ds(w0, wlen)], pw.astype(BF16), preferred_element_type=F32)
    o_win = o_win / jnp.sum(pw, axis=0, keepdims=True)

    gates = jax.nn.sigmoid(glt_ref[0])
    oc = _heads_on_lanes(oc_ref[0])
    for p in range(HPG // 2):
        halves = []
        for hh in (2 * p, 2 * p + 1):
            cs = slice(hh * tq, (hh + 1) * tq)
            halves.append(gates[3 * hh:3 * hh + 1] * oc[:, cs] + gates[3 * hh + 1:3 * hh + 2] * o_sel[:, cs]
                          + gates[3 * hh + 2:3 * hh + 3] * o_win[:, cs])
        o_ref[0, :, p * LANES:(p + 1) * LANES] = jnp.concatenate(halves, axis=0).T.astype(o_ref.dtype)


def _attn(qt, ks, kw, vs_t, vw_t, bias_t, oc_t, gl_t, tq=256, kt=512):
    b, _, s = qt.shape
    nsel = s // SEL_BLOCK
    kt = min(kt, s)
    onehot = jnp.asarray((np.arange(s)[:, None] // SEL_BLOCK == np.arange(nsel)[None, :]), dtype=BF16)
    k_spec = pl.BlockSpec((1, s, KV_GROUPS * DK), lambda i, g, j: (i, 0, 0))
    vt_spec = pl.BlockSpec((1, DK, s), lambda i, g, j: (i, g, 0))
    cols = HPG * tq
    return pl.pallas_call(
        functools.partial(_attn_kernel, kt),
        grid=(b, KV_GROUPS, s // tq),
        in_specs=[
            pl.BlockSpec((1, GW, tq), lambda i, g, j: (i, g, j)),
            k_spec, k_spec, vt_spec, vt_spec,
            pl.BlockSpec((1, 1, nsel, tq), lambda i, g, j: (i, g, 0, j)),
            pl.BlockSpec((s, nsel), lambda i, g, j: (0, 0)),
            pl.BlockSpec((1, GW, tq), lambda i, g, j: (i, g, j)),
            pl.BlockSpec((1, LANES, tq), lambda i, g, j: (i, g, j)),
        ],
        out_specs=pl.BlockSpec((1, tq, GW), lambda i, g, j: (i, j, g)),
        out_shape=jax.ShapeDtypeStruct((b, s, B_HEADS * DK), BF16),
        scratch_shapes=[pltpu.VMEM((LANES + nsel, cols), BF16),
                        pltpu.VMEM((kt, cols), F32), pltpu.VMEM((kt, cols), F32),
                        pltpu.VMEM((1, cols), F32), pltpu.VMEM((SUBLANES, cols), F32),
                        pltpu.VMEM((DK, cols), F32)],
        compiler_params=_params(("parallel", "parallel", "arbitrary")),
        name="attn",
    )(qt, ks, kw, vs_t, vw_t, bias_t, onehot, oc_t, gl_t)


def _post_kernel(ya_ref, yb_ref, wo_ref, x_ref, g1_ref, ng_ref, sc_ref, sh_ref, rw_ref, rb_ref,
                 xo_ref, h_ref, gt_ref):
    mixed = jnp.dot(ya_ref[0], wo_ref[:A_WIDTH], preferred_element_type=F32)
    mixed = mixed + jnp.dot(yb_ref[0], wo_ref[A_WIDTH:], preferred_element_type=F32)
    x = x_ref[0] + g1_ref[0] * mixed
    xo_ref[0] = x
    ms = jnp.mean(x * x, axis=-1, keepdims=True)
    h = x * lax.rsqrt(ms + NORM_EPS) * ng_ref[...]
    h = h * (1.0 + sc_ref[0]) + sh_ref[0]
    h_ref[0] = h.astype(BF16)
    logits = jnp.dot(h, rw_ref[...], precision=HIGHEST, preferred_element_type=F32) + rb_ref[...]
    lane = lax.broadcasted_iota(jnp.int32, logits.shape, 1)
    vals = logits
    top = jnp.max(vals, axis=-1, keepdims=True)
    expv = jnp.zeros_like(logits)
    for _ in range(TOP_K):
        m = jnp.max(vals, axis=-1, keepdims=True)
        first = jnp.min(jnp.where(vals == m, lane, LANES), axis=-1, keepdims=True)
        pick = lane == first
        expv = jnp.where(pick, jnp.exp(m - top), expv)
        vals = jnp.where(pick, -jnp.inf, vals)
    gates = expv / jnp.sum(expv, axis=-1, keepdims=True)
    gt_ref[...] = gates.T[:N_EXPERTS]


def _post(y_a, y_b, w_out, x, g1, norm_g, sc, sh, router_w, router_b, tm=512):
    b, s, d = x.shape
    tm = min(tm, s)
    nt = s // tm
    rw = jnp.zeros((d, LANES), F32).at[:, :N_EXPERTS].set(router_w)
    rb = jnp.full((1, LANES), -1e30, F32).at[0, :N_EXPERTS].set(router_b)
    vec = pl.BlockSpec((1, 1, d), lambda i, j: (i, 0, 0))
    row = lambda w: pl.BlockSpec((1, tm, w), lambda i, j: (i, j, 0))
    return pl.pallas_call(
        _post_kernel,
        grid=(b, nt),
        in_specs=[
            row(A_WIDTH), row(B_HEADS * DK),
            pl.BlockSpec((d, d), lambda i, j: (0, 0)),
            row(d), vec,
            pl.BlockSpec((1, d), lambda i, j: (0, 0)), vec, vec,
            pl.BlockSpec((d, LANES), lambda i, j: (0, 0)),
            pl.BlockSpec((1, LANES), lambda i, j: (0, 0)),
        ],
        out_specs=[row(d), row(d), pl.BlockSpec((N_EXPERTS, tm), lambda i, j: (0, i * nt + j))],
        out_shape=[
            jax.ShapeDtypeStruct((b, s, d), F32),
            jax.ShapeDtypeStruct((b, s, d), BF16),
            jax.ShapeDtypeStruct((N_EXPERTS, b * s), F32),
        ],
        compiler_params=_params(("parallel", "parallel")),
        name="post",
    )(y_a, y_b, w_out.astype(BF16), x, g1, norm_g.reshape(1, d), sc, sh, rw, rb)


def _moe_kernel(rows, h_ref, gt_ref, w1_ref, b1_ref, w2_ref, b2_ref, y_ref, pos_ref):
    e = pl.program_id(1)
    tile = h_ref.shape[0]
    sub = 256

    @pl.when(e == 0)
    def _():
        y_ref[...] = jnp.zeros(y_ref.shape, F32)
        r = lax.broadcasted_iota(jnp.int32, (sub, sub), 0)
        c = lax.broadcasted_iota(jnp.int32, (sub, sub), 1)
        before = jnp.where(r < c, 1.0, 0.0).astype(BF16)
        carry = jnp.zeros((N_EXPERTS, 1), F32)
        for k in range(tile // sub):
            routed = jnp.where(gt_ref[:, k * sub:(k + 1) * sub] > 0, 1.0, 0.0)
            pos_ref[:, k * sub:(k + 1) * sub] = carry + jnp.dot(routed.astype(BF16), before,
                                                                 preferred_element_type=F32)
            carry = carry + jnp.sum(routed, axis=-1, keepdims=True)

    gate_row = gt_ref[pl.ds(e, 1), :]
    routed_row = gate_row > 0
    pos_row = pos_ref[pl.ds(e, 1), :]
    count = jnp.sum(jnp.where(routed_row, 1, 0))
    tn = (((0,), (0,)), ((), ()))

    def chunk(start, rows):
        slot = lax.broadcasted_iota(jnp.int32, (rows, tile), 0).astype(F32) + start.astype(F32)
        hit = (pos_row == slot) & routed_row
        onehot = jnp.where(hit, 1.0, 0.0).astype(BF16)
        xe = jnp.dot(onehot, h_ref[...], preferred_element_type=F32).astype(BF16)
        hdn = jnp.dot(xe, w1_ref[0], preferred_element_type=F32) + b1_ref[0]
        glu = jnp.minimum(hdn[:, :D_EXPERT], SWIGLU_LIMIT)
        lin = jnp.clip(hdn[:, D_EXPERT:], -SWIGLU_LIMIT, SWIGLU_LIMIT)
        act = glu * jax.nn.sigmoid(SWIGLU_ALPHA * glu) * (lin + 1.0)
        out = jnp.dot(act.astype(BF16), w2_ref[0], preferred_element_type=F32) + b2_ref[0]
        row_gate = jnp.sum(jnp.where(hit, gate_row, 0.0), axis=-1, keepdims=True)
        scaled = (out * row_gate).astype(BF16)
        y_ref[...] += lax.dot_general(onehot, scaled, tn, preferred_element_type=F32)

    n_big = count // rows
    rem = count - n_big * rows

    def body(c, carry):
        chunk(c * rows, rows)
        return carry

    lax.fori_loop(0, n_big, body, 0)

    @pl.when(rem > rows // 2)
    def _():
        chunk(n_big * rows, rows)

    @pl.when((rem > 0) & (rem <= rows // 2))
    def _():
        chunk(n_big * rows, rows // 2)


def _moe(h2, gates_t, w1, b1, w2, b2, tile=2048, rows=256):
    t, d = h2.shape
    tile = min(tile, t)
    n_e, _, f2 = w1.shape
    return pl.pallas_call(
        functools.partial(_moe_kernel, rows),
        grid=(t // tile, n_e),
        in_specs=[
            pl.BlockSpec((tile, d), lambda i, e: (i, 0)),
            pl.BlockSpec((n_e, tile), lambda i, e: (0, i)),
            pl.BlockSpec((1, d, f2), lambda i, e: (e, 0, 0)),
            pl.BlockSpec((1, 1, f2), lambda i, e: (e, 0, 0)),
            pl.BlockSpec((1, f2 // 2, d), lambda i, e: (e, 0, 0)),
            pl.BlockSpec((1, 1, d), lambda i, e: (e, 0, 0)),
        ],
        out_specs=pl.BlockSpec((tile, d), lambda i, e: (i, 0)),
        out_shape=jax.ShapeDtypeStruct((t, d), F32),
        scratch_shapes=[pltpu.VMEM((n_e, tile), F32)],
        compiler_params=_params(("parallel", "arbitrary")),
        name="moe",
    )(h2, gates_t, w1.astype(BF16), b1.reshape(n_e, 1, f2), w2.astype(BF16), b2.reshape(n_e, 1, d))


def _final_kernel(x_ref, y_ref, g2_ref, fg_ref, o_ref):
    x = x_ref[0] + g2_ref[0] * y_ref[0]
    ms = jnp.mean(x * x, axis=-1, keepdims=True)
    o_ref[0] = x * lax.rsqrt(ms + NORM_EPS) * fg_ref[...]


def _final(x, y, g2, final_g, tm=512):
    b, s, d = x.shape
    tm = min(tm, s)
    row = pl.BlockSpec((1, tm, d), lambda i, j: (i, j, 0))
    return pl.pallas_call(
        _final_kernel,
        grid=(b, s // tm),
        in_specs=[row, row, pl.BlockSpec((1, 1, d), lambda i, j: (i, 0, 0)),
                  pl.BlockSpec((1, d), lambda i, j: (0, 0))],
        out_specs=row,
        out_shape=jax.ShapeDtypeStruct((b, s, d), F32),
        compiler_params=_params(("parallel", "parallel")),
        name="final",
    )(x, y, g2, final_g.reshape(1, d))


def _widen_w_in(w):
    gl0 = 2 * A_WIDTH + B_HEADS * DK + 6 * KV_GROUPS * DK
    per_group = HPG * 3
    gl_groups = [jnp.pad(w[:, gl0 + g * per_group:gl0 + (g + 1) * per_group], ((0, 0), (0, LANES - per_group)))
                 for g in range(KV_GROUPS)]
    wide = jnp.concatenate([w[:, :gl0]] + gl_groups, axis=1)
    assert gl0 == C_GL and wide.shape[1] == P_WIDE
    return wide.astype(BF16)


def _rope_tables(pos):
    half = DK // 2
    inv = ROPE_THETA ** (-jnp.arange(half, dtype=F32) / half)
    ang = pos.astype(F32)[..., None] * inv
    cos = jnp.cos(ang)
    sin = jnp.sin(ang)
    reps = LANES // DK
    return (jnp.concatenate([cos, cos] * reps, axis=-1),
            jnp.concatenate([-sin, sin] * reps, axis=-1))


def kernel(x, c, positions, ada_w, ada_b, norm1_g, norm2_g, w_in, w_out, sg_ln_g, sg_ln_b, sg_w, sg_b,
           cmp_pe_k, cmp_pe_v, cmp_w1_k, cmp_w2_k, cmp_w1_v, cmp_w2_v, router_w, router_b,
           exp_w1, exp_b1, exp_w2, exp_b2, final_g):
    b, s, d = x.shape
    n_layers = ada_w.shape[0]
    mod = _ada_mod(c, ada_w, ada_b)
    cos, sin = _rope_tables(positions)
    cmp_end = jnp.minimum(CMP_STRIDE * jnp.arange(s // CMP_STRIDE) + CMP_BLOCK - 1, s - 1)
    cos_c, sin_c = _rope_tables(positions[:, cmp_end])
    res = None
    for l in range(n_layers):
        sh1, sc1, g1, sh2, sc2, g2 = [m.reshape(b, 1, d) for m in jnp.split(mod[l], 6, axis=-1)]
        x, uv, qt, kc, vc, ks, vs_t, kw, vw_t, gl_t = _pre(x, res, norm1_g[l], sc1, sh1,
                                                          _widen_w_in(w_in[l]), cos, sin)
        y_a = _sgu(uv, sg_ln_g[l], sg_ln_b[l], sg_w[l], sg_b[l])
        kcmp, vcmp_t = _compress(kc, vc, cmp_pe_k[l], cmp_pe_v[l], cmp_w1_k[l], cmp_w2_k[l],
                                 cmp_w1_v[l], cmp_w2_v[l], cos_c, sin_c)
        oc_t, bias_t = _cmp_sel(qt, kcmp, vcmp_t)
        y_b = _attn(qt, ks, kw, vs_t, vw_t, bias_t, oc_t, gl_t)
        x, h2, gates_t = _post(y_a, y_b, w_out[l], x, g1, norm2_g[l], sc2, sh2, router_w[l], router_b[l])
        y = _moe(h2.reshape(b * s, d), gates_t, exp_w1[l], exp_b1[l], exp_w2[l], exp_b2[l])
        res = (y.reshape(b, s, d), g2)
    return _final(x, res[0], res[1], final_g)
```

```python
import functools

import numpy as np
import jax
import jax.numpy as jnp
from jax import lax
from jax.experimental import pallas as pl
from jax.experimental.pallas import tpu as pltpu
from jax.experimental.pallas import tpu_sc as plsc

F32 = jnp.float32
BF16 = jnp.bfloat16
HIGHEST = lax.Precision.HIGHEST

D_MODEL = 1024
A_WIDTH = 512
A_HEADS = 8
CHUNK = 128
B_HEADS = 8
DK = 64
KV_GROUPS = 2
HPG = B_HEADS // KV_GROUPS
CMP_BLOCK = 32
CMP_STRIDE = 16
SEL_BLOCK = 64
SEL_TOP = 16
WINDOW = 512
ROPE_THETA = 10000.0
N_EXPERTS = 32
TOP_K = 4
D_EXPERT = 1024
SWIGLU_LIMIT = 7.0
SWIGLU_ALPHA = 1.702
NORM_EPS = 1e-6

LANES = 128
SUBLANES = 8
GW = HPG * DK
N_FORCED = 3
MASK_BIAS = -32768.0
Q_SCALE = DK ** -0.5 * 1.4426950408889634
VMEM_LIMIT = 56 * 1024 * 1024
ROW_BLOCK = 256
SC_CORES = 2
SC_SUBCORES = 16
SC_WORKERS = SC_CORES * SC_SUBCORES
SC_WINDOW = 128

C_U, C_V, C_Q = 0, 512, 1024
C_KC, C_VC, C_KS, C_VS, C_KW, C_VW = 1536, 1664, 1792, 1920, 2048, 2176
C_GL = 2304
P_WIDE = 2560


def _params(sem):
    return pltpu.CompilerParams(dimension_semantics=sem, vmem_limit_bytes=VMEM_LIMIT)


def _split_bf16(x, parts):
    out = []
    for _ in range(parts):
        piece = x.astype(BF16)
        out.append(piece)
        x = x - piece.astype(F32)
    return out


def _ada_kernel(c_ref, w_ref, b_ref, o_ref):
    c = c_ref[...]
    cond = c * jax.nn.sigmoid(c)
    o_ref[0] = jnp.dot(cond, w_ref[0], precision=HIGHEST, preferred_element_type=F32) + b_ref[0]


def _ada_mod(c, ada_w, ada_b):
    n_layers, d, d6 = ada_w.shape
    b = c.shape[0]
    rows = SUBLANES
    c_pad = jnp.zeros((rows, d), F32).at[:b].set(c)
    out = pl.pallas_call(
        _ada_kernel,
        grid=(n_layers, d6 // d),
        in_specs=[
            pl.BlockSpec((rows, d), lambda l, j: (0, 0)),
            pl.BlockSpec((1, d, d), lambda l, j: (l, 0, j)),
            pl.BlockSpec((1, 1, d), lambda l, j: (l, 0, j)),
        ],
        out_specs=pl.BlockSpec((1, rows, d), lambda l, j: (l, 0, j)),
        out_shape=jax.ShapeDtypeStruct((n_layers, rows, d6), F32),
        compiler_params=_params(("arbitrary", "arbitrary")),
        name="ada",
    )(c_pad, ada_w, ada_b.reshape(n_layers, 1, d6))
    return out[:, :b]


def _rope_slab(t, cos, sin_signed, lo):
    partner = jnp.where(lo, pltpu.roll(t, LANES - DK // 2, 1), pltpu.roll(t, DK // 2, 1))
    return t * cos + partner * sin_signed


def _rope(t, cos, sin_signed):
    lane = lax.broadcasted_iota(jnp.int32, (1, LANES), 1)
    lo = (lane % DK) < (DK // 2)
    slabs = [_rope_slab(t[:, s * LANES:(s + 1) * LANES], cos, sin_signed, lo)
             for s in range(t.shape[1] // LANES)]
    return slabs[0] if len(slabs) == 1 else jnp.concatenate(slabs, axis=1)


def _pack_halves(x):
    n = x.shape[1] // 2
    lo = lax.bitcast_convert_type(x[:, :n].astype(BF16).astype(F32), jnp.int32)
    hi = lax.bitcast_convert_type(x[:, n:].astype(BF16).astype(F32), jnp.int32)
    return lax.shift_right_logical(lo, jnp.int32(16)) | (hi & jnp.int32(-65536))


def _unpack_halves(p):
    lo = lax.bitcast_convert_type(lax.shift_left(p, jnp.int32(16)), F32)
    hi = lax.bitcast_convert_type(p & jnp.int32(-65536), F32)
    return jnp.concatenate([lo, hi], axis=1)


def _moe_residual(x_ref, rows_ref, gate_ref, g2_ref):
    gate = gate_ref[0]
    y = gate[:, 0:1] * _unpack_halves(rows_ref[0, 0])
    for k in range(1, TOP_K):
        y = y + gate[:, k:k + 1] * _unpack_halves(rows_ref[k, 0])
    return x_ref[0] + g2_ref[0] * y


def _moe_residual_specs(tm, d):
    return [pl.BlockSpec((TOP_K, 1, tm, d // 2), lambda i, j: (0, i, j, 0)),
            pl.BlockSpec((1, tm, TOP_K), lambda i, j: (i, j, 0)),
            pl.BlockSpec((1, 1, d), lambda i, j: (i, 0, 0))]


def _pre_kernel(has_res, *refs):
    if has_res:
        (x_ref, rows_ref, gate_ref, g2_ref, ng_ref, sc_ref, sh_ref, w_ref, cos_ref, sin_ref,
         xo_ref, uv_ref, qt_ref, kc_ref, vc_ref, ks_ref, vst_ref, kw_ref, vwt_ref, glt_ref) = refs
        x = _moe_residual(x_ref, rows_ref, gate_ref, g2_ref)
        xo_ref[0] = x
    else:
        (x_ref, ng_ref, sc_ref, sh_ref, w_ref, cos_ref, sin_ref,
         uv_ref, qt_ref, kc_ref, vc_ref, ks_ref, vst_ref, kw_ref, vwt_ref, glt_ref) = refs
        x = x_ref[0]
    ms = jnp.mean(x * x, axis=-1, keepdims=True)
    h = x * lax.rsqrt(ms + NORM_EPS) * ng_ref[...]
    h = h * (1.0 + sc_ref[0]) + sh_ref[0]
    proj = jnp.dot(h.astype(BF16), w_ref[...], preferred_element_type=F32)
    cos = cos_ref[0]
    sin = sin_ref[0]
    uv_ref[0] = proj[:, C_U:C_Q]
    qt_ref[0] = (_rope(proj[:, C_Q:C_KC], cos, sin) * Q_SCALE).T
    kc_ref[0] = proj[:, C_KC:C_VC]
    vc_ref[0] = proj[:, C_VC:C_KS]
    ks_ref[0] = _rope(proj[:, C_KS:C_VS], cos, sin).astype(BF16)
    vst_ref[0] = proj[:, C_VS:C_KW].T.astype(BF16)
    kw_ref[0] = _rope(proj[:, C_KW:C_VW], cos, sin).astype(BF16)
    vwt_ref[0] = proj[:, C_VW:C_GL].T.astype(BF16)
    glt_ref[0] = proj[:, C_GL:P_WIDE].T


def _pre(x, res, norm_g, sc, sh, w_wide, cos, sin, tm=512):
    b, s, d = x.shape
    tm = min(tm, s)
    row = lambda w: pl.BlockSpec((1, tm, w), lambda i, j: (i, j, 0))
    col = lambda w: pl.BlockSpec((1, w, tm), lambda i, j: (i, 0, j))
    vec = pl.BlockSpec((1, 1, d), lambda i, j: (i, 0, 0))
    in_specs = [row(d)]
    args = [x]
    if res is not None:
        in_specs += _moe_residual_specs(tm, d)
        args += list(res)
    in_specs += [pl.BlockSpec((1, d), lambda i, j: (0, 0)), vec, vec,
                 pl.BlockSpec((d, P_WIDE), lambda i, j: (0, 0)), row(LANES), row(LANES)]
    args += [norm_g.reshape(1, d), sc, sh, w_wide, cos, sin]
    kv = KV_GROUPS * DK
    outs = [(d, F32, False)] if res is not None else []
    outs += [(2 * A_WIDTH, F32, False), (B_HEADS * DK, F32, True), (kv, F32, False), (kv, F32, False),
             (kv, BF16, False), (kv, BF16, True), (kv, BF16, False), (kv, BF16, True),
             (KV_GROUPS * LANES, F32, True)]
    res_out = pl.pallas_call(
        functools.partial(_pre_kernel, res is not None),
        grid=(b, s // tm),
        in_specs=in_specs,
        out_specs=[col(w) if t else row(w) for w, _, t in outs],
        out_shape=[jax.ShapeDtypeStruct((b, w, s) if t else (b, s, w), dt) for w, dt, t in outs],
        compiler_params=_params(("parallel", "parallel")),
        name="pre",
    )(*args)
    if res is None:
        res_out = [x] + list(res_out)
    return res_out


def _sgu_kernel(uv_ref, lng_ref, lnb_ref, w_ref, bias_ref, o_ref):
    rows = uv_ref.shape[1]
    uv = uv_ref[0]
    gu = jax.nn.gelu(uv[:, :A_WIDTH])
    gv = jax.nn.gelu(uv[:, A_WIDTH:])
    mu = jnp.mean(gv, axis=-1, keepdims=True)
    var = jnp.mean(jnp.square(gv - mu), axis=-1, keepdims=True)
    vn = ((gv - mu) * lax.rsqrt(var + NORM_EPS) * lng_ref[...] + lnb_ref[...]).astype(BF16)
    r = lax.broadcasted_iota(jnp.int32, (CHUNK, CHUNK), 0)
    c = lax.broadcasted_iota(jnp.int32, (CHUNK, CHUNK), 1)
    causal = c <= r
    lane_lo = lax.broadcasted_iota(jnp.int32, (CHUNK, LANES), 1) < DK
    for p in range(A_HEADS // 2):
        w0 = jnp.where(causal, w_ref[2 * p], 0.0).astype(BF16)
        w1 = jnp.where(causal, w_ref[2 * p + 1], 0.0).astype(BF16)
        bias = bias_ref[:, p * LANES:(p + 1) * LANES]
        for ch in range(rows // CHUNK):
            rs = slice(ch * CHUNK, (ch + 1) * CHUNK)
            cs = slice(p * LANES, (p + 1) * LANES)
            vp = vn[rs, cs]
            m0 = jnp.dot(w0, vp, preferred_element_type=F32)
            m1 = jnp.dot(w1, vp, preferred_element_type=F32)
            mixed = jnp.where(lane_lo, m0, m1) + bias
            o_ref[0, rs, cs] = (gu[rs, cs] * mixed).astype(o_ref.dtype)


def _sgu(uv, ln_g, ln_b, w_s, b_s, tm=512):
    b, s, _ = uv.shape
    tm = min(tm, s)
    bias = jnp.repeat(b_s.T, A_WIDTH // A_HEADS, axis=1)
    return pl.pallas_call(
        _sgu_kernel,
        grid=(b, s // tm),
        in_specs=[
            pl.BlockSpec((1, tm, 2 * A_WIDTH), lambda i, j: (i, j, 0)),
            pl.BlockSpec((1, A_WIDTH), lambda i, j: (0, 0)),
            pl.BlockSpec((1, A_WIDTH), lambda i, j: (0, 0)),
            pl.BlockSpec((A_HEADS, CHUNK, CHUNK), lambda i, j: (0, 0, 0)),
            pl.BlockSpec((CHUNK, A_WIDTH), lambda i, j: (0, 0)),
        ],
        out_specs=pl.BlockSpec((1, tm, A_WIDTH), lambda i, j: (i, j, 0)),
        out_shape=jax.ShapeDtypeStruct((b, s, A_WIDTH), BF16),
        compiler_params=_params(("parallel", "parallel")),
        name="sgu",
    )(uv, ln_g.reshape(1, -1), ln_b.reshape(1, -1), w_s, bias)


def _compress_kernel(kr_ref, vr_ref, pek_ref, pev_ref, w1k_ref, w1v_ref, w2k_ref, w2v_ref,
                     cos_ref, sin_ref, ko_ref, vt_ref):
    nc = kr_ref.shape[1]

    def mlp(r, pe_ref, w1_ref, w2_ref):
        top = jnp.dot(r + pe_ref[0:1], w1_ref[0], precision=HIGHEST, preferred_element_type=F32)
        bot = jnp.dot(r + pe_ref[1:2], w1_ref[1], precision=HIGHEST, preferred_element_type=F32)
        pre = top + pltpu.roll(bot, nc - 1, 0)
        return jnp.dot(jax.nn.gelu(pre), w2_ref[...], precision=HIGHEST, preferred_element_type=F32)

    kc = _rope(mlp(kr_ref[0], pek_ref, w1k_ref, w2k_ref), cos_ref[0], sin_ref[0])
    vc = mlp(vr_ref[0], pev_ref, w1v_ref, w2v_ref)
    lo = lax.broadcasted_iota(jnp.int32, (nc, LANES), 1) < DK
    rolled = pltpu.roll(kc, DK, 1)
    for g in range(KV_GROUPS):
        dup = jnp.where(lo, kc, rolled) if g == 0 else jnp.where(lo, rolled, kc)
        hi, low = _split_bf16(dup, 2)
        ko_ref[0, g, :, 0:LANES] = hi
        ko_ref[0, g, :, LANES:2 * LANES] = jnp.where(lo, low, jnp.zeros_like(low))
    vt_ref[0] = vc.T.astype(BF16)


def _compress_weights(pe, w1, w2):
    half = CMP_BLOCK // 2
    eye = jnp.eye(KV_GROUPS, dtype=F32)
    w1r = w1.reshape(CMP_BLOCK, DK, DK)
    wfull = jnp.einsum('lde,gh->lgdhe', w1r, eye)
    w1s = wfull.reshape(2, half * KV_GROUPS * DK, KV_GROUPS * DK)
    pes = jnp.broadcast_to(pe.reshape(2, half, 1, DK), (2, half, KV_GROUPS, DK)).reshape(2, -1)
    w2bd = jnp.einsum('de,gh->gdhe', w2, eye).reshape(KV_GROUPS * DK, KV_GROUPS * DK)
    return pes, w1s, w2bd


def _compress(kc, vc, pe_k, pe_v, w1_k, w2_k, w1_v, w2_v, cos_c, sin_c):
    b, s, _ = kc.shape
    nc = s // CMP_STRIDE
    rw = CMP_STRIDE * LANES
    pek, w1ks, w2kb = _compress_weights(pe_k, w1_k, w2_k)
    pev, w1vs, w2vb = _compress_weights(pe_v, w1_v, w2_v)
    full = lambda shape: pl.BlockSpec(shape, lambda i: (0,) * len(shape))
    return pl.pallas_call(
        _compress_kernel,
        grid=(b,),
        in_specs=[
            pl.BlockSpec((1, nc, rw), lambda i: (i, 0, 0)),
            pl.BlockSpec((1, nc, rw), lambda i: (i, 0, 0)),
            full((2, rw)), full((2, rw)),
            full((2, rw, LANES)), full((2, rw, LANES)),
            full((LANES, LANES)), full((LANES, LANES)),
            pl.BlockSpec((1, nc, LANES), lambda i: (i, 0, 0)),
            pl.BlockSpec((1, nc, LANES), lambda i: (i, 0, 0)),
        ],
        out_specs=[pl.BlockSpec((1, KV_GROUPS, nc, 2 * LANES), lambda i: (i, 0, 0, 0)),
                   pl.BlockSpec((1, KV_GROUPS * DK, nc), lambda i: (i, 0, 0))],
        out_shape=[jax.ShapeDtypeStruct((b, KV_GROUPS, nc, 2 * LANES), BF16),
                   jax.ShapeDtypeStruct((b, KV_GROUPS * DK, nc), BF16)],
        compiler_params=_params(("parallel",)),
        name="compress",
    )(kc.reshape(b, nc, rw), vc.reshape(b, nc, rw), pek, pev, w1ks, w1vs, w2kb, w2vb, cos_c, sin_c)


def _heads_on_lanes(qt):
    return jnp.concatenate([qt[h * DK:(h + 1) * DK] for h in range(HPG)], axis=1)


def _cmp_sel_kernel(n_top, qt_ref, k_ref, vt_ref, map_ref, oc_ref, bias_ref):
    tq = qt_ref.shape[2]
    cols = HPG * tq
    nc = k_ref.shape[2]
    nsel = map_ref.shape[0]
    q0 = pl.program_id(2) * tq
    q_hi, q_lo = _split_bf16(_heads_on_lanes(qt_ref[0]), 2)
    q3t = jnp.concatenate([q_hi, q_lo, q_hi, jnp.zeros_like(q_hi)], axis=0)
    s = jnp.dot(k_ref[0, 0], q3t, preferred_element_type=F32)
    n_idx = lax.broadcasted_iota(jnp.int32, (nc, 1), 0)
    t_col = q0 + lax.broadcasted_iota(jnp.int32, (1, cols), 1) % tq
    s = jnp.where((CMP_STRIDE * n_idx + CMP_BLOCK - 1) <= t_col, s, -jnp.inf)
    m = jnp.max(s, axis=0, keepdims=True)
    m = jnp.where(m == -jnp.inf, 0.0, m)
    e = jnp.exp2(s - m)
    d = jnp.sum(e, axis=0, keepdims=True)
    p = e / jnp.where(d > 0, d, 1.0)
    oc = jnp.dot(vt_ref[0], p.astype(BF16), preferred_element_type=F32)
    oc_ref[0] = jnp.concatenate([oc[:, h * tq:(h + 1) * tq] for h in range(HPG)], axis=0)
    psum = p[:, 0:tq]
    for h in range(1, HPG):
        psum = psum + p[:, h * tq:(h + 1) * tq]
    imp = jnp.dot(map_ref[...], jnp.concatenate(_split_bf16(psum, 3), axis=0),
                  preferred_element_type=F32)
    j = lax.broadcasted_iota(jnp.int32, (nsel, tq), 0)
    cur = (q0 + lax.broadcasted_iota(jnp.int32, (nsel, tq), 1)) // SEL_BLOCK
    valid = j <= cur
    forced = (j == 0) | (j == cur) | (j == cur - 1)
    keep = forced | (valid & (cur < n_top))
    vals = jnp.where(valid & jnp.logical_not(forced), imp, -jnp.inf)
    bias = jnp.where(keep, 0.0, MASK_BIAS)
    for _ in range(n_top - N_FORCED):
        mx = jnp.max(vals, axis=0, keepdims=True)
        first = jnp.min(jnp.where(vals == mx, j, nsel), axis=0, keepdims=True)
        pick = (j == first) & (mx > -jnp.inf)
        bias = jnp.where(pick, 0.0, bias)
        vals = jnp.where(pick, -jnp.inf, vals)
    bias_ref[0, 0] = bias.astype(BF16)


def _sel_map_t(s):
    nc = s // CMP_STRIDE
    n_cmp = (s - CMP_BLOCK) // CMP_STRIDE + 1
    n_sel = s // SEL_BLOCK
    cs = CMP_STRIDE * np.arange(n_cmp)[:, None]
    ce = cs + CMP_BLOCK
    ss = SEL_BLOCK * np.arange(n_sel)[None, :]
    se = ss + SEL_BLOCK
    ov = np.clip(np.minimum(ce, se) - np.maximum(cs, ss), 0, None) / CMP_STRIDE
    out = np.zeros((n_sel, nc), np.float32)
    out[:, :n_cmp] = ov.T
    return out


def _cmp_sel(qt, kcmp, vcmp_t, tq=128):
    b, _, s = qt.shape
    nc = s // CMP_STRIDE
    nsel = s // SEL_BLOCK
    n_top = min(SEL_TOP, nsel)
    sel_map = jnp.asarray(np.tile(_sel_map_t(s), (1, 3)), dtype=BF16)
    return pl.pallas_call(
        functools.partial(_cmp_sel_kernel, n_top),
        grid=(b, KV_GROUPS, s // tq),
        in_specs=[
            pl.BlockSpec((1, GW, tq), lambda i, g, j: (i, g, j)),
            pl.BlockSpec((1, 1, nc, 2 * LANES), lambda i, g, j: (i, g, 0, 0)),
            pl.BlockSpec((1, DK, nc), lambda i, g, j: (i, g, 0)),
            pl.BlockSpec((nsel, 3 * nc), lambda i, g, j: (0, 0)),
        ],
        out_specs=[
            pl.BlockSpec((1, GW, tq), lambda i, g, j: (i, g, j)),
            pl.BlockSpec((1, 1, nsel, tq), lambda i, g, j: (i, g, 0, j)),
        ],
        out_shape=[
            jax.ShapeDtypeStruct((b, B_HEADS * DK, s), F32),
            jax.ShapeDtypeStruct((b, KV_GROUPS, nsel, s), BF16),
        ],
        compiler_params=_params(("parallel", "parallel", "parallel")),
        name="cmp_sel",
    )(qt, kcmp, vcmp_t, sel_map)


def _attn_kernel(kt, qt_ref, ks_ref, kw_ref, vst_ref, vwt_ref, bias_ref, oh_ref, oc_ref, glt_ref,
                 o_ref, qa_ref, s0_ref, s1_ref, m_ref, l_ref, acc_ref):
    tq = qt_ref.shape[2]
    cols = HPG * tq
    nsel = bias_ref.shape[2]
    g = pl.program_id(1)
    q0 = pl.program_id(2) * tq

    q4t = _heads_on_lanes(qt_ref[0]).astype(BF16)
    zero = jnp.zeros_like(q4t)
    qa_ref[0:DK] = jnp.where(g == 0, q4t, zero)
    qa_ref[DK:2 * DK] = jnp.where(g == 1, q4t, zero)
    qa_ref[LANES:LANES + nsel] = jnp.concatenate([bias_ref[0, 0]] * HPG, axis=1)
    qaug = qa_ref[...]
    t_col = q0 + lax.broadcasted_iota(jnp.int32, (1, cols), 1) % tq

    def fold(x, op):
        return op(x.reshape(x.shape[0] // SUBLANES, SUBLANES, cols), axis=0)

    def stage(c, s_ref):
        k0 = pl.multiple_of(c * kt, kt)
        kaug = jnp.concatenate([ks_ref[0, pl.ds(k0, kt), :], oh_ref[pl.ds(k0, kt), :]], axis=1)
        s_ref[...] = jnp.dot(kaug, qaug, preferred_element_type=F32)
        r0 = pl.multiple_of(jnp.clip(q0 - k0, 0, kt - tq), tq)
        kpos = k0 + r0 + lax.broadcasted_iota(jnp.int32, (tq, 1), 0)
        s_ref[pl.ds(r0, tq), :] = jnp.where(kpos <= t_col, s_ref[pl.ds(r0, tq), :], -jnp.inf)

    def consume(c, s_ref):
        k0 = pl.multiple_of(c * kt, kt)
        sc = s_ref[...]
        m_old = m_ref[...]
        m_new = jnp.maximum(m_old, jnp.max(fold(sc, jnp.max), axis=0, keepdims=True))
        alpha = jnp.exp2(m_old - m_new)
        pr = jnp.exp2(sc - m_new)
        l_ref[...] = alpha * l_ref[...] + fold(pr, jnp.sum)
        acc_ref[...] = alpha * acc_ref[...] + jnp.dot(vst_ref[0, :, pl.ds(k0, kt)], pr.astype(BF16),
                                                      preferred_element_type=F32)
        m_ref[...] = m_new

    m_ref[...] = jnp.full(m_ref.shape, -jnp.inf, F32)
    l_ref[...] = jnp.zeros(l_ref.shape, F32)
    acc_ref[...] = jnp.zeros(acc_ref.shape, F32)
    n_full = q0 // kt
    n_pairs = n_full // 2
    odd = n_full % 2 == 1
    stage(0, s0_ref)

    def pair(i, carry):
        stage(2 * i + 1, s1_ref)
        consume(2 * i, s0_ref)
        stage(2 * i + 2, s0_ref)
        consume(2 * i + 1, s1_ref)
        return carry

    lax.fori_loop(0, n_pairs, pair, 0)

    @pl.when(odd)
    def _():
        stage(n_full, s1_ref)

    consume(2 * n_pairs, s0_ref)

    @pl.when(odd)
    def _():
        consume(n_full, s1_ref)

    o_sel = acc_ref[...] / jnp.sum(l_ref[...], axis=0, keepdims=True)

    wlen = WINDOW + tq
    w0 = pl.multiple_of(jnp.maximum(q0 - WINDOW, 0), tq)
    sw = jnp.dot(kw_ref[0, pl.ds(w0, wlen), :], qaug[0:LANES], preferred_element_type=F32)
    dlt = t_col - (w0 + lax.broadcasted_iota(jnp.int32, (wlen, 1), 0))
    sw = jnp.where((dlt >= 0) & (dlt < WINDOW), sw, -jnp.inf)
    pw = jnp.exp2(sw - jnp.max(sw, axis=0, keepdims=True))
    o_win = jnp.dot(vwt_ref[0, :, pl.ds(w0, wlen)], pw.astype(BF16), preferred_element_type=F32)
    o_win = o_win / jnp.sum(pw, axis=0, keepdims=True)

    gates = jax.nn.sigmoid(glt_ref[0])
    oc = _heads_on_lanes(oc_ref[0])
    for p in range(HPG // 2):
        halves = []
        for hh in (2 * p, 2 * p + 1):
            cs = slice(hh * tq, (hh + 1) * tq)
            halves.append(gates[3 * hh:3 * hh + 1] * oc[:, cs] + gates[3 * hh + 1:3 * hh + 2] * o_sel[:, cs]
                          + gates[3 * hh + 2:3 * hh + 3] * o_win[:, cs])
        o_ref[0, :, p * LANES:(p + 1) * LANES] = jnp.concatenate(halves, axis=0).T.astype(o_ref.dtype)


def _attn(qt, ks, kw, vs_t, vw_t, bias_t, oc_t, gl_t, tq=256, kt=512):
    b, _, s = qt.shape
    nsel = s // SEL_BLOCK
    kt = min(kt, s)
    onehot = jnp.asarray((np.arange(s)[:, None] // SEL_BLOCK == np.arange(nsel)[None, :]), dtype=BF16)
    k_spec = pl.BlockSpec((1, s, KV_GROUPS * DK), lambda i, g, j: (i, 0, 0))
    vt_spec = pl.BlockSpec((1, DK, s), lambda i, g, j: (i, g, 0))
    cols = HPG * tq
    return pl.pallas_call(
        functools.partial(_attn_kernel, kt),
        grid=(b, KV_GROUPS, s // tq),
        in_specs=[
            pl.BlockSpec((1, GW, tq), lambda i, g, j: (i, g, j)),
            k_spec, k_spec, vt_spec, vt_spec,
            pl.BlockSpec((1, 1, nsel, tq), lambda i, g, j: (i, g, 0, j)),
            pl.BlockSpec((s, nsel), lambda i, g, j: (0, 0)),
            pl.BlockSpec((1, GW, tq), lambda i, g, j: (i, g, j)),
            pl.BlockSpec((1, LANES, tq), lambda i, g, j: (i, g, j)),
        ],
        out_specs=pl.BlockSpec((1, tq, GW), lambda i, g, j: (i, j, g)),
        out_shape=jax.ShapeDtypeStruct((b, s, B_HEADS * DK), BF16),
        scratch_shapes=[pltpu.VMEM((LANES + nsel, cols), BF16),
                        pltpu.VMEM((kt, cols), F32), pltpu.VMEM((kt, cols), F32),
                        pltpu.VMEM((1, cols), F32), pltpu.VMEM((SUBLANES, cols), F32),
                        pltpu.VMEM((DK, cols), F32)],
        compiler_params=_params(("parallel", "parallel", "arbitrary")),
        name="attn",
    )(qt, ks, kw, vs_t, vw_t, bias_t, onehot, oc_t, gl_t)


def _post_kernel(ya_ref, yb_ref, wo_ref, x_ref, g1_ref, ng_ref, sc_ref, sh_ref, rw_ref, rb_ref,
                 xo_ref, h_ref, idx_ref, gate_ref, rank_ref, cnt_ref):
    tm = x_ref.shape[1]

    @pl.when((pl.program_id(0) == 0) & (pl.program_id(1) == 0))
    def _():
        cnt_ref[...] = jnp.zeros(cnt_ref.shape, F32)

    mixed = jnp.dot(ya_ref[0], wo_ref[:A_WIDTH], preferred_element_type=F32)
    mixed = mixed + jnp.dot(yb_ref[0], wo_ref[A_WIDTH:], preferred_element_type=F32)
    x = x_ref[0] + g1_ref[0] * mixed
    xo_ref[0] = x
    ms = jnp.mean(x * x, axis=-1, keepdims=True)
    h = x * lax.rsqrt(ms + NORM_EPS) * ng_ref[...]
    h = h * (1.0 + sc_ref[0]) + sh_ref[0]
    h_ref[0] = _pack_halves(h)
    logits = jnp.dot(h, rw_ref[...], precision=HIGHEST, preferred_element_type=F32) + rb_ref[...]
    lane = lax.broadcasted_iota(jnp.int32, logits.shape, 1)
    vals = logits
    top = jnp.max(vals, axis=-1, keepdims=True)
    picked = jnp.zeros_like(logits)
    slot_idx = jnp.zeros_like(logits)
    slot_exp = jnp.zeros_like(logits)
    for k in range(TOP_K):
        m = jnp.max(vals, axis=-1, keepdims=True)
        first = jnp.min(jnp.where(vals == m, lane, LANES), axis=-1, keepdims=True)
        pick = lane == first
        picked = jnp.where(pick, 1.0, picked)
        slot_idx = jnp.where(lane == k, first.astype(F32), slot_idx)
        slot_exp = jnp.where(lane == k, jnp.exp(m - top), slot_exp)
        vals = jnp.where(pick, -jnp.inf, vals)
    slot_gate = slot_exp / jnp.sum(slot_exp, axis=-1, keepdims=True)
    idx_t = slot_idx.T[:TOP_K]
    idx_ref[...] = idx_t.astype(jnp.int32)
    gate_ref[...] = slot_gate.T[:TOP_K]
    routed_t = picked.T[:N_EXPERTS]
    r = lax.broadcasted_iota(jnp.int32, (tm, tm), 0)
    c = lax.broadcasted_iota(jnp.int32, (tm, tm), 1)
    before = jnp.where(r < c, 1.0, 0.0).astype(BF16)
    rank_t = cnt_ref[...] + jnp.dot(routed_t.astype(BF16), before, preferred_element_type=F32)
    cnt_ref[...] += jnp.sum(routed_t, axis=-1, keepdims=True)
    e_row = lax.broadcasted_iota(jnp.int32, (N_EXPERTS, tm), 0).astype(F32)
    ranks = [jnp.sum(jnp.where(e_row == idx_t[k:k + 1], rank_t, 0.0), axis=0, keepdims=True)
             for k in range(TOP_K)]
    rank_ref[...] = jnp.concatenate(ranks, axis=0).astype(jnp.int32)


def _post(y_a, y_b, w_out, x, g1, norm_g, sc, sh, router_w, router_b, tm=512):
    b, s, d = x.shape
    tm = min(tm, s)
    nt = s // tm
    rw = jnp.zeros((d, LANES), F32).at[:, :N_EXPERTS].set(router_w)
    rb = jnp.full((1, LANES), -1e30, F32).at[0, :N_EXPERTS].set(router_b)
    vec = pl.BlockSpec((1, 1, d), lambda i, j: (i, 0, 0))
    row = lambda w: pl.BlockSpec((1, tm, w), lambda i, j: (i, j, 0))
    return pl.pallas_call(
        _post_kernel,
        grid=(b, nt),
        in_specs=[
            row(A_WIDTH), row(B_HEADS * DK),
            pl.BlockSpec((d, d), lambda i, j: (0, 0)),
            row(d), vec,
            pl.BlockSpec((1, d), lambda i, j: (0, 0)), vec, vec,
            pl.BlockSpec((d, LANES), lambda i, j: (0, 0)),
            pl.BlockSpec((1, LANES), lambda i, j: (0, 0)),
        ],
        out_specs=[row(d), row(d // 2)] + [pl.BlockSpec((TOP_K, tm), lambda i, j: (0, i * nt + j))] * 3,
        out_shape=[
            jax.ShapeDtypeStruct((b, s, d), F32),
            jax.ShapeDtypeStruct((b, s, d // 2), jnp.int32),
            jax.ShapeDtypeStruct((TOP_K, b * s), jnp.int32),
            jax.ShapeDtypeStruct((TOP_K, b * s), F32),
            jax.ShapeDtypeStruct((TOP_K, b * s), jnp.int32),
        ],
        scratch_shapes=[pltpu.VMEM((N_EXPERTS, 1), F32)],
        compiler_params=_params(("arbitrary", "arbitrary")),
        name="post",
    )(y_a, y_b, w_out.astype(BF16), x, g1, norm_g.reshape(1, d), sc, sh, rw, rb)


def _moe_kernel(rows, h_ref, gt_ref, w1_ref, b1_ref, w2_ref, b2_ref, y_ref, pos_ref):
    e = pl.program_id(1)
    tile = h_ref.shape[0]
    sub = 256

    @pl.when(e == 0)
    def _():
        y_ref[...] = jnp.zeros(y_ref.shape, F32)
        r = lax.broadcasted_iota(jnp.int32, (sub, sub), 0)
        c = lax.broadcasted_iota(jnp.int32, (sub, sub), 1)
        before = jnp.where(r < c, 1.0, 0.0).astype(BF16)
        carry = jnp.zeros((N_EXPERTS, 1), F32)
        for k in range(tile // sub):
            routed = jnp.where(gt_ref[:, k * sub:(k + 1) * sub] > 0, 1.0, 0.0)
            pos_ref[:, k * sub:(k + 1) * sub] = carry + jnp.dot(routed.astype(BF16), before,
                                                                 preferred_element_type=F32)
            carry = carry + jnp.sum(routed, axis=-1, keepdims=True)

    gate_row = gt_ref[pl.ds(e, 1), :]
    routed_row = gate_row > 0
    pos_row = pos_ref[pl.ds(e, 1), :]
    count = jnp.sum(jnp.where(routed_row, 1, 0))
    tn = (((0,), (0,)), ((), ()))

    def chunk(start, rows):
        slot = lax.broadcasted_iota(jnp.int32, (rows, tile), 0).astype(F32) + start.astype(F32)
        hit = (pos_row == slot) & routed_row
        onehot = jnp.where(hit, 1.0, 0.0).astype(BF16)
        xe = jnp.dot(onehot, h_ref[...], preferred_element_type=F32).astype(BF16)
        hdn = jnp.dot(xe, w1_ref[0], preferred_element_type=F32) + b1_ref[0]
        glu = jnp.minimum(hdn[:, :D_EXPERT], SWIGLU_LIMIT)
        lin = jnp.clip(hdn[:, D_EXPERT:], -SWIGLU_LIMIT, SWIGLU_LIMIT)
        act = glu * jax.nn.sigmoid(SWIGLU_ALPHA * glu) * (lin + 1.0)
        out = jnp.dot(act.astype(BF16), w2_ref[0], preferred_element_type=F32) + b2_ref[0]
        row_gate = jnp.sum(jnp.where(hit, gate_row, 0.0), axis=-1, keepdims=True)
        scaled = (out * row_gate).astype(BF16)
        y_ref[...] += lax.dot_general(onehot, scaled, tn, preferred_element_type=F32)

    n_big = count // rows
    rem = count - n_big * rows

    def body(c, carry):
        chunk(c * rows, rows)
        return carry

    lax.fori_loop(0, n_big, body, 0)

    @pl.when(rem > rows // 2)
    def _():
        chunk(n_big * rows, rows)

    @pl.when((rem > 0) & (rem <= rows // 2))
    def _():
        chunk(n_big * rows, rows // 2)


def _moe(h2, gates_t, w1, b1, w2, b2, tile=2048, rows=256):
    t, d = h2.shape
    tile = min(tile, t)
    n_e, _, f2 = w1.shape
    return pl.pallas_call(
        functools.partial(_moe_kernel, rows),
        grid=(t // tile, n_e),
        in_specs=[
            pl.BlockSpec((tile, d), lambda i, e: (i, 0)),
            pl.BlockSpec((n_e, tile), lambda i, e: (0, i)),
            pl.BlockSpec((1, d, f2), lambda i, e: (e, 0, 0)),
            pl.BlockSpec((1, 1, f2), lambda i, e: (e, 0, 0)),
            pl.BlockSpec((1, f2 // 2, d), lambda i, e: (e, 0, 0)),
            pl.BlockSpec((1, 1, d), lambda i, e: (e, 0, 0)),
        ],
        out_specs=pl.BlockSpec((tile, d), lambda i, e: (i, 0)),
        out_shape=jax.ShapeDtypeStruct((t, d), F32),
        scratch_shapes=[pltpu.VMEM((n_e, tile), F32)],
        compiler_params=_params(("parallel", "arbitrary")),
        name="moe",
    )(h2, gates_t, w1.astype(BF16), b1.reshape(n_e, 1, f2), w2.astype(BF16), b2.reshape(n_e, 1, d))


def _sc_mesh():
    return plsc.VectorSubcoreMesh(core_axis_name="c", subcore_axis_name="s")


def _sc_worker():
    return lax.axis_index("c") * SC_SUBCORES + lax.axis_index("s")


def _dispatch_rows(h2, dest_c, n_rows):
    t, d = h2.shape
    per_worker = t // SC_WINDOW // SC_WORKERS

    def body(x_hbm, i_hbm, o_hbm, buf, idx):
        worker = _sc_worker()

        @pl.loop(0, per_worker)
        def _(j):
            ch = worker * per_worker + j
            pltpu.sync_copy(i_hbm.at[ch], idx)
            pltpu.sync_copy(x_hbm.at[pl.ds(ch * SC_WINDOW, SC_WINDOW)], buf)
            for k in range(TOP_K):
                pltpu.sync_copy(buf, o_hbm.at[idx.at[k]])

    return pl.kernel(
        body, out_type=jax.ShapeDtypeStruct((n_rows, d), h2.dtype), mesh=_sc_mesh(),
        scratch_types=[pltpu.VMEM((SC_WINDOW, d), h2.dtype), pltpu.VMEM((TOP_K, SC_WINDOW), jnp.int32)],
        name="dispatch_rows",
    )(h2, dest_c)


def _collect_rows(rows, dest_c, t):
    d = rows.shape[1]
    per_worker = t // SC_WINDOW // SC_WORKERS

    def body(r_hbm, i_hbm, o_hbm, buf, idx):
        worker = _sc_worker()

        @pl.loop(0, per_worker)
        def _(j):
            ch = worker * per_worker + j
            pltpu.sync_copy(i_hbm.at[ch], idx)
            for k in range(TOP_K):
                pltpu.sync_copy(r_hbm.at[idx.at[k]], buf)
                pltpu.sync_copy(buf, o_hbm.at[k, pl.ds(ch * SC_WINDOW, SC_WINDOW)])

    return pl.kernel(
        body, out_type=jax.ShapeDtypeStruct((TOP_K, t, d), rows.dtype), mesh=_sc_mesh(),
        scratch_types=[pltpu.VMEM((SC_WINDOW, d), rows.dtype), pltpu.VMEM((TOP_K, SC_WINDOW), jnp.int32)],
        name="collect_rows",
    )(rows, dest_c)


def _experts_kernel(be_ref, nb_ref, x_ref, w1_ref, b1_ref, w2_ref, b2_ref, o_ref, w1b_ref, w2b_ref):
    i = pl.program_id(0)
    e = be_ref[i]
    live = i < nb_ref[0]

    @pl.when(live & ((i == 0) | (e != be_ref[jnp.maximum(i - 1, 0)])))
    def _():
        w1b_ref[...] = w1_ref[0, 0].astype(BF16)
        w2b_ref[...] = w2_ref[0, 0].astype(BF16)

    @pl.when(live)
    def _():
        x = _unpack_halves(x_ref[...]).astype(BF16)
        hdn = jnp.dot(x, w1b_ref[...], preferred_element_type=F32) + b1_ref[0, 0]
        glu = jnp.minimum(hdn[:, :D_EXPERT], SWIGLU_LIMIT)
        lin = jnp.clip(hdn[:, D_EXPERT:], -SWIGLU_LIMIT, SWIGLU_LIMIT)
        act = glu * jax.nn.sigmoid(SWIGLU_ALPHA * glu) * (lin + 1.0)
        out = jnp.dot(act.astype(BF16), w2b_ref[...], preferred_element_type=F32) + b2_ref[0, 0]
        o_ref[...] = _pack_halves(out)


def _experts(rows, block_exp, n_live, layer, w1, b1, w2, b2):
    n_rows, dp = rows.shape
    n_layers, n_e, d, f2 = w1.shape
    n_blocks = n_rows // ROW_BLOCK
    grid_spec = pltpu.PrefetchScalarGridSpec(
        num_scalar_prefetch=2,
        grid=(n_blocks,),
        in_specs=[
            pl.BlockSpec((ROW_BLOCK, dp), lambda i, be, nb: (i, 0)),
            pl.BlockSpec((1, 1, d, f2), lambda i, be, nb: (layer, be[i], 0, 0)),
            pl.BlockSpec((1, 1, 1, f2), lambda i, be, nb: (layer, be[i], 0, 0)),
            pl.BlockSpec((1, 1, f2 // 2, d), lambda i, be, nb: (layer, be[i], 0, 0)),
            pl.BlockSpec((1, 1, 1, d), lambda i, be, nb: (layer, be[i], 0, 0)),
        ],
        out_specs=pl.BlockSpec((ROW_BLOCK, dp), lambda i, be, nb: (i, 0)),
        scratch_shapes=[pltpu.VMEM((d, f2), BF16), pltpu.VMEM((f2 // 2, d), BF16)],
    )
    return pl.pallas_call(
        _experts_kernel,
        grid_spec=grid_spec,
        out_shape=jax.ShapeDtypeStruct((n_rows, dp), jnp.int32),
        compiler_params=_params(("arbitrary",)),
        name="experts",
    )(block_exp, n_live, rows, w1, b1.reshape(n_layers, n_e, 1, f2), w2, b2.reshape(n_layers, n_e, 1, d))


def _moe_routed(h2, idx, rank, layer, w1, b1, w2, b2):
    t, d = h2.shape
    n_e = w1.shape[1]
    n_blocks = t * TOP_K // ROW_BLOCK + n_e
    counts = jnp.sum((idx[:, :, None] == jnp.arange(n_e, dtype=jnp.int32)).astype(jnp.int32), axis=(0, 1))
    padded = (counts + ROW_BLOCK - 1) // ROW_BLOCK * ROW_BLOCK
    pad_end = jnp.cumsum(padded)
    dest = (pad_end - padded)[idx] + rank
    block_exp = jnp.minimum(jnp.searchsorted(pad_end, jnp.arange(n_blocks, dtype=jnp.int32) * ROW_BLOCK,
                                             side='right'), n_e - 1).astype(jnp.int32)
    n_live = (pad_end[-1:] // ROW_BLOCK).astype(jnp.int32)
    dest_c = dest.reshape(TOP_K, t // SC_WINDOW, SC_WINDOW).transpose(1, 0, 2)
    rows = _dispatch_rows(h2, dest_c, n_blocks * ROW_BLOCK)
    out = _experts(rows, block_exp, n_live, layer, w1, b1, w2, b2)
    return _collect_rows(out, dest_c, t)


def _final_kernel(x_ref, rows_ref, gate_ref, g2_ref, fg_ref, o_ref):
    x = _moe_residual(x_ref, rows_ref, gate_ref, g2_ref)
    ms = jnp.mean(x * x, axis=-1, keepdims=True)
    o_ref[0] = x * lax.rsqrt(ms + NORM_EPS) * fg_ref[...]


def _final(x, res, final_g, tm=512):
    b, s, d = x.shape
    tm = min(tm, s)
    row = pl.BlockSpec((1, tm, d), lambda i, j: (i, j, 0))
    return pl.pallas_call(
        _final_kernel,
        grid=(b, s // tm),
        in_specs=[row] + _moe_residual_specs(tm, d) + [pl.BlockSpec((1, d), lambda i, j: (0, 0))],
        out_specs=row,
        out_shape=jax.ShapeDtypeStruct((b, s, d), F32),
        compiler_params=_params(("parallel", "parallel")),
        name="final",
    )(x, *res, final_g.reshape(1, d))


def _widen_w_in(w):
    gl0 = 2 * A_WIDTH + B_HEADS * DK + 6 * KV_GROUPS * DK
    per_group = HPG * 3
    gl_groups = [jnp.pad(w[:, gl0 + g * per_group:gl0 + (g + 1) * per_group], ((0, 0), (0, LANES - per_group)))
                 for g in range(KV_GROUPS)]
    wide = jnp.concatenate([w[:, :gl0]] + gl_groups, axis=1)
    assert gl0 == C_GL and wide.shape[1] == P_WIDE
    return wide.astype(BF16)


def _rope_tables(pos):
    half = DK // 2
    inv = ROPE_THETA ** (-jnp.arange(half, dtype=F32) / half)
    ang = pos.astype(F32)[..., None] * inv
    cos = jnp.cos(ang)
    sin = jnp.sin(ang)
    reps = LANES // DK
    return (jnp.concatenate([cos, cos] * reps, axis=-1),
            jnp.concatenate([-sin, sin] * reps, axis=-1))


def kernel(x, c, positions, ada_w, ada_b, norm1_g, norm2_g, w_in, w_out, sg_ln_g, sg_ln_b, sg_w, sg_b,
           cmp_pe_k, cmp_pe_v, cmp_w1_k, cmp_w2_k, cmp_w1_v, cmp_w2_v, router_w, router_b,
           exp_w1, exp_b1, exp_w2, exp_b2, final_g):
    b, s, d = x.shape
    n_layers = ada_w.shape[0]
    mod = _ada_mod(c, ada_w, ada_b)
    cos, sin = _rope_tables(positions)
    cmp_end = jnp.minimum(CMP_STRIDE * jnp.arange(s // CMP_STRIDE) + CMP_BLOCK - 1, s - 1)
    cos_c, sin_c = _rope_tables(positions[:, cmp_end])
    res = None
    for l in range(n_layers):
        sh1, sc1, g1, sh2, sc2, g2 = [m.reshape(b, 1, d) for m in jnp.split(mod[l], 6, axis=-1)]
        x, uv, qt, kc, vc, ks, vs_t, kw, vw_t, gl_t = _pre(x, res, norm1_g[l], sc1, sh1,
                                                          _widen_w_in(w_in[l]), cos, sin)
        y_a = _sgu(uv, sg_ln_g[l], sg_ln_b[l], sg_w[l], sg_b[l])
        kcmp, vcmp_t = _compress(kc, vc, cmp_pe_k[l], cmp_pe_v[l], cmp_w1_k[l], cmp_w2_k[l],
                                 cmp_w1_v[l], cmp_w2_v[l], cos_c, sin_c)
        oc_t, bias_t = _cmp_sel(qt, kcmp, vcmp_t)
        y_b = _attn(qt, ks, kw, vs_t, vw_t, bias_t, oc_t, gl_t)
        x, h2, idx, gate, rank = _post(y_a, y_b, w_out[l], x, g1, norm2_g[l], sc2, sh2,
                                       router_w[l], router_b[l])
        rows = _moe_routed(h2.reshape(b * s, d // 2), idx, rank, l, exp_w1, exp_b1, exp_w2, exp_b2)
        res = (rows.reshape(TOP_K, b, s, d // 2), gate.T.reshape(b, s, TOP_K), g2)
    return _final(x, res, final_g)
```

```python
import functools

import numpy as np
import jax
import jax.numpy as jnp
from jax import lax
from jax.experimental import pallas as pl
from jax.experimental.pallas import tpu as pltpu
from jax.experimental.pallas import tpu_sc as plsc

F32 = jnp.float32
BF16 = jnp.bfloat16
HIGHEST = lax.Precision.HIGHEST

D_MODEL = 1024
A_WIDTH = 512
A_HEADS = 8
CHUNK = 128
B_HEADS = 8
DK = 64
KV_GROUPS = 2
HPG = B_HEADS // KV_GROUPS
CMP_BLOCK = 32
CMP_STRIDE = 16
SEL_BLOCK = 64
SEL_TOP = 16
WINDOW = 512
ROPE_THETA = 10000.0
N_EXPERTS = 32
TOP_K = 4
D_EXPERT = 1024
SWIGLU_LIMIT = 7.0
SWIGLU_ALPHA = 1.702
NORM_EPS = 1e-6

LANES = 128
SUBLANES = 8
GW = HPG * DK
N_FORCED = 3
MASK_BIAS = -32768.0
Q_SCALE = DK ** -0.5 * 1.4426950408889634
VMEM_LIMIT = 56 * 1024 * 1024
ROW_BLOCK = 256
SC_CORES = 2
SC_SUBCORES = 16
SC_WORKERS = SC_CORES * SC_SUBCORES
SC_WINDOW = 128

C_U, C_V, C_Q = 0, 512, 1024
C_KC, C_VC, C_KS, C_VS, C_KW, C_VW = 1536, 1664, 1792, 1920, 2048, 2176
C_GL = 2304
P_WIDE = 2560


def _params(sem):
    return pltpu.CompilerParams(dimension_semantics=sem, vmem_limit_bytes=VMEM_LIMIT)


def _split_bf16(x, parts):
    out = []
    for _ in range(parts):
        piece = x.astype(BF16)
        out.append(piece)
        x = x - piece.astype(F32)
    return out


def _ada_kernel(c_ref, w_ref, b_ref, o_ref):
    c = c_ref[...]
    cond = c * jax.nn.sigmoid(c)
    o_ref[0] = jnp.dot(cond, w_ref[0], precision=HIGHEST, preferred_element_type=F32) + b_ref[0]


def _ada_mod(c, ada_w, ada_b):
    n_layers, d, d6 = ada_w.shape
    b = c.shape[0]
    rows = SUBLANES
    c_pad = jnp.zeros((rows, d), F32).at[:b].set(c)
    out = pl.pallas_call(
        _ada_kernel,
        grid=(n_layers, d6 // d),
        in_specs=[
            pl.BlockSpec((rows, d), lambda l, j: (0, 0)),
            pl.BlockSpec((1, d, d), lambda l, j: (l, 0, j)),
            pl.BlockSpec((1, 1, d), lambda l, j: (l, 0, j)),
        ],
        out_specs=pl.BlockSpec((1, rows, d), lambda l, j: (l, 0, j)),
        out_shape=jax.ShapeDtypeStruct((n_layers, rows, d6), F32),
        compiler_params=_params(("arbitrary", "arbitrary")),
        name="ada",
    )(c_pad, ada_w, ada_b.reshape(n_layers, 1, d6))
    return out[:, :b]


def _rope_slab(t, cos, sin_signed, lo):
    partner = jnp.where(lo, pltpu.roll(t, LANES - DK // 2, 1), pltpu.roll(t, DK // 2, 1))
    return t * cos + partner * sin_signed


def _rope(t, cos, sin_signed):
    lane = lax.broadcasted_iota(jnp.int32, (1, LANES), 1)
    lo = (lane % DK) < (DK // 2)
    slabs = [_rope_slab(t[:, s * LANES:(s + 1) * LANES], cos, sin_signed, lo)
             for s in range(t.shape[1] // LANES)]
    return slabs[0] if len(slabs) == 1 else jnp.concatenate(slabs, axis=1)


def _pack_halves(x):
    n = x.shape[1] // 2
    lo = lax.bitcast_convert_type(x[:, :n].astype(BF16).astype(F32), jnp.int32)
    hi = lax.bitcast_convert_type(x[:, n:].astype(BF16).astype(F32), jnp.int32)
    return lax.shift_right_logical(lo, jnp.int32(16)) | (hi & jnp.int32(-65536))


def _unpack_halves(p):
    lo = lax.bitcast_convert_type(lax.shift_left(p, jnp.int32(16)), F32)
    hi = lax.bitcast_convert_type(p & jnp.int32(-65536), F32)
    return jnp.concatenate([lo, hi], axis=1)


def _moe_residual(x_ref, rows_ref, gate_ref, g2_ref):
    gate = gate_ref[0]
    y = gate[:, 0:1] * _unpack_halves(rows_ref[0, 0])
    for k in range(1, TOP_K):
        y = y + gate[:, k:k + 1] * _unpack_halves(rows_ref[k, 0])
    return x_ref[0] + g2_ref[0] * y


def _moe_residual_specs(tm, d):
    return [pl.BlockSpec((TOP_K, 1, tm, d // 2), lambda i, j: (0, i, j, 0)),
            pl.BlockSpec((1, tm, TOP_K), lambda i, j: (i, j, 0)),
            pl.BlockSpec((1, 1, d), lambda i, j: (i, 0, 0))]


def _pre_kernel(has_res, *refs):
    if has_res:
        (x_ref, rows_ref, gate_ref, g2_ref, ng_ref, sc_ref, sh_ref, w_ref, cos_ref, sin_ref,
         xo_ref, uv_ref, qt_ref, kc_ref, vc_ref, ks_ref, vst_ref, kw_ref, vwt_ref, glt_ref) = refs
        x = _moe_residual(x_ref, rows_ref, gate_ref, g2_ref)
        xo_ref[0] = x
    else:
        (x_ref, ng_ref, sc_ref, sh_ref, w_ref, cos_ref, sin_ref,
         uv_ref, qt_ref, kc_ref, vc_ref, ks_ref, vst_ref, kw_ref, vwt_ref, glt_ref) = refs
        x = x_ref[0]
    ms = jnp.mean(x * x, axis=-1, keepdims=True)
    h = x * lax.rsqrt(ms + NORM_EPS) * ng_ref[...]
    h = h * (1.0 + sc_ref[0]) + sh_ref[0]
    proj = jnp.dot(h.astype(BF16), w_ref[...], preferred_element_type=F32)
    cos = cos_ref[0]
    sin = sin_ref[0]
    uv_ref[0] = proj[:, C_U:C_Q]
    qt_ref[0] = (_rope(proj[:, C_Q:C_KC], cos, sin) * Q_SCALE).T
    kc_ref[0] = proj[:, C_KC:C_VC]
    vc_ref[0] = proj[:, C_VC:C_KS]
    ks_ref[0] = _rope(proj[:, C_KS:C_VS], cos, sin).astype(BF16)
    vst_ref[0] = proj[:, C_VS:C_KW].T.astype(BF16)
    kw_ref[0] = _rope(proj[:, C_KW:C_VW], cos, sin).astype(BF16)
    vwt_ref[0] = proj[:, C_VW:C_GL].T.astype(BF16)
    glt_ref[0] = proj[:, C_GL:P_WIDE].T


def _pre(x, res, norm_g, sc, sh, w_wide, cos, sin, tm=512):
    b, s, d = x.shape
    tm = min(tm, s)
    row = lambda w: pl.BlockSpec((1, tm, w), lambda i, j: (i, j, 0))
    col = lambda w: pl.BlockSpec((1, w, tm), lambda i, j: (i, 0, j))
    vec = pl.BlockSpec((1, 1, d), lambda i, j: (i, 0, 0))
    in_specs = [row(d)]
    args = [x]
    if res is not None:
        in_specs += _moe_residual_specs(tm, d)
        args += list(res)
    in_specs += [pl.BlockSpec((1, d), lambda i, j: (0, 0)), vec, vec,
                 pl.BlockSpec((d, P_WIDE), lambda i, j: (0, 0)), row(LANES), row(LANES)]
    args += [norm_g.reshape(1, d), sc, sh, w_wide, cos, sin]
    kv = KV_GROUPS * DK
    outs = [(d, F32, False)] if res is not None else []
    outs += [(2 * A_WIDTH, F32, False), (B_HEADS * DK, F32, True), (kv, F32, False), (kv, F32, False),
             (kv, BF16, False), (kv, BF16, True), (kv, BF16, False), (kv, BF16, True),
             (KV_GROUPS * LANES, F32, True)]
    res_out = pl.pallas_call(
        functools.partial(_pre_kernel, res is not None),
        grid=(b, s // tm),
        in_specs=in_specs,
        out_specs=[col(w) if t else row(w) for w, _, t in outs],
        out_shape=[jax.ShapeDtypeStruct((b, w, s) if t else (b, s, w), dt) for w, dt, t in outs],
        compiler_params=_params(("parallel", "parallel")),
        name="pre",
    )(*args)
    if res is None:
        res_out = [x] + list(res_out)
    return res_out


def _sgu_kernel(uv_ref, lng_ref, lnb_ref, w_ref, bias_ref, o_ref):
    rows = uv_ref.shape[1]
    uv = uv_ref[0]
    gu = jax.nn.gelu(uv[:, :A_WIDTH])
    gv = jax.nn.gelu(uv[:, A_WIDTH:])
    mu = jnp.mean(gv, axis=-1, keepdims=True)
    var = jnp.mean(jnp.square(gv - mu), axis=-1, keepdims=True)
    vn = ((gv - mu) * lax.rsqrt(var + NORM_EPS) * lng_ref[...] + lnb_ref[...]).astype(BF16)
    r = lax.broadcasted_iota(jnp.int32, (CHUNK, CHUNK), 0)
    c = lax.broadcasted_iota(jnp.int32, (CHUNK, CHUNK), 1)
    causal = c <= r
    lane_lo = lax.broadcasted_iota(jnp.int32, (CHUNK, LANES), 1) < DK
    for p in range(A_HEADS // 2):
        w0 = jnp.where(causal, w_ref[2 * p], 0.0).astype(BF16)
        w1 = jnp.where(causal, w_ref[2 * p + 1], 0.0).astype(BF16)
        bias = bias_ref[:, p * LANES:(p + 1) * LANES]
        for ch in range(rows // CHUNK):
            rs = slice(ch * CHUNK, (ch + 1) * CHUNK)
            cs = slice(p * LANES, (p + 1) * LANES)
            vp = vn[rs, cs]
            m0 = jnp.dot(w0, vp, preferred_element_type=F32)
            m1 = jnp.dot(w1, vp, preferred_element_type=F32)
            mixed = jnp.where(lane_lo, m0, m1) + bias
            o_ref[0, rs, cs] = (gu[rs, cs] * mixed).astype(o_ref.dtype)


def _sgu(uv, ln_g, ln_b, w_s, b_s, tm=512):
    b, s, _ = uv.shape
    tm = min(tm, s)
    bias = jnp.repeat(b_s.T, A_WIDTH // A_HEADS, axis=1)
    return pl.pallas_call(
        _sgu_kernel,
        grid=(b, s // tm),
        in_specs=[
            pl.BlockSpec((1, tm, 2 * A_WIDTH), lambda i, j: (i, j, 0)),
            pl.BlockSpec((1, A_WIDTH), lambda i, j: (0, 0)),
            pl.BlockSpec((1, A_WIDTH), lambda i, j: (0, 0)),
            pl.BlockSpec((A_HEADS, CHUNK, CHUNK), lambda i, j: (0, 0, 0)),
            pl.BlockSpec((CHUNK, A_WIDTH), lambda i, j: (0, 0)),
        ],
        out_specs=pl.BlockSpec((1, tm, A_WIDTH), lambda i, j: (i, j, 0)),
        out_shape=jax.ShapeDtypeStruct((b, s, A_WIDTH), BF16),
        compiler_params=_params(("parallel", "parallel")),
        name="sgu",
    )(uv, ln_g.reshape(1, -1), ln_b.reshape(1, -1), w_s, bias)


def _compress_kernel(kr_ref, vr_ref, pek_ref, pev_ref, w1k_ref, w1v_ref, w2k_ref, w2v_ref,
                     cos_ref, sin_ref, ko_ref, vt_ref):
    nc = kr_ref.shape[1]

    def mlp(r, pe_ref, w1_ref, w2_ref):
        top = jnp.dot(r + pe_ref[0:1], w1_ref[0], precision=HIGHEST, preferred_element_type=F32)
        bot = jnp.dot(r + pe_ref[1:2], w1_ref[1], precision=HIGHEST, preferred_element_type=F32)
        pre = top + pltpu.roll(bot, nc - 1, 0)
        return jnp.dot(jax.nn.gelu(pre), w2_ref[...], precision=HIGHEST, preferred_element_type=F32)

    kc = _rope(mlp(kr_ref[0], pek_ref, w1k_ref, w2k_ref), cos_ref[0], sin_ref[0])
    vc = mlp(vr_ref[0], pev_ref, w1v_ref, w2v_ref)
    lo = lax.broadcasted_iota(jnp.int32, (nc, LANES), 1) < DK
    rolled = pltpu.roll(kc, DK, 1)
    for g in range(KV_GROUPS):
        dup = jnp.where(lo, kc, rolled) if g == 0 else jnp.where(lo, rolled, kc)
        hi, low = _split_bf16(dup, 2)
        ko_ref[0, g, :, 0:LANES] = hi
        ko_ref[0, g, :, LANES:2 * LANES] = jnp.where(lo, low, jnp.zeros_like(low))
    vt_ref[0] = vc.T.astype(BF16)


def _compress_weights(pe, w1, w2):
    half = CMP_BLOCK // 2
    eye = jnp.eye(KV_GROUPS, dtype=F32)
    w1r = w1.reshape(CMP_BLOCK, DK, DK)
    wfull = jnp.einsum('lde,gh->lgdhe', w1r, eye)
    w1s = wfull.reshape(2, half * KV_GROUPS * DK, KV_GROUPS * DK)
    pes = jnp.broadcast_to(pe.reshape(2, half, 1, DK), (2, half, KV_GROUPS, DK)).reshape(2, -1)
    w2bd = jnp.einsum('de,gh->gdhe', w2, eye).reshape(KV_GROUPS * DK, KV_GROUPS * DK)
    return pes, w1s, w2bd


def _compress(kc, vc, pe_k, pe_v, w1_k, w2_k, w1_v, w2_v, cos_c, sin_c):
    b, s, _ = kc.shape
    nc = s // CMP_STRIDE
    rw = CMP_STRIDE * LANES
    pek, w1ks, w2kb = _compress_weights(pe_k, w1_k, w2_k)
    pev, w1vs, w2vb = _compress_weights(pe_v, w1_v, w2_v)
    full = lambda shape: pl.BlockSpec(shape, lambda i: (0,) * len(shape))
    return pl.pallas_call(
        _compress_kernel,
        grid=(b,),
        in_specs=[
            pl.BlockSpec((1, nc, rw), lambda i: (i, 0, 0)),
            pl.BlockSpec((1, nc, rw), lambda i: (i, 0, 0)),
            full((2, rw)), full((2, rw)),
            full((2, rw, LANES)), full((2, rw, LANES)),
            full((LANES, LANES)), full((LANES, LANES)),
            pl.BlockSpec((1, nc, LANES), lambda i: (i, 0, 0)),
            pl.BlockSpec((1, nc, LANES), lambda i: (i, 0, 0)),
        ],
        out_specs=[pl.BlockSpec((1, KV_GROUPS, nc, 2 * LANES), lambda i: (i, 0, 0, 0)),
                   pl.BlockSpec((1, KV_GROUPS * DK, nc), lambda i: (i, 0, 0))],
        out_shape=[jax.ShapeDtypeStruct((b, KV_GROUPS, nc, 2 * LANES), BF16),
                   jax.ShapeDtypeStruct((b, KV_GROUPS * DK, nc), BF16)],
        compiler_params=_params(("parallel",)),
        name="compress",
    )(kc.reshape(b, nc, rw), vc.reshape(b, nc, rw), pek, pev, w1ks, w1vs, w2kb, w2vb, cos_c, sin_c)


def _heads_on_lanes(qt):
    return jnp.concatenate([qt[h * DK:(h + 1) * DK] for h in range(HPG)], axis=1)


def _cmp_sel_kernel(n_top, qt_ref, k_ref, vt_ref, map_ref, oc_ref, bias_ref):
    tq = qt_ref.shape[2]
    cols = HPG * tq
    nc = k_ref.shape[2]
    nsel = map_ref.shape[0]
    q0 = pl.program_id(2) * tq
    q_hi, q_lo = _split_bf16(_heads_on_lanes(qt_ref[0]), 2)
    q3t = jnp.concatenate([q_hi, q_lo, q_hi, jnp.zeros_like(q_hi)], axis=0)
    s = jnp.dot(k_ref[0, 0], q3t, preferred_element_type=F32)
    n_idx = lax.broadcasted_iota(jnp.int32, (nc, 1), 0)
    t_col = q0 + lax.broadcasted_iota(jnp.int32, (1, cols), 1) % tq
    s = jnp.where((CMP_STRIDE * n_idx + CMP_BLOCK - 1) <= t_col, s, -jnp.inf)
    m = jnp.max(s, axis=0, keepdims=True)
    m = jnp.where(m == -jnp.inf, 0.0, m)
    e = jnp.exp2(s - m)
    d = jnp.sum(e, axis=0, keepdims=True)
    p = e / jnp.where(d > 0, d, 1.0)
    oc = jnp.dot(vt_ref[0], p.astype(BF16), preferred_element_type=F32)
    oc_ref[0] = jnp.concatenate([oc[:, h * tq:(h + 1) * tq] for h in range(HPG)], axis=0)
    psum = p[:, 0:tq]
    for h in range(1, HPG):
        psum = psum + p[:, h * tq:(h + 1) * tq]
    imp = jnp.dot(map_ref[...], jnp.concatenate(_split_bf16(psum, 3), axis=0),
                  preferred_element_type=F32)
    j = lax.broadcasted_iota(jnp.int32, (nsel, tq), 0)
    cur = (q0 + lax.broadcasted_iota(jnp.int32, (nsel, tq), 1)) // SEL_BLOCK
    valid = j <= cur
    forced = (j == 0) | (j == cur) | (j == cur - 1)
    keep = forced | (valid & (cur < n_top))
    vals = jnp.where(valid & jnp.logical_not(forced), imp, -jnp.inf)
    bias = jnp.where(keep, 0.0, MASK_BIAS)
    for _ in range(n_top - N_FORCED):
        mx = jnp.max(vals, axis=0, keepdims=True)
        first = jnp.min(jnp.where(vals == mx, j, nsel), axis=0, keepdims=True)
        pick = (j == first) & (mx > -jnp.inf)
        bias = jnp.where(pick, 0.0, bias)
        vals = jnp.where(pick, -jnp.inf, vals)
    bias_ref[0, 0] = bias.astype(BF16)


def _sel_map_t(s):
    nc = s // CMP_STRIDE
    n_cmp = (s - CMP_BLOCK) // CMP_STRIDE + 1
    n_sel = s // SEL_BLOCK
    cs = CMP_STRIDE * np.arange(n_cmp)[:, None]
    ce = cs + CMP_BLOCK
    ss = SEL_BLOCK * np.arange(n_sel)[None, :]
    se = ss + SEL_BLOCK
    ov = np.clip(np.minimum(ce, se) - np.maximum(cs, ss), 0, None) / CMP_STRIDE
    out = np.zeros((n_sel, nc), np.float32)
    out[:, :n_cmp] = ov.T
    return out


def _cmp_sel(qt, kcmp, vcmp_t, tq=128):
    b, _, s = qt.shape
    nc = s // CMP_STRIDE
    nsel = s // SEL_BLOCK
    n_top = min(SEL_TOP, nsel)
    sel_map = jnp.asarray(np.tile(_sel_map_t(s), (1, 3)), dtype=BF16)
    return pl.pallas_call(
        functools.partial(_cmp_sel_kernel, n_top),
        grid=(b, KV_GROUPS, s // tq),
        in_specs=[
            pl.BlockSpec((1, GW, tq), lambda i, g, j: (i, g, j)),
            pl.BlockSpec((1, 1, nc, 2 * LANES), lambda i, g, j: (i, g, 0, 0)),
            pl.BlockSpec((1, DK, nc), lambda i, g, j: (i, g, 0)),
            pl.BlockSpec((nsel, 3 * nc), lambda i, g, j: (0, 0)),
        ],
        out_specs=[
            pl.BlockSpec((1, GW, tq), lambda i, g, j: (i, g, j)),
            pl.BlockSpec((1, 1, nsel, tq), lambda i, g, j: (i, g, 0, j)),
        ],
        out_shape=[
            jax.ShapeDtypeStruct((b, B_HEADS * DK, s), F32),
            jax.ShapeDtypeStruct((b, KV_GROUPS, nsel, s), BF16),
        ],
        compiler_params=_params(("parallel", "parallel", "parallel")),
        name="cmp_sel",
    )(qt, kcmp, vcmp_t, sel_map)


def _attn_kernel(kt, qt_ref, ks_ref, kw_ref, vst_ref, vwt_ref, bias_ref, oh_ref, oc_ref, glt_ref,
                 o_ref, qa_ref, s0_ref, s1_ref, m_ref, l_ref, acc_ref):
    tq = qt_ref.shape[2]
    cols = HPG * tq
    nsel = bias_ref.shape[2]
    g = pl.program_id(1)
    q0 = pl.program_id(2) * tq

    q4t = _heads_on_lanes(qt_ref[0]).astype(BF16)
    zero = jnp.zeros_like(q4t)
    qa_ref[0:DK] = jnp.where(g == 0, q4t, zero)
    qa_ref[DK:2 * DK] = jnp.where(g == 1, q4t, zero)
    qa_ref[LANES:LANES + nsel] = jnp.concatenate([bias_ref[0, 0]] * HPG, axis=1)
    qaug = qa_ref[...]
    t_col = q0 + lax.broadcasted_iota(jnp.int32, (1, cols), 1) % tq

    def fold(x, op):
        return op(x.reshape(x.shape[0] // SUBLANES, SUBLANES, cols), axis=0)

    def stage(c, s_ref):
        k0 = pl.multiple_of(c * kt, kt)
        kaug = jnp.concatenate([ks_ref[0, pl.ds(k0, kt), :], oh_ref[pl.ds(k0, kt), :]], axis=1)
        s_ref[...] = jnp.dot(kaug, qaug, preferred_element_type=F32)
        r0 = pl.multiple_of(jnp.clip(q0 - k0, 0, kt - tq), tq)
        kpos = k0 + r0 + lax.broadcasted_iota(jnp.int32, (tq, 1), 0)
        s_ref[pl.ds(r0, tq), :] = jnp.where(kpos <= t_col, s_ref[pl.ds(r0, tq), :], -jnp.inf)

    def consume(c, s_ref):
        k0 = pl.multiple_of(c * kt, kt)
        sc = s_ref[...]
        m_old = m_ref[...]
        m_new = jnp.maximum(m_old, jnp.max(fold(sc, jnp.max), axis=0, keepdims=True))
        alpha = jnp.exp2(m_old - m_new)
        pr = jnp.exp2(sc - m_new)
        l_ref[...] = alpha * l_ref[...] + fold(pr, jnp.sum)
        acc_ref[...] = alpha * acc_ref[...] + jnp.dot(vst_ref[0, :, pl.ds(k0, kt)], pr.astype(BF16),
                                                      preferred_element_type=F32)
        m_ref[...] = m_new

    m_ref[...] = jnp.full(m_ref.shape, -jnp.inf, F32)
    l_ref[...] = jnp.zeros(l_ref.shape, F32)
    acc_ref[...] = jnp.zeros(acc_ref.shape, F32)
    n_full = q0 // kt
    n_pairs = n_full // 2
    odd = n_full % 2 == 1
    stage(0, s0_ref)

    def pair(i, carry):
        stage(2 * i + 1, s1_ref)
        consume(2 * i, s0_ref)
        stage(2 * i + 2, s0_ref)
        consume(2 * i + 1, s1_ref)
        return carry

    lax.fori_loop(0, n_pairs, pair, 0)

    @pl.when(odd)
    def _():
        stage(n_full, s1_ref)

    consume(2 * n_pairs, s0_ref)

    @pl.when(odd)
    def _():
        consume(n_full, s1_ref)

    o_sel = acc_ref[...] / jnp.sum(l_ref[...], axis=0, keepdims=True)

    wlen = WINDOW + tq
    w0 = pl.multiple_of(jnp.maximum(q0 - WINDOW, 0), tq)
    sw = jnp.dot(kw_ref[0, pl.ds(w0, wlen), :], qaug[0:LANES], preferred_element_type=F32)
    dlt = t_col - (w0 + lax.broadcasted_iota(jnp.int32, (wlen, 1), 0))
    sw = jnp.where((dlt >= 0) & (dlt < WINDOW), sw, -jnp.inf)
    pw = jnp.exp2(sw - jnp.max(sw, axis=0, keepdims=True))
    o_win = jnp.dot(vwt_ref[0, :, pl.ds(w0, wlen)], pw.astype(BF16), preferred_element_type=F32)
    o_win = o_win / jnp.sum(pw, axis=0, keepdims=True)

    gates = jax.nn.sigmoid(glt_ref[0])
    oc = _heads_on_lanes(oc_ref[0])
    for p in range(HPG // 2):
        halves = []
        for hh in (2 * p, 2 * p + 1):
            cs = slice(hh * tq, (hh + 1) * tq)
            halves.append(gates[3 * hh:3 * hh + 1] * oc[:, cs] + gates[3 * hh + 1:3 * hh + 2] * o_sel[:, cs]
                          + gates[3 * hh + 2:3 * hh + 3] * o_win[:, cs])
        o_ref[0, :, p * LANES:(p + 1) * LANES] = jnp.concatenate(halves, axis=0).T.astype(o_ref.dtype)


def _attn(qt, ks, kw, vs_t, vw_t, bias_t, oc_t, gl_t, tq=256, kt=512):
    b, _, s = qt.shape
    nsel = s // SEL_BLOCK
    kt = min(kt, s)
    onehot = jnp.asarray((np.arange(s)[:, None] // SEL_BLOCK == np.arange(nsel)[None, :]), dtype=BF16)
    k_spec = pl.BlockSpec((1, s, KV_GROUPS * DK), lambda i, g, j: (i, 0, 0))
    vt_spec = pl.BlockSpec((1, DK, s), lambda i, g, j: (i, g, 0))
    cols = HPG * tq
    return pl.pallas_call(
        functools.partial(_attn_kernel, kt),
        grid=(b, KV_GROUPS, s // tq),
        in_specs=[
            pl.BlockSpec((1, GW, tq), lambda i, g, j: (i, g, j)),
            k_spec, k_spec, vt_spec, vt_spec,
            pl.BlockSpec((1, 1, nsel, tq), lambda i, g, j: (i, g, 0, j)),
            pl.BlockSpec((s, nsel), lambda i, g, j: (0, 0)),
            pl.BlockSpec((1, GW, tq), lambda i, g, j: (i, g, j)),
            pl.BlockSpec((1, LANES, tq), lambda i, g, j: (i, g, j)),
        ],
        out_specs=pl.BlockSpec((1, tq, GW), lambda i, g, j: (i, j, g)),
        out_shape=jax.ShapeDtypeStruct((b, s, B_HEADS * DK), BF16),
        scratch_shapes=[pltpu.VMEM((LANES + nsel, cols), BF16),
                        pltpu.VMEM((kt, cols), F32), pltpu.VMEM((kt, cols), F32),
                        pltpu.VMEM((1, cols), F32), pltpu.VMEM((SUBLANES, cols), F32),
                        pltpu.VMEM((DK, cols), F32)],
        compiler_params=_params(("parallel", "parallel", "arbitrary")),
        name="attn",
    )(qt, ks, kw, vs_t, vw_t, bias_t, onehot, oc_t, gl_t)


def _post_kernel(ya_ref, yb_ref, wo_ref, x_ref, g1_ref, ng_ref, sc_ref, sh_ref, rw_ref, rb_ref,
                 xo_ref, h_ref, idx_ref, gate_ref, rank_ref, cnt_ref):
    tm = x_ref.shape[1]

    @pl.when((pl.program_id(0) == 0) & (pl.program_id(1) == 0))
    def _():
        cnt_ref[...] = jnp.zeros(cnt_ref.shape, F32)

    mixed = jnp.dot(ya_ref[0], wo_ref[:A_WIDTH], preferred_element_type=F32)
    mixed = mixed + jnp.dot(yb_ref[0], wo_ref[A_WIDTH:], preferred_element_type=F32)
    x = x_ref[0] + g1_ref[0] * mixed
    xo_ref[0] = x
    ms = jnp.mean(x * x, axis=-1, keepdims=True)
    h = x * lax.rsqrt(ms + NORM_EPS) * ng_ref[...]
    h = h * (1.0 + sc_ref[0]) + sh_ref[0]
    h_ref[0] = _pack_halves(h)
    logits = jnp.dot(h, rw_ref[...], precision=HIGHEST, preferred_element_type=F32) + rb_ref[...]
    lane = lax.broadcasted_iota(jnp.int32, logits.shape, 1)
    vals = logits
    top = jnp.max(vals, axis=-1, keepdims=True)
    picked = jnp.zeros_like(logits)
    slot_idx = jnp.zeros_like(logits)
    slot_exp = jnp.zeros_like(logits)
    for k in range(TOP_K):
        m = jnp.max(vals, axis=-1, keepdims=True)
        first = jnp.min(jnp.where(vals == m, lane, LANES), axis=-1, keepdims=True)
        pick = lane == first
        picked = jnp.where(pick, 1.0, picked)
        slot_idx = jnp.where(lane == k, first.astype(F32), slot_idx)
        slot_exp = jnp.where(lane == k, jnp.exp(m - top), slot_exp)
        vals = jnp.where(pick, -jnp.inf, vals)
    slot_gate = slot_exp / jnp.sum(slot_exp, axis=-1, keepdims=True)
    idx_t = slot_idx.T[:TOP_K]
    idx_ref[...] = idx_t.astype(jnp.int32)
    gate_ref[...] = slot_gate.T[:TOP_K]
    routed_t = picked.T[:N_EXPERTS]
    r = lax.broadcasted_iota(jnp.int32, (tm, tm), 0)
    c = lax.broadcasted_iota(jnp.int32, (tm, tm), 1)
    before = jnp.where(r < c, 1.0, 0.0).astype(BF16)
    rank_t = cnt_ref[...] + jnp.dot(routed_t.astype(BF16), before, preferred_element_type=F32)
    cnt_ref[...] += jnp.sum(routed_t, axis=-1, keepdims=True)
    e_row = lax.broadcasted_iota(jnp.int32, (N_EXPERTS, tm), 0).astype(F32)
    ranks = [jnp.sum(jnp.where(e_row == idx_t[k:k + 1], rank_t, 0.0), axis=0, keepdims=True)
             for k in range(TOP_K)]
    rank_ref[...] = jnp.concatenate(ranks, axis=0).astype(jnp.int32)


def _post(y_a, y_b, w_out, x, g1, norm_g, sc, sh, router_w, router_b, tm=512):
    b, s, d = x.shape
    tm = min(tm, s)
    nt = s // tm
    rw = jnp.zeros((d, LANES), F32).at[:, :N_EXPERTS].set(router_w)
    rb = jnp.full((1, LANES), -1e30, F32).at[0, :N_EXPERTS].set(router_b)
    vec = pl.BlockSpec((1, 1, d), lambda i, j: (i, 0, 0))
    row = lambda w: pl.BlockSpec((1, tm, w), lambda i, j: (i, j, 0))
    return pl.pallas_call(
        _post_kernel,
        grid=(b, nt),
        in_specs=[
            row(A_WIDTH), row(B_HEADS * DK),
            pl.BlockSpec((d, d), lambda i, j: (0, 0)),
            row(d), vec,
            pl.BlockSpec((1, d), lambda i, j: (0, 0)), vec, vec,
            pl.BlockSpec((d, LANES), lambda i, j: (0, 0)),
            pl.BlockSpec((1, LANES), lambda i, j: (0, 0)),
        ],
        out_specs=[row(d), row(d // 2)] + [pl.BlockSpec((TOP_K, tm), lambda i, j: (0, i * nt + j))] * 3,
        out_shape=[
            jax.ShapeDtypeStruct((b, s, d), F32),
            jax.ShapeDtypeStruct((b, s, d // 2), jnp.int32),
            jax.ShapeDtypeStruct((TOP_K, b * s), jnp.int32),
            jax.ShapeDtypeStruct((TOP_K, b * s), F32),
            jax.ShapeDtypeStruct((TOP_K, b * s), jnp.int32),
        ],
        scratch_shapes=[pltpu.VMEM((N_EXPERTS, 1), F32)],
        compiler_params=_params(("arbitrary", "arbitrary")),
        name="post",
    )(y_a, y_b, w_out.astype(BF16), x, g1, norm_g.reshape(1, d), sc, sh, rw, rb)


def _sc_mesh():
    return plsc.VectorSubcoreMesh(core_axis_name="c", subcore_axis_name="s")


def _sc_worker():
    return lax.axis_index("c") * SC_SUBCORES + lax.axis_index("s")


def _dispatch_rows(h2, dest_c, n_rows):
    t, d = h2.shape
    per_worker = t // SC_WINDOW // SC_WORKERS

    def body(x_hbm, i_hbm, o_hbm, buf, idx):
        worker = _sc_worker()

        @pl.loop(0, per_worker)
        def _(j):
            ch = worker * per_worker + j
            pltpu.sync_copy(i_hbm.at[ch], idx)
            pltpu.sync_copy(x_hbm.at[pl.ds(ch * SC_WINDOW, SC_WINDOW)], buf)
            for k in range(TOP_K):
                pltpu.sync_copy(buf, o_hbm.at[idx.at[k]])

    return pl.kernel(
        body, out_type=jax.ShapeDtypeStruct((n_rows, d), h2.dtype), mesh=_sc_mesh(),
        scratch_types=[pltpu.VMEM((SC_WINDOW, d), h2.dtype), pltpu.VMEM((TOP_K, SC_WINDOW), jnp.int32)],
        name="dispatch_rows",
    )(h2, dest_c)


def _collect_rows(rows, dest_c, t):
    d = rows.shape[1]
    per_worker = t // SC_WINDOW // SC_WORKERS

    def body(r_hbm, i_hbm, o_hbm, buf, idx):
        worker = _sc_worker()

        @pl.loop(0, per_worker)
        def _(j):
            ch = worker * per_worker + j
            pltpu.sync_copy(i_hbm.at[ch], idx)
            for k in range(TOP_K):
                pltpu.sync_copy(r_hbm.at[idx.at[k]], buf)
                pltpu.sync_copy(buf, o_hbm.at[k, pl.ds(ch * SC_WINDOW, SC_WINDOW)])

    return pl.kernel(
        body, out_type=jax.ShapeDtypeStruct((TOP_K, t, d), rows.dtype), mesh=_sc_mesh(),
        scratch_types=[pltpu.VMEM((SC_WINDOW, d), rows.dtype), pltpu.VMEM((TOP_K, SC_WINDOW), jnp.int32)],
        name="collect_rows",
    )(rows, dest_c)


def _experts_kernel(be_ref, nb_ref, x_ref, w1_ref, b1_ref, w2_ref, b2_ref, o_ref, w1b_ref, w2b_ref):
    i = pl.program_id(0)
    e = be_ref[i]
    live = i < nb_ref[0]

    @pl.when(live & ((i == 0) | (e != be_ref[jnp.maximum(i - 1, 0)])))
    def _():
        w1b_ref[...] = w1_ref[0, 0].astype(BF16)
        w2b_ref[...] = w2_ref[0, 0].astype(BF16)

    @pl.when(live)
    def _():
        x = _unpack_halves(x_ref[...]).astype(BF16)
        hdn = jnp.dot(x, w1b_ref[...], preferred_element_type=F32) + b1_ref[0, 0]
        glu = jnp.minimum(hdn[:, :D_EXPERT], SWIGLU_LIMIT)
        lin = jnp.clip(hdn[:, D_EXPERT:], -SWIGLU_LIMIT, SWIGLU_LIMIT)
        act = glu * jax.nn.sigmoid(SWIGLU_ALPHA * glu) * (lin + 1.0)
        out = jnp.dot(act.astype(BF16), w2b_ref[...], preferred_element_type=F32) + b2_ref[0, 0]
        o_ref[...] = _pack_halves(out)


def _experts(rows, block_exp, n_live, layer, w1, b1, w2, b2):
    n_rows, dp = rows.shape
    n_layers, n_e, d, f2 = w1.shape
    n_blocks = n_rows // ROW_BLOCK
    grid_spec = pltpu.PrefetchScalarGridSpec(
        num_scalar_prefetch=2,
        grid=(n_blocks,),
        in_specs=[
            pl.BlockSpec((ROW_BLOCK, dp), lambda i, be, nb: (i, 0)),
            pl.BlockSpec((1, 1, d, f2), lambda i, be, nb: (layer, be[i], 0, 0)),
            pl.BlockSpec((1, 1, 1, f2), lambda i, be, nb: (layer, be[i], 0, 0)),
            pl.BlockSpec((1, 1, f2 // 2, d), lambda i, be, nb: (layer, be[i], 0, 0)),
            pl.BlockSpec((1, 1, 1, d), lambda i, be, nb: (layer, be[i], 0, 0)),
        ],
        out_specs=pl.BlockSpec((ROW_BLOCK, dp), lambda i, be, nb: (i, 0)),
        scratch_shapes=[pltpu.VMEM((d, f2), BF16), pltpu.VMEM((f2 // 2, d), BF16)],
    )
    return pl.pallas_call(
        _experts_kernel,
        grid_spec=grid_spec,
        out_shape=jax.ShapeDtypeStruct((n_rows, dp), jnp.int32),
        compiler_params=_params(("arbitrary",)),
        name="experts",
    )(block_exp, n_live, rows, w1, b1.reshape(n_layers, n_e, 1, f2), w2, b2.reshape(n_layers, n_e, 1, d))


def _moe_routed(h2, idx, rank, layer, w1, b1, w2, b2):
    t, d = h2.shape
    n_e = w1.shape[1]
    n_blocks = t * TOP_K // ROW_BLOCK + n_e
    onehot = (idx[:, :, None] == jnp.arange(n_e, dtype=jnp.int32)).astype(jnp.int32)
    counts = jnp.sum(onehot, axis=(0, 1))
    padded = (counts + ROW_BLOCK - 1) // ROW_BLOCK * ROW_BLOCK
    pad_end = jnp.cumsum(padded)
    dest = jnp.sum(onehot * (pad_end - padded), axis=-1) + rank
    block_row = jnp.arange(n_blocks, dtype=jnp.int32)[:, None] * ROW_BLOCK
    block_exp = jnp.minimum(jnp.sum((pad_end[None, :] <= block_row).astype(jnp.int32), axis=-1), n_e - 1)
    n_live = (pad_end[-1:] // ROW_BLOCK).astype(jnp.int32)
    dest_c = dest.reshape(TOP_K, t // SC_WINDOW, SC_WINDOW).transpose(1, 0, 2)
    rows = _dispatch_rows(h2, dest_c, n_blocks * ROW_BLOCK)
    out = _experts(rows, block_exp, n_live, layer, w1, b1, w2, b2)
    return _collect_rows(out, dest_c, t)


def _final_kernel(x_ref, rows_ref, gate_ref, g2_ref, fg_ref, o_ref):
    x = _moe_residual(x_ref, rows_ref, gate_ref, g2_ref)
    ms = jnp.mean(x * x, axis=-1, keepdims=True)
    o_ref[0] = x * lax.rsqrt(ms + NORM_EPS) * fg_ref[...]


def _final(x, res, final_g, tm=512):
    b, s, d = x.shape
    tm = min(tm, s)
    row = pl.BlockSpec((1, tm, d), lambda i, j: (i, j, 0))
    return pl.pallas_call(
        _final_kernel,
        grid=(b, s // tm),
        in_specs=[row] + _moe_residual_specs(tm, d) + [pl.BlockSpec((1, d), lambda i, j: (0, 0))],
        out_specs=row,
        out_shape=jax.ShapeDtypeStruct((b, s, d), F32),
        compiler_params=_params(("parallel", "parallel")),
        name="final",
    )(x, *res, final_g.reshape(1, d))


def _widen_w_in(w):
    gl0 = 2 * A_WIDTH + B_HEADS * DK + 6 * KV_GROUPS * DK
    per_group = HPG * 3
    gl_groups = [jnp.pad(w[:, gl0 + g * per_group:gl0 + (g + 1) * per_group], ((0, 0), (0, LANES - per_group)))
                 for g in range(KV_GROUPS)]
    wide = jnp.concatenate([w[:, :gl0]] + gl_groups, axis=1)
    assert gl0 == C_GL and wide.shape[1] == P_WIDE
    return wide.astype(BF16)


def _rope_tables(pos):
    half = DK // 2
    inv = ROPE_THETA ** (-jnp.arange(half, dtype=F32) / half)
    ang = pos.astype(F32)[..., None] * inv
    cos = jnp.cos(ang)
    sin = jnp.sin(ang)
    reps = LANES // DK
    return (jnp.concatenate([cos, cos] * reps, axis=-1),
            jnp.concatenate([-sin, sin] * reps, axis=-1))


def kernel(x, c, positions, ada_w, ada_b, norm1_g, norm2_g, w_in, w_out, sg_ln_g, sg_ln_b, sg_w, sg_b,
           cmp_pe_k, cmp_pe_v, cmp_w1_k, cmp_w2_k, cmp_w1_v, cmp_w2_v, router_w, router_b,
           exp_w1, exp_b1, exp_w2, exp_b2, final_g):
    b, s, d = x.shape
    n_layers = ada_w.shape[0]
    mod = _ada_mod(c, ada_w, ada_b)
    cos, sin = _rope_tables(positions)
    cmp_end = jnp.minimum(CMP_STRIDE * jnp.arange(s // CMP_STRIDE) + CMP_BLOCK - 1, s - 1)
    cos_c, sin_c = _rope_tables(positions[:, cmp_end])
    res = None
    for l in range(n_layers):
        sh1, sc1, g1, sh2, sc2, g2 = [m.reshape(b, 1, d) for m in jnp.split(mod[l], 6, axis=-1)]
        x, uv, qt, kc, vc, ks, vs_t, kw, vw_t, gl_t = _pre(x, res, norm1_g[l], sc1, sh1,
                                                          _widen_w_in(w_in[l]), cos, sin)
        y_a = _sgu(uv, sg_ln_g[l], sg_ln_b[l], sg_w[l], sg_b[l])
        kcmp, vcmp_t = _compress(kc, vc, cmp_pe_k[l], cmp_pe_v[l], cmp_w1_k[l], cmp_w2_k[l],
                                 cmp_w1_v[l], cmp_w2_v[l], cos_c, sin_c)
        oc_t, bias_t = _cmp_sel(qt, kcmp, vcmp_t)
        y_b = _attn(qt, ks, kw, vs_t, vw_t, bias_t, oc_t, gl_t)
        x, h2, idx, gate, rank = _post(y_a, y_b, w_out[l], x, g1, norm2_g[l], sc2, sh2,
                                       router_w[l], router_b[l])
        rows = _moe_routed(h2.reshape(b * s, d // 2), idx, rank, l, exp_w1, exp_b1, exp_w2, exp_b2)
        res = (rows.reshape(TOP_K, b, s, d // 2), gate.T.reshape(b, s, TOP_K), g2)
    return _final(x, res, final_g)
```

```python
import functools

import numpy as np
import jax
import jax.numpy as jnp
from jax import lax
from jax.experimental import pallas as pl
from jax.experimental.pallas import tpu as pltpu
from jax.experimental.pallas import tpu_sc as plsc

F32 = jnp.float32
BF16 = jnp.bfloat16
HIGHEST = lax.Precision.HIGHEST

D_MODEL = 1024
A_WIDTH = 512
A_HEADS = 8
CHUNK = 128
B_HEADS = 8
DK = 64
KV_GROUPS = 2
HPG = B_HEADS // KV_GROUPS
CMP_BLOCK = 32
CMP_STRIDE = 16
SEL_BLOCK = 64
SEL_TOP = 16
WINDOW = 512
ROPE_THETA = 10000.0
N_EXPERTS = 32
TOP_K = 4
D_EXPERT = 1024
SWIGLU_LIMIT = 7.0
SWIGLU_ALPHA = 1.702
NORM_EPS = 1e-6

LANES = 128
SUBLANES = 8
GW = HPG * DK
N_FORCED = 3
ONES_ROWS = 16
MASK_BIAS = -32768.0
Q_SCALE = DK ** -0.5 * 1.4426950408889634
VMEM_LIMIT = 56 * 1024 * 1024
ROW_BLOCK = 256
SC_CORES = 2
SC_SUBCORES = 16
SC_WORKERS = SC_CORES * SC_SUBCORES
SC_WINDOW = 128

C_U, C_V, C_Q = 0, 512, 1024
C_KC, C_VC, C_KS, C_VS, C_KW, C_VW = 1536, 1664, 1792, 1920, 2048, 2176
C_GL = 2304
P_WIDE = 2560


def _params(sem):
    return pltpu.CompilerParams(dimension_semantics=sem, vmem_limit_bytes=VMEM_LIMIT)


def _split_bf16(x, parts):
    out = []
    for _ in range(parts):
        piece = x.astype(BF16)
        out.append(piece)
        x = x - piece.astype(F32)
    return out


def _ada_kernel(c_ref, w_ref, b_ref, o_ref):
    c = c_ref[...]
    cond = c * jax.nn.sigmoid(c)
    o_ref[0] = jnp.dot(cond, w_ref[0], precision=HIGHEST, preferred_element_type=F32) + b_ref[0]


def _ada_mod(c, ada_w, ada_b):
    n_layers, d, d6 = ada_w.shape
    b = c.shape[0]
    rows = SUBLANES
    c_pad = jnp.zeros((rows, d), F32).at[:b].set(c)
    out = pl.pallas_call(
        _ada_kernel,
        grid=(n_layers, d6 // d),
        in_specs=[
            pl.BlockSpec((rows, d), lambda l, j: (0, 0)),
            pl.BlockSpec((1, d, d), lambda l, j: (l, 0, j)),
            pl.BlockSpec((1, 1, d), lambda l, j: (l, 0, j)),
        ],
        out_specs=pl.BlockSpec((1, rows, d), lambda l, j: (l, 0, j)),
        out_shape=jax.ShapeDtypeStruct((n_layers, rows, d6), F32),
        compiler_params=_params(("arbitrary", "arbitrary")),
        name="ada",
    )(c_pad, ada_w, ada_b.reshape(n_layers, 1, d6))
    return out[:, :b]


def _rope_slab(t, cos, sin_signed, lo):
    partner = jnp.where(lo, pltpu.roll(t, LANES - DK // 2, 1), pltpu.roll(t, DK // 2, 1))
    return t * cos + partner * sin_signed


def _rope(t, cos, sin_signed):
    lane = lax.broadcasted_iota(jnp.int32, (1, LANES), 1)
    lo = (lane % DK) < (DK // 2)
    slabs = [_rope_slab(t[:, s * LANES:(s + 1) * LANES], cos, sin_signed, lo)
             for s in range(t.shape[1] // LANES)]
    return slabs[0] if len(slabs) == 1 else jnp.concatenate(slabs, axis=1)


def _pack_halves(x):
    n = x.shape[1] // 2
    lo = lax.bitcast_convert_type(x[:, :n].astype(BF16).astype(F32), jnp.int32)
    hi = lax.bitcast_convert_type(x[:, n:].astype(BF16).astype(F32), jnp.int32)
    return lax.shift_right_logical(lo, jnp.int32(16)) | (hi & jnp.int32(-65536))


def _unpack_halves(p):
    lo = lax.bitcast_convert_type(lax.shift_left(p, jnp.int32(16)), F32)
    hi = lax.bitcast_convert_type(p & jnp.int32(-65536), F32)
    return jnp.concatenate([lo, hi], axis=1)


def _moe_residual(x_ref, rows_ref, gate_ref, g2_ref):
    gate = gate_ref[0]
    y = gate[:, 0:1] * _unpack_halves(rows_ref[0, 0])
    for k in range(1, TOP_K):
        y = y + gate[:, k:k + 1] * _unpack_halves(rows_ref[k, 0])
    return x_ref[0] + g2_ref[0] * y


def _moe_residual_specs(tm, d):
    return [pl.BlockSpec((TOP_K, 1, tm, d // 2), lambda i, j: (0, i, j, 0)),
            pl.BlockSpec((1, tm, TOP_K), lambda i, j: (i, j, 0)),
            pl.BlockSpec((1, 1, d), lambda i, j: (i, 0, 0))]


def _pre_kernel(has_res, *refs):
    if has_res:
        (x_ref, rows_ref, gate_ref, g2_ref, ng_ref, sc_ref, sh_ref, w_ref, cos_ref, sin_ref,
         xo_ref, uv_ref, qt_ref, kc_ref, vc_ref, ks_ref, vst_ref, kw_ref, vwt_ref, glt_ref) = refs
        x = _moe_residual(x_ref, rows_ref, gate_ref, g2_ref)
        xo_ref[0] = x
    else:
        (x_ref, ng_ref, sc_ref, sh_ref, w_ref, cos_ref, sin_ref,
         uv_ref, qt_ref, kc_ref, vc_ref, ks_ref, vst_ref, kw_ref, vwt_ref, glt_ref) = refs
        x = x_ref[0]
    ms = jnp.mean(x * x, axis=-1, keepdims=True)
    h = x * lax.rsqrt(ms + NORM_EPS) * ng_ref[...]
    h = h * (1.0 + sc_ref[0]) + sh_ref[0]
    proj = jnp.dot(h.astype(BF16), w_ref[...], preferred_element_type=F32)
    cos = cos_ref[0]
    sin = sin_ref[0]
    uv_ref[0] = proj[:, C_U:C_Q]
    qt_ref[0] = (_rope(proj[:, C_Q:C_KC], cos, sin) * Q_SCALE).T
    kc_ref[0] = proj[:, C_KC:C_VC]
    vc_ref[0] = proj[:, C_VC:C_KS]
    ks_ref[0] = _rope(proj[:, C_KS:C_VS], cos, sin).astype(BF16)
    vst_ref[0] = proj[:, C_VS:C_KW].T.astype(BF16)
    kw_ref[0] = _rope(proj[:, C_KW:C_VW], cos, sin).astype(BF16)
    vwt_ref[0] = proj[:, C_VW:C_GL].T.astype(BF16)
    glt_ref[0] = proj[:, C_GL:P_WIDE].T


def _pre(x, res, norm_g, sc, sh, w_wide, cos, sin, tm=512):
    b, s, d = x.shape
    tm = min(tm, s)
    row = lambda w: pl.BlockSpec((1, tm, w), lambda i, j: (i, j, 0))
    col = lambda w: pl.BlockSpec((1, w, tm), lambda i, j: (i, 0, j))
    vec = pl.BlockSpec((1, 1, d), lambda i, j: (i, 0, 0))
    in_specs = [row(d)]
    args = [x]
    if res is not None:
        in_specs += _moe_residual_specs(tm, d)
        args += list(res)
    in_specs += [pl.BlockSpec((1, d), lambda i, j: (0, 0)), vec, vec,
                 pl.BlockSpec((d, P_WIDE), lambda i, j: (0, 0)), row(LANES), row(LANES)]
    args += [norm_g.reshape(1, d), sc, sh, w_wide, cos, sin]
    kv = KV_GROUPS * DK
    outs = [(d, F32, False)] if res is not None else []
    outs += [(2 * A_WIDTH, F32, False), (B_HEADS * DK, F32, True), (kv, F32, False), (kv, F32, False),
             (kv, BF16, False), (kv, BF16, True), (kv, BF16, False), (kv, BF16, True),
             (KV_GROUPS * LANES, F32, True)]
    res_out = pl.pallas_call(
        functools.partial(_pre_kernel, res is not None),
        grid=(b, s // tm),
        in_specs=in_specs,
        out_specs=[col(w) if t else row(w) for w, _, t in outs],
        out_shape=[jax.ShapeDtypeStruct((b, w, s) if t else (b, s, w), dt) for w, dt, t in outs],
        compiler_params=_params(("parallel", "parallel")),
        name="pre",
    )(*args)
    if res is None:
        res_out = [x] + list(res_out)
    return res_out


def _sgu_kernel(uv_ref, lng_ref, lnb_ref, w_ref, bias_ref, o_ref):
    rows = uv_ref.shape[1]
    uv = uv_ref[0]
    gu = jax.nn.gelu(uv[:, :A_WIDTH])
    gv = jax.nn.gelu(uv[:, A_WIDTH:])
    mu = jnp.mean(gv, axis=-1, keepdims=True)
    var = jnp.mean(jnp.square(gv - mu), axis=-1, keepdims=True)
    vn = ((gv - mu) * lax.rsqrt(var + NORM_EPS) * lng_ref[...] + lnb_ref[...]).astype(BF16)
    r = lax.broadcasted_iota(jnp.int32, (CHUNK, CHUNK), 0)
    c = lax.broadcasted_iota(jnp.int32, (CHUNK, CHUNK), 1)
    causal = c <= r
    lane_lo = lax.broadcasted_iota(jnp.int32, (CHUNK, LANES), 1) < DK
    for p in range(A_HEADS // 2):
        w0 = jnp.where(causal, w_ref[2 * p], 0.0).astype(BF16)
        w1 = jnp.where(causal, w_ref[2 * p + 1], 0.0).astype(BF16)
        bias = bias_ref[:, p * LANES:(p + 1) * LANES]
        for ch in range(rows // CHUNK):
            rs = slice(ch * CHUNK, (ch + 1) * CHUNK)
            cs = slice(p * LANES, (p + 1) * LANES)
            vp = vn[rs, cs]
            m0 = jnp.dot(w0, vp, preferred_element_type=F32)
            m1 = jnp.dot(w1, vp, preferred_element_type=F32)
            mixed = jnp.where(lane_lo, m0, m1) + bias
            o_ref[0, rs, cs] = (gu[rs, cs] * mixed).astype(o_ref.dtype)


def _sgu(uv, ln_g, ln_b, w_s, b_s, tm=512):
    b, s, _ = uv.shape
    tm = min(tm, s)
    bias = jnp.repeat(b_s.T, A_WIDTH // A_HEADS, axis=1)
    return pl.pallas_call(
        _sgu_kernel,
        grid=(b, s // tm),
        in_specs=[
            pl.BlockSpec((1, tm, 2 * A_WIDTH), lambda i, j: (i, j, 0)),
            pl.BlockSpec((1, A_WIDTH), lambda i, j: (0, 0)),
            pl.BlockSpec((1, A_WIDTH), lambda i, j: (0, 0)),
            pl.BlockSpec((A_HEADS, CHUNK, CHUNK), lambda i, j: (0, 0, 0)),
            pl.BlockSpec((CHUNK, A_WIDTH), lambda i, j: (0, 0)),
        ],
        out_specs=pl.BlockSpec((1, tm, A_WIDTH), lambda i, j: (i, j, 0)),
        out_shape=jax.ShapeDtypeStruct((b, s, A_WIDTH), BF16),
        compiler_params=_params(("parallel", "parallel")),
        name="sgu",
    )(uv, ln_g.reshape(1, -1), ln_b.reshape(1, -1), w_s, bias)


def _compress_kernel(kr_ref, vr_ref, pek_ref, pev_ref, w1k_ref, w1v_ref, w2k_ref, w2v_ref,
                     cos_ref, sin_ref, ko_ref, vt_ref):
    nc = kr_ref.shape[1]

    def mlp(r, pe_ref, w1_ref, w2_ref):
        top = jnp.dot(r + pe_ref[0:1], w1_ref[0], precision=HIGHEST, preferred_element_type=F32)
        bot = jnp.dot(r + pe_ref[1:2], w1_ref[1], precision=HIGHEST, preferred_element_type=F32)
        pre = top + pltpu.roll(bot, nc - 1, 0)
        return jnp.dot(jax.nn.gelu(pre), w2_ref[...], precision=HIGHEST, preferred_element_type=F32)

    kc = _rope(mlp(kr_ref[0], pek_ref, w1k_ref, w2k_ref), cos_ref[0], sin_ref[0])
    vc = mlp(vr_ref[0], pev_ref, w1v_ref, w2v_ref)
    lo = lax.broadcasted_iota(jnp.int32, (nc, LANES), 1) < DK
    rolled = pltpu.roll(kc, DK, 1)
    for g in range(KV_GROUPS):
        dup = jnp.where(lo, kc, rolled) if g == 0 else jnp.where(lo, rolled, kc)
        hi, low = _split_bf16(dup, 2)
        ko_ref[0, g, :, 0:LANES] = hi
        ko_ref[0, g, :, LANES:2 * LANES] = jnp.where(lo, low, jnp.zeros_like(low))
    vt_ref[0] = vc.T.astype(BF16)


def _compress_weights(pe, w1, w2):
    half = CMP_BLOCK // 2
    eye = jnp.eye(KV_GROUPS, dtype=F32)
    w1r = w1.reshape(CMP_BLOCK, DK, DK)
    wfull = jnp.einsum('lde,gh->lgdhe', w1r, eye)
    w1s = wfull.reshape(2, half * KV_GROUPS * DK, KV_GROUPS * DK)
    pes = jnp.broadcast_to(pe.reshape(2, half, 1, DK), (2, half, KV_GROUPS, DK)).reshape(2, -1)
    w2bd = jnp.einsum('de,gh->gdhe', w2, eye).reshape(KV_GROUPS * DK, KV_GROUPS * DK)
    return pes, w1s, w2bd


def _compress(kc, vc, pe_k, pe_v, w1_k, w2_k, w1_v, w2_v, cos_c, sin_c):
    b, s, _ = kc.shape
    nc = s // CMP_STRIDE
    rw = CMP_STRIDE * LANES
    pek, w1ks, w2kb = _compress_weights(pe_k, w1_k, w2_k)
    pev, w1vs, w2vb = _compress_weights(pe_v, w1_v, w2_v)
    full = lambda shape: pl.BlockSpec(shape, lambda i: (0,) * len(shape))
    return pl.pallas_call(
        _compress_kernel,
        grid=(b,),
        in_specs=[
            pl.BlockSpec((1, nc, rw), lambda i: (i, 0, 0)),
            pl.BlockSpec((1, nc, rw), lambda i: (i, 0, 0)),
            full((2, rw)), full((2, rw)),
            full((2, rw, LANES)), full((2, rw, LANES)),
            full((LANES, LANES)), full((LANES, LANES)),
            pl.BlockSpec((1, nc, LANES), lambda i: (i, 0, 0)),
            pl.BlockSpec((1, nc, LANES), lambda i: (i, 0, 0)),
        ],
        out_specs=[pl.BlockSpec((1, KV_GROUPS, nc, 2 * LANES), lambda i: (i, 0, 0, 0)),
                   pl.BlockSpec((1, KV_GROUPS * DK, nc), lambda i: (i, 0, 0))],
        out_shape=[jax.ShapeDtypeStruct((b, KV_GROUPS, nc, 2 * LANES), BF16),
                   jax.ShapeDtypeStruct((b, KV_GROUPS * DK, nc), BF16)],
        compiler_params=_params(("parallel",)),
        name="compress",
    )(kc.reshape(b, nc, rw), vc.reshape(b, nc, rw), pek, pev, w1ks, w1vs, w2kb, w2vb, cos_c, sin_c)


def _heads_on_lanes(qt):
    return jnp.concatenate([qt[h * DK:(h + 1) * DK] for h in range(HPG)], axis=1)


def _cmp_sel_kernel(n_top, qt_ref, k_ref, vt_ref, map_ref, oc_ref, bias_ref):
    tq = qt_ref.shape[2]
    cols = HPG * tq
    nc = k_ref.shape[2]
    nsel = map_ref.shape[0]
    q0 = pl.program_id(2) * tq
    q_hi, q_lo = _split_bf16(_heads_on_lanes(qt_ref[0]), 2)
    q3t = jnp.concatenate([q_hi, q_lo, q_hi, jnp.zeros_like(q_hi)], axis=0)
    s = jnp.dot(k_ref[0, 0], q3t, preferred_element_type=F32)
    n_idx = lax.broadcasted_iota(jnp.int32, (nc, 1), 0)
    t_col = q0 + lax.broadcasted_iota(jnp.int32, (1, cols), 1) % tq
    s = jnp.where((CMP_STRIDE * n_idx + CMP_BLOCK - 1) <= t_col, s, -jnp.inf)
    m = jnp.max(s, axis=0, keepdims=True)
    m = jnp.where(m == -jnp.inf, 0.0, m)
    e = jnp.exp2(s - m)
    d = jnp.sum(e, axis=0, keepdims=True)
    p = e / jnp.where(d > 0, d, 1.0)
    oc = jnp.dot(vt_ref[0], p.astype(BF16), preferred_element_type=F32)
    oc_ref[0] = jnp.concatenate([oc[:, h * tq:(h + 1) * tq] for h in range(HPG)], axis=0)
    psum = p[:, 0:tq]
    for h in range(1, HPG):
        psum = psum + p[:, h * tq:(h + 1) * tq]
    imp = jnp.dot(map_ref[...], jnp.concatenate(_split_bf16(psum, 3), axis=0),
                  preferred_element_type=F32)
    j = lax.broadcasted_iota(jnp.int32, (nsel, tq), 0)
    cur = (q0 + lax.broadcasted_iota(jnp.int32, (nsel, tq), 1)) // SEL_BLOCK
    valid = j <= cur
    forced = (j == 0) | (j == cur) | (j == cur - 1)
    keep = forced | (valid & (cur < n_top))
    vals = jnp.where(valid & jnp.logical_not(forced), imp, -jnp.inf)
    bias = jnp.where(keep, 0.0, MASK_BIAS)
    for _ in range(n_top - N_FORCED):
        mx = jnp.max(vals, axis=0, keepdims=True)
        first = jnp.min(jnp.where(vals == mx, j, nsel), axis=0, keepdims=True)
        pick = (j == first) & (mx > -jnp.inf)
        bias = jnp.where(pick, 0.0, bias)
        vals = jnp.where(pick, -jnp.inf, vals)
    bias_ref[0, 0] = bias.astype(BF16)


def _sel_map_t(s):
    nc = s // CMP_STRIDE
    n_cmp = (s - CMP_BLOCK) // CMP_STRIDE + 1
    n_sel = s // SEL_BLOCK
    cs = CMP_STRIDE * np.arange(n_cmp)[:, None]
    ce = cs + CMP_BLOCK
    ss = SEL_BLOCK * np.arange(n_sel)[None, :]
    se = ss + SEL_BLOCK
    ov = np.clip(np.minimum(ce, se) - np.maximum(cs, ss), 0, None) / CMP_STRIDE
    out = np.zeros((n_sel, nc), np.float32)
    out[:, :n_cmp] = ov.T
    return out


def _cmp_sel(qt, kcmp, vcmp_t, tq=256):
    b, _, s = qt.shape
    nc = s // CMP_STRIDE
    nsel = s // SEL_BLOCK
    n_top = min(SEL_TOP, nsel)
    sel_map = jnp.asarray(np.tile(_sel_map_t(s), (1, 3)), dtype=BF16)
    return pl.pallas_call(
        functools.partial(_cmp_sel_kernel, n_top),
        grid=(b, KV_GROUPS, s // tq),
        in_specs=[
            pl.BlockSpec((1, GW, tq), lambda i, g, j: (i, g, j)),
            pl.BlockSpec((1, 1, nc, 2 * LANES), lambda i, g, j: (i, g, 0, 0)),
            pl.BlockSpec((1, DK, nc), lambda i, g, j: (i, g, 0)),
            pl.BlockSpec((nsel, 3 * nc), lambda i, g, j: (0, 0)),
        ],
        out_specs=[
            pl.BlockSpec((1, GW, tq), lambda i, g, j: (i, g, j)),
            pl.BlockSpec((1, 1, nsel, tq), lambda i, g, j: (i, g, 0, j)),
        ],
        out_shape=[
            jax.ShapeDtypeStruct((b, B_HEADS * DK, s), F32),
            jax.ShapeDtypeStruct((b, KV_GROUPS, nsel, s), BF16),
        ],
        compiler_params=_params(("parallel", "parallel", "parallel")),
        name="cmp_sel",
    )(qt, kcmp, vcmp_t, sel_map)


def _attn_kernel(kt, qt_ref, ks_ref, kw_ref, vst_ref, vwt_ref, bias_ref, oh_ref, oc_ref, glt_ref,
                 o_ref, qa_ref, s0_ref, s1_ref, m_ref, acc_ref, ow_ref):
    tq = qt_ref.shape[2]
    cols = HPG * tq
    nsel = bias_ref.shape[2]
    g = pl.program_id(1)
    q0 = pl.program_id(2) * tq

    q4t = _heads_on_lanes(qt_ref[0]).astype(BF16)
    zero = jnp.zeros_like(q4t)
    qa_ref[0:DK] = jnp.where(g == 0, q4t, zero)
    qa_ref[DK:2 * DK] = jnp.where(g == 1, q4t, zero)
    qa_ref[LANES:LANES + nsel] = jnp.concatenate([bias_ref[0, 0]] * HPG, axis=1)
    qaug = qa_ref[...]
    t_col = q0 + lax.broadcasted_iota(jnp.int32, (1, cols), 1) % tq

    def fold(x, op):
        return op(x.reshape(x.shape[0] // SUBLANES, SUBLANES, cols), axis=0)

    def with_ones(vt):
        return jnp.concatenate([vt, jnp.ones((ONES_ROWS, vt.shape[1]), BF16)], axis=0)

    def stage(c, s_ref):
        k0 = pl.multiple_of(c * kt, kt)
        kaug = jnp.concatenate([ks_ref[0, pl.ds(k0, kt), :], oh_ref[pl.ds(k0, kt), :]], axis=1)
        s_ref[...] = jnp.dot(kaug, qaug, preferred_element_type=F32)
        r0 = pl.multiple_of(jnp.clip(q0 - k0, 0, kt - tq), tq)
        kpos = k0 + r0 + lax.broadcasted_iota(jnp.int32, (tq, 1), 0)
        s_ref[pl.ds(r0, tq), :] = jnp.where(kpos <= t_col, s_ref[pl.ds(r0, tq), :], -jnp.inf)

    def consume(c, s_ref):
        k0 = pl.multiple_of(c * kt, kt)
        sc = s_ref[...]
        m_old = m_ref[...]
        m_new = jnp.maximum(m_old, jnp.max(fold(sc, jnp.max), axis=0, keepdims=True))
        alpha = jnp.exp2(m_old - m_new)
        pr = jnp.exp2(sc - m_new)
        acc_ref[...] = alpha * acc_ref[...] + jnp.dot(with_ones(vst_ref[0, :, pl.ds(k0, kt)]), pr.astype(BF16),
                                                      preferred_element_type=F32)
        m_ref[...] = m_new

    m_ref[...] = jnp.full(m_ref.shape, -jnp.inf, F32)
    acc_ref[...] = jnp.zeros(acc_ref.shape, F32)
    n_full = q0 // kt
    n_pairs = n_full // 2
    odd = n_full % 2 == 1
    stage(0, s0_ref)

    wlen = WINDOW + tq
    w0 = pl.multiple_of(jnp.maximum(q0 - WINDOW, 0), tq)
    sw = jnp.dot(kw_ref[0, pl.ds(w0, wlen), :], qaug[0:LANES], preferred_element_type=F32)
    dlt = t_col - (w0 + lax.broadcasted_iota(jnp.int32, (wlen, 1), 0))
    sw = jnp.where((dlt >= 0) & (dlt < WINDOW), sw, -jnp.inf)
    pw = jnp.exp2(sw - jnp.max(sw, axis=0, keepdims=True))
    o_win = jnp.dot(with_ones(vwt_ref[0, :, pl.ds(w0, wlen)]), pw.astype(BF16), preferred_element_type=F32)
    ow_ref[...] = o_win[0:DK] / o_win[DK:DK + 1]

    def pair(i, carry):
        stage(2 * i + 1, s1_ref)
        consume(2 * i, s0_ref)
        stage(2 * i + 2, s0_ref)
        consume(2 * i + 1, s1_ref)
        return carry

    lax.fori_loop(0, n_pairs, pair, 0)

    @pl.when(odd)
    def _():
        stage(n_full, s1_ref)

    consume(2 * n_pairs, s0_ref)

    @pl.when(odd)
    def _():
        consume(n_full, s1_ref)

    o_sel = acc_ref[0:DK] / acc_ref[DK:DK + 1]

    o_win = ow_ref[...]
    gates = jax.nn.sigmoid(glt_ref[0])
    oc = _heads_on_lanes(oc_ref[0])
    for p in range(HPG // 2):
        halves = []
        for hh in (2 * p, 2 * p + 1):
            cs = slice(hh * tq, (hh + 1) * tq)
            halves.append(gates[3 * hh:3 * hh + 1] * oc[:, cs] + gates[3 * hh + 1:3 * hh + 2] * o_sel[:, cs]
                          + gates[3 * hh + 2:3 * hh + 3] * o_win[:, cs])
        o_ref[0, :, p * LANES:(p + 1) * LANES] = jnp.concatenate(halves, axis=0).T.astype(o_ref.dtype)


def _attn(qt, ks, kw, vs_t, vw_t, bias_t, oc_t, gl_t, tq=256, kt=512):
    b, _, s = qt.shape
    nsel = s // SEL_BLOCK
    kt = min(kt, s)
    onehot = jnp.asarray((np.arange(s)[:, None] // SEL_BLOCK == np.arange(nsel)[None, :]), dtype=BF16)
    k_spec = pl.BlockSpec((1, s, KV_GROUPS * DK), lambda i, g, j: (i, 0, 0))
    vt_spec = pl.BlockSpec((1, DK, s), lambda i, g, j: (i, g, 0))
    cols = HPG * tq
    return pl.pallas_call(
        functools.partial(_attn_kernel, kt),
        grid=(b, KV_GROUPS, s // tq),
        in_specs=[
            pl.BlockSpec((1, GW, tq), lambda i, g, j: (i, g, j)),
            k_spec, k_spec, vt_spec, vt_spec,
            pl.BlockSpec((1, 1, nsel, tq), lambda i, g, j: (i, g, 0, j)),
            pl.BlockSpec((s, nsel), lambda i, g, j: (0, 0)),
            pl.BlockSpec((1, GW, tq), lambda i, g, j: (i, g, j)),
            pl.BlockSpec((1, LANES, tq), lambda i, g, j: (i, g, j)),
        ],
        out_specs=pl.BlockSpec((1, tq, GW), lambda i, g, j: (i, j, g)),
        out_shape=jax.ShapeDtypeStruct((b, s, B_HEADS * DK), BF16),
        scratch_shapes=[pltpu.VMEM((LANES + nsel, cols), BF16),
                        pltpu.VMEM((kt, cols), F32), pltpu.VMEM((kt, cols), F32),
                        pltpu.VMEM((1, cols), F32), pltpu.VMEM((DK + ONES_ROWS, cols), F32),
                        pltpu.VMEM((DK, cols), F32)],
        compiler_params=_params(("parallel", "parallel", "arbitrary")),
        name="attn",
    )(qt, ks, kw, vs_t, vw_t, bias_t, onehot, oc_t, gl_t)


def _post_kernel(ya_ref, yb_ref, wo_ref, x_ref, g1_ref, ng_ref, sc_ref, sh_ref, rw_ref, rb_ref,
                 xo_ref, h_ref, idx_ref, gate_ref, rank_ref, cnt_ref):
    tm = x_ref.shape[1]

    @pl.when((pl.program_id(0) == 0) & (pl.program_id(1) == 0))
    def _():
        cnt_ref[...] = jnp.zeros(cnt_ref.shape, F32)

    mixed = jnp.dot(ya_ref[0], wo_ref[:A_WIDTH], preferred_element_type=F32)
    mixed = mixed + jnp.dot(yb_ref[0], wo_ref[A_WIDTH:], preferred_element_type=F32)
    x = x_ref[0] + g1_ref[0] * mixed
    xo_ref[0] = x
    ms = jnp.mean(x * x, axis=-1, keepdims=True)
    h = x * lax.rsqrt(ms + NORM_EPS) * ng_ref[...]
    h = h * (1.0 + sc_ref[0]) + sh_ref[0]
    h_ref[0] = _pack_halves(h)
    h_hi, h_mid = _split_bf16(h, 2)
    logits = jnp.dot(jnp.concatenate([h_hi, h_hi, h_mid], axis=1), rw_ref[...],
                     preferred_element_type=F32) + rb_ref[...]
    lane = lax.broadcasted_iota(jnp.int32, logits.shape, 1)
    vals = logits
    top = jnp.max(vals, axis=-1, keepdims=True)
    picked = jnp.zeros_like(logits)
    slot_idx = jnp.zeros_like(logits)
    slot_exp = jnp.zeros_like(logits)
    for k in range(TOP_K):
        m = jnp.max(vals, axis=-1, keepdims=True)
        first = jnp.min(jnp.where(vals == m, lane, LANES), axis=-1, keepdims=True)
        pick = lane == first
        picked = jnp.where(pick, 1.0, picked)
        slot_idx = jnp.where(lane == k, first.astype(F32), slot_idx)
        slot_exp = jnp.where(lane == k, jnp.exp(m - top), slot_exp)
        vals = jnp.where(pick, -jnp.inf, vals)
    slot_gate = slot_exp / jnp.sum(slot_exp, axis=-1, keepdims=True)
    idx_t = slot_idx.T[:TOP_K]
    idx_ref[...] = idx_t.astype(jnp.int32)
    gate_ref[...] = slot_gate.T[:TOP_K]
    routed_t = picked.T[:N_EXPERTS]
    r = lax.broadcasted_iota(jnp.int32, (tm, tm), 0)
    c = lax.broadcasted_iota(jnp.int32, (tm, tm), 1)
    before = jnp.where(r < c, 1.0, 0.0).astype(BF16)
    rank_t = cnt_ref[...] + jnp.dot(routed_t.astype(BF16), before, preferred_element_type=F32)
    cnt_ref[...] += jnp.sum(routed_t, axis=-1, keepdims=True)
    e_row = lax.broadcasted_iota(jnp.int32, (N_EXPERTS, tm), 0).astype(F32)
    ranks = [jnp.sum(jnp.where(e_row == idx_t[k:k + 1], rank_t, 0.0), axis=0, keepdims=True)
             for k in range(TOP_K)]
    rank_ref[...] = jnp.concatenate(ranks, axis=0).astype(jnp.int32)


def _post(y_a, y_b, w_out, x, g1, norm_g, sc, sh, router_w, router_b, tm=512):
    b, s, d = x.shape
    tm = min(tm, s)
    nt = s // tm
    w_hi, w_mid = _split_bf16(jnp.zeros((d, LANES), F32).at[:, :N_EXPERTS].set(router_w), 2)
    rw = jnp.concatenate([w_hi, w_mid, w_hi], axis=0)
    rb = jnp.full((1, LANES), -1e30, F32).at[0, :N_EXPERTS].set(router_b)
    vec = pl.BlockSpec((1, 1, d), lambda i, j: (i, 0, 0))
    row = lambda w: pl.BlockSpec((1, tm, w), lambda i, j: (i, j, 0))
    return pl.pallas_call(
        _post_kernel,
        grid=(b, nt),
        in_specs=[
            row(A_WIDTH), row(B_HEADS * DK),
            pl.BlockSpec((d, d), lambda i, j: (0, 0)),
            row(d), vec,
            pl.BlockSpec((1, d), lambda i, j: (0, 0)), vec, vec,
            pl.BlockSpec((3 * d, LANES), lambda i, j: (0, 0)),
            pl.BlockSpec((1, LANES), lambda i, j: (0, 0)),
        ],
        out_specs=[row(d), row(d // 2)] + [pl.BlockSpec((TOP_K, tm), lambda i, j: (0, i * nt + j))] * 3,
        out_shape=[
            jax.ShapeDtypeStruct((b, s, d), F32),
            jax.ShapeDtypeStruct((b, s, d // 2), jnp.int32),
            jax.ShapeDtypeStruct((TOP_K, b * s), jnp.int32),
            jax.ShapeDtypeStruct((TOP_K, b * s), F32),
            jax.ShapeDtypeStruct((TOP_K, b * s), jnp.int32),
        ],
        scratch_shapes=[pltpu.VMEM((N_EXPERTS, 1), F32)],
        compiler_params=_params(("arbitrary", "arbitrary")),
        name="post",
    )(y_a, y_b, w_out.astype(BF16), x, g1, norm_g.reshape(1, d), sc, sh, rw, rb)


def _sc_mesh():
    return plsc.VectorSubcoreMesh(core_axis_name="c", subcore_axis_name="s")


def _sc_worker():
    return lax.axis_index("c") * SC_SUBCORES + lax.axis_index("s")


def _dispatch_rows(h2, dest_c, n_rows):
    t, d = h2.shape
    per_worker = t // SC_WINDOW // SC_WORKERS

    def body(x_hbm, i_hbm, o_hbm, buf, idx):
        worker = _sc_worker()

        @pl.loop(0, per_worker)
        def _(j):
            ch = worker * per_worker + j
            pltpu.sync_copy(i_hbm.at[ch], idx)
            pltpu.sync_copy(x_hbm.at[pl.ds(ch * SC_WINDOW, SC_WINDOW)], buf)
            for k in range(TOP_K):
                pltpu.sync_copy(buf, o_hbm.at[idx.at[k]])

    return pl.kernel(
        body, out_type=jax.ShapeDtypeStruct((n_rows, d), h2.dtype), mesh=_sc_mesh(),
        scratch_types=[pltpu.VMEM((SC_WINDOW, d), h2.dtype), pltpu.VMEM((TOP_K, SC_WINDOW), jnp.int32)],
        name="dispatch_rows",
    )(h2, dest_c)


def _collect_rows(rows, dest_c, t):
    d = rows.shape[1]
    per_worker = t // SC_WINDOW // SC_WORKERS

    def body(r_hbm, i_hbm, o_hbm, buf, idx):
        worker = _sc_worker()

        @pl.loop(0, per_worker)
        def _(j):
            ch = worker * per_worker + j
            pltpu.sync_copy(i_hbm.at[ch], idx)
            for k in range(TOP_K):
                pltpu.sync_copy(r_hbm.at[idx.at[k]], buf)
                pltpu.sync_copy(buf, o_hbm.at[k, pl.ds(ch * SC_WINDOW, SC_WINDOW)])

    return pl.kernel(
        body, out_type=jax.ShapeDtypeStruct((TOP_K, t, d), rows.dtype), mesh=_sc_mesh(),
        scratch_types=[pltpu.VMEM((SC_WINDOW, d), rows.dtype), pltpu.VMEM((TOP_K, SC_WINDOW), jnp.int32)],
        name="collect_rows",
    )(rows, dest_c)


def _experts_kernel(be_ref, nb_ref, x_ref, w1_ref, b1_ref, w2_ref, b2_ref, o_ref, w1b_ref, w2b_ref):
    i = pl.program_id(0)
    e = be_ref[i]
    live = i < nb_ref[0]

    @pl.when(live & ((i == 0) | (e != be_ref[jnp.maximum(i - 1, 0)])))
    def _():
        w1b_ref[...] = w1_ref[0, 0].astype(BF16)
        w2b_ref[...] = w2_ref[0, 0].astype(BF16)

    @pl.when(live)
    def _():
        x = _unpack_halves(x_ref[...]).astype(BF16)
        hdn = jnp.dot(x, w1b_ref[...], preferred_element_type=F32) + b1_ref[0, 0]
        glu = jnp.minimum(hdn[:, :D_EXPERT], SWIGLU_LIMIT)
        lin = jnp.clip(hdn[:, D_EXPERT:], -SWIGLU_LIMIT, SWIGLU_LIMIT)
        act = glu * jax.nn.sigmoid(SWIGLU_ALPHA * glu) * (lin + 1.0)
        out = jnp.dot(act.astype(BF16), w2b_ref[...], preferred_element_type=F32) + b2_ref[0, 0]
        o_ref[...] = _pack_halves(out)


def _experts(rows, block_exp, n_live, layer, w1, b1, w2, b2):
    n_rows, dp = rows.shape
    n_layers, n_e, d, f2 = w1.shape
    n_blocks = n_rows // ROW_BLOCK
    grid_spec = pltpu.PrefetchScalarGridSpec(
        num_scalar_prefetch=2,
        grid=(n_blocks,),
        in_specs=[
            pl.BlockSpec((ROW_BLOCK, dp), lambda i, be, nb: (i, 0)),
            pl.BlockSpec((1, 1, d, f2), lambda i, be, nb: (layer, be[i], 0, 0)),
            pl.BlockSpec((1, 1, 1, f2), lambda i, be, nb: (layer, be[i], 0, 0)),
            pl.BlockSpec((1, 1, f2 // 2, d), lambda i, be, nb: (layer, be[i], 0, 0)),
            pl.BlockSpec((1, 1, 1, d), lambda i, be, nb: (layer, be[i], 0, 0)),
        ],
        out_specs=pl.BlockSpec((ROW_BLOCK, dp), lambda i, be, nb: (i, 0)),
        scratch_shapes=[pltpu.VMEM((d, f2), BF16), pltpu.VMEM((f2 // 2, d), BF16)],
    )
    return pl.pallas_call(
        _experts_kernel,
        grid_spec=grid_spec,
        out_shape=jax.ShapeDtypeStruct((n_rows, dp), jnp.int32),
        compiler_params=_params(("arbitrary",)),
        name="experts",
    )(block_exp, n_live, rows, w1, b1.reshape(n_layers, n_e, 1, f2), w2, b2.reshape(n_layers, n_e, 1, d))


def _moe_routed(h2, idx, rank, layer, w1, b1, w2, b2):
    t, d = h2.shape
    n_e = w1.shape[1]
    n_blocks = t * TOP_K // ROW_BLOCK + n_e
    onehot = (idx[:, :, None] == jnp.arange(n_e, dtype=jnp.int32)).astype(jnp.int32)
    counts = jnp.sum(onehot, axis=(0, 1))
    padded = (counts + ROW_BLOCK - 1) // ROW_BLOCK * ROW_BLOCK
    pad_end = jnp.cumsum(padded)
    dest = jnp.sum(onehot * (pad_end - padded), axis=-1) + rank
    block_row = jnp.arange(n_blocks, dtype=jnp.int32)[:, None] * ROW_BLOCK
    block_exp = jnp.minimum(jnp.sum((pad_end[None, :] <= block_row).astype(jnp.int32), axis=-1), n_e - 1)
    n_live = (pad_end[-1:] // ROW_BLOCK).astype(jnp.int32)
    dest_c = dest.reshape(TOP_K, t // SC_WINDOW, SC_WINDOW).transpose(1, 0, 2)
    rows = _dispatch_rows(h2, dest_c, n_blocks * ROW_BLOCK)
    out = _experts(rows, block_exp, n_live, layer, w1, b1, w2, b2)
    return _collect_rows(out, dest_c, t)


def _final_kernel(x_ref, rows_ref, gate_ref, g2_ref, fg_ref, o_ref):
    x = _moe_residual(x_ref, rows_ref, gate_ref, g2_ref)
    ms = jnp.mean(x * x, axis=-1, keepdims=True)
    o_ref[0] = x * lax.rsqrt(ms + NORM_EPS) * fg_ref[...]


def _final(x, res, final_g, tm=512):
    b, s, d = x.shape
    tm = min(tm, s)
    row = pl.BlockSpec((1, tm, d), lambda i, j: (i, j, 0))
    return pl.pallas_call(
        _final_kernel,
        grid=(b, s // tm),
        in_specs=[row] + _moe_residual_specs(tm, d) + [pl.BlockSpec((1, d), lambda i, j: (0, 0))],
        out_specs=row,
        out_shape=jax.ShapeDtypeStruct((b, s, d), F32),
        compiler_params=_params(("parallel", "parallel")),
        name="final",
    )(x, *res, final_g.reshape(1, d))


def _widen_w_in(w):
    gl0 = 2 * A_WIDTH + B_HEADS * DK + 6 * KV_GROUPS * DK
    per_group = HPG * 3
    gl_groups = [jnp.pad(w[:, gl0 + g * per_group:gl0 + (g + 1) * per_group], ((0, 0), (0, LANES - per_group)))
                 for g in range(KV_GROUPS)]
    wide = jnp.concatenate([w[:, :gl0]] + gl_groups, axis=1)
    assert gl0 == C_GL and wide.shape[1] == P_WIDE
    return wide.astype(BF16)


def _rope_tables(pos):
    half = DK // 2
    inv = ROPE_THETA ** (-jnp.arange(half, dtype=F32) / half)
    ang = pos.astype(F32)[..., None] * inv
    cos = jnp.cos(ang)
    sin = jnp.sin(ang)
    reps = LANES // DK
    return (jnp.concatenate([cos, cos] * reps, axis=-1),
            jnp.concatenate([-sin, sin] * reps, axis=-1))


def kernel(x, c, positions, ada_w, ada_b, norm1_g, norm2_g, w_in, w_out, sg_ln_g, sg_ln_b, sg_w, sg_b,
           cmp_pe_k, cmp_pe_v, cmp_w1_k, cmp_w2_k, cmp_w1_v, cmp_w2_v, router_w, router_b,
           exp_w1, exp_b1, exp_w2, exp_b2, final_g):
    b, s, d = x.shape
    n_layers = ada_w.shape[0]
    mod = _ada_mod(c, ada_w, ada_b)
    cos, sin = _rope_tables(positions)
    cmp_end = jnp.minimum(CMP_STRIDE * jnp.arange(s // CMP_STRIDE) + CMP_BLOCK - 1, s - 1)
    cos_c, sin_c = _rope_tables(positions[:, cmp_end])
    res = None
    for l in range(n_layers):
        sh1, sc1, g1, sh2, sc2, g2 = [m.reshape(b, 1, d) for m in jnp.split(mod[l], 6, axis=-1)]
        x, uv, qt, kc, vc, ks, vs_t, kw, vw_t, gl_t = _pre(x, res, norm1_g[l], sc1, sh1,
                                                          _widen_w_in(w_in[l]), cos, sin)
        y_a = _sgu(uv, sg_ln_g[l], sg_ln_b[l], sg_w[l], sg_b[l])
        kcmp, vcmp_t = _compress(kc, vc, cmp_pe_k[l], cmp_pe_v[l], cmp_w1_k[l], cmp_w2_k[l],
                                 cmp_w1_v[l], cmp_w2_v[l], cos_c, sin_c)
        oc_t, bias_t = _cmp_sel(qt, kcmp, vcmp_t)
        y_b = _attn(qt, ks, kw, vs_t, vw_t, bias_t, oc_t, gl_t)
        x, h2, idx, gate, rank = _post(y_a, y_b, w_out[l], x, g1, norm2_g[l], sc2, sh2,
                                       router_w[l], router_b[l])
        rows = _moe_routed(h2.reshape(b * s, d // 2), idx, rank, l, exp_w1, exp_b1, exp_w2, exp_b2)
        res = (rows.reshape(TOP_K, b, s, d // 2), gate.T.reshape(b, s, TOP_K), g2)
    return _final(x, res, final_g)
```

```python
import functools

import numpy as np
import jax
import jax.numpy as jnp
from jax import lax
from jax.experimental import pallas as pl
from jax.experimental.pallas import tpu as pltpu
from jax.experimental.pallas import tpu_sc as plsc

F32 = jnp.float32
BF16 = jnp.bfloat16
HIGHEST = lax.Precision.HIGHEST

D_MODEL = 1024
A_WIDTH = 512
A_HEADS = 8
CHUNK = 128
B_HEADS = 8
DK = 64
KV_GROUPS = 2
HPG = B_HEADS // KV_GROUPS
CMP_BLOCK = 32
CMP_STRIDE = 16
SEL_BLOCK = 64
SEL_TOP = 16
WINDOW = 512
ROPE_THETA = 10000.0
N_EXPERTS = 32
TOP_K = 4
D_EXPERT = 1024
SWIGLU_LIMIT = 7.0
SWIGLU_ALPHA = 1.702
NORM_EPS = 1e-6

LANES = 128
SUBLANES = 8
GW = HPG * DK
N_FORCED = 3
ONES_ROWS = 16
MASK_BIAS = -32768.0
Q_SCALE = DK ** -0.5 * 1.4426950408889634
VMEM_LIMIT = 56 * 1024 * 1024
ROW_BLOCK = 256
SC_CORES = 2
SC_SUBCORES = 16
SC_WORKERS = SC_CORES * SC_SUBCORES
SC_WINDOW = 128

C_U, C_V, C_Q = 0, 512, 1024
C_KC, C_VC, C_KS, C_VS, C_KW, C_VW = 1536, 1664, 1792, 1920, 2048, 2176
C_GL = 2304
P_WIDE = 2560


def _params(sem):
    return pltpu.CompilerParams(dimension_semantics=sem, vmem_limit_bytes=VMEM_LIMIT)


def _split_bf16(x, parts):
    out = []
    for _ in range(parts):
        piece = x.astype(BF16)
        out.append(piece)
        x = x - piece.astype(F32)
    return out


def _ada_kernel(c_ref, w_ref, b_ref, o_ref):
    c = c_ref[...]
    cond = c * jax.nn.sigmoid(c)
    o_ref[0] = jnp.dot(cond, w_ref[0], precision=HIGHEST, preferred_element_type=F32) + b_ref[0]


def _ada_mod(c, ada_w, ada_b):
    n_layers, d, d6 = ada_w.shape
    b = c.shape[0]
    rows = SUBLANES
    c_pad = jnp.zeros((rows, d), F32).at[:b].set(c)
    out = pl.pallas_call(
        _ada_kernel,
        grid=(n_layers, d6 // d),
        in_specs=[
            pl.BlockSpec((rows, d), lambda l, j: (0, 0)),
            pl.BlockSpec((1, d, d), lambda l, j: (l, 0, j)),
            pl.BlockSpec((1, 1, d), lambda l, j: (l, 0, j)),
        ],
        out_specs=pl.BlockSpec((1, rows, d), lambda l, j: (l, 0, j)),
        out_shape=jax.ShapeDtypeStruct((n_layers, rows, d6), F32),
        compiler_params=_params(("arbitrary", "arbitrary")),
        name="ada",
    )(c_pad, ada_w, ada_b.reshape(n_layers, 1, d6))
    return out[:, :b]


def _rope_slab(t, cos, sin_signed, lo):
    partner = jnp.where(lo, pltpu.roll(t, LANES - DK // 2, 1), pltpu.roll(t, DK // 2, 1))
    return t * cos + partner * sin_signed


def _rope(t, cos, sin_signed):
    lane = lax.broadcasted_iota(jnp.int32, (1, LANES), 1)
    lo = (lane % DK) < (DK // 2)
    slabs = [_rope_slab(t[:, s * LANES:(s + 1) * LANES], cos, sin_signed, lo)
             for s in range(t.shape[1] // LANES)]
    return slabs[0] if len(slabs) == 1 else jnp.concatenate(slabs, axis=1)


def _pack_halves(x):
    n = x.shape[1] // 2
    lo = lax.bitcast_convert_type(x[:, :n].astype(BF16).astype(F32), jnp.int32)
    hi = lax.bitcast_convert_type(x[:, n:].astype(BF16).astype(F32), jnp.int32)
    return lax.shift_right_logical(lo, jnp.int32(16)) | (hi & jnp.int32(-65536))


def _unpack_halves(p):
    lo = lax.bitcast_convert_type(lax.shift_left(p, jnp.int32(16)), F32)
    hi = lax.bitcast_convert_type(p & jnp.int32(-65536), F32)
    return jnp.concatenate([lo, hi], axis=1)


def _moe_residual(x_ref, rows_ref, gate_ref, g2_ref):
    gate = gate_ref[0]
    y = gate[:, 0:1] * _unpack_halves(rows_ref[0, 0])
    for k in range(1, TOP_K):
        y = y + gate[:, k:k + 1] * _unpack_halves(rows_ref[k, 0])
    return x_ref[0] + g2_ref[0] * y


def _moe_residual_specs(tm, d):
    return [pl.BlockSpec((TOP_K, 1, tm, d // 2), lambda i, j: (0, i, j, 0)),
            pl.BlockSpec((1, tm, TOP_K), lambda i, j: (i, j, 0)),
            pl.BlockSpec((1, 1, d), lambda i, j: (i, 0, 0))]


def _pre_kernel(has_res, *refs):
    if has_res:
        (x_ref, rows_ref, gate_ref, g2_ref, ng_ref, sc_ref, sh_ref, w_ref, cos_ref, sin_ref,
         xo_ref, uv_ref, qt_ref, kc_ref, vc_ref, ks_ref, vst_ref, kw_ref, vwt_ref, glt_ref) = refs
        x = _moe_residual(x_ref, rows_ref, gate_ref, g2_ref)
        xo_ref[0] = x
    else:
        (x_ref, ng_ref, sc_ref, sh_ref, w_ref, cos_ref, sin_ref,
         uv_ref, qt_ref, kc_ref, vc_ref, ks_ref, vst_ref, kw_ref, vwt_ref, glt_ref) = refs
        x = x_ref[0]
    ms = jnp.mean(x * x, axis=-1, keepdims=True)
    h = x * lax.rsqrt(ms + NORM_EPS) * ng_ref[...]
    h = h * (1.0 + sc_ref[0]) + sh_ref[0]
    proj = jnp.dot(h.astype(BF16), w_ref[...], preferred_element_type=F32)
    cos = cos_ref[0]
    sin = sin_ref[0]
    uv_ref[0] = proj[:, C_U:C_Q]
    qt_ref[0] = (_rope(proj[:, C_Q:C_KC], cos, sin) * Q_SCALE).T
    kc_ref[0] = proj[:, C_KC:C_VC]
    vc_ref[0] = proj[:, C_VC:C_KS]
    ks_ref[0] = _rope(proj[:, C_KS:C_VS], cos, sin).astype(BF16)
    vst_ref[0] = proj[:, C_VS:C_KW].T.astype(BF16)
    kw_ref[0] = _rope(proj[:, C_KW:C_VW], cos, sin).astype(BF16)
    vwt_ref[0] = proj[:, C_VW:C_GL].T.astype(BF16)
    glt_ref[0] = proj[:, C_GL:P_WIDE].T


def _pre(x, res, norm_g, sc, sh, w_wide, cos, sin, tm=512):
    b, s, d = x.shape
    tm = min(tm, s)
    row = lambda w: pl.BlockSpec((1, tm, w), lambda i, j: (i, j, 0))
    col = lambda w: pl.BlockSpec((1, w, tm), lambda i, j: (i, 0, j))
    vec = pl.BlockSpec((1, 1, d), lambda i, j: (i, 0, 0))
    in_specs = [row(d)]
    args = [x]
    if res is not None:
        in_specs += _moe_residual_specs(tm, d)
        args += list(res)
    in_specs += [pl.BlockSpec((1, d), lambda i, j: (0, 0)), vec, vec,
                 pl.BlockSpec((d, P_WIDE), lambda i, j: (0, 0)), row(LANES), row(LANES)]
    args += [norm_g.reshape(1, d), sc, sh, w_wide, cos, sin]
    kv = KV_GROUPS * DK
    outs = [(d, F32, False)] if res is not None else []
    outs += [(2 * A_WIDTH, F32, False), (B_HEADS * DK, F32, True), (kv, F32, False), (kv, F32, False),
             (kv, BF16, False), (kv, BF16, True), (kv, BF16, False), (kv, BF16, True),
             (KV_GROUPS * LANES, F32, True)]
    res_out = pl.pallas_call(
        functools.partial(_pre_kernel, res is not None),
        grid=(b, s // tm),
        in_specs=in_specs,
        out_specs=[col(w) if t else row(w) for w, _, t in outs],
        out_shape=[jax.ShapeDtypeStruct((b, w, s) if t else (b, s, w), dt) for w, dt, t in outs],
        compiler_params=_params(("parallel", "parallel")),
        name="pre",
    )(*args)
    if res is None:
        res_out = [x] + list(res_out)
    return res_out


def _sgu_kernel(uv_ref, lng_ref, lnb_ref, w_ref, bias_ref, o_ref):
    rows = uv_ref.shape[1]
    uv = uv_ref[0]
    gu = jax.nn.gelu(uv[:, :A_WIDTH])
    gv = jax.nn.gelu(uv[:, A_WIDTH:])
    mu = jnp.mean(gv, axis=-1, keepdims=True)
    var = jnp.mean(jnp.square(gv - mu), axis=-1, keepdims=True)
    vn = ((gv - mu) * lax.rsqrt(var + NORM_EPS) * lng_ref[...] + lnb_ref[...]).astype(BF16)
    r = lax.broadcasted_iota(jnp.int32, (CHUNK, CHUNK), 0)
    c = lax.broadcasted_iota(jnp.int32, (CHUNK, CHUNK), 1)
    causal = c <= r
    lane_lo = lax.broadcasted_iota(jnp.int32, (CHUNK, LANES), 1) < DK
    for p in range(A_HEADS // 2):
        w0 = jnp.where(causal, w_ref[2 * p], 0.0).astype(BF16)
        w1 = jnp.where(causal, w_ref[2 * p + 1], 0.0).astype(BF16)
        bias = bias_ref[:, p * LANES:(p + 1) * LANES]
        for ch in range(rows // CHUNK):
            rs = slice(ch * CHUNK, (ch + 1) * CHUNK)
            cs = slice(p * LANES, (p + 1) * LANES)
            vp = vn[rs, cs]
            m0 = jnp.dot(w0, vp, preferred_element_type=F32)
            m1 = jnp.dot(w1, vp, preferred_element_type=F32)
            mixed = jnp.where(lane_lo, m0, m1) + bias
            o_ref[0, rs, cs] = (gu[rs, cs] * mixed).astype(o_ref.dtype)


def _sgu(uv, ln_g, ln_b, w_s, b_s, tm=512):
    b, s, _ = uv.shape
    tm = min(tm, s)
    bias = jnp.repeat(b_s.T, A_WIDTH // A_HEADS, axis=1)
    return pl.pallas_call(
        _sgu_kernel,
        grid=(b, s // tm),
        in_specs=[
            pl.BlockSpec((1, tm, 2 * A_WIDTH), lambda i, j: (i, j, 0)),
            pl.BlockSpec((1, A_WIDTH), lambda i, j: (0, 0)),
            pl.BlockSpec((1, A_WIDTH), lambda i, j: (0, 0)),
            pl.BlockSpec((A_HEADS, CHUNK, CHUNK), lambda i, j: (0, 0, 0)),
            pl.BlockSpec((CHUNK, A_WIDTH), lambda i, j: (0, 0)),
        ],
        out_specs=pl.BlockSpec((1, tm, A_WIDTH), lambda i, j: (i, j, 0)),
        out_shape=jax.ShapeDtypeStruct((b, s, A_WIDTH), BF16),
        compiler_params=_params(("parallel", "parallel")),
        name="sgu",
    )(uv, ln_g.reshape(1, -1), ln_b.reshape(1, -1), w_s, bias)


def _compress_kernel(kr_ref, vr_ref, pek_ref, pev_ref, w1k_ref, w1v_ref, w2k_ref, w2v_ref,
                     cos_ref, sin_ref, ko_ref, vt_ref):
    nc = kr_ref.shape[1]

    def mlp(r, pe_ref, w1_ref, w2_ref):
        top = jnp.dot(r + pe_ref[0:1], w1_ref[0], precision=HIGHEST, preferred_element_type=F32)
        bot = jnp.dot(r + pe_ref[1:2], w1_ref[1], precision=HIGHEST, preferred_element_type=F32)
        pre = top + pltpu.roll(bot, nc - 1, 0)
        return jnp.dot(jax.nn.gelu(pre), w2_ref[...], precision=HIGHEST, preferred_element_type=F32)

    kc = _rope(mlp(kr_ref[0], pek_ref, w1k_ref, w2k_ref), cos_ref[0], sin_ref[0])
    vc = mlp(vr_ref[0], pev_ref, w1v_ref, w2v_ref)
    lo = lax.broadcasted_iota(jnp.int32, (nc, LANES), 1) < DK
    rolled = pltpu.roll(kc, DK, 1)
    for g in range(KV_GROUPS):
        dup = jnp.where(lo, kc, rolled) if g == 0 else jnp.where(lo, rolled, kc)
        hi, low = _split_bf16(dup, 2)
        ko_ref[0, g, :, 0:LANES] = hi
        ko_ref[0, g, :, LANES:2 * LANES] = jnp.where(lo, low, jnp.zeros_like(low))
    vt_ref[0] = vc.T.astype(BF16)


def _compress_weights(pe, w1, w2):
    half = CMP_BLOCK // 2
    eye = jnp.eye(KV_GROUPS, dtype=F32)
    w1r = w1.reshape(CMP_BLOCK, DK, DK)
    wfull = jnp.einsum('lde,gh->lgdhe', w1r, eye)
    w1s = wfull.reshape(2, half * KV_GROUPS * DK, KV_GROUPS * DK)
    pes = jnp.broadcast_to(pe.reshape(2, half, 1, DK), (2, half, KV_GROUPS, DK)).reshape(2, -1)
    w2bd = jnp.einsum('de,gh->gdhe', w2, eye).reshape(KV_GROUPS * DK, KV_GROUPS * DK)
    return pes, w1s, w2bd


def _compress(kc, vc, pe_k, pe_v, w1_k, w2_k, w1_v, w2_v, cos_c, sin_c):
    b, s, _ = kc.shape
    nc = s // CMP_STRIDE
    rw = CMP_STRIDE * LANES
    pek, w1ks, w2kb = _compress_weights(pe_k, w1_k, w2_k)
    pev, w1vs, w2vb = _compress_weights(pe_v, w1_v, w2_v)
    full = lambda shape: pl.BlockSpec(shape, lambda i: (0,) * len(shape))
    return pl.pallas_call(
        _compress_kernel,
        grid=(b,),
        in_specs=[
            pl.BlockSpec((1, nc, rw), lambda i: (i, 0, 0)),
            pl.BlockSpec((1, nc, rw), lambda i: (i, 0, 0)),
            full((2, rw)), full((2, rw)),
            full((2, rw, LANES)), full((2, rw, LANES)),
            full((LANES, LANES)), full((LANES, LANES)),
            pl.BlockSpec((1, nc, LANES), lambda i: (i, 0, 0)),
            pl.BlockSpec((1, nc, LANES), lambda i: (i, 0, 0)),
        ],
        out_specs=[pl.BlockSpec((1, KV_GROUPS, nc, 2 * LANES), lambda i: (i, 0, 0, 0)),
                   pl.BlockSpec((1, KV_GROUPS * DK, nc), lambda i: (i, 0, 0))],
        out_shape=[jax.ShapeDtypeStruct((b, KV_GROUPS, nc, 2 * LANES), BF16),
                   jax.ShapeDtypeStruct((b, KV_GROUPS * DK, nc), BF16)],
        compiler_params=_params(("parallel",)),
        name="compress",
    )(kc.reshape(b, nc, rw), vc.reshape(b, nc, rw), pek, pev, w1ks, w1vs, w2kb, w2vb, cos_c, sin_c)


def _heads_on_lanes(qt):
    return jnp.concatenate([qt[h * DK:(h + 1) * DK] for h in range(HPG)], axis=1)


def _cmp_sel_kernel(n_top, qt_ref, k_ref, vt_ref, map_ref, oc_ref, bias_ref):
    tq = qt_ref.shape[2]
    cols = HPG * tq
    nc = k_ref.shape[2]
    nsel = map_ref.shape[0]
    q0 = pl.program_id(2) * tq
    q_hi, q_lo = _split_bf16(_heads_on_lanes(qt_ref[0]), 2)
    q3t = jnp.concatenate([q_hi, q_lo, q_hi, jnp.zeros_like(q_hi)], axis=0)
    s = jnp.dot(k_ref[0, 0], q3t, preferred_element_type=F32)
    n_idx = lax.broadcasted_iota(jnp.int32, (nc, 1), 0)
    t_col = q0 + lax.broadcasted_iota(jnp.int32, (1, cols), 1) % tq
    s = jnp.where((CMP_STRIDE * n_idx + CMP_BLOCK - 1) <= t_col, s, -jnp.inf)
    m = jnp.max(s, axis=0, keepdims=True)
    m = jnp.where(m == -jnp.inf, 0.0, m)
    e = jnp.exp2(s - m)
    d = jnp.sum(e, axis=0, keepdims=True)
    p = e / jnp.where(d > 0, d, 1.0)
    oc = jnp.dot(vt_ref[0], p.astype(BF16), preferred_element_type=F32)
    oc_ref[0] = jnp.concatenate([oc[:, h * tq:(h + 1) * tq] for h in range(HPG)], axis=0)
    psum = p[:, 0:tq]
    for h in range(1, HPG):
        psum = psum + p[:, h * tq:(h + 1) * tq]
    imp = jnp.dot(map_ref[...], jnp.concatenate(_split_bf16(psum, 3), axis=0),
                  preferred_element_type=F32)
    j = lax.broadcasted_iota(jnp.int32, (nsel, tq), 0)
    cur = (q0 + lax.broadcasted_iota(jnp.int32, (nsel, tq), 1)) // SEL_BLOCK
    valid = j <= cur
    forced = (j == 0) | (j == cur) | (j == cur - 1)
    keep = forced | (valid & (cur < n_top))
    vals = jnp.where(valid & jnp.logical_not(forced), imp, -jnp.inf)
    bias = jnp.where(keep, 0.0, MASK_BIAS)
    for _ in range(n_top - N_FORCED):
        mx = jnp.max(vals, axis=0, keepdims=True)
        first = jnp.min(jnp.where(vals == mx, j, nsel), axis=0, keepdims=True)
        pick = (j == first) & (mx > -jnp.inf)
        bias = jnp.where(pick, 0.0, bias)
        vals = jnp.where(pick, -jnp.inf, vals)
    bias_ref[0, 0] = bias.astype(BF16)


def _sel_map_t(s):
    nc = s // CMP_STRIDE
    n_cmp = (s - CMP_BLOCK) // CMP_STRIDE + 1
    n_sel = s // SEL_BLOCK
    cs = CMP_STRIDE * np.arange(n_cmp)[:, None]
    ce = cs + CMP_BLOCK
    ss = SEL_BLOCK * np.arange(n_sel)[None, :]
    se = ss + SEL_BLOCK
    ov = np.clip(np.minimum(ce, se) - np.maximum(cs, ss), 0, None) / CMP_STRIDE
    out = np.zeros((n_sel, nc), np.float32)
    out[:, :n_cmp] = ov.T
    return out


def _cmp_sel(qt, kcmp, vcmp_t, tq=256):
    b, _, s = qt.shape
    nc = s // CMP_STRIDE
    nsel = s // SEL_BLOCK
    n_top = min(SEL_TOP, nsel)
    sel_map = jnp.asarray(np.tile(_sel_map_t(s), (1, 3)), dtype=BF16)
    return pl.pallas_call(
        functools.partial(_cmp_sel_kernel, n_top),
        grid=(b, KV_GROUPS, s // tq),
        in_specs=[
            pl.BlockSpec((1, GW, tq), lambda i, g, j: (i, g, j)),
            pl.BlockSpec((1, 1, nc, 2 * LANES), lambda i, g, j: (i, g, 0, 0)),
            pl.BlockSpec((1, DK, nc), lambda i, g, j: (i, g, 0)),
            pl.BlockSpec((nsel, 3 * nc), lambda i, g, j: (0, 0)),
        ],
        out_specs=[
            pl.BlockSpec((1, GW, tq), lambda i, g, j: (i, g, j)),
            pl.BlockSpec((1, 1, nsel, tq), lambda i, g, j: (i, g, 0, j)),
        ],
        out_shape=[
            jax.ShapeDtypeStruct((b, B_HEADS * DK, s), F32),
            jax.ShapeDtypeStruct((b, KV_GROUPS, nsel, s), BF16),
        ],
        compiler_params=_params(("parallel", "parallel", "parallel")),
        name="cmp_sel",
    )(qt, kcmp, vcmp_t, sel_map)


def _attn_kernel(kt, qt_ref, ks_ref, kw_ref, vst_ref, vwt_ref, bias_ref, oh_ref, oc_ref, glt_ref,
                 o_ref, qa_ref, s0_ref, s1_ref, m_ref, acc_ref, ow_ref):
    tq = qt_ref.shape[2]
    cols = HPG * tq
    nsel = bias_ref.shape[2]
    g = pl.program_id(1)
    q0 = pl.program_id(2) * tq

    q4t = _heads_on_lanes(qt_ref[0]).astype(BF16)
    zero = jnp.zeros_like(q4t)
    qa_ref[0:DK] = jnp.where(g == 0, q4t, zero)
    qa_ref[DK:2 * DK] = jnp.where(g == 1, q4t, zero)
    qa_ref[LANES:LANES + nsel] = jnp.concatenate([bias_ref[0, 0]] * HPG, axis=1)
    qaug = qa_ref[...]
    t_col = q0 + lax.broadcasted_iota(jnp.int32, (1, cols), 1) % tq

    def fold(x, op):
        return op(x.reshape(x.shape[0] // SUBLANES, SUBLANES, cols), axis=0)

    def with_ones(vt):
        return jnp.concatenate([vt, jnp.ones((ONES_ROWS, vt.shape[1]), BF16)], axis=0)

    def stage(c, s_ref):
        k0 = pl.multiple_of(c * kt, kt)
        kaug = jnp.concatenate([ks_ref[0, pl.ds(k0, kt), :], oh_ref[pl.ds(k0, kt), :]], axis=1)
        s_ref[...] = jnp.dot(kaug, qaug, preferred_element_type=F32)
        r0 = pl.multiple_of(jnp.clip(q0 - k0, 0, kt - tq), tq)
        kpos = k0 + r0 + lax.broadcasted_iota(jnp.int32, (tq, 1), 0)
        s_ref[pl.ds(r0, tq), :] = jnp.where(kpos <= t_col, s_ref[pl.ds(r0, tq), :], -jnp.inf)

    def consume(c, s_ref):
        k0 = pl.multiple_of(c * kt, kt)
        sc = s_ref[...]
        m_old = m_ref[...]
        m_new = jnp.maximum(m_old, jnp.max(fold(sc, jnp.max), axis=0, keepdims=True))
        alpha = jnp.exp2(m_old - m_new)
        pr = jnp.exp2(sc - m_new)
        acc_ref[...] = alpha * acc_ref[...] + jnp.dot(with_ones(vst_ref[0, :, pl.ds(k0, kt)]), pr.astype(BF16),
                                                      preferred_element_type=F32)
        m_ref[...] = m_new

    m_ref[...] = jnp.full(m_ref.shape, -jnp.inf, F32)
    acc_ref[...] = jnp.zeros(acc_ref.shape, F32)
    n_full = q0 // kt
    n_pairs = n_full // 2
    odd = n_full % 2 == 1
    stage(0, s0_ref)

    wlen = WINDOW + tq
    w0 = pl.multiple_of(jnp.maximum(q0 - WINDOW, 0), tq)
    sw = jnp.dot(kw_ref[0, pl.ds(w0, wlen), :], qaug[0:LANES], preferred_element_type=F32)
    dlt = t_col - (w0 + lax.broadcasted_iota(jnp.int32, (wlen, 1), 0))
    sw = jnp.where((dlt >= 0) & (dlt < WINDOW), sw, -jnp.inf)
    pw = jnp.exp2(sw - jnp.max(sw, axis=0, keepdims=True))
    o_win = jnp.dot(with_ones(vwt_ref[0, :, pl.ds(w0, wlen)]), pw.astype(BF16), preferred_element_type=F32)
    ow_ref[...] = o_win[0:DK] / o_win[DK:DK + 1]

    def pair(i, carry):
        stage(2 * i + 1, s1_ref)
        consume(2 * i, s0_ref)
        stage(2 * i + 2, s0_ref)
        consume(2 * i + 1, s1_ref)
        return carry

    lax.fori_loop(0, n_pairs, pair, 0)

    @pl.when(odd)
    def _():
        stage(n_full, s1_ref)

    consume(2 * n_pairs, s0_ref)

    @pl.when(odd)
    def _():
        consume(n_full, s1_ref)

    o_sel = acc_ref[0:DK] / acc_ref[DK:DK + 1]

    o_win = ow_ref[...]
    gates = jax.nn.sigmoid(glt_ref[0])
    oc = _heads_on_lanes(oc_ref[0])
    for p in range(HPG // 2):
        halves = []
        for hh in (2 * p, 2 * p + 1):
            cs = slice(hh * tq, (hh + 1) * tq)
            halves.append(gates[3 * hh:3 * hh + 1] * oc[:, cs] + gates[3 * hh + 1:3 * hh + 2] * o_sel[:, cs]
                          + gates[3 * hh + 2:3 * hh + 3] * o_win[:, cs])
        o_ref[0, :, p * LANES:(p + 1) * LANES] = jnp.concatenate(halves, axis=0).T.astype(o_ref.dtype)


def _attn(qt, ks, kw, vs_t, vw_t, bias_t, oc_t, gl_t, tq=256, kt=512):
    b, _, s = qt.shape
    nsel = s // SEL_BLOCK
    kt = min(kt, s)
    onehot = jnp.asarray((np.arange(s)[:, None] // SEL_BLOCK == np.arange(nsel)[None, :]), dtype=BF16)
    k_spec = pl.BlockSpec((1, s, KV_GROUPS * DK), lambda i, g, j: (i, 0, 0))
    vt_spec = pl.BlockSpec((1, DK, s), lambda i, g, j: (i, g, 0))
    cols = HPG * tq
    return pl.pallas_call(
        functools.partial(_attn_kernel, kt),
        grid=(b, KV_GROUPS, s // tq),
        in_specs=[
            pl.BlockSpec((1, GW, tq), lambda i, g, j: (i, g, j)),
            k_spec, k_spec, vt_spec, vt_spec,
            pl.BlockSpec((1, 1, nsel, tq), lambda i, g, j: (i, g, 0, j)),
            pl.BlockSpec((s, nsel), lambda i, g, j: (0, 0)),
            pl.BlockSpec((1, GW, tq), lambda i, g, j: (i, g, j)),
            pl.BlockSpec((1, LANES, tq), lambda i, g, j: (i, g, j)),
        ],
        out_specs=pl.BlockSpec((1, tq, GW), lambda i, g, j: (i, j, g)),
        out_shape=jax.ShapeDtypeStruct((b, s, B_HEADS * DK), BF16),
        scratch_shapes=[pltpu.VMEM((LANES + nsel, cols), BF16),
                        pltpu.VMEM((kt, cols), F32), pltpu.VMEM((kt, cols), F32),
                        pltpu.VMEM((1, cols), F32), pltpu.VMEM((DK + ONES_ROWS, cols), F32),
                        pltpu.VMEM((DK, cols), F32)],
        compiler_params=_params(("parallel", "parallel", "arbitrary")),
        name="attn",
    )(qt, ks, kw, vs_t, vw_t, bias_t, onehot, oc_t, gl_t)


def _post_kernel(ya_ref, yb_ref, wo_ref, x_ref, g1_ref, ng_ref, sc_ref, sh_ref, rw_ref, rb_ref,
                 xo_ref, h_ref, idx_ref, gate_ref, rank_ref, cnt_ref):
    tm = x_ref.shape[1]

    @pl.when((pl.program_id(0) == 0) & (pl.program_id(1) == 0))
    def _():
        cnt_ref[...] = jnp.zeros(cnt_ref.shape, F32)

    mixed = jnp.dot(ya_ref[0], wo_ref[:A_WIDTH], preferred_element_type=F32)
    mixed = mixed + jnp.dot(yb_ref[0], wo_ref[A_WIDTH:], preferred_element_type=F32)
    x = x_ref[0] + g1_ref[0] * mixed
    xo_ref[0] = x
    ms = jnp.mean(x * x, axis=-1, keepdims=True)
    h = x * lax.rsqrt(ms + NORM_EPS) * ng_ref[...]
    h = h * (1.0 + sc_ref[0]) + sh_ref[0]
    h_ref[0] = _pack_halves(h)
    h_hi, h_mid = _split_bf16(h, 2)
    logits = jnp.dot(jnp.concatenate([h_hi, h_hi, h_mid], axis=1), rw_ref[...],
                     preferred_element_type=F32) + rb_ref[...]
    lane = lax.broadcasted_iota(jnp.int32, logits.shape, 1)
    vals = logits
    top = jnp.max(vals, axis=-1, keepdims=True)
    picked = jnp.zeros_like(logits)
    slot_idx = jnp.zeros_like(logits)
    slot_exp = jnp.zeros_like(logits)
    for k in range(TOP_K):
        m = jnp.max(vals, axis=-1, keepdims=True)
        first = jnp.min(jnp.where(vals == m, lane, LANES), axis=-1, keepdims=True)
        pick = lane == first
        picked = jnp.where(pick, 1.0, picked)
        slot_idx = jnp.where(lane == k, first.astype(F32), slot_idx)
        slot_exp = jnp.where(lane == k, jnp.exp(m - top), slot_exp)
        vals = jnp.where(pick, -jnp.inf, vals)
    slot_gate = slot_exp / jnp.sum(slot_exp, axis=-1, keepdims=True)
    idx_t = slot_idx.T[:TOP_K]
    idx_ref[...] = idx_t.astype(jnp.int32)
    gate_ref[...] = slot_gate.T[:TOP_K]
    routed_t = picked.T[:N_EXPERTS]
    r = lax.broadcasted_iota(jnp.int32, (tm, tm), 0)
    c = lax.broadcasted_iota(jnp.int32, (tm, tm), 1)
    before = jnp.where(r < c, 1.0, 0.0).astype(BF16)
    rank_t = cnt_ref[...] + jnp.dot(routed_t.astype(BF16), before, preferred_element_type=F32)
    cnt_ref[...] += jnp.sum(routed_t, axis=-1, keepdims=True)
    e_row = lax.broadcasted_iota(jnp.int32, (N_EXPERTS, tm), 0).astype(F32)
    ranks = [jnp.sum(jnp.where(e_row == idx_t[k:k + 1], rank_t, 0.0), axis=0, keepdims=True)
             for k in range(TOP_K)]
    rank_ref[...] = jnp.concatenate(ranks, axis=0).astype(jnp.int32)


def _post(y_a, y_b, w_out, x, g1, norm_g, sc, sh, router_w, router_b, tm=512):
    b, s, d = x.shape
    tm = min(tm, s)
    nt = s // tm
    w_hi, w_mid = _split_bf16(jnp.zeros((d, LANES), F32).at[:, :N_EXPERTS].set(router_w), 2)
    rw = jnp.concatenate([w_hi, w_mid, w_hi], axis=0)
    rb = jnp.full((1, LANES), -1e30, F32).at[0, :N_EXPERTS].set(router_b)
    vec = pl.BlockSpec((1, 1, d), lambda i, j: (i, 0, 0))
    row = lambda w: pl.BlockSpec((1, tm, w), lambda i, j: (i, j, 0))
    return pl.pallas_call(
        _post_kernel,
        grid=(b, nt),
        in_specs=[
            row(A_WIDTH), row(B_HEADS * DK),
            pl.BlockSpec((d, d), lambda i, j: (0, 0)),
            row(d), vec,
            pl.BlockSpec((1, d), lambda i, j: (0, 0)), vec, vec,
            pl.BlockSpec((3 * d, LANES), lambda i, j: (0, 0)),
            pl.BlockSpec((1, LANES), lambda i, j: (0, 0)),
        ],
        out_specs=[row(d), row(d // 2)] + [pl.BlockSpec((TOP_K, tm), lambda i, j: (0, i * nt + j))] * 3,
        out_shape=[
            jax.ShapeDtypeStruct((b, s, d), F32),
            jax.ShapeDtypeStruct((b, s, d // 2), jnp.int32),
            jax.ShapeDtypeStruct((TOP_K, b * s), jnp.int32),
            jax.ShapeDtypeStruct((TOP_K, b * s), F32),
            jax.ShapeDtypeStruct((TOP_K, b * s), jnp.int32),
        ],
        scratch_shapes=[pltpu.VMEM((N_EXPERTS, 1), F32)],
        compiler_params=_params(("arbitrary", "arbitrary")),
        name="post",
    )(y_a, y_b, w_out.astype(BF16), x, g1, norm_g.reshape(1, d), sc, sh, rw, rb)


def _sc_mesh():
    return plsc.VectorSubcoreMesh(core_axis_name="c", subcore_axis_name="s")


def _sc_worker():
    return lax.axis_index("c") * SC_SUBCORES + lax.axis_index("s")


def _dispatch_rows(h2, dest_c, n_rows):
    t, d = h2.shape
    per_worker = t // SC_WINDOW // SC_WORKERS

    def body(x_hbm, i_hbm, o_hbm, buf, idx):
        worker = _sc_worker()

        @pl.loop(0, per_worker)
        def _(j):
            ch = worker * per_worker + j
            pltpu.sync_copy(i_hbm.at[ch], idx)
            pltpu.sync_copy(x_hbm.at[pl.ds(ch * SC_WINDOW, SC_WINDOW)], buf)
            for k in range(TOP_K):
                pltpu.sync_copy(buf, o_hbm.at[idx.at[k]])

    return pl.kernel(
        body, out_type=jax.ShapeDtypeStruct((n_rows, d), h2.dtype), mesh=_sc_mesh(),
        scratch_types=[pltpu.VMEM((SC_WINDOW, d), h2.dtype), pltpu.VMEM((TOP_K, SC_WINDOW), jnp.int32)],
        name="dispatch_rows",
    )(h2, dest_c)


def _collect_rows(rows, dest_c, t):
    d = rows.shape[1]
    per_worker = t // SC_WINDOW // SC_WORKERS

    def body(r_hbm, i_hbm, o_hbm, buf, idx):
        worker = _sc_worker()

        @pl.loop(0, per_worker)
        def _(j):
            ch = worker * per_worker + j
            pltpu.sync_copy(i_hbm.at[ch], idx)
            for k in range(TOP_K):
                pltpu.sync_copy(r_hbm.at[idx.at[k]], buf)
                pltpu.sync_copy(buf, o_hbm.at[k, pl.ds(ch * SC_WINDOW, SC_WINDOW)])

    return pl.kernel(
        body, out_type=jax.ShapeDtypeStruct((TOP_K, t, d), rows.dtype), mesh=_sc_mesh(),
        scratch_types=[pltpu.VMEM((SC_WINDOW, d), rows.dtype), pltpu.VMEM((TOP_K, SC_WINDOW), jnp.int32)],
        name="collect_rows",
    )(rows, dest_c)


def _experts_kernel(layer, be_ref, nb_ref, slot_ref, next_ref, x_ref, w1_hbm, b1_ref, w2_hbm, b2_ref, o_ref,
                    w1f_ref, w2f_ref, w1b_ref, w2b_ref, sem):
    i = pl.program_id(0)
    e = be_ref[i]
    live = i < nb_ref[0]

    def weight_copies(expert, slot):
        return (pltpu.make_async_copy(w1_hbm.at[layer, expert], w1f_ref.at[slot], sem.at[slot, 0]),
                pltpu.make_async_copy(w2_hbm.at[layer, expert], w2f_ref.at[slot], sem.at[slot, 1]))

    @pl.when(live & (i == 0))
    def _():
        for cp in weight_copies(e, 0):
            cp.start()

    @pl.when(live & ((i == 0) | (e != be_ref[jnp.maximum(i - 1, 0)])))
    def _():
        slot = slot_ref[i]
        for cp in weight_copies(e, slot):
            cp.wait()
        w1b_ref[...] = w1f_ref[slot].astype(BF16)
        w2b_ref[...] = w2f_ref[slot].astype(BF16)

        @pl.when(next_ref[i] >= 0)
        def _():
            for cp in weight_copies(next_ref[i], 1 - slot):
                cp.start()

    @pl.when(live)
    def _():
        x = _unpack_halves(x_ref[...]).astype(BF16)
        hdn = jnp.dot(x, w1b_ref[...], preferred_element_type=F32) + b1_ref[0, 0]
        glu = jnp.minimum(hdn[:, :D_EXPERT], SWIGLU_LIMIT)
        lin = jnp.clip(hdn[:, D_EXPERT:], -SWIGLU_LIMIT, SWIGLU_LIMIT)
        act = glu * jax.nn.sigmoid(SWIGLU_ALPHA * glu) * (lin + 1.0)
        out = jnp.dot(act.astype(BF16), w2b_ref[...], preferred_element_type=F32) + b2_ref[0, 0]
        o_ref[...] = _pack_halves(out)


def _experts(rows, block_exp, n_live, slot, next_exp, layer, w1, b1, w2, b2):
    n_rows, dp = rows.shape
    n_layers, n_e, d, f2 = w1.shape
    n_blocks = n_rows // ROW_BLOCK
    by_expert = lambda i, be, nb, sl, nx: (layer, be[i], 0, 0)
    grid_spec = pltpu.PrefetchScalarGridSpec(
        num_scalar_prefetch=4,
        grid=(n_blocks,),
        in_specs=[
            pl.BlockSpec((ROW_BLOCK, dp), lambda i, be, nb, sl, nx: (i, 0)),
            pl.BlockSpec(memory_space=pl.ANY),
            pl.BlockSpec((1, 1, 1, f2), by_expert),
            pl.BlockSpec(memory_space=pl.ANY),
            pl.BlockSpec((1, 1, 1, d), by_expert),
        ],
        out_specs=pl.BlockSpec((ROW_BLOCK, dp), lambda i, be, nb, sl, nx: (i, 0)),
        scratch_shapes=[pltpu.VMEM((2, d, f2), F32), pltpu.VMEM((2, f2 // 2, d), F32),
                        pltpu.VMEM((d, f2), BF16), pltpu.VMEM((f2 // 2, d), BF16),
                        pltpu.SemaphoreType.DMA((2, 2))],
    )
    return pl.pallas_call(
        functools.partial(_experts_kernel, layer),
        grid_spec=grid_spec,
        out_shape=jax.ShapeDtypeStruct((n_rows, dp), jnp.int32),
        compiler_params=_params(("arbitrary",)),
        name="experts",
    )(block_exp, n_live, slot, next_exp, rows, w1, b1.reshape(n_layers, n_e, 1, f2), w2,
      b2.reshape(n_layers, n_e, 1, d))


def _moe_routed(h2, idx, rank, layer, w1, b1, w2, b2):
    t, d = h2.shape
    n_e = w1.shape[1]
    n_blocks = t * TOP_K // ROW_BLOCK + n_e
    onehot = (idx[:, :, None] == jnp.arange(n_e, dtype=jnp.int32)).astype(jnp.int32)
    counts = jnp.sum(onehot, axis=(0, 1))
    padded = (counts + ROW_BLOCK - 1) // ROW_BLOCK * ROW_BLOCK
    pad_end = jnp.cumsum(padded)
    dest = jnp.sum(onehot * (pad_end - padded), axis=-1) + rank
    block_row = jnp.arange(n_blocks, dtype=jnp.int32)[:, None] * ROW_BLOCK
    block_exp = jnp.minimum(jnp.sum((pad_end[None, :] <= block_row).astype(jnp.int32), axis=-1), n_e - 1)
    n_live = (pad_end[-1:] // ROW_BLOCK).astype(jnp.int32)
    experts = jnp.arange(n_e, dtype=jnp.int32)
    later = (experts[None, :] > experts[:, None]) & (counts[None, :] > 0)
    next_of = jnp.min(jnp.where(later, experts[None, :], n_e), axis=-1)
    next_of = jnp.where(next_of == n_e, -1, next_of)
    block_onehot = (block_exp[:, None] == experts[None, :]).astype(jnp.int32)
    next_exp = jnp.sum(block_onehot * next_of, axis=-1)
    run_of = jnp.cumsum((counts > 0).astype(jnp.int32)) - 1
    slot = jnp.sum(block_onehot * run_of, axis=-1) % 2
    dest_c = dest.reshape(TOP_K, t // SC_WINDOW, SC_WINDOW).transpose(1, 0, 2)
    rows = _dispatch_rows(h2, dest_c, n_blocks * ROW_BLOCK)
    out = _experts(rows, block_exp, n_live, slot, next_exp, layer, w1, b1, w2, b2)
    return _collect_rows(out, dest_c, t)


def _final_kernel(x_ref, rows_ref, gate_ref, g2_ref, fg_ref, o_ref):
    x = _moe_residual(x_ref, rows_ref, gate_ref, g2_ref)
    ms = jnp.mean(x * x, axis=-1, keepdims=True)
    o_ref[0] = x * lax.rsqrt(ms + NORM_EPS) * fg_ref[...]


def _final(x, res, final_g, tm=512):
    b, s, d = x.shape
    tm = min(tm, s)
    row = pl.BlockSpec((1, tm, d), lambda i, j: (i, j, 0))
    return pl.pallas_call(
        _final_kernel,
        grid=(b, s // tm),
        in_specs=[row] + _moe_residual_specs(tm, d) + [pl.BlockSpec((1, d), lambda i, j: (0, 0))],
        out_specs=row,
        out_shape=jax.ShapeDtypeStruct((b, s, d), F32),
        compiler_params=_params(("parallel", "parallel")),
        name="final",
    )(x, *res, final_g.reshape(1, d))


def _widen_w_in(w):
    gl0 = 2 * A_WIDTH + B_HEADS * DK + 6 * KV_GROUPS * DK
    per_group = HPG * 3
    gl_groups = [jnp.pad(w[:, gl0 + g * per_group:gl0 + (g + 1) * per_group], ((0, 0), (0, LANES - per_group)))
                 for g in range(KV_GROUPS)]
    wide = jnp.concatenate([w[:, :gl0]] + gl_groups, axis=1)
    assert gl0 == C_GL and wide.shape[1] == P_WIDE
    return wide.astype(BF16)


def _rope_tables(pos):
    half = DK // 2
    inv = ROPE_THETA ** (-jnp.arange(half, dtype=F32) / half)
    ang = pos.astype(F32)[..., None] * inv
    cos = jnp.cos(ang)
    sin = jnp.sin(ang)
    reps = LANES // DK
    return (jnp.concatenate([cos, cos] * reps, axis=-1),
            jnp.concatenate([-sin, sin] * reps, axis=-1))


def kernel(x, c, positions, ada_w, ada_b, norm1_g, norm2_g, w_in, w_out, sg_ln_g, sg_ln_b, sg_w, sg_b,
           cmp_pe_k, cmp_pe_v, cmp_w1_k, cmp_w2_k, cmp_w1_v, cmp_w2_v, router_w, router_b,
           exp_w1, exp_b1, exp_w2, exp_b2, final_g):
    b, s, d = x.shape
    n_layers = ada_w.shape[0]
    mod = _ada_mod(c, ada_w, ada_b)
    cos, sin = _rope_tables(positions)
    cmp_end = jnp.minimum(CMP_STRIDE * jnp.arange(s // CMP_STRIDE) + CMP_BLOCK - 1, s - 1)
    cos_c, sin_c = _rope_tables(positions[:, cmp_end])
    res = None
    for l in range(n_layers):
        sh1, sc1, g1, sh2, sc2, g2 = [m.reshape(b, 1, d) for m in jnp.split(mod[l], 6, axis=-1)]
        x, uv, qt, kc, vc, ks, vs_t, kw, vw_t, gl_t = _pre(x, res, norm1_g[l], sc1, sh1,
                                                          _widen_w_in(w_in[l]), cos, sin)
        y_a = _sgu(uv, sg_ln_g[l], sg_ln_b[l], sg_w[l], sg_b[l])
        kcmp, vcmp_t = _compress(kc, vc, cmp_pe_k[l], cmp_pe_v[l], cmp_w1_k[l], cmp_w2_k[l],
                                 cmp_w1_v[l], cmp_w2_v[l], cos_c, sin_c)
        oc_t, bias_t = _cmp_sel(qt, kcmp, vcmp_t)
        y_b = _attn(qt, ks, kw, vs_t, vw_t, bias_t, oc_t, gl_t)
        x, h2, idx, gate, rank = _post(y_a, y_b, w_out[l], x, g1, norm2_g[l], sc2, sh2,
                                       router_w[l], router_b[l])
        rows = _moe_routed(h2.reshape(b * s, d // 2), idx, rank, l, exp_w1, exp_b1, exp_w2, exp_b2)
        res = (rows.reshape(TOP_K, b, s, d // 2), gate.T.reshape(b, s, TOP_K), g2)
    return _final(x, res, final_g)
```

```python
import functools

import numpy as np
import jax
import jax.numpy as jnp
from jax import lax
from jax.experimental import pallas as pl
from jax.experimental.pallas import tpu as pltpu
from jax.experimental.pallas import tpu_sc as plsc

F32 = jnp.float32
BF16 = jnp.bfloat16
HIGHEST = lax.Precision.HIGHEST

D_MODEL = 1024
A_WIDTH = 512
A_HEADS = 8
CHUNK = 128
B_HEADS = 8
DK = 64
KV_GROUPS = 2
HPG = B_HEADS // KV_GROUPS
CMP_BLOCK = 32
CMP_STRIDE = 16
SEL_BLOCK = 64
SEL_TOP = 16
WINDOW = 512
ROPE_THETA = 10000.0
N_EXPERTS = 32
TOP_K = 4
D_EXPERT = 1024
SWIGLU_LIMIT = 7.0
SWIGLU_ALPHA = 1.702
NORM_EPS = 1e-6

LANES = 128
SUBLANES = 8
GW = HPG * DK
N_FORCED = 3
ONES_ROWS = 16
MASK_BIAS = -32768.0
Q_SCALE = DK ** -0.5 * 1.4426950408889634
VMEM_LIMIT = 56 * 1024 * 1024
ROW_BLOCK = 256
SC_CORES = 2
SC_SUBCORES = 16
SC_WORKERS = SC_CORES * SC_SUBCORES
SC_WINDOW = 128

C_U, C_V, C_Q = 0, 512, 1024
C_KC, C_VC, C_KS, C_VS, C_KW, C_VW = 1536, 1664, 1792, 1920, 2048, 2176
C_GL = 2304
P_WIDE = 2560


def _params(sem):
    return pltpu.CompilerParams(dimension_semantics=sem, vmem_limit_bytes=VMEM_LIMIT)


def _split_bf16(x, parts):
    out = []
    for _ in range(parts):
        piece = x.astype(BF16)
        out.append(piece)
        x = x - piece.astype(F32)
    return out


def _ada_kernel(c_ref, w_ref, b_ref, o_ref):
    c = c_ref[...]
    cond = c * jax.nn.sigmoid(c)
    o_ref[0] = jnp.dot(cond, w_ref[0], precision=HIGHEST, preferred_element_type=F32) + b_ref[0]


def _ada_mod(c, ada_w, ada_b):
    n_layers, d, d6 = ada_w.shape
    b = c.shape[0]
    rows = SUBLANES
    c_pad = jnp.zeros((rows, d), F32).at[:b].set(c)
    out = pl.pallas_call(
        _ada_kernel,
        grid=(n_layers, d6 // d),
        in_specs=[
            pl.BlockSpec((rows, d), lambda l, j: (0, 0)),
            pl.BlockSpec((1, d, d), lambda l, j: (l, 0, j)),
            pl.BlockSpec((1, 1, d), lambda l, j: (l, 0, j)),
        ],
        out_specs=pl.BlockSpec((1, rows, d), lambda l, j: (l, 0, j)),
        out_shape=jax.ShapeDtypeStruct((n_layers, rows, d6), F32),
        compiler_params=_params(("arbitrary", "arbitrary")),
        name="ada",
    )(c_pad, ada_w, ada_b.reshape(n_layers, 1, d6))
    return out[:, :b]


def _rope_slab(t, cos, sin_signed, lo):
    partner = jnp.where(lo, pltpu.roll(t, LANES - DK // 2, 1), pltpu.roll(t, DK // 2, 1))
    return t * cos + partner * sin_signed


def _rope(t, cos, sin_signed):
    lane = lax.broadcasted_iota(jnp.int32, (1, LANES), 1)
    lo = (lane % DK) < (DK // 2)
    slabs = [_rope_slab(t[:, s * LANES:(s + 1) * LANES], cos, sin_signed, lo)
             for s in range(t.shape[1] // LANES)]
    return slabs[0] if len(slabs) == 1 else jnp.concatenate(slabs, axis=1)


def _pack_halves(x):
    n = x.shape[1] // 2
    lo = lax.bitcast_convert_type(x[:, :n].astype(BF16).astype(F32), jnp.int32)
    hi = lax.bitcast_convert_type(x[:, n:].astype(BF16).astype(F32), jnp.int32)
    return lax.shift_right_logical(lo, jnp.int32(16)) | (hi & jnp.int32(-65536))


def _unpack_halves(p):
    lo = lax.bitcast_convert_type(lax.shift_left(p, jnp.int32(16)), F32)
    hi = lax.bitcast_convert_type(p & jnp.int32(-65536), F32)
    return jnp.concatenate([lo, hi], axis=1)


def _moe_residual(x_ref, rows_ref, gate_ref, g2_ref):
    gate = gate_ref[0]
    y = gate[:, 0:1] * _unpack_halves(rows_ref[0, 0])
    for k in range(1, TOP_K):
        y = y + gate[:, k:k + 1] * _unpack_halves(rows_ref[k, 0])
    return x_ref[0] + g2_ref[0] * y


def _moe_residual_specs(tm, d):
    return [pl.BlockSpec((TOP_K, 1, tm, d // 2), lambda i, j: (0, i, j, 0)),
            pl.BlockSpec((1, tm, TOP_K), lambda i, j: (i, j, 0)),
            pl.BlockSpec((1, 1, d), lambda i, j: (i, 0, 0))]


def _pre_kernel(has_res, *refs):
    if has_res:
        (x_ref, rows_ref, gate_ref, g2_ref, ng_ref, sc_ref, sh_ref, w_ref, cos_ref, sin_ref,
         xo_ref, uv_ref, qt_ref, kc_ref, vc_ref, ks_ref, vst_ref, kw_ref, vwt_ref, glt_ref) = refs
        x = _moe_residual(x_ref, rows_ref, gate_ref, g2_ref)
        xo_ref[0] = x
    else:
        (x_ref, ng_ref, sc_ref, sh_ref, w_ref, cos_ref, sin_ref,
         uv_ref, qt_ref, kc_ref, vc_ref, ks_ref, vst_ref, kw_ref, vwt_ref, glt_ref) = refs
        x = x_ref[0]
    ms = jnp.mean(x * x, axis=-1, keepdims=True)
    h = x * lax.rsqrt(ms + NORM_EPS) * ng_ref[...]
    h = h * (1.0 + sc_ref[0]) + sh_ref[0]
    proj = jnp.dot(h.astype(BF16), w_ref[...], preferred_element_type=F32)
    cos = cos_ref[0]
    sin = sin_ref[0]
    uv_ref[0] = proj[:, C_U:C_Q]
    qt_ref[0] = (_rope(proj[:, C_Q:C_KC], cos, sin) * Q_SCALE).T
    kc_ref[0] = proj[:, C_KC:C_VC]
    vc_ref[0] = proj[:, C_VC:C_KS]
    ks_ref[0] = _rope(proj[:, C_KS:C_VS], cos, sin).astype(BF16)
    vst_ref[0] = proj[:, C_VS:C_KW].T.astype(BF16)
    kw_ref[0] = _rope(proj[:, C_KW:C_VW], cos, sin).astype(BF16)
    vwt_ref[0] = proj[:, C_VW:C_GL].T.astype(BF16)
    glt_ref[0] = proj[:, C_GL:P_WIDE].T


def _pre(x, res, norm_g, sc, sh, w_wide, cos, sin, tm=512):
    b, s, d = x.shape
    tm = min(tm, s)
    row = lambda w: pl.BlockSpec((1, tm, w), lambda i, j: (i, j, 0))
    col = lambda w: pl.BlockSpec((1, w, tm), lambda i, j: (i, 0, j))
    vec = pl.BlockSpec((1, 1, d), lambda i, j: (i, 0, 0))
    in_specs = [row(d)]
    args = [x]
    if res is not None:
        in_specs += _moe_residual_specs(tm, d)
        args += list(res)
    in_specs += [pl.BlockSpec((1, d), lambda i, j: (0, 0)), vec, vec,
                 pl.BlockSpec((d, P_WIDE), lambda i, j: (0, 0)), row(LANES), row(LANES)]
    args += [norm_g.reshape(1, d), sc, sh, w_wide, cos, sin]
    kv = KV_GROUPS * DK
    outs = [(d, F32, False)] if res is not None else []
    outs += [(2 * A_WIDTH, F32, False), (B_HEADS * DK, F32, True), (kv, F32, False), (kv, F32, False),
             (kv, BF16, False), (kv, BF16, True), (kv, BF16, False), (kv, BF16, True),
             (KV_GROUPS * LANES, F32, True)]
    res_out = pl.pallas_call(
        functools.partial(_pre_kernel, res is not None),
        grid=(b, s // tm),
        in_specs=in_specs,
        out_specs=[col(w) if t else row(w) for w, _, t in outs],
        out_shape=[jax.ShapeDtypeStruct((b, w, s) if t else (b, s, w), dt) for w, dt, t in outs],
        compiler_params=_params(("parallel", "parallel")),
        name="pre",
    )(*args)
    if res is None:
        res_out = [x] + list(res_out)
    return res_out


def _sgu_kernel(uv_ref, lng_ref, lnb_ref, w_ref, bias_ref, o_ref):
    rows = uv_ref.shape[1]
    uv = uv_ref[0]
    gu = jax.nn.gelu(uv[:, :A_WIDTH])
    gv = jax.nn.gelu(uv[:, A_WIDTH:])
    mu = jnp.mean(gv, axis=-1, keepdims=True)
    var = jnp.mean(jnp.square(gv - mu), axis=-1, keepdims=True)
    vn = ((gv - mu) * lax.rsqrt(var + NORM_EPS) * lng_ref[...] + lnb_ref[...]).astype(BF16)
    r = lax.broadcasted_iota(jnp.int32, (CHUNK, CHUNK), 0)
    c = lax.broadcasted_iota(jnp.int32, (CHUNK, CHUNK), 1)
    causal = c <= r
    lane_lo = lax.broadcasted_iota(jnp.int32, (CHUNK, LANES), 1) < DK
    for p in range(A_HEADS // 2):
        w0 = jnp.where(causal, w_ref[2 * p], 0.0).astype(BF16)
        w1 = jnp.where(causal, w_ref[2 * p + 1], 0.0).astype(BF16)
        bias = bias_ref[:, p * LANES:(p + 1) * LANES]
        for ch in range(rows // CHUNK):
            rs = slice(ch * CHUNK, (ch + 1) * CHUNK)
            cs = slice(p * LANES, (p + 1) * LANES)
            vp = vn[rs, cs]
            m0 = jnp.dot(w0, vp, preferred_element_type=F32)
            m1 = jnp.dot(w1, vp, preferred_element_type=F32)
            mixed = jnp.where(lane_lo, m0, m1) + bias
            o_ref[0, rs, cs] = (gu[rs, cs] * mixed).astype(o_ref.dtype)


def _sgu(uv, ln_g, ln_b, w_s, b_s, tm=512):
    b, s, _ = uv.shape
    tm = min(tm, s)
    bias = jnp.repeat(b_s.T, A_WIDTH // A_HEADS, axis=1)
    return pl.pallas_call(
        _sgu_kernel,
        grid=(b, s // tm),
        in_specs=[
            pl.BlockSpec((1, tm, 2 * A_WIDTH), lambda i, j: (i, j, 0)),
            pl.BlockSpec((1, A_WIDTH), lambda i, j: (0, 0)),
            pl.BlockSpec((1, A_WIDTH), lambda i, j: (0, 0)),
            pl.BlockSpec((A_HEADS, CHUNK, CHUNK), lambda i, j: (0, 0, 0)),
            pl.BlockSpec((CHUNK, A_WIDTH), lambda i, j: (0, 0)),
        ],
        out_specs=pl.BlockSpec((1, tm, A_WIDTH), lambda i, j: (i, j, 0)),
        out_shape=jax.ShapeDtypeStruct((b, s, A_WIDTH), BF16),
        compiler_params=_params(("parallel", "parallel")),
        name="sgu",
    )(uv, ln_g.reshape(1, -1), ln_b.reshape(1, -1), w_s, bias)


def _compress_kernel(kr_ref, vr_ref, pek_ref, pev_ref, w1k_ref, w1v_ref, w2k_ref, w2v_ref,
                     cos_ref, sin_ref, ko_ref, vt_ref):
    nc = kr_ref.shape[1]

    def mlp(r, pe_ref, w1_ref, w2_ref):
        top = jnp.dot(r + pe_ref[0:1], w1_ref[0], precision=HIGHEST, preferred_element_type=F32)
        bot = jnp.dot(r + pe_ref[1:2], w1_ref[1], precision=HIGHEST, preferred_element_type=F32)
        pre = top + pltpu.roll(bot, nc - 1, 0)
        return jnp.dot(jax.nn.gelu(pre), w2_ref[...], precision=HIGHEST, preferred_element_type=F32)

    kc = _rope(mlp(kr_ref[0], pek_ref, w1k_ref, w2k_ref), cos_ref[0], sin_ref[0])
    vc = mlp(vr_ref[0], pev_ref, w1v_ref, w2v_ref)
    lo = lax.broadcasted_iota(jnp.int32, (nc, LANES), 1) < DK
    rolled = pltpu.roll(kc, DK, 1)
    for g in range(KV_GROUPS):
        dup = jnp.where(lo, kc, rolled) if g == 0 else jnp.where(lo, rolled, kc)
        hi, low = _split_bf16(dup, 2)
        ko_ref[0, g, :, 0:LANES] = hi
        ko_ref[0, g, :, LANES:2 * LANES] = jnp.where(lo, low, jnp.zeros_like(low))
    vt_ref[0] = vc.T.astype(BF16)


def _compress_weights(pe, w1, w2):
    half = CMP_BLOCK // 2
    eye = jnp.eye(KV_GROUPS, dtype=F32)
    w1r = w1.reshape(CMP_BLOCK, DK, DK)
    wfull = jnp.einsum('lde,gh->lgdhe', w1r, eye)
    w1s = wfull.reshape(2, half * KV_GROUPS * DK, KV_GROUPS * DK)
    pes = jnp.broadcast_to(pe.reshape(2, half, 1, DK), (2, half, KV_GROUPS, DK)).reshape(2, -1)
    w2bd = jnp.einsum('de,gh->gdhe', w2, eye).reshape(KV_GROUPS * DK, KV_GROUPS * DK)
    return pes, w1s, w2bd


def _compress(kc, vc, pe_k, pe_v, w1_k, w2_k, w1_v, w2_v, cos_c, sin_c):
    b, s, _ = kc.shape
    nc = s // CMP_STRIDE
    rw = CMP_STRIDE * LANES
    pek, w1ks, w2kb = _compress_weights(pe_k, w1_k, w2_k)
    pev, w1vs, w2vb = _compress_weights(pe_v, w1_v, w2_v)
    full = lambda shape: pl.BlockSpec(shape, lambda i: (0,) * len(shape))
    return pl.pallas_call(
        _compress_kernel,
        grid=(b,),
        in_specs=[
            pl.BlockSpec((1, nc, rw), lambda i: (i, 0, 0)),
            pl.BlockSpec((1, nc, rw), lambda i: (i, 0, 0)),
            full((2, rw)), full((2, rw)),
            full((2, rw, LANES)), full((2, rw, LANES)),
            full((LANES, LANES)), full((LANES, LANES)),
            pl.BlockSpec((1, nc, LANES), lambda i: (i, 0, 0)),
            pl.BlockSpec((1, nc, LANES), lambda i: (i, 0, 0)),
        ],
        out_specs=[pl.BlockSpec((1, KV_GROUPS, nc, 2 * LANES), lambda i: (i, 0, 0, 0)),
                   pl.BlockSpec((1, KV_GROUPS * DK, nc), lambda i: (i, 0, 0))],
        out_shape=[jax.ShapeDtypeStruct((b, KV_GROUPS, nc, 2 * LANES), BF16),
                   jax.ShapeDtypeStruct((b, KV_GROUPS * DK, nc), BF16)],
        compiler_params=_params(("parallel",)),
        name="compress",
    )(kc.reshape(b, nc, rw), vc.reshape(b, nc, rw), pek, pev, w1ks, w1vs, w2kb, w2vb, cos_c, sin_c)


def _heads_on_lanes(qt):
    return jnp.concatenate([qt[h * DK:(h + 1) * DK] for h in range(HPG)], axis=1)


def _cmp_sel_kernel(n_top, qt_ref, k_ref, vt_ref, map_ref, oc_ref, bias_ref):
    tq = qt_ref.shape[2]
    cols = HPG * tq
    nc = k_ref.shape[2]
    nsel = map_ref.shape[0]
    q0 = pl.program_id(2) * tq
    q_hi, q_lo = _split_bf16(_heads_on_lanes(qt_ref[0]), 2)
    q3t = jnp.concatenate([q_hi, q_lo, q_hi, jnp.zeros_like(q_hi)], axis=0)
    s = jnp.dot(k_ref[0, 0], q3t, preferred_element_type=F32)
    n_idx = lax.broadcasted_iota(jnp.int32, (nc, 1), 0)
    t_col = q0 + lax.broadcasted_iota(jnp.int32, (1, cols), 1) % tq
    s = jnp.where((CMP_STRIDE * n_idx + CMP_BLOCK - 1) <= t_col, s, -jnp.inf)
    m = jnp.max(s, axis=0, keepdims=True)
    m = jnp.where(m == -jnp.inf, 0.0, m)
    e = jnp.exp2(s - m)
    d = jnp.sum(e, axis=0, keepdims=True)
    p = e / jnp.where(d > 0, d, 1.0)
    oc = jnp.dot(vt_ref[0], p.astype(BF16), preferred_element_type=F32)
    oc_ref[0] = jnp.concatenate([oc[:, h * tq:(h + 1) * tq] for h in range(HPG)], axis=0)
    psum = p[:, 0:tq]
    for h in range(1, HPG):
        psum = psum + p[:, h * tq:(h + 1) * tq]
    imp = jnp.dot(map_ref[...], jnp.concatenate(_split_bf16(psum, 3), axis=0),
                  preferred_element_type=F32)
    j = lax.broadcasted_iota(jnp.int32, (nsel, tq), 0)
    cur = (q0 + lax.broadcasted_iota(jnp.int32, (nsel, tq), 1)) // SEL_BLOCK
    valid = j <= cur
    forced = (j == 0) | (j == cur) | (j == cur - 1)
    keep = forced | (valid & (cur < n_top))
    vals = jnp.where(valid & jnp.logical_not(forced), imp, -jnp.inf)
    bias = jnp.where(keep, 0.0, MASK_BIAS)
    for _ in range(n_top - N_FORCED):
        mx = jnp.max(vals, axis=0, keepdims=True)
        first = jnp.min(jnp.where(vals == mx, j, nsel), axis=0, keepdims=True)
        pick = (j == first) & (mx > -jnp.inf)
        bias = jnp.where(pick, 0.0, bias)
        vals = jnp.where(pick, -jnp.inf, vals)
    bias_ref[0, 0] = bias.astype(BF16)


def _sel_map_t(s):
    nc = s // CMP_STRIDE
    n_cmp = (s - CMP_BLOCK) // CMP_STRIDE + 1
    n_sel = s // SEL_BLOCK
    cs = CMP_STRIDE * np.arange(n_cmp)[:, None]
    ce = cs + CMP_BLOCK
    ss = SEL_BLOCK * np.arange(n_sel)[None, :]
    se = ss + SEL_BLOCK
    ov = np.clip(np.minimum(ce, se) - np.maximum(cs, ss), 0, None) / CMP_STRIDE
    out = np.zeros((n_sel, nc), np.float32)
    out[:, :n_cmp] = ov.T
    return out


def _cmp_sel(qt, kcmp, vcmp_t, tq=256):
    b, _, s = qt.shape
    nc = s // CMP_STRIDE
    nsel = s // SEL_BLOCK
    n_top = min(SEL_TOP, nsel)
    sel_map = jnp.asarray(np.tile(_sel_map_t(s), (1, 3)), dtype=BF16)
    return pl.pallas_call(
        functools.partial(_cmp_sel_kernel, n_top),
        grid=(b, KV_GROUPS, s // tq),
        in_specs=[
            pl.BlockSpec((1, GW, tq), lambda i, g, j: (i, g, j)),
            pl.BlockSpec((1, 1, nc, 2 * LANES), lambda i, g, j: (i, g, 0, 0)),
            pl.BlockSpec((1, DK, nc), lambda i, g, j: (i, g, 0)),
            pl.BlockSpec((nsel, 3 * nc), lambda i, g, j: (0, 0)),
        ],
        out_specs=[
            pl.BlockSpec((1, GW, tq), lambda i, g, j: (i, g, j)),
            pl.BlockSpec((1, 1, nsel, tq), lambda i, g, j: (i, g, 0, j)),
        ],
        out_shape=[
            jax.ShapeDtypeStruct((b, B_HEADS * DK, s), F32),
            jax.ShapeDtypeStruct((b, KV_GROUPS, nsel, s), BF16),
        ],
        compiler_params=_params(("parallel", "parallel", "parallel")),
        name="cmp_sel",
    )(qt, kcmp, vcmp_t, sel_map)


def _attn_kernel(kt, qt_ref, ks_ref, kw_ref, vst_ref, vwt_ref, bias_ref, oh_ref, oc_ref, glt_ref,
                 o_ref, qa_ref, s0_ref, s1_ref, m_ref, acc_ref, ow_ref):
    tq = qt_ref.shape[2]
    cols = HPG * tq
    nsel = bias_ref.shape[2]
    g = pl.program_id(1)
    q0 = pl.program_id(2) * tq

    q4t = _heads_on_lanes(qt_ref[0]).astype(BF16)
    zero = jnp.zeros_like(q4t)
    qa_ref[0:DK] = jnp.where(g == 0, q4t, zero)
    qa_ref[DK:2 * DK] = jnp.where(g == 1, q4t, zero)
    qa_ref[LANES:LANES + nsel] = jnp.concatenate([bias_ref[0, 0]] * HPG, axis=1)
    qaug = qa_ref[...]
    t_col = q0 + lax.broadcasted_iota(jnp.int32, (1, cols), 1) % tq

    def fold(x, op):
        return op(x.reshape(x.shape[0] // SUBLANES, SUBLANES, cols), axis=0)

    def with_ones(vt):
        return jnp.concatenate([vt, jnp.ones((ONES_ROWS, vt.shape[1]), BF16)], axis=0)

    def stage(c, s_ref):
        k0 = pl.multiple_of(c * kt, kt)
        kaug = jnp.concatenate([ks_ref[0, pl.ds(k0, kt), :], oh_ref[pl.ds(k0, kt), :]], axis=1)
        s_ref[...] = jnp.dot(kaug, qaug, preferred_element_type=F32)
        r0 = pl.multiple_of(jnp.clip(q0 - k0, 0, kt - tq), tq)
        kpos = k0 + r0 + lax.broadcasted_iota(jnp.int32, (tq, 1), 0)
        s_ref[pl.ds(r0, tq), :] = jnp.where(kpos <= t_col, s_ref[pl.ds(r0, tq), :], -jnp.inf)

    def consume(c, s_ref):
        k0 = pl.multiple_of(c * kt, kt)
        sc = s_ref[...]
        m_old = m_ref[...]
        m_new = jnp.maximum(m_old, jnp.max(fold(sc, jnp.max), axis=0, keepdims=True))
        alpha = jnp.exp2(m_old - m_new)
        pr = jnp.exp2(sc - m_new)
        acc_ref[...] = alpha * acc_ref[...] + jnp.dot(with_ones(vst_ref[0, :, pl.ds(k0, kt)]), pr.astype(BF16),
                                                      preferred_element_type=F32)
        m_ref[...] = m_new

    m_ref[...] = jnp.full(m_ref.shape, -jnp.inf, F32)
    acc_ref[...] = jnp.zeros(acc_ref.shape, F32)
    n_full = q0 // kt
    n_pairs = n_full // 2
    odd = n_full % 2 == 1
    stage(0, s0_ref)

    wlen = WINDOW + tq
    w0 = pl.multiple_of(jnp.maximum(q0 - WINDOW, 0), tq)
    sw = jnp.dot(kw_ref[0, pl.ds(w0, wlen), :], qaug[0:LANES], preferred_element_type=F32)
    dlt = t_col - (w0 + lax.broadcasted_iota(jnp.int32, (wlen, 1), 0))
    sw = jnp.where(lax.bitcast_convert_type(dlt, jnp.uint32) < jnp.uint32(WINDOW), sw, -jnp.inf)
    pw = jnp.exp2(sw - jnp.max(sw, axis=0, keepdims=True))
    o_win = jnp.dot(with_ones(vwt_ref[0, :, pl.ds(w0, wlen)]), pw.astype(BF16), preferred_element_type=F32)
    ow_ref[...] = o_win[0:DK] / o_win[DK:DK + 1]

    def pair(i, carry):
        stage(2 * i + 1, s1_ref)
        consume(2 * i, s0_ref)
        stage(2 * i + 2, s0_ref)
        consume(2 * i + 1, s1_ref)
        return carry

    lax.fori_loop(0, n_pairs, pair, 0)

    @pl.when(odd)
    def _():
        stage(n_full, s1_ref)
        consume(n_full - 1, s0_ref)
        consume(n_full, s1_ref)

    @pl.when(jnp.logical_not(odd))
    def _():
        consume(n_full, s0_ref)

    o_sel = acc_ref[0:DK] / acc_ref[DK:DK + 1]

    o_win = ow_ref[...]
    gates = jax.nn.sigmoid(glt_ref[0])
    oc = _heads_on_lanes(oc_ref[0])
    for p in range(HPG // 2):
        halves = []
        for hh in (2 * p, 2 * p + 1):
            cs = slice(hh * tq, (hh + 1) * tq)
            halves.append(gates[3 * hh:3 * hh + 1] * oc[:, cs] + gates[3 * hh + 1:3 * hh + 2] * o_sel[:, cs]
                          + gates[3 * hh + 2:3 * hh + 3] * o_win[:, cs])
        o_ref[0, :, p * LANES:(p + 1) * LANES] = jnp.concatenate(halves, axis=0).T.astype(o_ref.dtype)


def _attn(qt, ks, kw, vs_t, vw_t, bias_t, oc_t, gl_t, tq=256, kt=512):
    b, _, s = qt.shape
    nsel = s // SEL_BLOCK
    kt = min(kt, s)
    onehot = jnp.asarray((np.arange(s)[:, None] // SEL_BLOCK == np.arange(nsel)[None, :]), dtype=BF16)
    k_spec = pl.BlockSpec((1, s, KV_GROUPS * DK), lambda i, g, j: (i, 0, 0))
    vt_spec = pl.BlockSpec((1, DK, s), lambda i, g, j: (i, g, 0))
    cols = HPG * tq
    return pl.pallas_call(
        functools.partial(_attn_kernel, kt),
        grid=(b, KV_GROUPS, s // tq),
        in_specs=[
            pl.BlockSpec((1, GW, tq), lambda i, g, j: (i, g, j)),
            k_spec, k_spec, vt_spec, vt_spec,
            pl.BlockSpec((1, 1, nsel, tq), lambda i, g, j: (i, g, 0, j)),
            pl.BlockSpec((s, nsel), lambda i, g, j: (0, 0)),
            pl.BlockSpec((1, GW, tq), lambda i, g, j: (i, g, j)),
            pl.BlockSpec((1, LANES, tq), lambda i, g, j: (i, g, j)),
        ],
        out_specs=pl.BlockSpec((1, tq, GW), lambda i, g, j: (i, j, g)),
        out_shape=jax.ShapeDtypeStruct((b, s, B_HEADS * DK), BF16),
        scratch_shapes=[pltpu.VMEM((LANES + nsel, cols), BF16),
                        pltpu.VMEM((kt, cols), F32), pltpu.VMEM((kt, cols), F32),
                        pltpu.VMEM((1, cols), F32), pltpu.VMEM((DK + ONES_ROWS, cols), F32),
                        pltpu.VMEM((DK, cols), F32)],
        compiler_params=_params(("parallel", "parallel", "arbitrary")),
        name="attn",
    )(qt, ks, kw, vs_t, vw_t, bias_t, onehot, oc_t, gl_t)


def _post_kernel(ya_ref, yb_ref, wo_ref, x_ref, g1_ref, ng_ref, sc_ref, sh_ref, rw_ref, rb_ref,
                 xo_ref, h_ref, idx_ref, gate_ref, rank_ref, cnt_ref):
    tm = x_ref.shape[1]

    @pl.when((pl.program_id(0) == 0) & (pl.program_id(1) == 0))
    def _():
        cnt_ref[...] = jnp.zeros(cnt_ref.shape, F32)

    mixed = jnp.dot(ya_ref[0], wo_ref[:A_WIDTH], preferred_element_type=F32)
    mixed = mixed + jnp.dot(yb_ref[0], wo_ref[A_WIDTH:], preferred_element_type=F32)
    x = x_ref[0] + g1_ref[0] * mixed
    xo_ref[0] = x
    ms = jnp.mean(x * x, axis=-1, keepdims=True)
    h = x * lax.rsqrt(ms + NORM_EPS) * ng_ref[...]
    h = h * (1.0 + sc_ref[0]) + sh_ref[0]
    h_ref[0] = _pack_halves(h)
    h_hi, h_mid, h_lo = _split_bf16(h, 3)
    logits = jnp.dot(jnp.concatenate([h_hi, h_hi, h_mid, h_mid, h_hi, h_lo], axis=1), rw_ref[...],
                     preferred_element_type=F32) + rb_ref[...]
    lane = lax.broadcasted_iota(jnp.int32, logits.shape, 1)
    vals = logits
    top = jnp.max(vals, axis=-1, keepdims=True)
    picked = jnp.zeros_like(logits)
    slot_idx = jnp.zeros_like(logits)
    slot_exp = jnp.zeros_like(logits)
    for k in range(TOP_K):
        m = jnp.max(vals, axis=-1, keepdims=True)
        first = jnp.min(jnp.where(vals == m, lane, LANES), axis=-1, keepdims=True)
        pick = lane == first
        picked = jnp.where(pick, 1.0, picked)
        slot_idx = jnp.where(lane == k, first.astype(F32), slot_idx)
        slot_exp = jnp.where(lane == k, jnp.exp(m - top), slot_exp)
        vals = jnp.where(pick, -jnp.inf, vals)
    slot_gate = slot_exp / jnp.sum(slot_exp, axis=-1, keepdims=True)
    idx_t = slot_idx.T[:TOP_K]
    idx_ref[...] = idx_t.astype(jnp.int32)
    gate_ref[...] = slot_gate.T[:TOP_K]
    routed_t = picked.T[:N_EXPERTS]
    r = lax.broadcasted_iota(jnp.int32, (tm, tm), 0)
    c = lax.broadcasted_iota(jnp.int32, (tm, tm), 1)
    before = jnp.where(r < c, 1.0, 0.0).astype(BF16)
    rank_t = cnt_ref[...] + jnp.dot(routed_t.astype(BF16), before, preferred_element_type=F32)
    cnt_ref[...] += jnp.sum(routed_t, axis=-1, keepdims=True)
    e_row = lax.broadcasted_iota(jnp.int32, (N_EXPERTS, tm), 0).astype(F32)
    ranks = [jnp.sum(jnp.where(e_row == idx_t[k:k + 1], rank_t, 0.0), axis=0, keepdims=True)
             for k in range(TOP_K)]
    rank_ref[...] = jnp.concatenate(ranks, axis=0).astype(jnp.int32)


def _post(y_a, y_b, w_out, x, g1, norm_g, sc, sh, router_w, router_b, tm=512):
    b, s, d = x.shape
    tm = min(tm, s)
    nt = s // tm
    w_hi, w_mid, w_lo = _split_bf16(jnp.zeros((d, LANES), F32).at[:, :N_EXPERTS].set(router_w), 3)
    rw = jnp.concatenate([w_hi, w_mid, w_hi, w_mid, w_lo, w_hi], axis=0)
    rb = jnp.full((1, LANES), -1e30, F32).at[0, :N_EXPERTS].set(router_b)
    vec = pl.BlockSpec((1, 1, d), lambda i, j: (i, 0, 0))
    row = lambda w: pl.BlockSpec((1, tm, w), lambda i, j: (i, j, 0))
    return pl.pallas_call(
        _post_kernel,
        grid=(b, nt),
        in_specs=[
            row(A_WIDTH), row(B_HEADS * DK),
            pl.BlockSpec((d, d), lambda i, j: (0, 0)),
            row(d), vec,
            pl.BlockSpec((1, d), lambda i, j: (0, 0)), vec, vec,
            pl.BlockSpec((6 * d, LANES), lambda i, j: (0, 0)),
            pl.BlockSpec((1, LANES), lambda i, j: (0, 0)),
        ],
        out_specs=[row(d), row(d // 2)] + [pl.BlockSpec((TOP_K, tm), lambda i, j: (0, i * nt + j))] * 3,
        out_shape=[
            jax.ShapeDtypeStruct((b, s, d), F32),
            jax.ShapeDtypeStruct((b, s, d // 2), jnp.int32),
            jax.ShapeDtypeStruct((TOP_K, b * s), jnp.int32),
            jax.ShapeDtypeStruct((TOP_K, b * s), F32),
            jax.ShapeDtypeStruct((TOP_K, b * s), jnp.int32),
        ],
        scratch_shapes=[pltpu.VMEM((N_EXPERTS, 1), F32)],
        compiler_params=_params(("arbitrary", "arbitrary")),
        name="post",
    )(y_a, y_b, w_out.astype(BF16), x, g1, norm_g.reshape(1, d), sc, sh, rw, rb)


def _sc_mesh():
    return plsc.VectorSubcoreMesh(core_axis_name="c", subcore_axis_name="s")


def _sc_worker():
    return lax.axis_index("c") * SC_SUBCORES + lax.axis_index("s")


def _dispatch_rows(h2, dest_c, n_rows):
    t, d = h2.shape
    per_worker = t // SC_WINDOW // SC_WORKERS

    def body(x_hbm, i_hbm, o_hbm, buf, idx):
        worker = _sc_worker()

        @pl.loop(0, per_worker)
        def _(j):
            ch = worker * per_worker + j
            pltpu.sync_copy(i_hbm.at[ch], idx)
            pltpu.sync_copy(x_hbm.at[pl.ds(ch * SC_WINDOW, SC_WINDOW)], buf)
            for k in range(TOP_K):
                pltpu.sync_copy(buf, o_hbm.at[idx.at[k]])

    return pl.kernel(
        body, out_type=jax.ShapeDtypeStruct((n_rows, d), h2.dtype), mesh=_sc_mesh(),
        scratch_types=[pltpu.VMEM((SC_WINDOW, d), h2.dtype), pltpu.VMEM((TOP_K, SC_WINDOW), jnp.int32)],
        name="dispatch_rows",
    )(h2, dest_c)


def _collect_rows(rows, dest_c, t):
    d = rows.shape[1]
    per_worker = t // SC_WINDOW // SC_WORKERS

    def body(r_hbm, i_hbm, o_hbm, buf, idx):
        worker = _sc_worker()

        @pl.loop(0, per_worker)
        def _(j):
            ch = worker * per_worker + j
            pltpu.sync_copy(i_hbm.at[ch], idx)
            for k in range(TOP_K):
                pltpu.sync_copy(r_hbm.at[idx.at[k]], buf)
                pltpu.sync_copy(buf, o_hbm.at[k, pl.ds(ch * SC_WINDOW, SC_WINDOW)])

    return pl.kernel(
        body, out_type=jax.ShapeDtypeStruct((TOP_K, t, d), rows.dtype), mesh=_sc_mesh(),
        scratch_types=[pltpu.VMEM((SC_WINDOW, d), rows.dtype), pltpu.VMEM((TOP_K, SC_WINDOW), jnp.int32)],
        name="collect_rows",
    )(rows, dest_c)


def _experts_kernel(layer, be_ref, nb_ref, slot_ref, next_ref, x_ref, w1_hbm, b1_ref, w2_hbm, b2_ref, o_ref,
                    w1f_ref, w2f_ref, w1b_ref, w2b_ref, sem):
    i = pl.program_id(0)
    e = be_ref[i]
    live = i < nb_ref[0]

    def weight_copies(expert, slot):
        return (pltpu.make_async_copy(w1_hbm.at[layer, expert], w1f_ref.at[slot], sem.at[slot, 0]),
                pltpu.make_async_copy(w2_hbm.at[layer, expert], w2f_ref.at[slot], sem.at[slot, 1]))

    @pl.when(live & (i == 0))
    def _():
        for cp in weight_copies(e, 0):
            cp.start()

    @pl.when(live & ((i == 0) | (e != be_ref[jnp.maximum(i - 1, 0)])))
    def _():
        slot = slot_ref[i]
        for cp in weight_copies(e, slot):
            cp.wait()
        w1b_ref[...] = w1f_ref[slot].astype(BF16)
        w2b_ref[...] = w2f_ref[slot].astype(BF16)

        @pl.when(next_ref[i] >= 0)
        def _():
            for cp in weight_copies(next_ref[i], 1 - slot):
                cp.start()

    @pl.when(live)
    def _():
        x = _unpack_halves(x_ref[...]).astype(BF16)
        hdn = jnp.dot(x, w1b_ref[...], preferred_element_type=F32) + b1_ref[0, 0]
        glu = jnp.minimum(hdn[:, :D_EXPERT], SWIGLU_LIMIT)
        lin = jnp.clip(hdn[:, D_EXPERT:], -SWIGLU_LIMIT, SWIGLU_LIMIT)
        act = glu * jax.nn.sigmoid(SWIGLU_ALPHA * glu) * (lin + 1.0)
        out = jnp.dot(act.astype(BF16), w2b_ref[...], preferred_element_type=F32) + b2_ref[0, 0]
        o_ref[...] = _pack_halves(out)


def _experts(rows, block_exp, n_live, slot, next_exp, layer, w1, b1, w2, b2):
    n_rows, dp = rows.shape
    n_layers, n_e, d, f2 = w1.shape
    n_blocks = n_rows // ROW_BLOCK
    by_expert = lambda i, be, nb, sl, nx: (layer, be[i], 0, 0)
    grid_spec = pltpu.PrefetchScalarGridSpec(
        num_scalar_prefetch=4,
        grid=(n_blocks,),
        in_specs=[
            pl.BlockSpec((ROW_BLOCK, dp), lambda i, be, nb, sl, nx: (i, 0)),
            pl.BlockSpec(memory_space=pl.ANY),
            pl.BlockSpec((1, 1, 1, f2), by_expert),
            pl.BlockSpec(memory_space=pl.ANY),
            pl.BlockSpec((1, 1, 1, d), by_expert),
        ],
        out_specs=pl.BlockSpec((ROW_BLOCK, dp), lambda i, be, nb, sl, nx: (i, 0)),
        scratch_shapes=[pltpu.VMEM((2, d, f2), F32), pltpu.VMEM((2, f2 // 2, d), F32),
                        pltpu.VMEM((d, f2), BF16), pltpu.VMEM((f2 // 2, d), BF16),
                        pltpu.SemaphoreType.DMA((2, 2))],
    )
    return pl.pallas_call(
        functools.partial(_experts_kernel, layer),
        grid_spec=grid_spec,
        out_shape=jax.ShapeDtypeStruct((n_rows, dp), jnp.int32),
        compiler_params=_params(("arbitrary",)),
        name="experts",
    )(block_exp, n_live, slot, next_exp, rows, w1, b1.reshape(n_layers, n_e, 1, f2), w2,
      b2.reshape(n_layers, n_e, 1, d))


def _moe_routed(h2, idx, rank, layer, w1, b1, w2, b2):
    t, d = h2.shape
    n_e = w1.shape[1]
    n_blocks = t * TOP_K // ROW_BLOCK + n_e
    onehot = (idx[:, :, None] == jnp.arange(n_e, dtype=jnp.int32)).astype(jnp.int32)
    counts = jnp.sum(onehot, axis=(0, 1))
    padded = (counts + ROW_BLOCK - 1) // ROW_BLOCK * ROW_BLOCK
    pad_end = jnp.cumsum(padded)
    dest = jnp.sum(onehot * (pad_end - padded), axis=-1) + rank
    block_row = jnp.arange(n_blocks, dtype=jnp.int32)[:, None] * ROW_BLOCK
    block_exp = jnp.minimum(jnp.sum((pad_end[None, :] <= block_row).astype(jnp.int32), axis=-1), n_e - 1)
    n_live = (pad_end[-1:] // ROW_BLOCK).astype(jnp.int32)
    experts = jnp.arange(n_e, dtype=jnp.int32)
    later = (experts[None, :] > experts[:, None]) & (counts[None, :] > 0)
    next_of = jnp.min(jnp.where(later, experts[None, :], n_e), axis=-1)
    next_of = jnp.where(next_of == n_e, -1, next_of)
    block_onehot = (block_exp[:, None] == experts[None, :]).astype(jnp.int32)
    next_exp = jnp.sum(block_onehot * next_of, axis=-1)
    run_of = jnp.cumsum((counts > 0).astype(jnp.int32)) - 1
    slot = jnp.sum(block_onehot * run_of, axis=-1) % 2
    dest_c = dest.reshape(TOP_K, t // SC_WINDOW, SC_WINDOW).transpose(1, 0, 2)
    rows = _dispatch_rows(h2, dest_c, n_blocks * ROW_BLOCK)
    out = _experts(rows, block_exp, n_live, slot, next_exp, layer, w1, b1, w2, b2)
    return _collect_rows(out, dest_c, t)


def _final_kernel(x_ref, rows_ref, gate_ref, g2_ref, fg_ref, o_ref):
    x = _moe_residual(x_ref, rows_ref, gate_ref, g2_ref)
    ms = jnp.mean(x * x, axis=-1, keepdims=True)
    o_ref[0] = x * lax.rsqrt(ms + NORM_EPS) * fg_ref[...]


def _final(x, res, final_g, tm=512):
    b, s, d = x.shape
    tm = min(tm, s)
    row = pl.BlockSpec((1, tm, d), lambda i, j: (i, j, 0))
    return pl.pallas_call(
        _final_kernel,
        grid=(b, s // tm),
        in_specs=[row] + _moe_residual_specs(tm, d) + [pl.BlockSpec((1, d), lambda i, j: (0, 0))],
        out_specs=row,
        out_shape=jax.ShapeDtypeStruct((b, s, d), F32),
        compiler_params=_params(("parallel", "parallel")),
        name="final",
    )(x, *res, final_g.reshape(1, d))


def _widen_w_in(w):
    gl0 = 2 * A_WIDTH + B_HEADS * DK + 6 * KV_GROUPS * DK
    per_group = HPG * 3
    gl_groups = [jnp.pad(w[:, gl0 + g * per_group:gl0 + (g + 1) * per_group], ((0, 0), (0, LANES - per_group)))
                 for g in range(KV_GROUPS)]
    wide = jnp.concatenate([w[:, :gl0]] + gl_groups, axis=1)
    assert gl0 == C_GL and wide.shape[1] == P_WIDE
    return wide.astype(BF16)


def _rope_tables(pos):
    half = DK // 2
    inv = ROPE_THETA ** (-jnp.arange(half, dtype=F32) / half)
    ang = pos.astype(F32)[..., None] * inv
    cos = jnp.cos(ang)
    sin = jnp.sin(ang)
    reps = LANES // DK
    return (jnp.concatenate([cos, cos] * reps, axis=-1),
            jnp.concatenate([-sin, sin] * reps, axis=-1))


def kernel(x, c, positions, ada_w, ada_b, norm1_g, norm2_g, w_in, w_out, sg_ln_g, sg_ln_b, sg_w, sg_b,
           cmp_pe_k, cmp_pe_v, cmp_w1_k, cmp_w2_k, cmp_w1_v, cmp_w2_v, router_w, router_b,
           exp_w1, exp_b1, exp_w2, exp_b2, final_g):
    b, s, d = x.shape
    n_layers = ada_w.shape[0]
    mod = _ada_mod(c, ada_w, ada_b)
    cos, sin = _rope_tables(positions)
    cmp_end = jnp.minimum(CMP_STRIDE * jnp.arange(s // CMP_STRIDE) + CMP_BLOCK - 1, s - 1)
    cos_c, sin_c = _rope_tables(positions[:, cmp_end])
    res = None
    for l in range(n_layers):
        sh1, sc1, g1, sh2, sc2, g2 = [m.reshape(b, 1, d) for m in jnp.split(mod[l], 6, axis=-1)]
        x, uv, qt, kc, vc, ks, vs_t, kw, vw_t, gl_t = _pre(x, res, norm1_g[l], sc1, sh1,
                                                          _widen_w_in(w_in[l]), cos, sin)
        y_a = _sgu(uv, sg_ln_g[l], sg_ln_b[l], sg_w[l], sg_b[l])
        kcmp, vcmp_t = _compress(kc, vc, cmp_pe_k[l], cmp_pe_v[l], cmp_w1_k[l], cmp_w2_k[l],
                                 cmp_w1_v[l], cmp_w2_v[l], cos_c, sin_c)
        oc_t, bias_t = _cmp_sel(qt, kcmp, vcmp_t)
        y_b = _attn(qt, ks, kw, vs_t, vw_t, bias_t, oc_t, gl_t)
        x, h2, idx, gate, rank = _post(y_a, y_b, w_out[l], x, g1, norm2_g[l], sc2, sh2,
                                       router_w[l], router_b[l])
        rows = _moe_routed(h2.reshape(b * s, d // 2), idx, rank, l, exp_w1, exp_b1, exp_w2, exp_b2)
        res = (rows.reshape(TOP_K, b, s, d // 2), gate.T.reshape(b, s, TOP_K), g2)
    return _final(x, res, final_g)
```

```python
import functools

import numpy as np
import jax
import jax.numpy as jnp
from jax import lax
from jax.experimental import pallas as pl
from jax.experimental.pallas import tpu as pltpu
from jax.experimental.pallas import tpu_sc as plsc

F32 = jnp.float32
BF16 = jnp.bfloat16
HIGHEST = lax.Precision.HIGHEST

D_MODEL = 1024
A_WIDTH = 512
A_HEADS = 8
CHUNK = 128
B_HEADS = 8
DK = 64
KV_GROUPS = 2
HPG = B_HEADS // KV_GROUPS
CMP_BLOCK = 32
CMP_STRIDE = 16
SEL_BLOCK = 64
SEL_TOP = 16
WINDOW = 512
ROPE_THETA = 10000.0
N_EXPERTS = 32
TOP_K = 4
D_EXPERT = 1024
SWIGLU_LIMIT = 7.0
SWIGLU_ALPHA = 1.702
NORM_EPS = 1e-6

LANES = 128
SUBLANES = 8
GW = HPG * DK
N_FORCED = 3
ONES_ROWS = 16
MASK_BIAS = -32768.0
Q_SCALE = DK ** -0.5 * 1.4426950408889634
VMEM_LIMIT = 56 * 1024 * 1024
ROW_BLOCK = 256
SC_CORES = 2
SC_SUBCORES = 16
SC_WORKERS = SC_CORES * SC_SUBCORES
SC_WINDOW = 128
WEIGHT_DMA_CHUNKS = 4

C_U, C_V, C_Q = 0, 512, 1024
C_KC, C_VC, C_KS, C_VS, C_KW, C_VW = 1536, 1664, 1792, 1920, 2048, 2176
C_GL = 2304
P_WIDE = 2560


def _params(sem):
    return pltpu.CompilerParams(dimension_semantics=sem, vmem_limit_bytes=VMEM_LIMIT)


def _split_bf16(x, parts):
    out = []
    for _ in range(parts):
        piece = x.astype(BF16)
        out.append(piece)
        x = x - piece.astype(F32)
    return out


def _split_bf16_bits(x, parts):
    out = []
    for _ in range(parts):
        piece = lax.bitcast_convert_type(lax.bitcast_convert_type(x, jnp.uint32) & jnp.uint32(0xFFFF0000), F32)
        out.append(piece.astype(BF16))
        x = x - piece
    return out


def _ada_kernel(c_ref, w_ref, b_ref, o_ref):
    c = c_ref[...]
    cond = c * jax.nn.sigmoid(c)
    o_ref[0] = jnp.dot(cond, w_ref[0], precision=HIGHEST, preferred_element_type=F32) + b_ref[0]


def _ada_mod(c, ada_w, ada_b):
    n_layers, d, d6 = ada_w.shape
    b = c.shape[0]
    rows = SUBLANES
    c_pad = jnp.zeros((rows, d), F32).at[:b].set(c)
    out = pl.pallas_call(
        _ada_kernel,
        grid=(n_layers, d6 // d),
        in_specs=[
            pl.BlockSpec((rows, d), lambda l, j: (0, 0)),
            pl.BlockSpec((1, d, d), lambda l, j: (l, 0, j)),
            pl.BlockSpec((1, 1, d), lambda l, j: (l, 0, j)),
        ],
        out_specs=pl.BlockSpec((1, rows, d), lambda l, j: (l, 0, j)),
        out_shape=jax.ShapeDtypeStruct((n_layers, rows, d6), F32),
        compiler_params=_params(("arbitrary", "arbitrary")),
        name="ada",
    )(c_pad, ada_w, ada_b.reshape(n_layers, 1, d6))
    return out[:, :b]


def _rope_slab(t, cos, sin_signed, lo):
    partner = jnp.where(lo, pltpu.roll(t, LANES - DK // 2, 1), pltpu.roll(t, DK // 2, 1))
    return t * cos + partner * sin_signed


def _rope(t, cos, sin_signed):
    lane = lax.broadcasted_iota(jnp.int32, (1, LANES), 1)
    lo = (lane % DK) < (DK // 2)
    slabs = [_rope_slab(t[:, s * LANES:(s + 1) * LANES], cos, sin_signed, lo)
             for s in range(t.shape[1] // LANES)]
    return slabs[0] if len(slabs) == 1 else jnp.concatenate(slabs, axis=1)


def _pack_halves(x):
    n = x.shape[1] // 2
    lo = lax.bitcast_convert_type(x[:, :n].astype(BF16).astype(F32), jnp.int32)
    hi = lax.bitcast_convert_type(x[:, n:].astype(BF16).astype(F32), jnp.int32)
    return lax.shift_right_logical(lo, jnp.int32(16)) | (hi & jnp.int32(-65536))


def _unpack_halves(p):
    lo = lax.bitcast_convert_type(lax.shift_left(p, jnp.int32(16)), F32)
    hi = lax.bitcast_convert_type(p & jnp.int32(-65536), F32)
    return jnp.concatenate([lo, hi], axis=1)


def _moe_residual(x_ref, rows_ref, gate_ref, g2_ref):
    gate = gate_ref[0]
    y = gate[:, 0:1] * _unpack_halves(rows_ref[0, 0])
    for k in range(1, TOP_K):
        y = y + gate[:, k:k + 1] * _unpack_halves(rows_ref[k, 0])
    return x_ref[0] + g2_ref[0] * y


def _moe_residual_specs(tm, d):
    return [pl.BlockSpec((TOP_K, 1, tm, d // 2), lambda i, j: (0, i, j, 0)),
            pl.BlockSpec((1, tm, TOP_K), lambda i, j: (i, j, 0)),
            pl.BlockSpec((1, 1, d), lambda i, j: (i, 0, 0))]


def _pre_kernel(has_res, *refs):
    if has_res:
        (x_ref, rows_ref, gate_ref, g2_ref, ng_ref, sc_ref, sh_ref, w_ref, cos_ref, sin_ref,
         xo_ref, uv_ref, qt_ref, kc_ref, vc_ref, ks_ref, vst_ref, kw_ref, vwt_ref, glt_ref) = refs
        x = _moe_residual(x_ref, rows_ref, gate_ref, g2_ref)
        xo_ref[0] = x
    else:
        (x_ref, ng_ref, sc_ref, sh_ref, w_ref, cos_ref, sin_ref,
         uv_ref, qt_ref, kc_ref, vc_ref, ks_ref, vst_ref, kw_ref, vwt_ref, glt_ref) = refs
        x = x_ref[0]
    ms = jnp.mean(x * x, axis=-1, keepdims=True)
    h = x * lax.rsqrt(ms + NORM_EPS) * ng_ref[...]
    h = h * (1.0 + sc_ref[0]) + sh_ref[0]
    proj = jnp.dot(h.astype(BF16), w_ref[...], preferred_element_type=F32)
    cos = cos_ref[0]
    sin = sin_ref[0]
    uv_ref[0] = proj[:, C_U:C_Q]
    qt_ref[0] = (_rope(proj[:, C_Q:C_KC], cos, sin) * Q_SCALE).T
    kc_ref[0] = proj[:, C_KC:C_VC]
    vc_ref[0] = proj[:, C_VC:C_KS]
    ks_ref[0] = _rope(proj[:, C_KS:C_VS], cos, sin).astype(BF16)
    vst_ref[0] = proj[:, C_VS:C_KW].T.astype(BF16)
    kw_ref[0] = _rope(proj[:, C_KW:C_VW], cos, sin).astype(BF16)
    vwt_ref[0] = proj[:, C_VW:C_GL].T.astype(BF16)
    glt_ref[0] = proj[:, C_GL:P_WIDE].T


def _pre(x, res, norm_g, sc, sh, w_wide, cos, sin, tm=512):
    b, s, d = x.shape
    tm = min(tm, s)
    row = lambda w: pl.BlockSpec((1, tm, w), lambda i, j: (i, j, 0))
    col = lambda w: pl.BlockSpec((1, w, tm), lambda i, j: (i, 0, j))
    vec = pl.BlockSpec((1, 1, d), lambda i, j: (i, 0, 0))
    in_specs = [row(d)]
    args = [x]
    if res is not None:
        in_specs += _moe_residual_specs(tm, d)
        args += list(res)
    in_specs += [pl.BlockSpec((1, d), lambda i, j: (0, 0)), vec, vec,
                 pl.BlockSpec((d, P_WIDE), lambda i, j: (0, 0)), row(LANES), row(LANES)]
    args += [norm_g.reshape(1, d), sc, sh, w_wide, cos, sin]
    kv = KV_GROUPS * DK
    outs = [(d, F32, False)] if res is not None else []
    outs += [(2 * A_WIDTH, F32, False), (B_HEADS * DK, F32, True), (kv, F32, False), (kv, F32, False),
             (kv, BF16, False), (kv, BF16, True), (kv, BF16, False), (kv, BF16, True),
             (KV_GROUPS * LANES, F32, True)]
    res_out = pl.pallas_call(
        functools.partial(_pre_kernel, res is not None),
        grid=(b, s // tm),
        in_specs=in_specs,
        out_specs=[col(w) if t else row(w) for w, _, t in outs],
        out_shape=[jax.ShapeDtypeStruct((b, w, s) if t else (b, s, w), dt) for w, dt, t in outs],
        compiler_params=_params(("parallel", "parallel")),
        name="pre",
    )(*args)
    if res is None:
        res_out = [x] + list(res_out)
    return res_out


def _sgu_kernel(uv_ref, lng_ref, lnb_ref, w_ref, bias_ref, o_ref):
    rows = uv_ref.shape[1]
    uv = uv_ref[0]
    gu = jax.nn.gelu(uv[:, :A_WIDTH])
    gv = jax.nn.gelu(uv[:, A_WIDTH:])
    mu = jnp.mean(gv, axis=-1, keepdims=True)
    var = jnp.mean(jnp.square(gv - mu), axis=-1, keepdims=True)
    vn = ((gv - mu) * lax.rsqrt(var + NORM_EPS) * lng_ref[...] + lnb_ref[...]).astype(BF16)
    r = lax.broadcasted_iota(jnp.int32, (CHUNK, CHUNK), 0)
    c = lax.broadcasted_iota(jnp.int32, (CHUNK, CHUNK), 1)
    causal = c <= r
    lane_lo = lax.broadcasted_iota(jnp.int32, (CHUNK, LANES), 1) < DK
    for p in range(A_HEADS // 2):
        w0 = jnp.where(causal, w_ref[2 * p], 0.0).astype(BF16)
        w1 = jnp.where(causal, w_ref[2 * p + 1], 0.0).astype(BF16)
        bias = bias_ref[:, p * LANES:(p + 1) * LANES]
        for ch in range(rows // CHUNK):
            rs = slice(ch * CHUNK, (ch + 1) * CHUNK)
            cs = slice(p * LANES, (p + 1) * LANES)
            vp = vn[rs, cs]
            m0 = jnp.dot(w0, vp, preferred_element_type=F32)
            m1 = jnp.dot(w1, vp, preferred_element_type=F32)
            mixed = jnp.where(lane_lo, m0, m1) + bias
            o_ref[0, rs, cs] = (gu[rs, cs] * mixed).astype(o_ref.dtype)


def _sgu(uv, ln_g, ln_b, w_s, b_s, tm=512):
    b, s, _ = uv.shape
    tm = min(tm, s)
    bias = jnp.repeat(b_s.T, A_WIDTH // A_HEADS, axis=1)
    return pl.pallas_call(
        _sgu_kernel,
        grid=(b, s // tm),
        in_specs=[
            pl.BlockSpec((1, tm, 2 * A_WIDTH), lambda i, j: (i, j, 0)),
            pl.BlockSpec((1, A_WIDTH), lambda i, j: (0, 0)),
            pl.BlockSpec((1, A_WIDTH), lambda i, j: (0, 0)),
            pl.BlockSpec((A_HEADS, CHUNK, CHUNK), lambda i, j: (0, 0, 0)),
            pl.BlockSpec((CHUNK, A_WIDTH), lambda i, j: (0, 0)),
        ],
        out_specs=pl.BlockSpec((1, tm, A_WIDTH), lambda i, j: (i, j, 0)),
        out_shape=jax.ShapeDtypeStruct((b, s, A_WIDTH), BF16),
        compiler_params=_params(("parallel", "parallel")),
        name="sgu",
    )(uv, ln_g.reshape(1, -1), ln_b.reshape(1, -1), w_s, bias)


def _compress_kernel(kr_ref, vr_ref, pek_ref, pev_ref, w1k_ref, w1v_ref, w2k_ref, w2v_ref,
                     cos_ref, sin_ref, ko_ref, vt_ref):
    nc = kr_ref.shape[1]

    def mlp(r, pe_ref, w1_ref, w2_ref):
        top = jnp.dot(r + pe_ref[0:1], w1_ref[0], precision=HIGHEST, preferred_element_type=F32)
        bot = jnp.dot(r + pe_ref[1:2], w1_ref[1], precision=HIGHEST, preferred_element_type=F32)
        pre = top + pltpu.roll(bot, nc - 1, 0)
        return jnp.dot(jax.nn.gelu(pre), w2_ref[...], precision=HIGHEST, preferred_element_type=F32)

    kc = _rope(mlp(kr_ref[0], pek_ref, w1k_ref, w2k_ref), cos_ref[0], sin_ref[0])
    vc = mlp(vr_ref[0], pev_ref, w1v_ref, w2v_ref)
    lo = lax.broadcasted_iota(jnp.int32, (nc, LANES), 1) < DK
    rolled = pltpu.roll(kc, DK, 1)
    for g in range(KV_GROUPS):
        dup = jnp.where(lo, kc, rolled) if g == 0 else jnp.where(lo, rolled, kc)
        hi, low = _split_bf16(dup, 2)
        ko_ref[0, g, :, 0:LANES] = hi
        ko_ref[0, g, :, LANES:2 * LANES] = jnp.where(lo, low, jnp.zeros_like(low))
    vt_ref[0] = vc.T.astype(BF16)


def _compress_weights(pe, w1, w2):
    half = CMP_BLOCK // 2
    eye = jnp.eye(KV_GROUPS, dtype=F32)
    w1r = w1.reshape(CMP_BLOCK, DK, DK)
    wfull = jnp.einsum('lde,gh->lgdhe', w1r, eye)
    w1s = wfull.reshape(2, half * KV_GROUPS * DK, KV_GROUPS * DK)
    pes = jnp.broadcast_to(pe.reshape(2, half, 1, DK), (2, half, KV_GROUPS, DK)).reshape(2, -1)
    w2bd = jnp.einsum('de,gh->gdhe', w2, eye).reshape(KV_GROUPS * DK, KV_GROUPS * DK)
    return pes, w1s, w2bd


def _compress(kc, vc, pe_k, pe_v, w1_k, w2_k, w1_v, w2_v, cos_c, sin_c):
    b, s, _ = kc.shape
    nc = s // CMP_STRIDE
    rw = CMP_STRIDE * LANES
    pek, w1ks, w2kb = _compress_weights(pe_k, w1_k, w2_k)
    pev, w1vs, w2vb = _compress_weights(pe_v, w1_v, w2_v)
    full = lambda shape: pl.BlockSpec(shape, lambda i: (0,) * len(shape))
    return pl.pallas_call(
        _compress_kernel,
        grid=(b,),
        in_specs=[
            pl.BlockSpec((1, nc, rw), lambda i: (i, 0, 0)),
            pl.BlockSpec((1, nc, rw), lambda i: (i, 0, 0)),
            full((2, rw)), full((2, rw)),
            full((2, rw, LANES)), full((2, rw, LANES)),
            full((LANES, LANES)), full((LANES, LANES)),
            pl.BlockSpec((1, nc, LANES), lambda i: (i, 0, 0)),
            pl.BlockSpec((1, nc, LANES), lambda i: (i, 0, 0)),
        ],
        out_specs=[pl.BlockSpec((1, KV_GROUPS, nc, 2 * LANES), lambda i: (i, 0, 0, 0)),
                   pl.BlockSpec((1, KV_GROUPS * DK, nc), lambda i: (i, 0, 0))],
        out_shape=[jax.ShapeDtypeStruct((b, KV_GROUPS, nc, 2 * LANES), BF16),
                   jax.ShapeDtypeStruct((b, KV_GROUPS * DK, nc), BF16)],
        compiler_params=_params(("parallel",)),
        name="compress",
    )(kc.reshape(b, nc, rw), vc.reshape(b, nc, rw), pek, pev, w1ks, w1vs, w2kb, w2vb, cos_c, sin_c)


def _heads_on_lanes(qt):
    return jnp.concatenate([qt[h * DK:(h + 1) * DK] for h in range(HPG)], axis=1)


def _cmp_sel_kernel(n_top, qt_ref, k_ref, vt_ref, map_ref, oc_ref, bias_ref):
    tq = qt_ref.shape[2]
    cols = HPG * tq
    nc = k_ref.shape[2]
    nsel = map_ref.shape[0]
    q0 = pl.program_id(2) * tq
    q_hi, q_lo = _split_bf16(_heads_on_lanes(qt_ref[0]), 2)
    q3t = jnp.concatenate([q_hi, q_lo, q_hi, jnp.zeros_like(q_hi)], axis=0)
    s = jnp.dot(k_ref[0, 0], q3t, preferred_element_type=F32)
    n_idx = lax.broadcasted_iota(jnp.int32, (nc, 1), 0)
    t_col = q0 + lax.broadcasted_iota(jnp.int32, (1, cols), 1) % tq
    s = jnp.where((CMP_STRIDE * n_idx + CMP_BLOCK - 1) <= t_col, s, -jnp.inf)
    m = jnp.max(s, axis=0, keepdims=True)
    m = jnp.where(m == -jnp.inf, 0.0, m)
    e = jnp.exp2(s - m)
    d = jnp.sum(e, axis=0, keepdims=True)
    p = e / jnp.where(d > 0, d, 1.0)
    oc = jnp.dot(vt_ref[0], p.astype(BF16), preferred_element_type=F32)
    oc_ref[0] = jnp.concatenate([oc[:, h * tq:(h + 1) * tq] for h in range(HPG)], axis=0)
    psum = p[:, 0:tq]
    for h in range(1, HPG):
        psum = psum + p[:, h * tq:(h + 1) * tq]
    imp = jnp.dot(map_ref[...], jnp.concatenate(_split_bf16(psum, 3), axis=0),
                  preferred_element_type=F32)
    j = lax.broadcasted_iota(jnp.int32, (nsel, tq), 0)
    cur = (q0 + lax.broadcasted_iota(jnp.int32, (nsel, tq), 1)) // SEL_BLOCK
    valid = j <= cur
    forced = (j == 0) | (j == cur) | (j == cur - 1)
    keep = forced | (valid & (cur < n_top))
    vals = jnp.where(valid & jnp.logical_not(forced), imp, -jnp.inf)
    bias = jnp.where(keep, 0.0, MASK_BIAS)
    for _ in range(n_top - N_FORCED):
        mx = jnp.max(vals, axis=0, keepdims=True)
        first = jnp.min(jnp.where(vals == mx, j, nsel), axis=0, keepdims=True)
        pick = (j == first) & (mx > -jnp.inf)
        bias = jnp.where(pick, 0.0, bias)
        vals = jnp.where(pick, -jnp.inf, vals)
    bias_ref[0, 0] = bias.astype(BF16)


def _sel_map_t(s):
    nc = s // CMP_STRIDE
    n_cmp = (s - CMP_BLOCK) // CMP_STRIDE + 1
    n_sel = s // SEL_BLOCK
    cs = CMP_STRIDE * np.arange(n_cmp)[:, None]
    ce = cs + CMP_BLOCK
    ss = SEL_BLOCK * np.arange(n_sel)[None, :]
    se = ss + SEL_BLOCK
    ov = np.clip(np.minimum(ce, se) - np.maximum(cs, ss), 0, None) / CMP_STRIDE
    out = np.zeros((n_sel, nc), np.float32)
    out[:, :n_cmp] = ov.T
    return out


def _cmp_sel(qt, kcmp, vcmp_t, tq=256):
    b, _, s = qt.shape
    nc = s // CMP_STRIDE
    nsel = s // SEL_BLOCK
    n_top = min(SEL_TOP, nsel)
    sel_map = jnp.asarray(np.tile(_sel_map_t(s), (1, 3)), dtype=BF16)
    return pl.pallas_call(
        functools.partial(_cmp_sel_kernel, n_top),
        grid=(b, KV_GROUPS, s // tq),
        in_specs=[
            pl.BlockSpec((1, GW, tq), lambda i, g, j: (i, g, j)),
            pl.BlockSpec((1, 1, nc, 2 * LANES), lambda i, g, j: (i, g, 0, 0)),
            pl.BlockSpec((1, DK, nc), lambda i, g, j: (i, g, 0)),
            pl.BlockSpec((nsel, 3 * nc), lambda i, g, j: (0, 0)),
        ],
        out_specs=[
            pl.BlockSpec((1, GW, tq), lambda i, g, j: (i, g, j)),
            pl.BlockSpec((1, 1, nsel, tq), lambda i, g, j: (i, g, 0, j)),
        ],
        out_shape=[
            jax.ShapeDtypeStruct((b, B_HEADS * DK, s), F32),
            jax.ShapeDtypeStruct((b, KV_GROUPS, nsel, s), BF16),
        ],
        compiler_params=_params(("parallel", "parallel", "parallel")),
        name="cmp_sel",
    )(qt, kcmp, vcmp_t, sel_map)


def _attn_kernel(kt, qt_ref, ks_ref, kw_ref, vst_ref, vwt_ref, bias_ref, oh_ref, oc_ref, glt_ref,
                 o_ref, qa_ref, s0_ref, s1_ref, m_ref, acc_ref, ow_ref):
    tq = qt_ref.shape[2]
    cols = HPG * tq
    nsel = bias_ref.shape[2]
    g = pl.program_id(1)
    q0 = pl.program_id(2) * tq

    q4t = _heads_on_lanes(qt_ref[0]).astype(BF16)
    zero = jnp.zeros_like(q4t)
    qa_ref[0:DK] = jnp.where(g == 0, q4t, zero)
    qa_ref[DK:2 * DK] = jnp.where(g == 1, q4t, zero)
    qa_ref[LANES:LANES + nsel] = jnp.concatenate([bias_ref[0, 0]] * HPG, axis=1)
    qaug = qa_ref[...]
    t_col = q0 + lax.broadcasted_iota(jnp.int32, (1, cols), 1) % tq

    def fold(x, op):
        return op(x.reshape(x.shape[0] // SUBLANES, SUBLANES, cols), axis=0)

    def with_ones(vt):
        return jnp.concatenate([vt, jnp.ones((ONES_ROWS, vt.shape[1]), BF16)], axis=0)

    def stage(c, s_ref):
        k0 = pl.multiple_of(c * kt, kt)
        kaug = jnp.concatenate([ks_ref[0, pl.ds(k0, kt), :], oh_ref[pl.ds(k0, kt), :]], axis=1)
        s_ref[...] = jnp.dot(kaug, qaug, preferred_element_type=F32)
        r0 = pl.multiple_of(jnp.clip(q0 - k0, 0, kt - tq), tq)
        kpos = k0 + r0 + lax.broadcasted_iota(jnp.int32, (tq, 1), 0)
        s_ref[pl.ds(r0, tq), :] = jnp.where(kpos <= t_col, s_ref[pl.ds(r0, tq), :], -jnp.inf)

    def consume(c, s_ref):
        k0 = pl.multiple_of(c * kt, kt)
        sc = s_ref[...]
        m_old = m_ref[...]
        m_new = jnp.maximum(m_old, jnp.max(fold(sc, jnp.max), axis=0, keepdims=True))
        alpha = jnp.exp2(m_old - m_new)
        pr = jnp.exp2(sc - m_new)
        acc_ref[...] = alpha * acc_ref[...] + jnp.dot(with_ones(vst_ref[0, :, pl.ds(k0, kt)]), pr.astype(BF16),
                                                      preferred_element_type=F32)
        m_ref[...] = m_new

    m_ref[...] = jnp.full(m_ref.shape, -jnp.inf, F32)
    acc_ref[...] = jnp.zeros(acc_ref.shape, F32)
    n_full = q0 // kt
    n_pairs = n_full // 2
    odd = n_full % 2 == 1
    stage(0, s0_ref)

    wlen = WINDOW + tq
    w0 = pl.multiple_of(jnp.maximum(q0 - WINDOW, 0), tq)
    sw = jnp.dot(kw_ref[0, pl.ds(w0, wlen), :], qaug[0:LANES], preferred_element_type=F32)
    dlt = t_col - (w0 + lax.broadcasted_iota(jnp.int32, (wlen, 1), 0))
    sw = jnp.where(lax.bitcast_convert_type(dlt, jnp.uint32) < jnp.uint32(WINDOW), sw, -jnp.inf)
    pw = jnp.exp2(sw - jnp.max(sw, axis=0, keepdims=True))
    o_win = jnp.dot(with_ones(vwt_ref[0, :, pl.ds(w0, wlen)]), pw.astype(BF16), preferred_element_type=F32)
    ow_ref[...] = o_win[0:DK] / o_win[DK:DK + 1]

    def pair(i, carry):
        stage(2 * i + 1, s1_ref)
        consume(2 * i, s0_ref)
        stage(2 * i + 2, s0_ref)
        consume(2 * i + 1, s1_ref)
        return carry

    lax.fori_loop(0, n_pairs, pair, 0)

    @pl.when(odd)
    def _():
        stage(n_full, s1_ref)
        consume(n_full - 1, s0_ref)
        consume(n_full, s1_ref)

    @pl.when(jnp.logical_not(odd))
    def _():
        consume(n_full, s0_ref)

    o_sel = acc_ref[0:DK] / acc_ref[DK:DK + 1]

    o_win = ow_ref[...]
    gates = jax.nn.sigmoid(glt_ref[0])
    oc = _heads_on_lanes(oc_ref[0])
    for p in range(HPG // 2):
        halves = []
        for hh in (2 * p, 2 * p + 1):
            cs = slice(hh * tq, (hh + 1) * tq)
            halves.append(gates[3 * hh:3 * hh + 1] * oc[:, cs] + gates[3 * hh + 1:3 * hh + 2] * o_sel[:, cs]
                          + gates[3 * hh + 2:3 * hh + 3] * o_win[:, cs])
        o_ref[0, :, p * LANES:(p + 1) * LANES] = jnp.concatenate(halves, axis=0).T.astype(o_ref.dtype)


def _attn(qt, ks, kw, vs_t, vw_t, bias_t, oc_t, gl_t, tq=256, kt=512):
    b, _, s = qt.shape
    nsel = s // SEL_BLOCK
    kt = min(kt, s)
    onehot = jnp.asarray((np.arange(s)[:, None] // SEL_BLOCK == np.arange(nsel)[None, :]), dtype=BF16)
    k_spec = pl.BlockSpec((1, s, KV_GROUPS * DK), lambda i, g, j: (i, 0, 0))
    vt_spec = pl.BlockSpec((1, DK, s), lambda i, g, j: (i, g, 0))
    cols = HPG * tq
    return pl.pallas_call(
        functools.partial(_attn_kernel, kt),
        grid=(b, KV_GROUPS, s // tq),
        in_specs=[
            pl.BlockSpec((1, GW, tq), lambda i, g, j: (i, g, j)),
            k_spec, k_spec, vt_spec, vt_spec,
            pl.BlockSpec((1, 1, nsel, tq), lambda i, g, j: (i, g, 0, j)),
            pl.BlockSpec((s, nsel), lambda i, g, j: (0, 0)),
            pl.BlockSpec((1, GW, tq), lambda i, g, j: (i, g, j)),
            pl.BlockSpec((1, LANES, tq), lambda i, g, j: (i, g, j)),
        ],
        out_specs=pl.BlockSpec((1, tq, GW), lambda i, g, j: (i, j, g)),
        out_shape=jax.ShapeDtypeStruct((b, s, B_HEADS * DK), BF16),
        scratch_shapes=[pltpu.VMEM((LANES + nsel, cols), BF16),
                        pltpu.VMEM((kt, cols), F32), pltpu.VMEM((kt, cols), F32),
                        pltpu.VMEM((1, cols), F32), pltpu.VMEM((DK + ONES_ROWS, cols), F32),
                        pltpu.VMEM((DK, cols), F32)],
        compiler_params=_params(("parallel", "parallel", "arbitrary")),
        name="attn",
    )(qt, ks, kw, vs_t, vw_t, bias_t, onehot, oc_t, gl_t)


def _post_kernel(ya_ref, yb_ref, wo_ref, x_ref, g1_ref, ng_ref, sc_ref, sh_ref, rw_ref, rb_ref,
                 xo_ref, h_ref, idx_ref, gate_ref, rank_ref, cnt_ref):
    tm = x_ref.shape[1]

    @pl.when((pl.program_id(0) == 0) & (pl.program_id(1) == 0))
    def _():
        cnt_ref[...] = jnp.zeros(cnt_ref.shape, F32)

    mixed = jnp.dot(ya_ref[0], wo_ref[:A_WIDTH], preferred_element_type=F32)
    mixed = mixed + jnp.dot(yb_ref[0], wo_ref[A_WIDTH:], preferred_element_type=F32)
    x = x_ref[0] + g1_ref[0] * mixed
    xo_ref[0] = x
    ms = jnp.mean(x * x, axis=-1, keepdims=True)
    h = x * lax.rsqrt(ms + NORM_EPS) * ng_ref[...]
    h = h * (1.0 + sc_ref[0]) + sh_ref[0]
    h_ref[0] = _pack_halves(h)
    h_hi, h_mid = _split_bf16(h, 2)
    logits = jnp.dot(jnp.concatenate([h_hi, h_hi, h_mid], axis=1), rw_ref[...],
                     preferred_element_type=F32) + rb_ref[...]
    lane = lax.broadcasted_iota(jnp.int32, logits.shape, 1)
    vals = logits
    top = jnp.max(vals, axis=-1, keepdims=True)
    picked = jnp.zeros_like(logits)
    slot_idx = jnp.zeros_like(logits)
    slot_exp = jnp.zeros_like(logits)
    for k in range(TOP_K):
        m = jnp.max(vals, axis=-1, keepdims=True)
        first = jnp.min(jnp.where(vals == m, lane, LANES), axis=-1, keepdims=True)
        pick = lane == first
        picked = jnp.where(pick, 1.0, picked)
        slot_idx = jnp.where(lane == k, first.astype(F32), slot_idx)
        slot_exp = jnp.where(lane == k, jnp.exp(m - top), slot_exp)
        vals = jnp.where(pick, -jnp.inf, vals)
    slot_gate = slot_exp / jnp.sum(slot_exp, axis=-1, keepdims=True)
    idx_t = slot_idx.T[:TOP_K]
    idx_ref[...] = idx_t.astype(jnp.int32)
    gate_ref[...] = slot_gate.T[:TOP_K]
    routed_t = picked.T[:N_EXPERTS]
    r = lax.broadcasted_iota(jnp.int32, (tm, tm), 0)
    c = lax.broadcasted_iota(jnp.int32, (tm, tm), 1)
    before = jnp.where(r < c, 1.0, 0.0).astype(BF16)
    rank_t = cnt_ref[...] + jnp.dot(routed_t.astype(BF16), before, preferred_element_type=F32)
    cnt_ref[...] += jnp.sum(routed_t, axis=-1, keepdims=True)
    e_row = lax.broadcasted_iota(jnp.int32, (N_EXPERTS, tm), 0).astype(F32)
    ranks = [jnp.sum(jnp.where(e_row == idx_t[k:k + 1], rank_t, 0.0), axis=0, keepdims=True)
             for k in range(TOP_K)]
    rank_ref[...] = jnp.concatenate(ranks, axis=0).astype(jnp.int32)


def _post(y_a, y_b, w_out, x, g1, norm_g, sc, sh, router_w, router_b, tm=512):
    b, s, d = x.shape
    tm = min(tm, s)
    nt = s // tm
    w_hi, w_mid = _split_bf16_bits(jnp.zeros((d, LANES), F32).at[:, :N_EXPERTS].set(router_w), 2)
    rw = jnp.concatenate([w_hi, w_mid, w_hi], axis=0)
    rb = jnp.full((1, LANES), -1e30, F32).at[0, :N_EXPERTS].set(router_b)
    vec = pl.BlockSpec((1, 1, d), lambda i, j: (i, 0, 0))
    row = lambda w: pl.BlockSpec((1, tm, w), lambda i, j: (i, j, 0))
    return pl.pallas_call(
        _post_kernel,
        grid=(b, nt),
        in_specs=[
            row(A_WIDTH), row(B_HEADS * DK),
            pl.BlockSpec((d, d), lambda i, j: (0, 0)),
            row(d), vec,
            pl.BlockSpec((1, d), lambda i, j: (0, 0)), vec, vec,
            pl.BlockSpec((3 * d, LANES), lambda i, j: (0, 0)),
            pl.BlockSpec((1, LANES), lambda i, j: (0, 0)),
        ],
        out_specs=[row(d), row(d // 2)] + [pl.BlockSpec((TOP_K, tm), lambda i, j: (0, i * nt + j))] * 3,
        out_shape=[
            jax.ShapeDtypeStruct((b, s, d), F32),
            jax.ShapeDtypeStruct((b, s, d // 2), jnp.int32),
            jax.ShapeDtypeStruct((TOP_K, b * s), jnp.int32),
            jax.ShapeDtypeStruct((TOP_K, b * s), F32),
            jax.ShapeDtypeStruct((TOP_K, b * s), jnp.int32),
        ],
        scratch_shapes=[pltpu.VMEM((N_EXPERTS, 1), F32)],
        compiler_params=_params(("arbitrary", "arbitrary")),
        name="post",
    )(y_a, y_b, w_out.astype(BF16), x, g1, norm_g.reshape(1, d), sc, sh, rw, rb)


def _sc_mesh():
    return plsc.VectorSubcoreMesh(core_axis_name="c", subcore_axis_name="s")


def _sc_worker():
    return lax.axis_index("c") * SC_SUBCORES + lax.axis_index("s")


def _dispatch_rows(h2, dest_c, n_rows):
    t, d = h2.shape
    per_worker = t // SC_WINDOW // SC_WORKERS

    def body(x_hbm, i_hbm, o_hbm, buf, idx):
        worker = _sc_worker()

        @pl.loop(0, per_worker)
        def _(j):
            ch = worker * per_worker + j
            pltpu.sync_copy(i_hbm.at[ch], idx)
            pltpu.sync_copy(x_hbm.at[pl.ds(ch * SC_WINDOW, SC_WINDOW)], buf)
            for k in range(TOP_K):
                pltpu.sync_copy(buf, o_hbm.at[idx.at[k]])

    return pl.kernel(
        body, out_type=jax.ShapeDtypeStruct((n_rows, d), h2.dtype), mesh=_sc_mesh(),
        scratch_types=[pltpu.VMEM((SC_WINDOW, d), h2.dtype), pltpu.VMEM((TOP_K, SC_WINDOW), jnp.int32)],
        name="dispatch_rows",
    )(h2, dest_c)


def _collect_rows(rows, dest_c, t):
    d = rows.shape[1]
    per_worker = t // SC_WINDOW // SC_WORKERS

    def body(r_hbm, i_hbm, o_hbm, buf, idx):
        worker = _sc_worker()

        @pl.loop(0, per_worker)
        def _(j):
            ch = worker * per_worker + j
            pltpu.sync_copy(i_hbm.at[ch], idx)
            for k in range(TOP_K):
                pltpu.sync_copy(r_hbm.at[idx.at[k]], buf)
                pltpu.sync_copy(buf, o_hbm.at[k, pl.ds(ch * SC_WINDOW, SC_WINDOW)])

    return pl.kernel(
        body, out_type=jax.ShapeDtypeStruct((TOP_K, t, d), rows.dtype), mesh=_sc_mesh(),
        scratch_types=[pltpu.VMEM((SC_WINDOW, d), rows.dtype), pltpu.VMEM((TOP_K, SC_WINDOW), jnp.int32)],
        name="collect_rows",
    )(rows, dest_c)


def _experts_kernel(layer, be_ref, nb_ref, slot_ref, next_ref, x_ref, w1_hbm, b1_ref, w2_hbm, b2_ref, o_ref,
                    w1f_ref, w2f_ref, w1b_ref, w2b_ref, sem):
    i = pl.program_id(0)
    e = be_ref[i]
    live = i < nb_ref[0]

    def weight_copies(expert, slot):
        copies = []
        for hbm, buf in ((w1_hbm, w1f_ref), (w2_hbm, w2f_ref)):
            rows = buf.shape[1] // WEIGHT_DMA_CHUNKS
            for c in range(WEIGHT_DMA_CHUNKS):
                rs = pl.ds(c * rows, rows)
                copies.append(pltpu.make_async_copy(hbm.at[layer, expert, rs], buf.at[slot, rs],
                                                    sem.at[slot, len(copies)]))
        return copies

    @pl.when(live & (i == 0))
    def _():
        for cp in weight_copies(e, 0):
            cp.start()

    @pl.when(live & ((i == 0) | (e != be_ref[jnp.maximum(i - 1, 0)])))
    def _():
        slot = slot_ref[i]
        for cp in weight_copies(e, slot):
            cp.wait()
        w1b_ref[...] = w1f_ref[slot].astype(BF16)
        w2b_ref[...] = w2f_ref[slot].astype(BF16)

        @pl.when(next_ref[i] >= 0)
        def _():
            for cp in weight_copies(next_ref[i], 1 - slot):
                cp.start()

    @pl.when(live)
    def _():
        x = _unpack_halves(x_ref[...]).astype(BF16)
        hdn = jnp.dot(x, w1b_ref[...], preferred_element_type=F32) + b1_ref[0, 0]
        glu = jnp.minimum(hdn[:, :D_EXPERT], SWIGLU_LIMIT)
        lin = jnp.clip(hdn[:, D_EXPERT:], -SWIGLU_LIMIT, SWIGLU_LIMIT)
        act = glu * jax.nn.sigmoid(SWIGLU_ALPHA * glu) * (lin + 1.0)
        out = jnp.dot(act.astype(BF16), w2b_ref[...], preferred_element_type=F32) + b2_ref[0, 0]
        o_ref[...] = _pack_halves(out)


def _experts(rows, block_exp, n_live, slot, next_exp, layer, w1, b1, w2, b2):
    n_rows, dp = rows.shape
    n_layers, n_e, d, f2 = w1.shape
    n_blocks = n_rows // ROW_BLOCK
    by_expert = lambda i, be, nb, sl, nx: (layer, be[i], 0, 0)
    grid_spec = pltpu.PrefetchScalarGridSpec(
        num_scalar_prefetch=4,
        grid=(n_blocks,),
        in_specs=[
            pl.BlockSpec((ROW_BLOCK, dp), lambda i, be, nb, sl, nx: (i, 0)),
            pl.BlockSpec(memory_space=pl.ANY),
            pl.BlockSpec((1, 1, 1, f2), by_expert),
            pl.BlockSpec(memory_space=pl.ANY),
            pl.BlockSpec((1, 1, 1, d), by_expert),
        ],
        out_specs=pl.BlockSpec((ROW_BLOCK, dp), lambda i, be, nb, sl, nx: (i, 0)),
        scratch_shapes=[pltpu.VMEM((2, d, f2), F32), pltpu.VMEM((2, f2 // 2, d), F32),
                        pltpu.VMEM((d, f2), BF16), pltpu.VMEM((f2 // 2, d), BF16),
                        pltpu.SemaphoreType.DMA((2, 2 * WEIGHT_DMA_CHUNKS))],
    )
    return pl.pallas_call(
        functools.partial(_experts_kernel, layer),
        grid_spec=grid_spec,
        out_shape=jax.ShapeDtypeStruct((n_rows, dp), jnp.int32),
        compiler_params=_params(("arbitrary",)),
        name="experts",
    )(block_exp, n_live, slot, next_exp, rows, w1, b1.reshape(n_layers, n_e, 1, f2), w2,
      b2.reshape(n_layers, n_e, 1, d))


def _moe_routed(h2, idx, rank, layer, w1, b1, w2, b2):
    t, d = h2.shape
    n_e = w1.shape[1]
    n_blocks = t * TOP_K // ROW_BLOCK + n_e
    onehot = (idx[:, :, None] == jnp.arange(n_e, dtype=jnp.int32)).astype(jnp.int32)
    counts = jnp.sum(onehot, axis=(0, 1))
    padded = (counts + ROW_BLOCK - 1) // ROW_BLOCK * ROW_BLOCK
    pad_end = jnp.cumsum(padded)
    dest = jnp.sum(onehot * (pad_end - padded), axis=-1) + rank
    block_row = jnp.arange(n_blocks, dtype=jnp.int32)[:, None] * ROW_BLOCK
    block_exp = jnp.minimum(jnp.sum((pad_end[None, :] <= block_row).astype(jnp.int32), axis=-1), n_e - 1)
    n_live = (pad_end[-1:] // ROW_BLOCK).astype(jnp.int32)
    experts = jnp.arange(n_e, dtype=jnp.int32)
    later = (experts[None, :] > experts[:, None]) & (counts[None, :] > 0)
    next_of = jnp.min(jnp.where(later, experts[None, :], n_e), axis=-1)
    next_of = jnp.where(next_of == n_e, -1, next_of)
    block_onehot = (block_exp[:, None] == experts[None, :]).astype(jnp.int32)
    next_exp = jnp.sum(block_onehot * next_of, axis=-1)
    run_of = jnp.cumsum((counts > 0).astype(jnp.int32)) - 1
    slot = jnp.sum(block_onehot * run_of, axis=-1) % 2
    dest_c = dest.reshape(TOP_K, t // SC_WINDOW, SC_WINDOW).transpose(1, 0, 2)
    rows = _dispatch_rows(h2, dest_c, n_blocks * ROW_BLOCK)
    out = _experts(rows, block_exp, n_live, slot, next_exp, layer, w1, b1, w2, b2)
    return _collect_rows(out, dest_c, t)


def _final_kernel(x_ref, rows_ref, gate_ref, g2_ref, fg_ref, o_ref):
    x = _moe_residual(x_ref, rows_ref, gate_ref, g2_ref)
    ms = jnp.mean(x * x, axis=-1, keepdims=True)
    o_ref[0] = x * lax.rsqrt(ms + NORM_EPS) * fg_ref[...]


def _final(x, res, final_g, tm=512):
    b, s, d = x.shape
    tm = min(tm, s)
    row = pl.BlockSpec((1, tm, d), lambda i, j: (i, j, 0))
    return pl.pallas_call(
        _final_kernel,
        grid=(b, s // tm),
        in_specs=[row] + _moe_residual_specs(tm, d) + [pl.BlockSpec((1, d), lambda i, j: (0, 0))],
        out_specs=row,
        out_shape=jax.ShapeDtypeStruct((b, s, d), F32),
        compiler_params=_params(("parallel", "parallel")),
        name="final",
    )(x, *res, final_g.reshape(1, d))


def _widen_w_in(w):
    gl0 = 2 * A_WIDTH + B_HEADS * DK + 6 * KV_GROUPS * DK
    per_group = HPG * 3
    gl_groups = [jnp.pad(w[:, gl0 + g * per_group:gl0 + (g + 1) * per_group], ((0, 0), (0, LANES - per_group)))
                 for g in range(KV_GROUPS)]
    wide = jnp.concatenate([w[:, :gl0]] + gl_groups, axis=1)
    assert gl0 == C_GL and wide.shape[1] == P_WIDE
    return wide.astype(BF16)


def _rope_tables(pos):
    half = DK // 2
    inv = ROPE_THETA ** (-jnp.arange(half, dtype=F32) / half)
    ang = pos.astype(F32)[..., None] * inv
    cos = jnp.cos(ang)
    sin = jnp.sin(ang)
    reps = LANES // DK
    return (jnp.concatenate([cos, cos] * reps, axis=-1),
            jnp.concatenate([-sin, sin] * reps, axis=-1))


def kernel(x, c, positions, ada_w, ada_b, norm1_g, norm2_g, w_in, w_out, sg_ln_g, sg_ln_b, sg_w, sg_b,
           cmp_pe_k, cmp_pe_v, cmp_w1_k, cmp_w2_k, cmp_w1_v, cmp_w2_v, router_w, router_b,
           exp_w1, exp_b1, exp_w2, exp_b2, final_g):
    b, s, d = x.shape
    n_layers = ada_w.shape[0]
    mod = _ada_mod(c, ada_w, ada_b)
    cos, sin = _rope_tables(positions)
    cmp_end = jnp.minimum(CMP_STRIDE * jnp.arange(s // CMP_STRIDE) + CMP_BLOCK - 1, s - 1)
    cos_c, sin_c = _rope_tables(positions[:, cmp_end])
    res = None
    for l in range(n_layers):
        sh1, sc1, g1, sh2, sc2, g2 = [m.reshape(b, 1, d) for m in jnp.split(mod[l], 6, axis=-1)]
        x, uv, qt, kc, vc, ks, vs_t, kw, vw_t, gl_t = _pre(x, res, norm1_g[l], sc1, sh1,
                                                          _widen_w_in(w_in[l]), cos, sin)
        y_a = _sgu(uv, sg_ln_g[l], sg_ln_b[l], sg_w[l], sg_b[l])
        kcmp, vcmp_t = _compress(kc, vc, cmp_pe_k[l], cmp_pe_v[l], cmp_w1_k[l], cmp_w2_k[l],
                                 cmp_w1_v[l], cmp_w2_v[l], cos_c, sin_c)
        oc_t, bias_t = _cmp_sel(qt, kcmp, vcmp_t)
        y_b = _attn(qt, ks, kw, vs_t, vw_t, bias_t, oc_t, gl_t)
        x, h2, idx, gate, rank = _post(y_a, y_b, w_out[l], x, g1, norm2_g[l], sc2, sh2,
                                       router_w[l], router_b[l])
        rows = _moe_routed(h2.reshape(b * s, d // 2), idx, rank, l, exp_w1, exp_b1, exp_w2, exp_b2)
        res = (rows.reshape(TOP_K, b, s, d // 2), gate.T.reshape(b, s, TOP_K), g2)
    return _final(x, res, final_g)
```

```python
import functools

import numpy as np
import jax
import jax.numpy as jnp
from jax import lax
from jax.experimental import pallas as pl
from jax.experimental.pallas import tpu as pltpu
from jax.experimental.pallas import tpu_sc as plsc

F32 = jnp.float32
BF16 = jnp.bfloat16
HIGHEST = lax.Precision.HIGHEST

D_MODEL = 1024
A_WIDTH = 512
A_HEADS = 8
CHUNK = 128
B_HEADS = 8
DK = 64
KV_GROUPS = 2
HPG = B_HEADS // KV_GROUPS
CMP_BLOCK = 32
CMP_STRIDE = 16
SEL_BLOCK = 64
SEL_TOP = 16
WINDOW = 512
ROPE_THETA = 10000.0
N_EXPERTS = 32
TOP_K = 4
D_EXPERT = 1024
SWIGLU_LIMIT = 7.0
SWIGLU_ALPHA = 1.702
NORM_EPS = 1e-6

LANES = 128
SUBLANES = 8
GW = HPG * DK
N_FORCED = 3
ONES_ROWS = 16
MASK_BIAS = -32768.0
Q_SCALE = DK ** -0.5 * 1.4426950408889634
VMEM_LIMIT = 56 * 1024 * 1024
ROW_BLOCK = 256
SC_CORES = 2
SC_SUBCORES = 16
SC_WORKERS = SC_CORES * SC_SUBCORES
SC_WINDOW = 128
CMP_ROW_CHUNK = 128

C_U, C_V, C_Q = 0, 512, 1024
C_KC, C_VC, C_KS, C_VS, C_KW, C_VW = 1536, 1664, 1792, 1920, 2048, 2176
C_GL = 2304
P_WIDE = 2560


def _params(sem):
    return pltpu.CompilerParams(dimension_semantics=sem, vmem_limit_bytes=VMEM_LIMIT)


def _split_bf16(x, parts):
    out = []
    for _ in range(parts):
        piece = x.astype(BF16)
        out.append(piece)
        x = x - piece.astype(F32)
    return out


def _split_bf16_bits(x, parts):
    out = []
    for _ in range(parts):
        piece = lax.bitcast_convert_type(lax.bitcast_convert_type(x, jnp.uint32) & jnp.uint32(0xFFFF0000), F32)
        out.append(piece.astype(BF16))
        x = x - piece
    return out


def _ada_kernel(c_ref, w_ref, b_ref, o_ref):
    c = c_ref[...]
    cond = c * jax.nn.sigmoid(c)
    o_ref[0] = jnp.dot(cond, w_ref[0], precision=HIGHEST, preferred_element_type=F32) + b_ref[0]


def _ada_mod(c, ada_w, ada_b):
    n_layers, d, d6 = ada_w.shape
    b = c.shape[0]
    rows = SUBLANES
    c_pad = jnp.zeros((rows, d), F32).at[:b].set(c)
    out = pl.pallas_call(
        _ada_kernel,
        grid=(n_layers, d6 // d),
        in_specs=[
            pl.BlockSpec((rows, d), lambda l, j: (0, 0)),
            pl.BlockSpec((1, d, d), lambda l, j: (l, 0, j)),
            pl.BlockSpec((1, 1, d), lambda l, j: (l, 0, j)),
        ],
        out_specs=pl.BlockSpec((1, rows, d), lambda l, j: (l, 0, j)),
        out_shape=jax.ShapeDtypeStruct((n_layers, rows, d6), F32),
        compiler_params=_params(("arbitrary", "arbitrary")),
        name="ada",
    )(c_pad, ada_w, ada_b.reshape(n_layers, 1, d6))
    return out[:, :b]


def _rope_slab(t, cos, sin_signed, lo):
    partner = jnp.where(lo, pltpu.roll(t, LANES - DK // 2, 1), pltpu.roll(t, DK // 2, 1))
    return t * cos + partner * sin_signed


def _rope(t, cos, sin_signed):
    lane = lax.broadcasted_iota(jnp.int32, (1, LANES), 1)
    lo = (lane % DK) < (DK // 2)
    slabs = [_rope_slab(t[:, s * LANES:(s + 1) * LANES], cos, sin_signed, lo)
             for s in range(t.shape[1] // LANES)]
    return slabs[0] if len(slabs) == 1 else jnp.concatenate(slabs, axis=1)


def _pack_halves(x):
    n = x.shape[1] // 2
    lo = lax.bitcast_convert_type(x[:, :n].astype(BF16).astype(F32), jnp.int32)
    hi = lax.bitcast_convert_type(x[:, n:].astype(BF16).astype(F32), jnp.int32)
    return lax.shift_right_logical(lo, jnp.int32(16)) | (hi & jnp.int32(-65536))


def _unpack_halves(p):
    lo = lax.bitcast_convert_type(lax.shift_left(p, jnp.int32(16)), F32)
    hi = lax.bitcast_convert_type(p & jnp.int32(-65536), F32)
    return jnp.concatenate([lo, hi], axis=1)


def _moe_residual(x_ref, rows_ref, gate_ref, g2_ref):
    gate = gate_ref[0]
    y = gate[:, 0:1] * _unpack_halves(rows_ref[0, 0])
    for k in range(1, TOP_K):
        y = y + gate[:, k:k + 1] * _unpack_halves(rows_ref[k, 0])
    return x_ref[0] + g2_ref[0] * y


def _moe_residual_specs(tm, d):
    return [pl.BlockSpec((TOP_K, 1, tm, d // 2), lambda i, j: (0, i, j, 0)),
            pl.BlockSpec((1, tm, TOP_K), lambda i, j: (i, j, 0)),
            pl.BlockSpec((1, 1, d), lambda i, j: (i, 0, 0))]


def _pre_kernel(has_res, *refs):
    if has_res:
        (x_ref, rows_ref, gate_ref, g2_ref, ng_ref, sc_ref, sh_ref, w_ref, cos_ref, sin_ref,
         xo_ref, uv_ref, qt_ref, kc_ref, vc_ref, ks_ref, vst_ref, kw_ref, vwt_ref, glt_ref) = refs
        x = _moe_residual(x_ref, rows_ref, gate_ref, g2_ref)
        xo_ref[0] = x
    else:
        (x_ref, ng_ref, sc_ref, sh_ref, w_ref, cos_ref, sin_ref,
         uv_ref, qt_ref, kc_ref, vc_ref, ks_ref, vst_ref, kw_ref, vwt_ref, glt_ref) = refs
        x = x_ref[0]
    ms = jnp.mean(x * x, axis=-1, keepdims=True)
    h = x * lax.rsqrt(ms + NORM_EPS) * ng_ref[...]
    h = h * (1.0 + sc_ref[0]) + sh_ref[0]
    proj = jnp.dot(h.astype(BF16), w_ref[...], preferred_element_type=F32)
    cos = cos_ref[0]
    sin = sin_ref[0]
    uv_ref[0] = proj[:, C_U:C_Q]
    qt_ref[0] = (_rope(proj[:, C_Q:C_KC], cos, sin) * Q_SCALE).T
    kc_ref[0] = proj[:, C_KC:C_VC]
    vc_ref[0] = proj[:, C_VC:C_KS]
    ks_ref[0] = _rope(proj[:, C_KS:C_VS], cos, sin).astype(BF16)
    vst_ref[0] = proj[:, C_VS:C_KW].T.astype(BF16)
    kw_ref[0] = _rope(proj[:, C_KW:C_VW], cos, sin).astype(BF16)
    vwt_ref[0] = proj[:, C_VW:C_GL].T.astype(BF16)
    glt_ref[0] = proj[:, C_GL:P_WIDE].T


def _pre(x, res, norm_g, sc, sh, w_wide, cos, sin, tm=512):
    b, s, d = x.shape
    tm = min(tm, s)
    row = lambda w: pl.BlockSpec((1, tm, w), lambda i, j: (i, j, 0))
    col = lambda w: pl.BlockSpec((1, w, tm), lambda i, j: (i, 0, j))
    vec = pl.BlockSpec((1, 1, d), lambda i, j: (i, 0, 0))
    in_specs = [row(d)]
    args = [x]
    if res is not None:
        in_specs += _moe_residual_specs(tm, d)
        args += list(res)
    in_specs += [pl.BlockSpec((1, d), lambda i, j: (0, 0)), vec, vec,
                 pl.BlockSpec((d, P_WIDE), lambda i, j: (0, 0)), row(LANES), row(LANES)]
    args += [norm_g.reshape(1, d), sc, sh, w_wide, cos, sin]
    kv = KV_GROUPS * DK
    outs = [(d, F32, False)] if res is not None else []
    outs += [(2 * A_WIDTH, F32, False), (B_HEADS * DK, F32, True), (kv, F32, False), (kv, F32, False),
             (kv, BF16, False), (kv, BF16, True), (kv, BF16, False), (kv, BF16, True),
             (KV_GROUPS * LANES, F32, True)]
    res_out = pl.pallas_call(
        functools.partial(_pre_kernel, res is not None),
        grid=(b, s // tm),
        in_specs=in_specs,
        out_specs=[col(w) if t else row(w) for w, _, t in outs],
        out_shape=[jax.ShapeDtypeStruct((b, w, s) if t else (b, s, w), dt) for w, dt, t in outs],
        compiler_params=_params(("parallel", "parallel")),
        name="pre",
    )(*args)
    if res is None:
        res_out = [x] + list(res_out)
    return res_out


def _sgu_kernel(uv_ref, lng_ref, lnb_ref, w_ref, bias_ref, o_ref):
    rows = uv_ref.shape[1]
    uv = uv_ref[0]
    gu = jax.nn.gelu(uv[:, :A_WIDTH])
    gv = jax.nn.gelu(uv[:, A_WIDTH:])
    mu = jnp.mean(gv, axis=-1, keepdims=True)
    var = jnp.mean(jnp.square(gv - mu), axis=-1, keepdims=True)
    vn = ((gv - mu) * lax.rsqrt(var + NORM_EPS) * lng_ref[...] + lnb_ref[...]).astype(BF16)
    r = lax.broadcasted_iota(jnp.int32, (CHUNK, CHUNK), 0)
    c = lax.broadcasted_iota(jnp.int32, (CHUNK, CHUNK), 1)
    causal = c <= r
    lane_lo = lax.broadcasted_iota(jnp.int32, (CHUNK, LANES), 1) < DK
    for p in range(A_HEADS // 2):
        w0 = jnp.where(causal, w_ref[2 * p], 0.0).astype(BF16)
        w1 = jnp.where(causal, w_ref[2 * p + 1], 0.0).astype(BF16)
        bias = bias_ref[:, p * LANES:(p + 1) * LANES]
        for ch in range(rows // CHUNK):
            rs = slice(ch * CHUNK, (ch + 1) * CHUNK)
            cs = slice(p * LANES, (p + 1) * LANES)
            vp = vn[rs, cs]
            m0 = jnp.dot(w0, vp, preferred_element_type=F32)
            m1 = jnp.dot(w1, vp, preferred_element_type=F32)
            mixed = jnp.where(lane_lo, m0, m1) + bias
            o_ref[0, rs, cs] = (gu[rs, cs] * mixed).astype(o_ref.dtype)


def _sgu(uv, ln_g, ln_b, w_s, b_s, tm=512):
    b, s, _ = uv.shape
    tm = min(tm, s)
    bias = jnp.repeat(b_s.T, A_WIDTH // A_HEADS, axis=1)
    return pl.pallas_call(
        _sgu_kernel,
        grid=(b, s // tm),
        in_specs=[
            pl.BlockSpec((1, tm, 2 * A_WIDTH), lambda i, j: (i, j, 0)),
            pl.BlockSpec((1, A_WIDTH), lambda i, j: (0, 0)),
            pl.BlockSpec((1, A_WIDTH), lambda i, j: (0, 0)),
            pl.BlockSpec((A_HEADS, CHUNK, CHUNK), lambda i, j: (0, 0, 0)),
            pl.BlockSpec((CHUNK, A_WIDTH), lambda i, j: (0, 0)),
        ],
        out_specs=pl.BlockSpec((1, tm, A_WIDTH), lambda i, j: (i, j, 0)),
        out_shape=jax.ShapeDtypeStruct((b, s, A_WIDTH), BF16),
        compiler_params=_params(("parallel", "parallel")),
        name="sgu",
    )(uv, ln_g.reshape(1, -1), ln_b.reshape(1, -1), w_s, bias)


def _compress_kernel(kr_ref, vr_ref, pek_ref, pev_ref, w1k_ref, w1v_ref, w2k_ref, w2v_ref,
                     cos_ref, sin_ref, ko_ref, vt_ref):
    nc = kr_ref.shape[1]

    def mlp(r, pe_ref, w1_ref, w2_ref):
        top = jnp.dot(r + pe_ref[0:1], w1_ref[0], precision=HIGHEST, preferred_element_type=F32)
        bot = jnp.dot(r + pe_ref[1:2], w1_ref[1], precision=HIGHEST, preferred_element_type=F32)
        pre = top + pltpu.roll(bot, nc - 1, 0)
        return jnp.dot(jax.nn.gelu(pre), w2_ref[...], precision=HIGHEST, preferred_element_type=F32)

    kc = _rope(mlp(kr_ref[0], pek_ref, w1k_ref, w2k_ref), cos_ref[0], sin_ref[0])
    vc = mlp(vr_ref[0], pev_ref, w1v_ref, w2v_ref)
    lo = lax.broadcasted_iota(jnp.int32, (nc, LANES), 1) < DK
    rolled = pltpu.roll(kc, DK, 1)
    for g in range(KV_GROUPS):
        dup = jnp.where(lo, kc, rolled) if g == 0 else jnp.where(lo, rolled, kc)
        hi, low = _split_bf16(dup, 2)
        ko_ref[0, g, :, 0:LANES] = hi
        ko_ref[0, g, :, LANES:2 * LANES] = jnp.where(lo, low, jnp.zeros_like(low))
    vt_ref[0] = vc.T.astype(BF16)


def _compress_weights(pe, w1, w2):
    half = CMP_BLOCK // 2
    eye = jnp.eye(KV_GROUPS, dtype=F32)
    w1r = w1.reshape(CMP_BLOCK, DK, DK)
    wfull = jnp.einsum('lde,gh->lgdhe', w1r, eye)
    w1s = wfull.reshape(2, half * KV_GROUPS * DK, KV_GROUPS * DK)
    pes = jnp.broadcast_to(pe.reshape(2, half, 1, DK), (2, half, KV_GROUPS, DK)).reshape(2, -1)
    w2bd = jnp.einsum('de,gh->gdhe', w2, eye).reshape(KV_GROUPS * DK, KV_GROUPS * DK)
    return pes, w1s, w2bd


def _compress(kc, vc, pe_k, pe_v, w1_k, w2_k, w1_v, w2_v, cos_c, sin_c):
    b, s, _ = kc.shape
    nc = s // CMP_STRIDE
    rw = CMP_STRIDE * LANES
    pek, w1ks, w2kb = _compress_weights(pe_k, w1_k, w2_k)
    pev, w1vs, w2vb = _compress_weights(pe_v, w1_v, w2_v)
    full = lambda shape: pl.BlockSpec(shape, lambda i: (0,) * len(shape))
    return pl.pallas_call(
        _compress_kernel,
        grid=(b,),
        in_specs=[
            pl.BlockSpec((1, nc, rw), lambda i: (i, 0, 0)),
            pl.BlockSpec((1, nc, rw), lambda i: (i, 0, 0)),
            full((2, rw)), full((2, rw)),
            full((2, rw, LANES)), full((2, rw, LANES)),
            full((LANES, LANES)), full((LANES, LANES)),
            pl.BlockSpec((1, nc, LANES), lambda i: (i, 0, 0)),
            pl.BlockSpec((1, nc, LANES), lambda i: (i, 0, 0)),
        ],
        out_specs=[pl.BlockSpec((1, KV_GROUPS, nc, 2 * LANES), lambda i: (i, 0, 0, 0)),
                   pl.BlockSpec((1, KV_GROUPS * DK, nc), lambda i: (i, 0, 0))],
        out_shape=[jax.ShapeDtypeStruct((b, KV_GROUPS, nc, 2 * LANES), BF16),
                   jax.ShapeDtypeStruct((b, KV_GROUPS * DK, nc), BF16)],
        compiler_params=_params(("parallel",)),
        name="compress",
    )(kc.reshape(b, nc, rw), vc.reshape(b, nc, rw), pek, pev, w1ks, w1vs, w2kb, w2vb, cos_c, sin_c)


def _heads_on_lanes(qt):
    return jnp.concatenate([qt[h * DK:(h + 1) * DK] for h in range(HPG)], axis=1)


def _cmp_sel_kernel(n_top, qt_ref, k_ref, vt_ref, map_ref, oc_ref, bias_ref, imp_ref):
    tq = qt_ref.shape[2]
    cols = HPG * tq
    nc = k_ref.shape[2]
    nsel = map_ref.shape[0]
    q0 = pl.program_id(2) * tq
    q_hi, q_lo = _split_bf16(_heads_on_lanes(qt_ref[0]), 2)
    q3t = jnp.concatenate([q_hi, q_lo, q_hi, jnp.zeros_like(q_hi)], axis=0)
    t_col = q0 + lax.broadcasted_iota(jnp.int32, (1, cols), 1) % tq

    def compressed_branch(rows):
        s = jnp.dot(k_ref[0, 0, 0:rows], q3t, preferred_element_type=F32)
        n_idx = lax.broadcasted_iota(jnp.int32, (rows, 1), 0)
        s = jnp.where((CMP_STRIDE * n_idx + CMP_BLOCK - 1) <= t_col, s, -jnp.inf)
        m = jnp.max(s, axis=0, keepdims=True)
        m = jnp.where(m == -jnp.inf, 0.0, m)
        e = jnp.exp2(s - m)
        d = jnp.sum(e, axis=0, keepdims=True)
        p = e / jnp.where(d > 0, d, 1.0)
        oc = jnp.dot(vt_ref[0, :, 0:rows], p.astype(BF16), preferred_element_type=F32)
        oc_ref[0] = jnp.concatenate([oc[:, h * tq:(h + 1) * tq] for h in range(HPG)], axis=0)
        psum = p[:, 0:tq]
        for h in range(1, HPG):
            psum = psum + p[:, h * tq:(h + 1) * tq]
        imp_ref[...] = jnp.dot(jnp.concatenate([map_ref[:, 0:rows]] * 3, axis=1),
                               jnp.concatenate(_split_bf16(psum, 3), axis=0), preferred_element_type=F32)

    visible = (q0 + tq - CMP_BLOCK) // CMP_STRIDE + 1
    n_chunks = -(-nc // CMP_ROW_CHUNK)
    for r in range(1, n_chunks + 1):
        lo = (r - 1) * CMP_ROW_CHUNK if r > 1 else -nc
        pl.when((visible > lo) & (visible <= r * CMP_ROW_CHUNK) if r < n_chunks else visible > lo)(
            functools.partial(compressed_branch, min(r * CMP_ROW_CHUNK, nc)))

    imp = imp_ref[...]
    j = lax.broadcasted_iota(jnp.int32, (nsel, tq), 0)
    cur = (q0 + lax.broadcasted_iota(jnp.int32, (nsel, tq), 1)) // SEL_BLOCK
    valid = j <= cur
    forced = (j == 0) | (j == cur) | (j == cur - 1)
    keep = forced | (valid & (cur < n_top))
    vals = jnp.where(valid & jnp.logical_not(forced), imp, -jnp.inf)
    bias = jnp.where(keep, 0.0, MASK_BIAS)
    for _ in range(n_top - N_FORCED):
        mx = jnp.max(vals, axis=0, keepdims=True)
        first = jnp.min(jnp.where(vals == mx, j, nsel), axis=0, keepdims=True)
        pick = (j == first) & (mx > -jnp.inf)
        bias = jnp.where(pick, 0.0, bias)
        vals = jnp.where(pick, -jnp.inf, vals)
    bias_ref[0, 0] = bias.astype(BF16)


def _sel_map_t(s):
    nc = s // CMP_STRIDE
    n_cmp = (s - CMP_BLOCK) // CMP_STRIDE + 1
    n_sel = s // SEL_BLOCK
    cs = CMP_STRIDE * np.arange(n_cmp)[:, None]
    ce = cs + CMP_BLOCK
    ss = SEL_BLOCK * np.arange(n_sel)[None, :]
    se = ss + SEL_BLOCK
    ov = np.clip(np.minimum(ce, se) - np.maximum(cs, ss), 0, None) / CMP_STRIDE
    out = np.zeros((n_sel, nc), np.float32)
    out[:, :n_cmp] = ov.T
    return out


def _cmp_sel(qt, kcmp, vcmp_t, tq=256):
    b, _, s = qt.shape
    nc = s // CMP_STRIDE
    nsel = s // SEL_BLOCK
    n_top = min(SEL_TOP, nsel)
    sel_map = jnp.asarray(_sel_map_t(s), dtype=BF16)
    return pl.pallas_call(
        functools.partial(_cmp_sel_kernel, n_top),
        grid=(b, KV_GROUPS, s // tq),
        in_specs=[
            pl.BlockSpec((1, GW, tq), lambda i, g, j: (i, g, j)),
            pl.BlockSpec((1, 1, nc, 2 * LANES), lambda i, g, j: (i, g, 0, 0)),
            pl.BlockSpec((1, DK, nc), lambda i, g, j: (i, g, 0)),
            pl.BlockSpec((nsel, nc), lambda i, g, j: (0, 0)),
        ],
        out_specs=[
            pl.BlockSpec((1, GW, tq), lambda i, g, j: (i, g, j)),
            pl.BlockSpec((1, 1, nsel, tq), lambda i, g, j: (i, g, 0, j)),
        ],
        out_shape=[
            jax.ShapeDtypeStruct((b, B_HEADS * DK, s), F32),
            jax.ShapeDtypeStruct((b, KV_GROUPS, nsel, s), BF16),
        ],
        scratch_shapes=[pltpu.VMEM((nsel, tq), F32)],
        compiler_params=_params(("parallel", "parallel", "parallel")),
        name="cmp_sel",
    )(qt, kcmp, vcmp_t, sel_map)


def _attn_kernel(kt, qt_ref, ks_ref, kw_ref, vst_ref, vwt_ref, bias_ref, oh_ref, oc_ref, glt_ref,
                 o_ref, qa_ref, s0_ref, s1_ref, m_ref, acc_ref, ow_ref):
    tq = qt_ref.shape[2]
    cols = HPG * tq
    nsel = bias_ref.shape[2]
    g = pl.program_id(1)
    q0 = pl.program_id(2) * tq

    q4t = _heads_on_lanes(qt_ref[0]).astype(BF16)
    zero = jnp.zeros_like(q4t)
    qa_ref[0:DK] = jnp.where(g == 0, q4t, zero)
    qa_ref[DK:2 * DK] = jnp.where(g == 1, q4t, zero)
    qa_ref[LANES:LANES + nsel] = jnp.concatenate([bias_ref[0, 0]] * HPG, axis=1)
    qaug = qa_ref[...]
    t_col = q0 + lax.broadcasted_iota(jnp.int32, (1, cols), 1) % tq

    def fold(x, op):
        return op(x.reshape(x.shape[0] // SUBLANES, SUBLANES, cols), axis=0)

    def with_ones(vt):
        return jnp.concatenate([vt, jnp.ones((ONES_ROWS, vt.shape[1]), BF16)], axis=0)

    def stage(c, s_ref):
        k0 = pl.multiple_of(c * kt, kt)
        kaug = jnp.concatenate([ks_ref[0, pl.ds(k0, kt), :], oh_ref[pl.ds(k0, kt), :]], axis=1)
        s_ref[...] = jnp.dot(kaug, qaug, preferred_element_type=F32)
        r0 = pl.multiple_of(jnp.clip(q0 - k0, 0, kt - tq), tq)
        kpos = k0 + r0 + lax.broadcasted_iota(jnp.int32, (tq, 1), 0)
        s_ref[pl.ds(r0, tq), :] = jnp.where(kpos <= t_col, s_ref[pl.ds(r0, tq), :], -jnp.inf)

    def consume(c, s_ref):
        k0 = pl.multiple_of(c * kt, kt)
        sc = s_ref[...]
        m_old = m_ref[...]
        m_new = jnp.maximum(m_old, jnp.max(fold(sc, jnp.max), axis=0, keepdims=True))
        alpha = jnp.exp2(m_old - m_new)
        pr = jnp.exp2(sc - m_new)
        acc_ref[...] = alpha * acc_ref[...] + jnp.dot(with_ones(vst_ref[0, :, pl.ds(k0, kt)]), pr.astype(BF16),
                                                      preferred_element_type=F32)
        m_ref[...] = m_new

    m_ref[...] = jnp.full(m_ref.shape, -jnp.inf, F32)
    acc_ref[...] = jnp.zeros(acc_ref.shape, F32)
    n_full = q0 // kt
    n_pairs = n_full // 2
    odd = n_full % 2 == 1
    stage(0, s0_ref)

    wlen = WINDOW + tq
    w0 = pl.multiple_of(jnp.maximum(q0 - WINDOW, 0), tq)
    sw = jnp.dot(kw_ref[0, pl.ds(w0, wlen), :], qaug[0:LANES], preferred_element_type=F32)
    dlt = t_col - (w0 + lax.broadcasted_iota(jnp.int32, (wlen, 1), 0))
    sw = jnp.where(lax.bitcast_convert_type(dlt, jnp.uint32) < jnp.uint32(WINDOW), sw, -jnp.inf)
    pw = jnp.exp2(sw - jnp.max(sw, axis=0, keepdims=True))
    o_win = jnp.dot(with_ones(vwt_ref[0, :, pl.ds(w0, wlen)]), pw.astype(BF16), preferred_element_type=F32)
    ow_ref[...] = o_win[0:DK] / o_win[DK:DK + 1]

    def pair(i, carry):
        stage(2 * i + 1, s1_ref)
        consume(2 * i, s0_ref)
        stage(2 * i + 2, s0_ref)
        consume(2 * i + 1, s1_ref)
        return carry

    lax.fori_loop(0, n_pairs, pair, 0)

    @pl.when(odd)
    def _():
        stage(n_full, s1_ref)
        consume(n_full - 1, s0_ref)
        consume(n_full, s1_ref)

    @pl.when(jnp.logical_not(odd))
    def _():
        consume(n_full, s0_ref)

    o_sel = acc_ref[0:DK] / acc_ref[DK:DK + 1]

    o_win = ow_ref[...]
    gates = jax.nn.sigmoid(glt_ref[0])
    oc = _heads_on_lanes(oc_ref[0])
    for p in range(HPG // 2):
        halves = []
        for hh in (2 * p, 2 * p + 1):
            cs = slice(hh * tq, (hh + 1) * tq)
            halves.append(gates[3 * hh:3 * hh + 1] * oc[:, cs] + gates[3 * hh + 1:3 * hh + 2] * o_sel[:, cs]
                          + gates[3 * hh + 2:3 * hh + 3] * o_win[:, cs])
        o_ref[0, :, p * LANES:(p + 1) * LANES] = jnp.concatenate(halves, axis=0).T.astype(o_ref.dtype)


def _attn(qt, ks, kw, vs_t, vw_t, bias_t, oc_t, gl_t, tq=256, kt=512):
    b, _, s = qt.shape
    nsel = s // SEL_BLOCK
    kt = min(kt, s)
    onehot = jnp.asarray((np.arange(s)[:, None] // SEL_BLOCK == np.arange(nsel)[None, :]), dtype=BF16)
    k_spec = pl.BlockSpec((1, s, KV_GROUPS * DK), lambda i, g, j: (i, 0, 0))
    vt_spec = pl.BlockSpec((1, DK, s), lambda i, g, j: (i, g, 0))
    cols = HPG * tq
    return pl.pallas_call(
        functools.partial(_attn_kernel, kt),
        grid=(b, KV_GROUPS, s // tq),
        in_specs=[
            pl.BlockSpec((1, GW, tq), lambda i, g, j: (i, g, j)),
            k_spec, k_spec, vt_spec, vt_spec,
            pl.BlockSpec((1, 1, nsel, tq), lambda i, g, j: (i, g, 0, j)),
            pl.BlockSpec((s, nsel), lambda i, g, j: (0, 0)),
            pl.BlockSpec((1, GW, tq), lambda i, g, j: (i, g, j)),
            pl.BlockSpec((1, LANES, tq), lambda i, g, j: (i, g, j)),
        ],
        out_specs=pl.BlockSpec((1, tq, GW), lambda i, g, j: (i, j, g)),
        out_shape=jax.ShapeDtypeStruct((b, s, B_HEADS * DK), BF16),
        scratch_shapes=[pltpu.VMEM((LANES + nsel, cols), BF16),
                        pltpu.VMEM((kt, cols), F32), pltpu.VMEM((kt, cols), F32),
                        pltpu.VMEM((1, cols), F32), pltpu.VMEM((DK + ONES_ROWS, cols), F32),
                        pltpu.VMEM((DK, cols), F32)],
        compiler_params=_params(("parallel", "parallel", "arbitrary")),
        name="attn",
    )(qt, ks, kw, vs_t, vw_t, bias_t, onehot, oc_t, gl_t)


def _post_kernel(ya_ref, yb_ref, wo_ref, x_ref, g1_ref, ng_ref, sc_ref, sh_ref, rw_ref, rb_ref,
                 xo_ref, h_ref, idx_ref, gate_ref, rank_ref, cnt_ref):
    tm = x_ref.shape[1]

    @pl.when((pl.program_id(0) == 0) & (pl.program_id(1) == 0))
    def _():
        cnt_ref[...] = jnp.zeros(cnt_ref.shape, F32)

    mixed = jnp.dot(ya_ref[0], wo_ref[:A_WIDTH], preferred_element_type=F32)
    mixed = mixed + jnp.dot(yb_ref[0], wo_ref[A_WIDTH:], preferred_element_type=F32)
    x = x_ref[0] + g1_ref[0] * mixed
    xo_ref[0] = x
    ms = jnp.mean(x * x, axis=-1, keepdims=True)
    h = x * lax.rsqrt(ms + NORM_EPS) * ng_ref[...]
    h = h * (1.0 + sc_ref[0]) + sh_ref[0]
    h_ref[0] = _pack_halves(h)
    h_hi, h_mid = _split_bf16(h, 2)
    logits = jnp.dot(jnp.concatenate([h_hi, h_hi, h_mid], axis=1), rw_ref[...],
                     preferred_element_type=F32) + rb_ref[...]
    lane = lax.broadcasted_iota(jnp.int32, logits.shape, 1)
    vals = logits
    top = jnp.max(vals, axis=-1, keepdims=True)
    picked = jnp.zeros_like(logits)
    slot_idx = jnp.zeros_like(logits)
    slot_exp = jnp.zeros_like(logits)
    for k in range(TOP_K):
        m = jnp.max(vals, axis=-1, keepdims=True)
        first = jnp.min(jnp.where(vals == m, lane, LANES), axis=-1, keepdims=True)
        pick = lane == first
        picked = jnp.where(pick, 1.0, picked)
        slot_idx = jnp.where(lane == k, first.astype(F32), slot_idx)
        slot_exp = jnp.where(lane == k, jnp.exp(m - top), slot_exp)
        vals = jnp.where(pick, -jnp.inf, vals)
    slot_gate = slot_exp / jnp.sum(slot_exp, axis=-1, keepdims=True)
    idx_t = slot_idx.T[:TOP_K]
    idx_ref[...] = idx_t.astype(jnp.int32)
    gate_ref[...] = slot_gate.T[:TOP_K]
    routed_t = picked.T[:N_EXPERTS]
    r = lax.broadcasted_iota(jnp.int32, (tm, tm), 0)
    c = lax.broadcasted_iota(jnp.int32, (tm, tm), 1)
    before = jnp.where(r < c, 1.0, 0.0).astype(BF16)
    rank_t = cnt_ref[...] + jnp.dot(routed_t.astype(BF16), before, preferred_element_type=F32)
    cnt_ref[...] += jnp.sum(routed_t, axis=-1, keepdims=True)
    e_row = lax.broadcasted_iota(jnp.int32, (N_EXPERTS, tm), 0).astype(F32)
    ranks = [jnp.sum(jnp.where(e_row == idx_t[k:k + 1], rank_t, 0.0), axis=0, keepdims=True)
             for k in range(TOP_K)]
    rank_ref[...] = jnp.concatenate(ranks, axis=0).astype(jnp.int32)


def _post(y_a, y_b, w_out, x, g1, norm_g, sc, sh, router_w, router_b, tm=512):
    b, s, d = x.shape
    tm = min(tm, s)
    nt = s // tm
    w_hi, w_mid = _split_bf16_bits(jnp.zeros((d, LANES), F32).at[:, :N_EXPERTS].set(router_w), 2)
    rw = jnp.concatenate([w_hi, w_mid, w_hi], axis=0)
    rb = jnp.full((1, LANES), -1e30, F32).at[0, :N_EXPERTS].set(router_b)
    vec = pl.BlockSpec((1, 1, d), lambda i, j: (i, 0, 0))
    row = lambda w: pl.BlockSpec((1, tm, w), lambda i, j: (i, j, 0))
    return pl.pallas_call(
        _post_kernel,
        grid=(b, nt),
        in_specs=[
            row(A_WIDTH), row(B_HEADS * DK),
            pl.BlockSpec((d, d), lambda i, j: (0, 0)),
            row(d), vec,
            pl.BlockSpec((1, d), lambda i, j: (0, 0)), vec, vec,
            pl.BlockSpec((3 * d, LANES), lambda i, j: (0, 0)),
            pl.BlockSpec((1, LANES), lambda i, j: (0, 0)),
        ],
        out_specs=[row(d), row(d // 2)] + [pl.BlockSpec((TOP_K, tm), lambda i, j: (0, i * nt + j))] * 3,
        out_shape=[
            jax.ShapeDtypeStruct((b, s, d), F32),
            jax.ShapeDtypeStruct((b, s, d // 2), jnp.int32),
            jax.ShapeDtypeStruct((TOP_K, b * s), jnp.int32),
            jax.ShapeDtypeStruct((TOP_K, b * s), F32),
            jax.ShapeDtypeStruct((TOP_K, b * s), jnp.int32),
        ],
        scratch_shapes=[pltpu.VMEM((N_EXPERTS, 1), F32)],
        compiler_params=_params(("arbitrary", "arbitrary")),
        name="post",
    )(y_a, y_b, w_out.astype(BF16), x, g1, norm_g.reshape(1, d), sc, sh, rw, rb)


def _sc_mesh():
    return plsc.VectorSubcoreMesh(core_axis_name="c", subcore_axis_name="s")


def _sc_worker():
    return lax.axis_index("c") * SC_SUBCORES + lax.axis_index("s")


def _dispatch_rows(h2, dest_c, n_rows):
    t, d = h2.shape
    per_worker = t // SC_WINDOW // SC_WORKERS

    def body(x_hbm, i_hbm, o_hbm, buf, idx):
        worker = _sc_worker()

        @pl.loop(0, per_worker)
        def _(j):
            ch = worker * per_worker + j
            pltpu.sync_copy(i_hbm.at[ch], idx)
            pltpu.sync_copy(x_hbm.at[pl.ds(ch * SC_WINDOW, SC_WINDOW)], buf)
            for k in range(TOP_K):
                pltpu.sync_copy(buf, o_hbm.at[idx.at[k]])

    return pl.kernel(
        body, out_type=jax.ShapeDtypeStruct((n_rows, d), h2.dtype), mesh=_sc_mesh(),
        scratch_types=[pltpu.VMEM((SC_WINDOW, d), h2.dtype), pltpu.VMEM((TOP_K, SC_WINDOW), jnp.int32)],
        name="dispatch_rows",
    )(h2, dest_c)


def _collect_rows(rows, dest_c, t):
    d = rows.shape[1]
    per_worker = t // SC_WINDOW // SC_WORKERS

    def body(r_hbm, i_hbm, o_hbm, buf, idx):
        worker = _sc_worker()

        @pl.loop(0, per_worker)
        def _(j):
            ch = worker * per_worker + j
            pltpu.sync_copy(i_hbm.at[ch], idx)
            for k in range(TOP_K):
                pltpu.sync_copy(r_hbm.at[idx.at[k]], buf)
                pltpu.sync_copy(buf, o_hbm.at[k, pl.ds(ch * SC_WINDOW, SC_WINDOW)])

    return pl.kernel(
        body, out_type=jax.ShapeDtypeStruct((TOP_K, t, d), rows.dtype), mesh=_sc_mesh(),
        scratch_types=[pltpu.VMEM((SC_WINDOW, d), rows.dtype), pltpu.VMEM((TOP_K, SC_WINDOW), jnp.int32)],
        name="collect_rows",
    )(rows, dest_c)


def _experts_kernel(layer, be_ref, nb_ref, slot_ref, next_ref, x_ref, w1_hbm, b1_ref, w2_hbm, b2_ref, o_ref,
                    w1f_ref, w2f_ref, w1b_ref, w2b_ref, sem):
    i = pl.program_id(0)
    e = be_ref[i]
    live = i < nb_ref[0]

    def weight_copies(expert, slot):
        return (pltpu.make_async_copy(w1_hbm.at[layer, expert], w1f_ref.at[slot], sem.at[slot, 0]),
                pltpu.make_async_copy(w2_hbm.at[layer, expert], w2f_ref.at[slot], sem.at[slot, 1]))

    @pl.when(live & (i == 0))
    def _():
        for cp in weight_copies(e, 0):
            cp.start()

    @pl.when(live & ((i == 0) | (e != be_ref[jnp.maximum(i - 1, 0)])))
    def _():
        slot = slot_ref[i]
        for cp in weight_copies(e, slot):
            cp.wait()
        w1b_ref[...] = w1f_ref[slot].astype(BF16)
        w2b_ref[...] = w2f_ref[slot].astype(BF16)

        @pl.when(next_ref[i] >= 0)
        def _():
            for cp in weight_copies(next_ref[i], 1 - slot):
                cp.start()

    @pl.when(live)
    def _():
        x = _unpack_halves(x_ref[...]).astype(BF16)
        hdn = jnp.dot(x, w1b_ref[...], preferred_element_type=F32) + b1_ref[0, 0]
        glu = jnp.minimum(hdn[:, :D_EXPERT], SWIGLU_LIMIT)
        lin = jnp.clip(hdn[:, D_EXPERT:], -SWIGLU_LIMIT, SWIGLU_LIMIT)
        act = glu * jax.nn.sigmoid(SWIGLU_ALPHA * glu) * (lin + 1.0)
        out = jnp.dot(act.astype(BF16), w2b_ref[...], preferred_element_type=F32) + b2_ref[0, 0]
        o_ref[...] = _pack_halves(out)


def _experts(rows, block_exp, n_live, slot, next_exp, layer, w1, b1, w2, b2):
    n_rows, dp = rows.shape
    n_layers, n_e, d, f2 = w1.shape
    n_blocks = n_rows // ROW_BLOCK
    by_expert = lambda i, be, nb, sl, nx: (layer, be[i], 0, 0)
    grid_spec = pltpu.PrefetchScalarGridSpec(
        num_scalar_prefetch=4,
        grid=(n_blocks,),
        in_specs=[
            pl.BlockSpec((ROW_BLOCK, dp), lambda i, be, nb, sl, nx: (i, 0)),
            pl.BlockSpec(memory_space=pl.ANY),
            pl.BlockSpec((1, 1, 1, f2), by_expert),
            pl.BlockSpec(memory_space=pl.ANY),
            pl.BlockSpec((1, 1, 1, d), by_expert),
        ],
        out_specs=pl.BlockSpec((ROW_BLOCK, dp), lambda i, be, nb, sl, nx: (i, 0)),
        scratch_shapes=[pltpu.VMEM((2, d, f2), F32), pltpu.VMEM((2, f2 // 2, d), F32),
                        pltpu.VMEM((d, f2), BF16), pltpu.VMEM((f2 // 2, d), BF16),
                        pltpu.SemaphoreType.DMA((2, 2))],
    )
    return pl.pallas_call(
        functools.partial(_experts_kernel, layer),
        grid_spec=grid_spec,
        out_shape=jax.ShapeDtypeStruct((n_rows, dp), jnp.int32),
        compiler_params=_params(("arbitrary",)),
        name="experts",
    )(block_exp, n_live, slot, next_exp, rows, w1, b1.reshape(n_layers, n_e, 1, f2), w2,
      b2.reshape(n_layers, n_e, 1, d))


def _moe_routed(h2, idx, rank, layer, w1, b1, w2, b2):
    t, d = h2.shape
    n_e = w1.shape[1]
    n_blocks = t * TOP_K // ROW_BLOCK + n_e
    onehot = (idx[:, :, None] == jnp.arange(n_e, dtype=jnp.int32)).astype(jnp.int32)
    counts = jnp.sum(onehot, axis=(0, 1))
    padded = (counts + ROW_BLOCK - 1) // ROW_BLOCK * ROW_BLOCK
    pad_end = jnp.cumsum(padded)
    dest = jnp.sum(onehot * (pad_end - padded), axis=-1) + rank
    block_row = jnp.arange(n_blocks, dtype=jnp.int32)[:, None] * ROW_BLOCK
    block_exp = jnp.minimum(jnp.sum((pad_end[None, :] <= block_row).astype(jnp.int32), axis=-1), n_e - 1)
    n_live = (pad_end[-1:] // ROW_BLOCK).astype(jnp.int32)
    experts = jnp.arange(n_e, dtype=jnp.int32)
    later = (experts[None, :] > experts[:, None]) & (counts[None, :] > 0)
    next_of = jnp.min(jnp.where(later, experts[None, :], n_e), axis=-1)
    next_of = jnp.where(next_of == n_e, -1, next_of)
    block_onehot = (block_exp[:, None] == experts[None, :]).astype(jnp.int32)
    next_exp = jnp.sum(block_onehot * next_of, axis=-1)
    run_of = jnp.cumsum((counts > 0).astype(jnp.int32)) - 1
    slot = jnp.sum(block_onehot * run_of, axis=-1) % 2
    dest_c = dest.reshape(TOP_K, t // SC_WINDOW, SC_WINDOW).transpose(1, 0, 2)
    rows = _dispatch_rows(h2, dest_c, n_blocks * ROW_BLOCK)
    out = _experts(rows, block_exp, n_live, slot, next_exp, layer, w1, b1, w2, b2)
    return _collect_rows(out, dest_c, t)


def _final_kernel(x_ref, rows_ref, gate_ref, g2_ref, fg_ref, o_ref):
    x = _moe_residual(x_ref, rows_ref, gate_ref, g2_ref)
    ms = jnp.mean(x * x, axis=-1, keepdims=True)
    o_ref[0] = x * lax.rsqrt(ms + NORM_EPS) * fg_ref[...]


def _final(x, res, final_g, tm=512):
    b, s, d = x.shape
    tm = min(tm, s)
    row = pl.BlockSpec((1, tm, d), lambda i, j: (i, j, 0))
    return pl.pallas_call(
        _final_kernel,
        grid=(b, s // tm),
        in_specs=[row] + _moe_residual_specs(tm, d) + [pl.BlockSpec((1, d), lambda i, j: (0, 0))],
        out_specs=row,
        out_shape=jax.ShapeDtypeStruct((b, s, d), F32),
        compiler_params=_params(("parallel", "parallel")),
        name="final",
    )(x, *res, final_g.reshape(1, d))


def _widen_w_in(w):
    gl0 = 2 * A_WIDTH + B_HEADS * DK + 6 * KV_GROUPS * DK
    per_group = HPG * 3
    gl_groups = [jnp.pad(w[:, gl0 + g * per_group:gl0 + (g + 1) * per_group], ((0, 0), (0, LANES - per_group)))
                 for g in range(KV_GROUPS)]
    wide = jnp.concatenate([w[:, :gl0]] + gl_groups, axis=1)
    assert gl0 == C_GL and wide.shape[1] == P_WIDE
    return wide.astype(BF16)


def _rope_tables(pos):
    half = DK // 2
    inv = ROPE_THETA ** (-jnp.arange(half, dtype=F32) / half)
    ang = pos.astype(F32)[..., None] * inv
    cos = jnp.cos(ang)
    sin = jnp.sin(ang)
    reps = LANES // DK
    return (jnp.concatenate([cos, cos] * reps, axis=-1),
            jnp.concatenate([-sin, sin] * reps, axis=-1))


def kernel(x, c, positions, ada_w, ada_b, norm1_g, norm2_g, w_in, w_out, sg_ln_g, sg_ln_b, sg_w, sg_b,
           cmp_pe_k, cmp_pe_v, cmp_w1_k, cmp_w2_k, cmp_w1_v, cmp_w2_v, router_w, router_b,
           exp_w1, exp_b1, exp_w2, exp_b2, final_g):
    b, s, d = x.shape
    n_layers = ada_w.shape[0]
    mod = _ada_mod(c, ada_w, ada_b)
    cos, sin = _rope_tables(positions)
    cmp_end = jnp.minimum(CMP_STRIDE * jnp.arange(s // CMP_STRIDE) + CMP_BLOCK - 1, s - 1)
    cos_c, sin_c = _rope_tables(positions[:, cmp_end])
    res = None
    for l in range(n_layers):
        sh1, sc1, g1, sh2, sc2, g2 = [m.reshape(b, 1, d) for m in jnp.split(mod[l], 6, axis=-1)]
        x, uv, qt, kc, vc, ks, vs_t, kw, vw_t, gl_t = _pre(x, res, norm1_g[l], sc1, sh1,
                                                          _widen_w_in(w_in[l]), cos, sin)
        y_a = _sgu(uv, sg_ln_g[l], sg_ln_b[l], sg_w[l], sg_b[l])
        kcmp, vcmp_t = _compress(kc, vc, cmp_pe_k[l], cmp_pe_v[l], cmp_w1_k[l], cmp_w2_k[l],
                                 cmp_w1_v[l], cmp_w2_v[l], cos_c, sin_c)
        oc_t, bias_t = _cmp_sel(qt, kcmp, vcmp_t)
        y_b = _attn(qt, ks, kw, vs_t, vw_t, bias_t, oc_t, gl_t)
        x, h2, idx, gate, rank = _post(y_a, y_b, w_out[l], x, g1, norm2_g[l], sc2, sh2,
                                       router_w[l], router_b[l])
        rows = _moe_routed(h2.reshape(b * s, d // 2), idx, rank, l, exp_w1, exp_b1, exp_w2, exp_b2)
        res = (rows.reshape(TOP_K, b, s, d // 2), gate.T.reshape(b, s, TOP_K), g2)
    return _final(x, res, final_g)
```

```python
import functools

import numpy as np
import jax
import jax.numpy as jnp
from jax import lax
from jax.experimental import pallas as pl
from jax.experimental.pallas import tpu as pltpu
from jax.experimental.pallas import tpu_sc as plsc

F32 = jnp.float32
BF16 = jnp.bfloat16
HIGHEST = lax.Precision.HIGHEST

D_MODEL = 1024
A_WIDTH = 512
A_HEADS = 8
CHUNK = 128
B_HEADS = 8
DK = 64
KV_GROUPS = 2
HPG = B_HEADS // KV_GROUPS
CMP_BLOCK = 32
CMP_STRIDE = 16
SEL_BLOCK = 64
SEL_TOP = 16
WINDOW = 512
ROPE_THETA = 10000.0
N_EXPERTS = 32
TOP_K = 4
D_EXPERT = 1024
SWIGLU_LIMIT = 7.0
SWIGLU_ALPHA = 1.702
NORM_EPS = 1e-6

LANES = 128
SUBLANES = 8
GW = HPG * DK
N_FORCED = 3
ONES_ROWS = 16
MASK_BIAS = -32768.0
Q_SCALE = DK ** -0.5 * 1.4426950408889634
VMEM_LIMIT = 56 * 1024 * 1024
ROW_BLOCK = 256
BLOCKS_PER_STEP = 2
SC_CORES = 2
SC_SUBCORES = 16
SC_WORKERS = SC_CORES * SC_SUBCORES
SC_WINDOW = 128
CMP_ROW_CHUNK = 128

C_U, C_V, C_Q = 0, 512, 1024
C_KC, C_VC, C_KS, C_VS, C_KW, C_VW = 1536, 1664, 1792, 1920, 2048, 2176
C_GL = 2304
P_WIDE = 2560


def _params(sem):
    return pltpu.CompilerParams(dimension_semantics=sem, vmem_limit_bytes=VMEM_LIMIT)


def _split_bf16(x, parts):
    out = []
    for _ in range(parts):
        piece = x.astype(BF16)
        out.append(piece)
        x = x - piece.astype(F32)
    return out


def _split_bf16_bits(x, parts):
    out = []
    for _ in range(parts):
        piece = lax.bitcast_convert_type(lax.bitcast_convert_type(x, jnp.uint32) & jnp.uint32(0xFFFF0000), F32)
        out.append(piece.astype(BF16))
        x = x - piece
    return out


def _ada_kernel(c_ref, w_ref, b_ref, o_ref):
    c = c_ref[...]
    cond = c * jax.nn.sigmoid(c)
    o_ref[0] = jnp.dot(cond, w_ref[0], precision=HIGHEST, preferred_element_type=F32) + b_ref[0]


def _ada_mod(c, ada_w, ada_b):
    n_layers, d, d6 = ada_w.shape
    b = c.shape[0]
    rows = SUBLANES
    c_pad = jnp.zeros((rows, d), F32).at[:b].set(c)
    out = pl.pallas_call(
        _ada_kernel,
        grid=(n_layers, d6 // d),
        in_specs=[
            pl.BlockSpec((rows, d), lambda l, j: (0, 0)),
            pl.BlockSpec((1, d, d), lambda l, j: (l, 0, j)),
            pl.BlockSpec((1, 1, d), lambda l, j: (l, 0, j)),
        ],
        out_specs=pl.BlockSpec((1, rows, d), lambda l, j: (l, 0, j)),
        out_shape=jax.ShapeDtypeStruct((n_layers, rows, d6), F32),
        compiler_params=_params(("arbitrary", "arbitrary")),
        name="ada",
    )(c_pad, ada_w, ada_b.reshape(n_layers, 1, d6))
    return out[:, :b]


def _rope_slab(t, cos, sin_signed, lo):
    partner = jnp.where(lo, pltpu.roll(t, LANES - DK // 2, 1), pltpu.roll(t, DK // 2, 1))
    return t * cos + partner * sin_signed


def _rope(t, cos, sin_signed):
    lane = lax.broadcasted_iota(jnp.int32, (1, LANES), 1)
    lo = (lane % DK) < (DK // 2)
    slabs = [_rope_slab(t[:, s * LANES:(s + 1) * LANES], cos, sin_signed, lo)
             for s in range(t.shape[1] // LANES)]
    return slabs[0] if len(slabs) == 1 else jnp.concatenate(slabs, axis=1)


def _pack_halves(x):
    n = x.shape[1] // 2
    lo = lax.bitcast_convert_type(x[:, :n].astype(BF16).astype(F32), jnp.int32)
    hi = lax.bitcast_convert_type(x[:, n:].astype(BF16).astype(F32), jnp.int32)
    return lax.shift_right_logical(lo, jnp.int32(16)) | (hi & jnp.int32(-65536))


def _unpack_halves(p):
    lo = lax.bitcast_convert_type(lax.shift_left(p, jnp.int32(16)), F32)
    hi = lax.bitcast_convert_type(p & jnp.int32(-65536), F32)
    return jnp.concatenate([lo, hi], axis=1)


def _moe_residual(x_ref, rows_ref, gate_ref, g2_ref):
    gate = gate_ref[0]
    y = gate[:, 0:1] * _unpack_halves(rows_ref[0, 0])
    for k in range(1, TOP_K):
        y = y + gate[:, k:k + 1] * _unpack_halves(rows_ref[k, 0])
    return x_ref[0] + g2_ref[0] * y


def _moe_residual_specs(tm, d):
    return [pl.BlockSpec((TOP_K, 1, tm, d // 2), lambda i, j: (0, i, j, 0)),
            pl.BlockSpec((1, tm, TOP_K), lambda i, j: (i, j, 0)),
            pl.BlockSpec((1, 1, d), lambda i, j: (i, 0, 0))]


def _pre_kernel(has_res, *refs):
    if has_res:
        (x_ref, rows_ref, gate_ref, g2_ref, ng_ref, sc_ref, sh_ref, w_ref, cos_ref, sin_ref,
         xo_ref, uv_ref, qt_ref, kc_ref, vc_ref, ks_ref, vst_ref, kw_ref, vwt_ref, glt_ref) = refs
        x = _moe_residual(x_ref, rows_ref, gate_ref, g2_ref)
        xo_ref[0] = x
    else:
        (x_ref, ng_ref, sc_ref, sh_ref, w_ref, cos_ref, sin_ref,
         uv_ref, qt_ref, kc_ref, vc_ref, ks_ref, vst_ref, kw_ref, vwt_ref, glt_ref) = refs
        x = x_ref[0]
    ms = jnp.mean(x * x, axis=-1, keepdims=True)
    h = x * lax.rsqrt(ms + NORM_EPS) * ng_ref[...]
    h = h * (1.0 + sc_ref[0]) + sh_ref[0]
    proj = jnp.dot(h.astype(BF16), w_ref[...], preferred_element_type=F32)
    cos = cos_ref[0]
    sin = sin_ref[0]
    uv_ref[0] = proj[:, C_U:C_Q]
    qt_ref[0] = (_rope(proj[:, C_Q:C_KC], cos, sin) * Q_SCALE).T
    kc_ref[0] = proj[:, C_KC:C_VC]
    vc_ref[0] = proj[:, C_VC:C_KS]
    ks_ref[0] = _rope(proj[:, C_KS:C_VS], cos, sin).astype(BF16)
    vst_ref[0] = proj[:, C_VS:C_KW].T.astype(BF16)
    kw_ref[0] = _rope(proj[:, C_KW:C_VW], cos, sin).astype(BF16)
    vwt_ref[0] = proj[:, C_VW:C_GL].T.astype(BF16)
    glt_ref[0] = proj[:, C_GL:P_WIDE].T


def _pre(x, res, norm_g, sc, sh, w_wide, cos, sin, tm=512):
    b, s, d = x.shape
    tm = min(tm, s)
    row = lambda w: pl.BlockSpec((1, tm, w), lambda i, j: (i, j, 0))
    col = lambda w: pl.BlockSpec((1, w, tm), lambda i, j: (i, 0, j))
    vec = pl.BlockSpec((1, 1, d), lambda i, j: (i, 0, 0))
    in_specs = [row(d)]
    args = [x]
    if res is not None:
        in_specs += _moe_residual_specs(tm, d)
        args += list(res)
    in_specs += [pl.BlockSpec((1, d), lambda i, j: (0, 0)), vec, vec,
                 pl.BlockSpec((d, P_WIDE), lambda i, j: (0, 0)), row(LANES), row(LANES)]
    args += [norm_g.reshape(1, d), sc, sh, w_wide, cos, sin]
    kv = KV_GROUPS * DK
    outs = [(d, F32, False)] if res is not None else []
    outs += [(2 * A_WIDTH, F32, False), (B_HEADS * DK, F32, True), (kv, F32, False), (kv, F32, False),
             (kv, BF16, False), (kv, BF16, True), (kv, BF16, False), (kv, BF16, True),
             (KV_GROUPS * LANES, F32, True)]
    res_out = pl.pallas_call(
        functools.partial(_pre_kernel, res is not None),
        grid=(b, s // tm),
        in_specs=in_specs,
        out_specs=[col(w) if t else row(w) for w, _, t in outs],
        out_shape=[jax.ShapeDtypeStruct((b, w, s) if t else (b, s, w), dt) for w, dt, t in outs],
        compiler_params=_params(("parallel", "parallel")),
        name="pre",
    )(*args)
    if res is None:
        res_out = [x] + list(res_out)
    return res_out


def _sgu_kernel(uv_ref, lng_ref, lnb_ref, w_ref, bias_ref, o_ref):
    rows = uv_ref.shape[1]
    uv = uv_ref[0]
    gu = jax.nn.gelu(uv[:, :A_WIDTH])
    gv = jax.nn.gelu(uv[:, A_WIDTH:])
    mu = jnp.mean(gv, axis=-1, keepdims=True)
    var = jnp.mean(jnp.square(gv - mu), axis=-1, keepdims=True)
    vn = ((gv - mu) * lax.rsqrt(var + NORM_EPS) * lng_ref[...] + lnb_ref[...]).astype(BF16)
    r = lax.broadcasted_iota(jnp.int32, (CHUNK, CHUNK), 0)
    c = lax.broadcasted_iota(jnp.int32, (CHUNK, CHUNK), 1)
    causal = c <= r
    lane_lo = lax.broadcasted_iota(jnp.int32, (CHUNK, LANES), 1) < DK
    for p in range(A_HEADS // 2):
        w0 = jnp.where(causal, w_ref[2 * p], 0.0).astype(BF16)
        w1 = jnp.where(causal, w_ref[2 * p + 1], 0.0).astype(BF16)
        bias = bias_ref[:, p * LANES:(p + 1) * LANES]
        for ch in range(rows // CHUNK):
            rs = slice(ch * CHUNK, (ch + 1) * CHUNK)
            cs = slice(p * LANES, (p + 1) * LANES)
            vp = vn[rs, cs]
            m0 = jnp.dot(w0, vp, preferred_element_type=F32)
            m1 = jnp.dot(w1, vp, preferred_element_type=F32)
            mixed = jnp.where(lane_lo, m0, m1) + bias
            o_ref[0, rs, cs] = (gu[rs, cs] * mixed).astype(o_ref.dtype)


def _sgu(uv, ln_g, ln_b, w_s, b_s, tm=512):
    b, s, _ = uv.shape
    tm = min(tm, s)
    bias = jnp.repeat(b_s.T, A_WIDTH // A_HEADS, axis=1)
    return pl.pallas_call(
        _sgu_kernel,
        grid=(b, s // tm),
        in_specs=[
            pl.BlockSpec((1, tm, 2 * A_WIDTH), lambda i, j: (i, j, 0)),
            pl.BlockSpec((1, A_WIDTH), lambda i, j: (0, 0)),
            pl.BlockSpec((1, A_WIDTH), lambda i, j: (0, 0)),
            pl.BlockSpec((A_HEADS, CHUNK, CHUNK), lambda i, j: (0, 0, 0)),
            pl.BlockSpec((CHUNK, A_WIDTH), lambda i, j: (0, 0)),
        ],
        out_specs=pl.BlockSpec((1, tm, A_WIDTH), lambda i, j: (i, j, 0)),
        out_shape=jax.ShapeDtypeStruct((b, s, A_WIDTH), BF16),
        compiler_params=_params(("parallel", "parallel")),
        name="sgu",
    )(uv, ln_g.reshape(1, -1), ln_b.reshape(1, -1), w_s, bias)


def _compress_kernel(kr_ref, vr_ref, pek_ref, pev_ref, w1k_ref, w1v_ref, w2k_ref, w2v_ref,
                     cos_ref, sin_ref, ko_ref, vt_ref):
    nc = kr_ref.shape[1]

    def mlp(r, pe_ref, w1_ref, w2_ref):
        top = jnp.dot(r + pe_ref[0:1], w1_ref[0], precision=HIGHEST, preferred_element_type=F32)
        bot = jnp.dot(r + pe_ref[1:2], w1_ref[1], precision=HIGHEST, preferred_element_type=F32)
        pre = top + pltpu.roll(bot, nc - 1, 0)
        return jnp.dot(jax.nn.gelu(pre), w2_ref[...], precision=HIGHEST, preferred_element_type=F32)

    kc = _rope(mlp(kr_ref[0], pek_ref, w1k_ref, w2k_ref), cos_ref[0], sin_ref[0])
    vc = mlp(vr_ref[0], pev_ref, w1v_ref, w2v_ref)
    lo = lax.broadcasted_iota(jnp.int32, (nc, LANES), 1) < DK
    rolled = pltpu.roll(kc, DK, 1)
    for g in range(KV_GROUPS):
        dup = jnp.where(lo, kc, rolled) if g == 0 else jnp.where(lo, rolled, kc)
        hi, low = _split_bf16(dup, 2)
        ko_ref[0, g, :, 0:LANES] = hi
        ko_ref[0, g, :, LANES:2 * LANES] = jnp.where(lo, low, jnp.zeros_like(low))
    vt_ref[0] = vc.T.astype(BF16)


def _compress_weights(pe, w1, w2):
    half = CMP_BLOCK // 2
    eye = jnp.eye(KV_GROUPS, dtype=F32)
    w1r = w1.reshape(CMP_BLOCK, DK, DK)
    wfull = jnp.einsum('lde,gh->lgdhe', w1r, eye)
    w1s = wfull.reshape(2, half * KV_GROUPS * DK, KV_GROUPS * DK)
    pes = jnp.broadcast_to(pe.reshape(2, half, 1, DK), (2, half, KV_GROUPS, DK)).reshape(2, -1)
    w2bd = jnp.einsum('de,gh->gdhe', w2, eye).reshape(KV_GROUPS * DK, KV_GROUPS * DK)
    return pes, w1s, w2bd


def _compress(kc, vc, pe_k, pe_v, w1_k, w2_k, w1_v, w2_v, cos_c, sin_c):
    b, s, _ = kc.shape
    nc = s // CMP_STRIDE
    rw = CMP_STRIDE * LANES
    pek, w1ks, w2kb = _compress_weights(pe_k, w1_k, w2_k)
    pev, w1vs, w2vb = _compress_weights(pe_v, w1_v, w2_v)
    full = lambda shape: pl.BlockSpec(shape, lambda i: (0,) * len(shape))
    return pl.pallas_call(
        _compress_kernel,
        grid=(b,),
        in_specs=[
            pl.BlockSpec((1, nc, rw), lambda i: (i, 0, 0)),
            pl.BlockSpec((1, nc, rw), lambda i: (i, 0, 0)),
            full((2, rw)), full((2, rw)),
            full((2, rw, LANES)), full((2, rw, LANES)),
            full((LANES, LANES)), full((LANES, LANES)),
            pl.BlockSpec((1, nc, LANES), lambda i: (i, 0, 0)),
            pl.BlockSpec((1, nc, LANES), lambda i: (i, 0, 0)),
        ],
        out_specs=[pl.BlockSpec((1, KV_GROUPS, nc, 2 * LANES), lambda i: (i, 0, 0, 0)),
                   pl.BlockSpec((1, KV_GROUPS * DK, nc), lambda i: (i, 0, 0))],
        out_shape=[jax.ShapeDtypeStruct((b, KV_GROUPS, nc, 2 * LANES), BF16),
                   jax.ShapeDtypeStruct((b, KV_GROUPS * DK, nc), BF16)],
        compiler_params=_params(("parallel",)),
        name="compress",
    )(kc.reshape(b, nc, rw), vc.reshape(b, nc, rw), pek, pev, w1ks, w1vs, w2kb, w2vb, cos_c, sin_c)


def _heads_on_lanes(qt):
    return jnp.concatenate([qt[h * DK:(h + 1) * DK] for h in range(HPG)], axis=1)


def _cmp_sel_kernel(n_top, qt_ref, k_ref, vt_ref, map_ref, oc_ref, bias_ref, imp_ref):
    tq = qt_ref.shape[2]
    cols = HPG * tq
    nc = k_ref.shape[2]
    nsel = map_ref.shape[0]
    q0 = pl.program_id(2) * tq
    q_hi, q_lo = _split_bf16(_heads_on_lanes(qt_ref[0]), 2)
    q3t = jnp.concatenate([q_hi, q_lo, q_hi, jnp.zeros_like(q_hi)], axis=0)
    t_col = q0 + lax.broadcasted_iota(jnp.int32, (1, cols), 1) % tq

    def compressed_branch(rows):
        s = jnp.dot(k_ref[0, 0, 0:rows], q3t, preferred_element_type=F32)
        n_idx = lax.broadcasted_iota(jnp.int32, (rows, 1), 0)
        s = jnp.where((CMP_STRIDE * n_idx + CMP_BLOCK - 1) <= t_col, s, -jnp.inf)
        m = jnp.max(s, axis=0, keepdims=True)
        m = jnp.where(m == -jnp.inf, 0.0, m)
        e = jnp.exp2(s - m)
        d = jnp.sum(e, axis=0, keepdims=True)
        p = e / jnp.where(d > 0, d, 1.0)
        oc = jnp.dot(vt_ref[0, :, 0:rows], p.astype(BF16), preferred_element_type=F32)
        oc_ref[0] = jnp.concatenate([oc[:, h * tq:(h + 1) * tq] for h in range(HPG)], axis=0)
        psum = p[:, 0:tq]
        for h in range(1, HPG):
            psum = psum + p[:, h * tq:(h + 1) * tq]
        imp_ref[...] = jnp.dot(jnp.concatenate([map_ref[:, 0:rows]] * 3, axis=1),
                               jnp.concatenate(_split_bf16(psum, 3), axis=0), preferred_element_type=F32)

    visible = (q0 + tq - CMP_BLOCK) // CMP_STRIDE + 1
    n_chunks = -(-nc // CMP_ROW_CHUNK)
    for r in range(1, n_chunks + 1):
        lo = (r - 1) * CMP_ROW_CHUNK if r > 1 else -nc
        pl.when((visible > lo) & (visible <= r * CMP_ROW_CHUNK) if r < n_chunks else visible > lo)(
            functools.partial(compressed_branch, min(r * CMP_ROW_CHUNK, nc)))

    imp = imp_ref[...]
    j = lax.broadcasted_iota(jnp.int32, (nsel, tq), 0)
    cur = (q0 + lax.broadcasted_iota(jnp.int32, (nsel, tq), 1)) // SEL_BLOCK
    valid = j <= cur
    forced = (j == 0) | (j == cur) | (j == cur - 1)
    keep = forced | (valid & (cur < n_top))
    vals = jnp.where(valid & jnp.logical_not(forced), imp, -jnp.inf)
    bias = jnp.where(keep, 0.0, MASK_BIAS)
    for _ in range(n_top - N_FORCED):
        mx = jnp.max(vals, axis=0, keepdims=True)
        first = jnp.min(jnp.where(vals == mx, j, nsel), axis=0, keepdims=True)
        pick = (j == first) & (mx > -jnp.inf)
        bias = jnp.where(pick, 0.0, bias)
        vals = jnp.where(pick, -jnp.inf, vals)
    bias_ref[0, 0] = bias.astype(BF16)


def _sel_map_t(s):
    nc = s // CMP_STRIDE
    n_cmp = (s - CMP_BLOCK) // CMP_STRIDE + 1
    n_sel = s // SEL_BLOCK
    cs = CMP_STRIDE * np.arange(n_cmp)[:, None]
    ce = cs + CMP_BLOCK
    ss = SEL_BLOCK * np.arange(n_sel)[None, :]
    se = ss + SEL_BLOCK
    ov = np.clip(np.minimum(ce, se) - np.maximum(cs, ss), 0, None) / CMP_STRIDE
    out = np.zeros((n_sel, nc), np.float32)
    out[:, :n_cmp] = ov.T
    return out


def _cmp_sel(qt, kcmp, vcmp_t, tq=256):
    b, _, s = qt.shape
    nc = s // CMP_STRIDE
    nsel = s // SEL_BLOCK
    n_top = min(SEL_TOP, nsel)
    sel_map = jnp.asarray(_sel_map_t(s), dtype=BF16)
    return pl.pallas_call(
        functools.partial(_cmp_sel_kernel, n_top),
        grid=(b, KV_GROUPS, s // tq),
        in_specs=[
            pl.BlockSpec((1, GW, tq), lambda i, g, j: (i, g, j)),
            pl.BlockSpec((1, 1, nc, 2 * LANES), lambda i, g, j: (i, g, 0, 0)),
            pl.BlockSpec((1, DK, nc), lambda i, g, j: (i, g, 0)),
            pl.BlockSpec((nsel, nc), lambda i, g, j: (0, 0)),
        ],
        out_specs=[
            pl.BlockSpec((1, GW, tq), lambda i, g, j: (i, g, j)),
            pl.BlockSpec((1, 1, nsel, tq), lambda i, g, j: (i, g, 0, j)),
        ],
        out_shape=[
            jax.ShapeDtypeStruct((b, B_HEADS * DK, s), F32),
            jax.ShapeDtypeStruct((b, KV_GROUPS, nsel, s), BF16),
        ],
        scratch_shapes=[pltpu.VMEM((nsel, tq), F32)],
        compiler_params=_params(("parallel", "parallel", "parallel")),
        name="cmp_sel",
    )(qt, kcmp, vcmp_t, sel_map)


def _attn_kernel(kt, qt_ref, ks_ref, kw_ref, vst_ref, vwt_ref, bias_ref, oh_ref, oc_ref, glt_ref,
                 o_ref, qa_ref, s0_ref, s1_ref, m_ref, acc_ref, ow_ref):
    tq = qt_ref.shape[2]
    cols = HPG * tq
    nsel = bias_ref.shape[2]
    g = pl.program_id(1)
    q0 = pl.program_id(2) * tq

    q4t = _heads_on_lanes(qt_ref[0]).astype(BF16)
    zero = jnp.zeros_like(q4t)
    qa_ref[0:DK] = jnp.where(g == 0, q4t, zero)
    qa_ref[DK:2 * DK] = jnp.where(g == 1, q4t, zero)
    qa_ref[LANES:LANES + nsel] = jnp.concatenate([bias_ref[0, 0]] * HPG, axis=1)
    qaug = qa_ref[...]
    t_col = q0 + lax.broadcasted_iota(jnp.int32, (1, cols), 1) % tq

    def fold(x, op):
        return op(x.reshape(x.shape[0] // SUBLANES, SUBLANES, cols), axis=0)

    def with_ones(vt):
        return jnp.concatenate([vt, jnp.ones((ONES_ROWS, vt.shape[1]), BF16)], axis=0)

    def stage(c, s_ref):
        k0 = pl.multiple_of(c * kt, kt)
        kaug = jnp.concatenate([ks_ref[0, pl.ds(k0, kt), :], oh_ref[pl.ds(k0, kt), :]], axis=1)
        s_ref[...] = jnp.dot(kaug, qaug, preferred_element_type=F32)
        r0 = pl.multiple_of(jnp.clip(q0 - k0, 0, kt - tq), tq)
        kpos = k0 + r0 + lax.broadcasted_iota(jnp.int32, (tq, 1), 0)
        s_ref[pl.ds(r0, tq), :] = jnp.where(kpos <= t_col, s_ref[pl.ds(r0, tq), :], -jnp.inf)

    def consume(c, s_ref):
        k0 = pl.multiple_of(c * kt, kt)
        sc = s_ref[...]
        m_old = m_ref[...]
        m_new = jnp.maximum(m_old, jnp.max(fold(sc, jnp.max), axis=0, keepdims=True))
        alpha = jnp.exp2(m_old - m_new)
        pr = jnp.exp2(sc - m_new)
        acc_ref[...] = alpha * acc_ref[...] + jnp.dot(with_ones(vst_ref[0, :, pl.ds(k0, kt)]), pr.astype(BF16),
                                                      preferred_element_type=F32)
        m_ref[...] = m_new

    m_ref[...] = jnp.full(m_ref.shape, -jnp.inf, F32)
    acc_ref[...] = jnp.zeros(acc_ref.shape, F32)
    n_full = q0 // kt
    n_pairs = n_full // 2
    odd = n_full % 2 == 1
    stage(0, s0_ref)

    wlen = WINDOW + tq
    w0 = pl.multiple_of(jnp.maximum(q0 - WINDOW, 0), tq)
    sw = jnp.dot(kw_ref[0, pl.ds(w0, wlen), :], qaug[0:LANES], preferred_element_type=F32)
    dlt = t_col - (w0 + lax.broadcasted_iota(jnp.int32, (wlen, 1), 0))
    sw = jnp.where(lax.bitcast_convert_type(dlt, jnp.uint32) < jnp.uint32(WINDOW), sw, -jnp.inf)
    pw = jnp.exp2(sw - jnp.max(sw, axis=0, keepdims=True))
    o_win = jnp.dot(with_ones(vwt_ref[0, :, pl.ds(w0, wlen)]), pw.astype(BF16), preferred_element_type=F32)
    ow_ref[...] = o_win[0:DK] / o_win[DK:DK + 1]

    def pair(i, carry):
        stage(2 * i + 1, s1_ref)
        consume(2 * i, s0_ref)
        stage(2 * i + 2, s0_ref)
        consume(2 * i + 1, s1_ref)
        return carry

    lax.fori_loop(0, n_pairs, pair, 0)

    @pl.when(odd)
    def _():
        stage(n_full, s1_ref)
        consume(n_full - 1, s0_ref)
        consume(n_full, s1_ref)

    @pl.when(jnp.logical_not(odd))
    def _():
        consume(n_full, s0_ref)

    o_sel = acc_ref[0:DK] / acc_ref[DK:DK + 1]

    o_win = ow_ref[...]
    gates = jax.nn.sigmoid(glt_ref[0])
    oc = _heads_on_lanes(oc_ref[0])
    for p in range(HPG // 2):
        halves = []
        for hh in (2 * p, 2 * p + 1):
            cs = slice(hh * tq, (hh + 1) * tq)
            halves.append(gates[3 * hh:3 * hh + 1] * oc[:, cs] + gates[3 * hh + 1:3 * hh + 2] * o_sel[:, cs]
                          + gates[3 * hh + 2:3 * hh + 3] * o_win[:, cs])
        o_ref[0, :, p * LANES:(p + 1) * LANES] = jnp.concatenate(halves, axis=0).T.astype(o_ref.dtype)


def _attn(qt, ks, kw, vs_t, vw_t, bias_t, oc_t, gl_t, tq=256, kt=512):
    b, _, s = qt.shape
    nsel = s // SEL_BLOCK
    kt = min(kt, s)
    onehot = jnp.asarray((np.arange(s)[:, None] // SEL_BLOCK == np.arange(nsel)[None, :]), dtype=BF16)
    k_spec = pl.BlockSpec((1, s, KV_GROUPS * DK), lambda i, g, j: (i, 0, 0))
    vt_spec = pl.BlockSpec((1, DK, s), lambda i, g, j: (i, g, 0))
    cols = HPG * tq
    return pl.pallas_call(
        functools.partial(_attn_kernel, kt),
        grid=(b, KV_GROUPS, s // tq),
        in_specs=[
            pl.BlockSpec((1, GW, tq), lambda i, g, j: (i, g, j)),
            k_spec, k_spec, vt_spec, vt_spec,
            pl.BlockSpec((1, 1, nsel, tq), lambda i, g, j: (i, g, 0, j)),
            pl.BlockSpec((s, nsel), lambda i, g, j: (0, 0)),
            pl.BlockSpec((1, GW, tq), lambda i, g, j: (i, g, j)),
            pl.BlockSpec((1, LANES, tq), lambda i, g, j: (i, g, j)),
        ],
        out_specs=pl.BlockSpec((1, tq, GW), lambda i, g, j: (i, j, g)),
        out_shape=jax.ShapeDtypeStruct((b, s, B_HEADS * DK), BF16),
        scratch_shapes=[pltpu.VMEM((LANES + nsel, cols), BF16),
                        pltpu.VMEM((kt, cols), F32), pltpu.VMEM((kt, cols), F32),
                        pltpu.VMEM((1, cols), F32), pltpu.VMEM((DK + ONES_ROWS, cols), F32),
                        pltpu.VMEM((DK, cols), F32)],
        compiler_params=_params(("parallel", "parallel", "arbitrary")),
        name="attn",
    )(qt, ks, kw, vs_t, vw_t, bias_t, onehot, oc_t, gl_t)


def _post_kernel(ya_ref, yb_ref, wo_ref, x_ref, g1_ref, ng_ref, sc_ref, sh_ref, rw_ref, rb_ref,
                 xo_ref, h_ref, idx_ref, gate_ref, rank_ref, cnt_ref):
    tm = x_ref.shape[1]

    @pl.when((pl.program_id(0) == 0) & (pl.program_id(1) == 0))
    def _():
        cnt_ref[...] = jnp.zeros(cnt_ref.shape, F32)

    mixed = jnp.dot(ya_ref[0], wo_ref[:A_WIDTH], preferred_element_type=F32)
    mixed = mixed + jnp.dot(yb_ref[0], wo_ref[A_WIDTH:], preferred_element_type=F32)
    x = x_ref[0] + g1_ref[0] * mixed
    xo_ref[0] = x
    ms = jnp.mean(x * x, axis=-1, keepdims=True)
    h = x * lax.rsqrt(ms + NORM_EPS) * ng_ref[...]
    h = h * (1.0 + sc_ref[0]) + sh_ref[0]
    h_ref[0] = _pack_halves(h)
    h_hi, h_mid = _split_bf16(h, 2)
    logits = jnp.dot(jnp.concatenate([h_hi, h_hi, h_mid], axis=1), rw_ref[...],
                     preferred_element_type=F32) + rb_ref[...]
    lane = lax.broadcasted_iota(jnp.int32, logits.shape, 1)
    vals = logits
    top = jnp.max(vals, axis=-1, keepdims=True)
    picked = jnp.zeros_like(logits)
    slot_idx = jnp.zeros_like(logits)
    slot_exp = jnp.zeros_like(logits)
    for k in range(TOP_K):
        m = jnp.max(vals, axis=-1, keepdims=True)
        first = jnp.min(jnp.where(vals == m, lane, LANES), axis=-1, keepdims=True)
        pick = lane == first
        picked = jnp.where(pick, 1.0, picked)
        slot_idx = jnp.where(lane == k, first.astype(F32), slot_idx)
        slot_exp = jnp.where(lane == k, jnp.exp(m - top), slot_exp)
        vals = jnp.where(pick, -jnp.inf, vals)
    slot_gate = slot_exp / jnp.sum(slot_exp, axis=-1, keepdims=True)
    idx_t = slot_idx.T[:TOP_K]
    idx_ref[...] = idx_t.astype(jnp.int32)
    gate_ref[...] = slot_gate.T[:TOP_K]
    routed_t = picked.T[:N_EXPERTS]
    r = lax.broadcasted_iota(jnp.int32, (tm, tm), 0)
    c = lax.broadcasted_iota(jnp.int32, (tm, tm), 1)
    before = jnp.where(r < c, 1.0, 0.0).astype(BF16)
    rank_t = cnt_ref[...] + jnp.dot(routed_t.astype(BF16), before, preferred_element_type=F32)
    cnt_ref[...] += jnp.sum(routed_t, axis=-1, keepdims=True)
    e_row = lax.broadcasted_iota(jnp.int32, (N_EXPERTS, tm), 0).astype(F32)
    ranks = [jnp.sum(jnp.where(e_row == idx_t[k:k + 1], rank_t, 0.0), axis=0, keepdims=True)
             for k in range(TOP_K)]
    rank_ref[...] = jnp.concatenate(ranks, axis=0).astype(jnp.int32)


def _post(y_a, y_b, w_out, x, g1, norm_g, sc, sh, router_w, router_b, tm=512):
    b, s, d = x.shape
    tm = min(tm, s)
    nt = s // tm
    w_hi, w_mid = _split_bf16_bits(jnp.zeros((d, LANES), F32).at[:, :N_EXPERTS].set(router_w), 2)
    rw = jnp.concatenate([w_hi, w_mid, w_hi], axis=0)
    rb = jnp.full((1, LANES), -1e30, F32).at[0, :N_EXPERTS].set(router_b)
    vec = pl.BlockSpec((1, 1, d), lambda i, j: (i, 0, 0))
    row = lambda w: pl.BlockSpec((1, tm, w), lambda i, j: (i, j, 0))
    return pl.pallas_call(
        _post_kernel,
        grid=(b, nt),
        in_specs=[
            row(A_WIDTH), row(B_HEADS * DK),
            pl.BlockSpec((d, d), lambda i, j: (0, 0)),
            row(d), vec,
            pl.BlockSpec((1, d), lambda i, j: (0, 0)), vec, vec,
            pl.BlockSpec((3 * d, LANES), lambda i, j: (0, 0)),
            pl.BlockSpec((1, LANES), lambda i, j: (0, 0)),
        ],
        out_specs=[row(d), row(d // 2)] + [pl.BlockSpec((TOP_K, tm), lambda i, j: (0, i * nt + j))] * 3,
        out_shape=[
            jax.ShapeDtypeStruct((b, s, d), F32),
            jax.ShapeDtypeStruct((b, s, d // 2), jnp.int32),
            jax.ShapeDtypeStruct((TOP_K, b * s), jnp.int32),
            jax.ShapeDtypeStruct((TOP_K, b * s), F32),
            jax.ShapeDtypeStruct((TOP_K, b * s), jnp.int32),
        ],
        scratch_shapes=[pltpu.VMEM((N_EXPERTS, 1), F32)],
        compiler_params=_params(("arbitrary", "arbitrary")),
        name="post",
    )(y_a, y_b, w_out.astype(BF16), x, g1, norm_g.reshape(1, d), sc, sh, rw, rb)


def _sc_mesh():
    return plsc.VectorSubcoreMesh(core_axis_name="c", subcore_axis_name="s")


def _sc_worker():
    return lax.axis_index("c") * SC_SUBCORES + lax.axis_index("s")


def _dispatch_rows(h2, dest_c, n_rows):
    t, d = h2.shape
    per_worker = t // SC_WINDOW // SC_WORKERS

    def body(x_hbm, i_hbm, o_hbm, buf, idx):
        worker = _sc_worker()

        @pl.loop(0, per_worker)
        def _(j):
            ch = worker * per_worker + j
            pltpu.sync_copy(i_hbm.at[ch], idx)
            pltpu.sync_copy(x_hbm.at[pl.ds(ch * SC_WINDOW, SC_WINDOW)], buf)
            for k in range(TOP_K):
                pltpu.sync_copy(buf, o_hbm.at[idx.at[k]])

    return pl.kernel(
        body, out_type=jax.ShapeDtypeStruct((n_rows, d), h2.dtype), mesh=_sc_mesh(),
        scratch_types=[pltpu.VMEM((SC_WINDOW, d), h2.dtype), pltpu.VMEM((TOP_K, SC_WINDOW), jnp.int32)],
        name="dispatch_rows",
    )(h2, dest_c)


def _collect_rows(rows, dest_c, t):
    d = rows.shape[1]
    per_worker = t // SC_WINDOW // SC_WORKERS

    def body(r_hbm, i_hbm, o_hbm, buf, idx):
        worker = _sc_worker()

        @pl.loop(0, per_worker)
        def _(j):
            ch = worker * per_worker + j
            pltpu.sync_copy(i_hbm.at[ch], idx)
            for k in range(TOP_K):
                pltpu.sync_copy(r_hbm.at[idx.at[k]], buf)
                pltpu.sync_copy(buf, o_hbm.at[k, pl.ds(ch * SC_WINDOW, SC_WINDOW)])

    return pl.kernel(
        body, out_type=jax.ShapeDtypeStruct((TOP_K, t, d), rows.dtype), mesh=_sc_mesh(),
        scratch_types=[pltpu.VMEM((SC_WINDOW, d), rows.dtype), pltpu.VMEM((TOP_K, SC_WINDOW), jnp.int32)],
        name="collect_rows",
    )(rows, dest_c)


def _experts_kernel(layer, be_ref, nb_ref, slot_ref, next_ref, x_ref, w1_hbm, b1_ref, w2_hbm, b2_ref, o_ref,
                    w1f_ref, w2f_ref, w1b_ref, w2b_ref, sem):
    def weight_copies(expert, slot):
        return (pltpu.make_async_copy(w1_hbm.at[layer, expert], w1f_ref.at[slot], sem.at[slot, 0]),
                pltpu.make_async_copy(w2_hbm.at[layer, expert], w2f_ref.at[slot], sem.at[slot, 1]))

    def row_block(i, rs):
        e = be_ref[i]
        live = i < nb_ref[0]

        @pl.when(live & (i == 0))
        def _():
            for cp in weight_copies(e, 0):
                cp.start()

        @pl.when(live & ((i == 0) | (e != be_ref[jnp.maximum(i - 1, 0)])))
        def _():
            slot = slot_ref[i]
            for cp in weight_copies(e, slot):
                cp.wait()
            w1b_ref[...] = w1f_ref[slot].astype(BF16)
            w2b_ref[...] = w2f_ref[slot].astype(BF16)

            @pl.when(next_ref[i] >= 0)
            def _():
                for cp in weight_copies(next_ref[i], 1 - slot):
                    cp.start()

        @pl.when(live)
        def _():
            x = _unpack_halves(x_ref[rs, :]).astype(BF16)
            hdn = jnp.dot(x, w1b_ref[...], preferred_element_type=F32) + b1_ref[0, pl.ds(e, 1), :]
            glu = jnp.minimum(hdn[:, :D_EXPERT], SWIGLU_LIMIT)
            lin = jnp.clip(hdn[:, D_EXPERT:], -SWIGLU_LIMIT, SWIGLU_LIMIT)
            act = glu * jax.nn.sigmoid(SWIGLU_ALPHA * glu) * (lin + 1.0)
            out = jnp.dot(act.astype(BF16), w2b_ref[...], preferred_element_type=F32) + b2_ref[0, pl.ds(e, 1), :]
            o_ref[rs, :] = _pack_halves(out)

    for h in range(BLOCKS_PER_STEP):
        row_block(pl.program_id(0) * BLOCKS_PER_STEP + h, slice(h * ROW_BLOCK, (h + 1) * ROW_BLOCK))


def _experts(rows, block_exp, n_live, slot, next_exp, layer, w1, b1, w2, b2):
    n_rows, dp = rows.shape
    n_layers, n_e, d, f2 = w1.shape
    n_blocks = n_rows // ROW_BLOCK
    assert n_blocks % BLOCKS_PER_STEP == 0
    step_rows = BLOCKS_PER_STEP * ROW_BLOCK
    this_layer = lambda i, be, nb, sl, nx: (layer, 0, 0)
    grid_spec = pltpu.PrefetchScalarGridSpec(
        num_scalar_prefetch=4,
        grid=(n_blocks // BLOCKS_PER_STEP,),
        in_specs=[
            pl.BlockSpec((step_rows, dp), lambda i, be, nb, sl, nx: (i, 0)),
            pl.BlockSpec(memory_space=pl.ANY),
            pl.BlockSpec((1, n_e, f2), this_layer),
            pl.BlockSpec(memory_space=pl.ANY),
            pl.BlockSpec((1, n_e, d), this_layer),
        ],
        out_specs=pl.BlockSpec((step_rows, dp), lambda i, be, nb, sl, nx: (i, 0)),
        scratch_shapes=[pltpu.VMEM((2, d, f2), F32), pltpu.VMEM((2, f2 // 2, d), F32),
                        pltpu.VMEM((d, f2), BF16), pltpu.VMEM((f2 // 2, d), BF16),
                        pltpu.SemaphoreType.DMA((2, 2))],
    )
    return pl.pallas_call(
        functools.partial(_experts_kernel, layer),
        grid_spec=grid_spec,
        out_shape=jax.ShapeDtypeStruct((n_rows, dp), jnp.int32),
        compiler_params=_params(("arbitrary",)),
        name="experts",
    )(block_exp, n_live, slot, next_exp, rows, w1, b1, w2, b2)


def _moe_routed(h2, idx, rank, layer, w1, b1, w2, b2):
    t, d = h2.shape
    n_e = w1.shape[1]
    n_blocks = t * TOP_K // ROW_BLOCK + n_e
    onehot = (idx[:, :, None] == jnp.arange(n_e, dtype=jnp.int32)).astype(jnp.int32)
    counts = jnp.sum(onehot, axis=(0, 1))
    padded = (counts + ROW_BLOCK - 1) // ROW_BLOCK * ROW_BLOCK
    pad_end = jnp.cumsum(padded)
    dest = jnp.sum(onehot * (pad_end - padded), axis=-1) + rank
    block_row = jnp.arange(n_blocks, dtype=jnp.int32)[:, None] * ROW_BLOCK
    block_exp = jnp.minimum(jnp.sum((pad_end[None, :] <= block_row).astype(jnp.int32), axis=-1), n_e - 1)
    n_live = (pad_end[-1:] // ROW_BLOCK).astype(jnp.int32)
    experts = jnp.arange(n_e, dtype=jnp.int32)
    later = (experts[None, :] > experts[:, None]) & (counts[None, :] > 0)
    next_of = jnp.min(jnp.where(later, experts[None, :], n_e), axis=-1)
    next_of = jnp.where(next_of == n_e, -1, next_of)
    block_onehot = (block_exp[:, None] == experts[None, :]).astype(jnp.int32)
    next_exp = jnp.sum(block_onehot * next_of, axis=-1)
    run_of = jnp.cumsum((counts > 0).astype(jnp.int32)) - 1
    slot = jnp.sum(block_onehot * run_of, axis=-1) % 2
    dest_c = dest.reshape(TOP_K, t // SC_WINDOW, SC_WINDOW).transpose(1, 0, 2)
    rows = _dispatch_rows(h2, dest_c, n_blocks * ROW_BLOCK)
    out = _experts(rows, block_exp, n_live, slot, next_exp, layer, w1, b1, w2, b2)
    return _collect_rows(out, dest_c, t)


def _final_kernel(x_ref, rows_ref, gate_ref, g2_ref, fg_ref, o_ref):
    x = _moe_residual(x_ref, rows_ref, gate_ref, g2_ref)
    ms = jnp.mean(x * x, axis=-1, keepdims=True)
    o_ref[0] = x * lax.rsqrt(ms + NORM_EPS) * fg_ref[...]


def _final(x, res, final_g, tm=512):
    b, s, d = x.shape
    tm = min(tm, s)
    row = pl.BlockSpec((1, tm, d), lambda i, j: (i, j, 0))
    return pl.pallas_call(
        _final_kernel,
        grid=(b, s // tm),
        in_specs=[row] + _moe_residual_specs(tm, d) + [pl.BlockSpec((1, d), lambda i, j: (0, 0))],
        out_specs=row,
        out_shape=jax.ShapeDtypeStruct((b, s, d), F32),
        compiler_params=_params(("parallel", "parallel")),
        name="final",
    )(x, *res, final_g.reshape(1, d))


def _widen_w_in(w):
    gl0 = 2 * A_WIDTH + B_HEADS * DK + 6 * KV_GROUPS * DK
    per_group = HPG * 3
    gl_groups = [jnp.pad(w[:, gl0 + g * per_group:gl0 + (g + 1) * per_group], ((0, 0), (0, LANES - per_group)))
                 for g in range(KV_GROUPS)]
    wide = jnp.concatenate([w[:, :gl0]] + gl_groups, axis=1)
    assert gl0 == C_GL and wide.shape[1] == P_WIDE
    return wide.astype(BF16)


def _rope_tables(pos):
    half = DK // 2
    inv = ROPE_THETA ** (-jnp.arange(half, dtype=F32) / half)
    ang = pos.astype(F32)[..., None] * inv
    cos = jnp.cos(ang)
    sin = jnp.sin(ang)
    reps = LANES // DK
    return (jnp.concatenate([cos, cos] * reps, axis=-1),
            jnp.concatenate([-sin, sin] * reps, axis=-1))


def kernel(x, c, positions, ada_w, ada_b, norm1_g, norm2_g, w_in, w_out, sg_ln_g, sg_ln_b, sg_w, sg_b,
           cmp_pe_k, cmp_pe_v, cmp_w1_k, cmp_w2_k, cmp_w1_v, cmp_w2_v, router_w, router_b,
           exp_w1, exp_b1, exp_w2, exp_b2, final_g):
    b, s, d = x.shape
    n_layers = ada_w.shape[0]
    mod = _ada_mod(c, ada_w, ada_b)
    cos, sin = _rope_tables(positions)
    cmp_end = jnp.minimum(CMP_STRIDE * jnp.arange(s // CMP_STRIDE) + CMP_BLOCK - 1, s - 1)
    cos_c, sin_c = _rope_tables(positions[:, cmp_end])
    res = None
    for l in range(n_layers):
        sh1, sc1, g1, sh2, sc2, g2 = [m.reshape(b, 1, d) for m in jnp.split(mod[l], 6, axis=-1)]
        x, uv, qt, kc, vc, ks, vs_t, kw, vw_t, gl_t = _pre(x, res, norm1_g[l], sc1, sh1,
                                                          _widen_w_in(w_in[l]), cos, sin)
        y_a = _sgu(uv, sg_ln_g[l], sg_ln_b[l], sg_w[l], sg_b[l])
        kcmp, vcmp_t = _compress(kc, vc, cmp_pe_k[l], cmp_pe_v[l], cmp_w1_k[l], cmp_w2_k[l],
                                 cmp_w1_v[l], cmp_w2_v[l], cos_c, sin_c)
        oc_t, bias_t = _cmp_sel(qt, kcmp, vcmp_t)
        y_b = _attn(qt, ks, kw, vs_t, vw_t, bias_t, oc_t, gl_t)
        x, h2, idx, gate, rank = _post(y_a, y_b, w_out[l], x, g1, norm2_g[l], sc2, sh2,
                                       router_w[l], router_b[l])
        rows = _moe_routed(h2.reshape(b * s, d // 2), idx, rank, l, exp_w1, exp_b1, exp_w2, exp_b2)
        res = (rows.reshape(TOP_K, b, s, d // 2), gate.T.reshape(b, s, TOP_K), g2)
    return _final(x, res, final_g)
```

```python
import functools

import numpy as np
import jax
import jax.numpy as jnp
from jax import lax
from jax.experimental import pallas as pl
from jax.experimental.pallas import tpu as pltpu
from jax.experimental.pallas import tpu_sc as plsc

F32 = jnp.float32
BF16 = jnp.bfloat16
HIGHEST = lax.Precision.HIGHEST

D_MODEL = 1024
A_WIDTH = 512
A_HEADS = 8
CHUNK = 128
B_HEADS = 8
DK = 64
KV_GROUPS = 2
HPG = B_HEADS // KV_GROUPS
CMP_BLOCK = 32
CMP_STRIDE = 16
SEL_BLOCK = 64
SEL_TOP = 16
WINDOW = 512
ROPE_THETA = 10000.0
N_EXPERTS = 32
TOP_K = 4
D_EXPERT = 1024
SWIGLU_LIMIT = 7.0
SWIGLU_ALPHA = 1.702
NORM_EPS = 1e-6

LANES = 128
SUBLANES = 8
GW = HPG * DK
N_FORCED = 3
ONES_ROWS = 16
MASK_BIAS = -32768.0
Q_SCALE = DK ** -0.5 * 1.4426950408889634
VMEM_LIMIT = 56 * 1024 * 1024
ROW_BLOCK = 256
BLOCKS_PER_STEP = 4
SC_CORES = 2
SC_SUBCORES = 16
SC_WORKERS = SC_CORES * SC_SUBCORES
SC_WINDOW = 128
CMP_ROW_CHUNK = 128

C_U, C_V, C_Q = 0, 512, 1024
C_KC, C_VC, C_KS, C_VS, C_KW, C_VW = 1536, 1664, 1792, 1920, 2048, 2176
C_GL = 2304
P_WIDE = 2560


def _params(sem):
    return pltpu.CompilerParams(dimension_semantics=sem, vmem_limit_bytes=VMEM_LIMIT)


def _split_bf16(x, parts):
    out = []
    for _ in range(parts):
        piece = x.astype(BF16)
        out.append(piece)
        x = x - piece.astype(F32)
    return out


def _split_bf16_bits(x, parts):
    out = []
    for _ in range(parts):
        piece = lax.bitcast_convert_type(lax.bitcast_convert_type(x, jnp.uint32) & jnp.uint32(0xFFFF0000), F32)
        out.append(piece.astype(BF16))
        x = x - piece
    return out


def _ada_kernel(c_ref, w_ref, b_ref, o_ref):
    c = c_ref[...]
    cond = c * jax.nn.sigmoid(c)
    o_ref[0] = jnp.dot(cond, w_ref[0], precision=HIGHEST, preferred_element_type=F32) + b_ref[0]


def _ada_mod(c, ada_w, ada_b):
    n_layers, d, d6 = ada_w.shape
    b = c.shape[0]
    rows = SUBLANES
    c_pad = jnp.zeros((rows, d), F32).at[:b].set(c)
    out = pl.pallas_call(
        _ada_kernel,
        grid=(n_layers, d6 // d),
        in_specs=[
            pl.BlockSpec((rows, d), lambda l, j: (0, 0)),
            pl.BlockSpec((1, d, d), lambda l, j: (l, 0, j)),
            pl.BlockSpec((1, 1, d), lambda l, j: (l, 0, j)),
        ],
        out_specs=pl.BlockSpec((1, rows, d), lambda l, j: (l, 0, j)),
        out_shape=jax.ShapeDtypeStruct((n_layers, rows, d6), F32),
        compiler_params=_params(("arbitrary", "arbitrary")),
        name="ada",
    )(c_pad, ada_w, ada_b.reshape(n_layers, 1, d6))
    return out[:, :b]


def _rope_slab(t, cos, sin_signed, lo):
    partner = jnp.where(lo, pltpu.roll(t, LANES - DK // 2, 1), pltpu.roll(t, DK // 2, 1))
    return t * cos + partner * sin_signed


def _rope(t, cos, sin_signed):
    lane = lax.broadcasted_iota(jnp.int32, (1, LANES), 1)
    lo = (lane % DK) < (DK // 2)
    slabs = [_rope_slab(t[:, s * LANES:(s + 1) * LANES], cos, sin_signed, lo)
             for s in range(t.shape[1] // LANES)]
    return slabs[0] if len(slabs) == 1 else jnp.concatenate(slabs, axis=1)


def _pack_halves(x):
    n = x.shape[1] // 2
    lo = lax.bitcast_convert_type(x[:, :n].astype(BF16).astype(F32), jnp.int32)
    hi = lax.bitcast_convert_type(x[:, n:].astype(BF16).astype(F32), jnp.int32)
    return lax.shift_right_logical(lo, jnp.int32(16)) | (hi & jnp.int32(-65536))


def _unpack_halves(p):
    lo = lax.bitcast_convert_type(lax.shift_left(p, jnp.int32(16)), F32)
    hi = lax.bitcast_convert_type(p & jnp.int32(-65536), F32)
    return jnp.concatenate([lo, hi], axis=1)


def _moe_residual(x_ref, rows_ref, gate_ref, g2_ref):
    gate = gate_ref[0]
    y = gate[:, 0:1] * _unpack_halves(rows_ref[0, 0])
    for k in range(1, TOP_K):
        y = y + gate[:, k:k + 1] * _unpack_halves(rows_ref[k, 0])
    return x_ref[0] + g2_ref[0] * y


def _moe_residual_specs(tm, d):
    return [pl.BlockSpec((TOP_K, 1, tm, d // 2), lambda i, j: (0, i, j, 0)),
            pl.BlockSpec((1, tm, TOP_K), lambda i, j: (i, j, 0)),
            pl.BlockSpec((1, 1, d), lambda i, j: (i, 0, 0))]


def _pre_kernel(has_res, *refs):
    if has_res:
        (x_ref, rows_ref, gate_ref, g2_ref, ng_ref, sc_ref, sh_ref, w_ref, cos_ref, sin_ref,
         xo_ref, uv_ref, qt_ref, kc_ref, vc_ref, ks_ref, vst_ref, kw_ref, vwt_ref, glt_ref) = refs
        x = _moe_residual(x_ref, rows_ref, gate_ref, g2_ref)
        xo_ref[0] = x
    else:
        (x_ref, ng_ref, sc_ref, sh_ref, w_ref, cos_ref, sin_ref,
         uv_ref, qt_ref, kc_ref, vc_ref, ks_ref, vst_ref, kw_ref, vwt_ref, glt_ref) = refs
        x = x_ref[0]
    ms = jnp.mean(x * x, axis=-1, keepdims=True)
    h = x * lax.rsqrt(ms + NORM_EPS) * ng_ref[...]
    h = h * (1.0 + sc_ref[0]) + sh_ref[0]
    proj = jnp.dot(h.astype(BF16), w_ref[...], preferred_element_type=F32)
    cos = cos_ref[0]
    sin = sin_ref[0]
    uv_ref[0] = proj[:, C_U:C_Q]
    qt_ref[0] = (_rope(proj[:, C_Q:C_KC], cos, sin) * Q_SCALE).T
    kc_ref[0] = proj[:, C_KC:C_VC]
    vc_ref[0] = proj[:, C_VC:C_KS]
    ks_ref[0] = _rope(proj[:, C_KS:C_VS], cos, sin).astype(BF16)
    vst_ref[0] = proj[:, C_VS:C_KW].T.astype(BF16)
    kw_ref[0] = _rope(proj[:, C_KW:C_VW], cos, sin).astype(BF16)
    vwt_ref[0] = proj[:, C_VW:C_GL].T.astype(BF16)
    glt_ref[0] = proj[:, C_GL:P_WIDE].T


def _pre(x, res, norm_g, sc, sh, w_wide, cos, sin, tm=512):
    b, s, d = x.shape
    tm = min(tm, s)
    row = lambda w: pl.BlockSpec((1, tm, w), lambda i, j: (i, j, 0))
    col = lambda w: pl.BlockSpec((1, w, tm), lambda i, j: (i, 0, j))
    vec = pl.BlockSpec((1, 1, d), lambda i, j: (i, 0, 0))
    in_specs = [row(d)]
    args = [x]
    if res is not None:
        in_specs += _moe_residual_specs(tm, d)
        args += list(res)
    in_specs += [pl.BlockSpec((1, d), lambda i, j: (0, 0)), vec, vec,
                 pl.BlockSpec((d, P_WIDE), lambda i, j: (0, 0)), row(LANES), row(LANES)]
    args += [norm_g.reshape(1, d), sc, sh, w_wide, cos, sin]
    kv = KV_GROUPS * DK
    outs = [(d, F32, False)] if res is not None else []
    outs += [(2 * A_WIDTH, F32, False), (B_HEADS * DK, F32, True), (kv, F32, False), (kv, F32, False),
             (kv, BF16, False), (kv, BF16, True), (kv, BF16, False), (kv, BF16, True),
             (KV_GROUPS * LANES, F32, True)]
    res_out = pl.pallas_call(
        functools.partial(_pre_kernel, res is not None),
        grid=(b, s // tm),
        in_specs=in_specs,
        out_specs=[col(w) if t else row(w) for w, _, t in outs],
        out_shape=[jax.ShapeDtypeStruct((b, w, s) if t else (b, s, w), dt) for w, dt, t in outs],
        compiler_params=_params(("parallel", "parallel")),
        name="pre",
    )(*args)
    if res is None:
        res_out = [x] + list(res_out)
    return res_out


def _sgu_kernel(uv_ref, lng_ref, lnb_ref, w_ref, bias_ref, o_ref):
    rows = uv_ref.shape[1]
    uv = uv_ref[0]
    gu = jax.nn.gelu(uv[:, :A_WIDTH])
    gv = jax.nn.gelu(uv[:, A_WIDTH:])
    mu = jnp.mean(gv, axis=-1, keepdims=True)
    var = jnp.mean(jnp.square(gv - mu), axis=-1, keepdims=True)
    vn = ((gv - mu) * lax.rsqrt(var + NORM_EPS) * lng_ref[...] + lnb_ref[...]).astype(BF16)
    r = lax.broadcasted_iota(jnp.int32, (CHUNK, CHUNK), 0)
    c = lax.broadcasted_iota(jnp.int32, (CHUNK, CHUNK), 1)
    causal = c <= r
    lane_lo = lax.broadcasted_iota(jnp.int32, (CHUNK, LANES), 1) < DK
    for p in range(A_HEADS // 2):
        w0 = jnp.where(causal, w_ref[2 * p], 0.0).astype(BF16)
        w1 = jnp.where(causal, w_ref[2 * p + 1], 0.0).astype(BF16)
        bias = bias_ref[:, p * LANES:(p + 1) * LANES]
        for ch in range(rows // CHUNK):
            rs = slice(ch * CHUNK, (ch + 1) * CHUNK)
            cs = slice(p * LANES, (p + 1) * LANES)
            vp = vn[rs, cs]
            m0 = jnp.dot(w0, vp, preferred_element_type=F32)
            m1 = jnp.dot(w1, vp, preferred_element_type=F32)
            mixed = jnp.where(lane_lo, m0, m1) + bias
            o_ref[0, rs, cs] = (gu[rs, cs] * mixed).astype(o_ref.dtype)


def _sgu(uv, ln_g, ln_b, w_s, b_s, tm=512):
    b, s, _ = uv.shape
    tm = min(tm, s)
    bias = jnp.repeat(b_s.T, A_WIDTH // A_HEADS, axis=1)
    return pl.pallas_call(
        _sgu_kernel,
        grid=(b, s // tm),
        in_specs=[
            pl.BlockSpec((1, tm, 2 * A_WIDTH), lambda i, j: (i, j, 0)),
            pl.BlockSpec((1, A_WIDTH), lambda i, j: (0, 0)),
            pl.BlockSpec((1, A_WIDTH), lambda i, j: (0, 0)),
            pl.BlockSpec((A_HEADS, CHUNK, CHUNK), lambda i, j: (0, 0, 0)),
            pl.BlockSpec((CHUNK, A_WIDTH), lambda i, j: (0, 0)),
        ],
        out_specs=pl.BlockSpec((1, tm, A_WIDTH), lambda i, j: (i, j, 0)),
        out_shape=jax.ShapeDtypeStruct((b, s, A_WIDTH), BF16),
        compiler_params=_params(("parallel", "parallel")),
        name="sgu",
    )(uv, ln_g.reshape(1, -1), ln_b.reshape(1, -1), w_s, bias)


def _compress_kernel(kr_ref, vr_ref, pek_ref, pev_ref, w1k_ref, w1v_ref, w2k_ref, w2v_ref,
                     cos_ref, sin_ref, ko_ref, vt_ref):
    nc = kr_ref.shape[1]

    def mlp(r, pe_ref, w1_ref, w2_ref):
        top = jnp.dot(r + pe_ref[0:1], w1_ref[0], precision=HIGHEST, preferred_element_type=F32)
        bot = jnp.dot(r + pe_ref[1:2], w1_ref[1], precision=HIGHEST, preferred_element_type=F32)
        pre = top + pltpu.roll(bot, nc - 1, 0)
        return jnp.dot(jax.nn.gelu(pre), w2_ref[...], precision=HIGHEST, preferred_element_type=F32)

    kc = _rope(mlp(kr_ref[0], pek_ref, w1k_ref, w2k_ref), cos_ref[0], sin_ref[0])
    vc = mlp(vr_ref[0], pev_ref, w1v_ref, w2v_ref)
    lo = lax.broadcasted_iota(jnp.int32, (nc, LANES), 1) < DK
    rolled = pltpu.roll(kc, DK, 1)
    for g in range(KV_GROUPS):
        dup = jnp.where(lo, kc, rolled) if g == 0 else jnp.where(lo, rolled, kc)
        hi, low = _split_bf16(dup, 2)
        ko_ref[0, g, :, 0:LANES] = hi
        ko_ref[0, g, :, LANES:2 * LANES] = jnp.where(lo, low, jnp.zeros_like(low))
    vt_ref[0] = vc.T.astype(BF16)


def _compress_weights(pe, w1, w2):
    half = CMP_BLOCK // 2
    eye = jnp.eye(KV_GROUPS, dtype=F32)
    w1r = w1.reshape(CMP_BLOCK, DK, DK)
    wfull = jnp.einsum('lde,gh->lgdhe', w1r, eye)
    w1s = wfull.reshape(2, half * KV_GROUPS * DK, KV_GROUPS * DK)
    pes = jnp.broadcast_to(pe.reshape(2, half, 1, DK), (2, half, KV_GROUPS, DK)).reshape(2, -1)
    w2bd = jnp.einsum('de,gh->gdhe', w2, eye).reshape(KV_GROUPS * DK, KV_GROUPS * DK)
    return pes, w1s, w2bd


def _compress(kc, vc, pe_k, pe_v, w1_k, w2_k, w1_v, w2_v, cos_c, sin_c):
    b, s, _ = kc.shape
    nc = s // CMP_STRIDE
    rw = CMP_STRIDE * LANES
    pek, w1ks, w2kb = _compress_weights(pe_k, w1_k, w2_k)
    pev, w1vs, w2vb = _compress_weights(pe_v, w1_v, w2_v)
    full = lambda shape: pl.BlockSpec(shape, lambda i: (0,) * len(shape))
    return pl.pallas_call(
        _compress_kernel,
        grid=(b,),
        in_specs=[
            pl.BlockSpec((1, nc, rw), lambda i: (i, 0, 0)),
            pl.BlockSpec((1, nc, rw), lambda i: (i, 0, 0)),
            full((2, rw)), full((2, rw)),
            full((2, rw, LANES)), full((2, rw, LANES)),
            full((LANES, LANES)), full((LANES, LANES)),
            pl.BlockSpec((1, nc, LANES), lambda i: (i, 0, 0)),
            pl.BlockSpec((1, nc, LANES), lambda i: (i, 0, 0)),
        ],
        out_specs=[pl.BlockSpec((1, KV_GROUPS, nc, 2 * LANES), lambda i: (i, 0, 0, 0)),
                   pl.BlockSpec((1, KV_GROUPS * DK, nc), lambda i: (i, 0, 0))],
        out_shape=[jax.ShapeDtypeStruct((b, KV_GROUPS, nc, 2 * LANES), BF16),
                   jax.ShapeDtypeStruct((b, KV_GROUPS * DK, nc), BF16)],
        compiler_params=_params(("parallel",)),
        name="compress",
    )(kc.reshape(b, nc, rw), vc.reshape(b, nc, rw), pek, pev, w1ks, w1vs, w2kb, w2vb, cos_c, sin_c)


def _heads_on_lanes(qt):
    return jnp.concatenate([qt[h * DK:(h + 1) * DK] for h in range(HPG)], axis=1)


def _cmp_sel_kernel(n_top, qt_ref, k_ref, vt_ref, map_ref, oc_ref, bias_ref, imp_ref):
    tq = qt_ref.shape[2]
    cols = HPG * tq
    nc = k_ref.shape[2]
    nsel = map_ref.shape[0]
    q0 = pl.program_id(2) * tq
    q_hi, q_lo = _split_bf16(_heads_on_lanes(qt_ref[0]), 2)
    q3t = jnp.concatenate([q_hi, q_lo, q_hi, jnp.zeros_like(q_hi)], axis=0)
    t_col = q0 + lax.broadcasted_iota(jnp.int32, (1, cols), 1) % tq

    def compressed_branch(rows):
        s = jnp.dot(k_ref[0, 0, 0:rows], q3t, preferred_element_type=F32)
        n_idx = lax.broadcasted_iota(jnp.int32, (rows, 1), 0)
        s = jnp.where((CMP_STRIDE * n_idx + CMP_BLOCK - 1) <= t_col, s, -jnp.inf)
        m = jnp.max(s, axis=0, keepdims=True)
        m = jnp.where(m == -jnp.inf, 0.0, m)
        e = jnp.exp2(s - m)
        d = jnp.sum(e, axis=0, keepdims=True)
        p = e / jnp.where(d > 0, d, 1.0)
        oc = jnp.dot(vt_ref[0, :, 0:rows], p.astype(BF16), preferred_element_type=F32)
        oc_ref[0] = jnp.concatenate([oc[:, h * tq:(h + 1) * tq] for h in range(HPG)], axis=0)
        psum = p[:, 0:tq]
        for h in range(1, HPG):
            psum = psum + p[:, h * tq:(h + 1) * tq]
        imp_ref[...] = jnp.dot(jnp.concatenate([map_ref[:, 0:rows]] * 3, axis=1),
                               jnp.concatenate(_split_bf16(psum, 3), axis=0), preferred_element_type=F32)

    visible = (q0 + tq - CMP_BLOCK) // CMP_STRIDE + 1
    n_chunks = -(-nc // CMP_ROW_CHUNK)
    for r in range(1, n_chunks + 1):
        lo = (r - 1) * CMP_ROW_CHUNK if r > 1 else -nc
        pl.when((visible > lo) & (visible <= r * CMP_ROW_CHUNK) if r < n_chunks else visible > lo)(
            functools.partial(compressed_branch, min(r * CMP_ROW_CHUNK, nc)))

    imp = imp_ref[...]
    j = lax.broadcasted_iota(jnp.int32, (nsel, tq), 0)
    cur = (q0 + lax.broadcasted_iota(jnp.int32, (nsel, tq), 1)) // SEL_BLOCK
    valid = j <= cur
    forced = (j == 0) | (j == cur) | (j == cur - 1)
    keep = forced | (valid & (cur < n_top))
    vals = jnp.where(valid & jnp.logical_not(forced), imp, -jnp.inf)
    bias = jnp.where(keep, 0.0, MASK_BIAS)
    for _ in range(n_top - N_FORCED):
        mx = jnp.max(vals, axis=0, keepdims=True)
        first = jnp.min(jnp.where(vals == mx, j, nsel), axis=0, keepdims=True)
        pick = (j == first) & (mx > -jnp.inf)
        bias = jnp.where(pick, 0.0, bias)
        vals = jnp.where(pick, -jnp.inf, vals)
    bias_ref[0, 0] = bias.astype(BF16)


def _sel_map_t(s):
    nc = s // CMP_STRIDE
    n_cmp = (s - CMP_BLOCK) // CMP_STRIDE + 1
    n_sel = s // SEL_BLOCK
    cs = CMP_STRIDE * np.arange(n_cmp)[:, None]
    ce = cs + CMP_BLOCK
    ss = SEL_BLOCK * np.arange(n_sel)[None, :]
    se = ss + SEL_BLOCK
    ov = np.clip(np.minimum(ce, se) - np.maximum(cs, ss), 0, None) / CMP_STRIDE
    out = np.zeros((n_sel, nc), np.float32)
    out[:, :n_cmp] = ov.T
    return out


def _cmp_sel(qt, kcmp, vcmp_t, tq=256):
    b, _, s = qt.shape
    nc = s // CMP_STRIDE
    nsel = s // SEL_BLOCK
    n_top = min(SEL_TOP, nsel)
    sel_map = jnp.asarray(_sel_map_t(s), dtype=BF16)
    return pl.pallas_call(
        functools.partial(_cmp_sel_kernel, n_top),
        grid=(b, KV_GROUPS, s // tq),
        in_specs=[
            pl.BlockSpec((1, GW, tq), lambda i, g, j: (i, g, j)),
            pl.BlockSpec((1, 1, nc, 2 * LANES), lambda i, g, j: (i, g, 0, 0)),
            pl.BlockSpec((1, DK, nc), lambda i, g, j: (i, g, 0)),
            pl.BlockSpec((nsel, nc), lambda i, g, j: (0, 0)),
        ],
        out_specs=[
            pl.BlockSpec((1, GW, tq), lambda i, g, j: (i, g, j)),
            pl.BlockSpec((1, 1, nsel, tq), lambda i, g, j: (i, g, 0, j)),
        ],
        out_shape=[
            jax.ShapeDtypeStruct((b, B_HEADS * DK, s), F32),
            jax.ShapeDtypeStruct((b, KV_GROUPS, nsel, s), BF16),
        ],
        scratch_shapes=[pltpu.VMEM((nsel, tq), F32)],
        compiler_params=_params(("parallel", "parallel", "parallel")),
        name="cmp_sel",
    )(qt, kcmp, vcmp_t, sel_map)


def _attn_kernel(kt, qt_ref, ks_ref, kw_ref, vst_ref, vwt_ref, bias_ref, oh_ref, oc_ref, glt_ref,
                 o_ref, qa_ref, s0_ref, s1_ref, m_ref, acc_ref, ow_ref):
    tq = qt_ref.shape[2]
    cols = HPG * tq
    nsel = bias_ref.shape[2]
    g = pl.program_id(1)
    q0 = pl.program_id(2) * tq

    q4t = _heads_on_lanes(qt_ref[0]).astype(BF16)
    zero = jnp.zeros_like(q4t)
    qa_ref[0:DK] = jnp.where(g == 0, q4t, zero)
    qa_ref[DK:2 * DK] = jnp.where(g == 1, q4t, zero)
    qa_ref[LANES:LANES + nsel] = jnp.concatenate([bias_ref[0, 0]] * HPG, axis=1)
    qaug = qa_ref[...]
    t_col = q0 + lax.broadcasted_iota(jnp.int32, (1, cols), 1) % tq

    def fold(x, op):
        return op(x.reshape(x.shape[0] // SUBLANES, SUBLANES, cols), axis=0)

    def with_ones(vt):
        return jnp.concatenate([vt, jnp.ones((ONES_ROWS, vt.shape[1]), BF16)], axis=0)

    def stage(c, s_ref):
        k0 = pl.multiple_of(c * kt, kt)
        kaug = jnp.concatenate([ks_ref[0, pl.ds(k0, kt), :], oh_ref[pl.ds(k0, kt), :]], axis=1)
        s_ref[...] = jnp.dot(kaug, qaug, preferred_element_type=F32)
        r0 = pl.multiple_of(jnp.clip(q0 - k0, 0, kt - tq), tq)
        kpos = k0 + r0 + lax.broadcasted_iota(jnp.int32, (tq, 1), 0)
        s_ref[pl.ds(r0, tq), :] = jnp.where(kpos <= t_col, s_ref[pl.ds(r0, tq), :], -jnp.inf)

    def consume(c, s_ref):
        k0 = pl.multiple_of(c * kt, kt)
        sc = s_ref[...]
        m_old = m_ref[...]
        m_new = jnp.maximum(m_old, jnp.max(fold(sc, jnp.max), axis=0, keepdims=True))
        alpha = jnp.exp2(m_old - m_new)
        pr = jnp.exp2(sc - m_new)
        acc_ref[...] = alpha * acc_ref[...] + jnp.dot(with_ones(vst_ref[0, :, pl.ds(k0, kt)]), pr.astype(BF16),
                                                      preferred_element_type=F32)
        m_ref[...] = m_new

    m_ref[...] = jnp.full(m_ref.shape, -jnp.inf, F32)
    acc_ref[...] = jnp.zeros(acc_ref.shape, F32)
    n_full = q0 // kt
    n_pairs = n_full // 2
    odd = n_full % 2 == 1
    stage(0, s0_ref)

    wlen = WINDOW + tq
    w0 = pl.multiple_of(jnp.maximum(q0 - WINDOW, 0), tq)
    sw = jnp.dot(kw_ref[0, pl.ds(w0, wlen), :], qaug[0:LANES], preferred_element_type=F32)
    dlt = t_col - (w0 + lax.broadcasted_iota(jnp.int32, (wlen, 1), 0))
    sw = jnp.where(lax.bitcast_convert_type(dlt, jnp.uint32) < jnp.uint32(WINDOW), sw, -jnp.inf)
    pw = jnp.exp2(sw - jnp.max(sw, axis=0, keepdims=True))
    o_win = jnp.dot(with_ones(vwt_ref[0, :, pl.ds(w0, wlen)]), pw.astype(BF16), preferred_element_type=F32)
    ow_ref[...] = o_win[0:DK] / o_win[DK:DK + 1]

    def pair(i, carry):
        stage(2 * i + 1, s1_ref)
        consume(2 * i, s0_ref)
        stage(2 * i + 2, s0_ref)
        consume(2 * i + 1, s1_ref)
        return carry

    lax.fori_loop(0, n_pairs, pair, 0)

    @pl.when(odd)
    def _():
        stage(n_full, s1_ref)
        consume(n_full - 1, s0_ref)
        consume(n_full, s1_ref)

    @pl.when(jnp.logical_not(odd))
    def _():
        consume(n_full, s0_ref)

    o_sel = acc_ref[0:DK] / acc_ref[DK:DK + 1]

    o_win = ow_ref[...]
    gates = jax.nn.sigmoid(glt_ref[0])
    oc = _heads_on_lanes(oc_ref[0])
    for p in range(HPG // 2):
        halves = []
        for hh in (2 * p, 2 * p + 1):
            cs = slice(hh * tq, (hh + 1) * tq)
            halves.append(gates[3 * hh:3 * hh + 1] * oc[:, cs] + gates[3 * hh + 1:3 * hh + 2] * o_sel[:, cs]
                          + gates[3 * hh + 2:3 * hh + 3] * o_win[:, cs])
        o_ref[0, :, p * LANES:(p + 1) * LANES] = jnp.concatenate(halves, axis=0).T.astype(o_ref.dtype)


def _attn(qt, ks, kw, vs_t, vw_t, bias_t, oc_t, gl_t, tq=256, kt=512):
    b, _, s = qt.shape
    nsel = s // SEL_BLOCK
    kt = min(kt, s)
    onehot = jnp.asarray((np.arange(s)[:, None] // SEL_BLOCK == np.arange(nsel)[None, :]), dtype=BF16)
    k_spec = pl.BlockSpec((1, s, KV_GROUPS * DK), lambda i, g, j: (i, 0, 0))
    vt_spec = pl.BlockSpec((1, DK, s), lambda i, g, j: (i, g, 0))
    cols = HPG * tq
    return pl.pallas_call(
        functools.partial(_attn_kernel, kt),
        grid=(b, KV_GROUPS, s // tq),
        in_specs=[
            pl.BlockSpec((1, GW, tq), lambda i, g, j: (i, g, j)),
            k_spec, k_spec, vt_spec, vt_spec,
            pl.BlockSpec((1, 1, nsel, tq), lambda i, g, j: (i, g, 0, j)),
            pl.BlockSpec((s, nsel), lambda i, g, j: (0, 0)),
            pl.BlockSpec((1, GW, tq), lambda i, g, j: (i, g, j)),
            pl.BlockSpec((1, LANES, tq), lambda i, g, j: (i, g, j)),
        ],
        out_specs=pl.BlockSpec((1, tq, GW), lambda i, g, j: (i, j, g)),
        out_shape=jax.ShapeDtypeStruct((b, s, B_HEADS * DK), BF16),
        scratch_shapes=[pltpu.VMEM((LANES + nsel, cols), BF16),
                        pltpu.VMEM((kt, cols), F32), pltpu.VMEM((kt, cols), F32),
                        pltpu.VMEM((1, cols), F32), pltpu.VMEM((DK + ONES_ROWS, cols), F32),
                        pltpu.VMEM((DK, cols), F32)],
        compiler_params=_params(("parallel", "parallel", "arbitrary")),
        name="attn",
    )(qt, ks, kw, vs_t, vw_t, bias_t, onehot, oc_t, gl_t)


def _post_kernel(ya_ref, yb_ref, wo_ref, x_ref, g1_ref, ng_ref, sc_ref, sh_ref, rw_ref, rb_ref,
                 xo_ref, h_ref, idx_ref, gate_ref, rank_ref, cnt_ref):
    tm = x_ref.shape[1]

    @pl.when((pl.program_id(0) == 0) & (pl.program_id(1) == 0))
    def _():
        cnt_ref[...] = jnp.zeros(cnt_ref.shape, F32)

    mixed = jnp.dot(ya_ref[0], wo_ref[:A_WIDTH], preferred_element_type=F32)
    mixed = mixed + jnp.dot(yb_ref[0], wo_ref[A_WIDTH:], preferred_element_type=F32)
    x = x_ref[0] + g1_ref[0] * mixed
    xo_ref[0] = x
    ms = jnp.mean(x * x, axis=-1, keepdims=True)
    h = x * lax.rsqrt(ms + NORM_EPS) * ng_ref[...]
    h = h * (1.0 + sc_ref[0]) + sh_ref[0]
    h_ref[0] = _pack_halves(h)
    h_hi, h_mid = _split_bf16(h, 2)
    logits = jnp.dot(jnp.concatenate([h_hi, h_hi, h_mid], axis=1), rw_ref[...],
                     preferred_element_type=F32) + rb_ref[...]
    vals = logits.T[:N_EXPERTS]
    e_idx = lax.broadcasted_iota(jnp.int32, (N_EXPERTS, tm), 0)
    top = jnp.max(vals, axis=0, keepdims=True)
    routed_t = jnp.zeros_like(vals)
    firsts, exps = [], []
    for k in range(TOP_K):
        m = jnp.max(vals, axis=0, keepdims=True)
        first = jnp.min(jnp.where(vals == m, e_idx, N_EXPERTS), axis=0, keepdims=True)
        pick = e_idx == first
        routed_t = jnp.where(pick, 1.0, routed_t)
        firsts.append(first)
        exps.append(jnp.exp(m - top))
        vals = jnp.where(pick, -jnp.inf, vals)
    idx_i = jnp.concatenate(firsts, axis=0)
    idx_t = idx_i.astype(F32)
    slot_exp = jnp.concatenate(exps, axis=0)
    idx_ref[...] = idx_i
    gate_ref[...] = slot_exp / jnp.sum(slot_exp, axis=0, keepdims=True)
    r = lax.broadcasted_iota(jnp.int32, (tm, tm), 0)
    c = lax.broadcasted_iota(jnp.int32, (tm, tm), 1)
    before = jnp.where(r < c, 1.0, 0.0).astype(BF16)
    rank_t = cnt_ref[...] + jnp.dot(routed_t.astype(BF16), before, preferred_element_type=F32)
    cnt_ref[...] += jnp.sum(routed_t, axis=-1, keepdims=True)
    e_row = lax.broadcasted_iota(jnp.int32, (N_EXPERTS, tm), 0).astype(F32)
    ranks = [jnp.sum(jnp.where(e_row == idx_t[k:k + 1], rank_t, 0.0), axis=0, keepdims=True)
             for k in range(TOP_K)]
    rank_ref[...] = jnp.concatenate(ranks, axis=0).astype(jnp.int32)


def _post(y_a, y_b, w_out, x, g1, norm_g, sc, sh, router_w, router_b, tm=512):
    b, s, d = x.shape
    tm = min(tm, s)
    nt = s // tm
    w_hi, w_mid = _split_bf16_bits(jnp.zeros((d, LANES), F32).at[:, :N_EXPERTS].set(router_w), 2)
    rw = jnp.concatenate([w_hi, w_mid, w_hi], axis=0)
    rb = jnp.full((1, LANES), -1e30, F32).at[0, :N_EXPERTS].set(router_b)
    vec = pl.BlockSpec((1, 1, d), lambda i, j: (i, 0, 0))
    row = lambda w: pl.BlockSpec((1, tm, w), lambda i, j: (i, j, 0))
    return pl.pallas_call(
        _post_kernel,
        grid=(b, nt),
        in_specs=[
            row(A_WIDTH), row(B_HEADS * DK),
            pl.BlockSpec((d, d), lambda i, j: (0, 0)),
            row(d), vec,
            pl.BlockSpec((1, d), lambda i, j: (0, 0)), vec, vec,
            pl.BlockSpec((3 * d, LANES), lambda i, j: (0, 0)),
            pl.BlockSpec((1, LANES), lambda i, j: (0, 0)),
        ],
        out_specs=[row(d), row(d // 2)] + [pl.BlockSpec((TOP_K, tm), lambda i, j: (0, i * nt + j))] * 3,
        out_shape=[
            jax.ShapeDtypeStruct((b, s, d), F32),
            jax.ShapeDtypeStruct((b, s, d // 2), jnp.int32),
            jax.ShapeDtypeStruct((TOP_K, b * s), jnp.int32),
            jax.ShapeDtypeStruct((TOP_K, b * s), F32),
            jax.ShapeDtypeStruct((TOP_K, b * s), jnp.int32),
        ],
        scratch_shapes=[pltpu.VMEM((N_EXPERTS, 1), F32)],
        compiler_params=_params(("arbitrary", "arbitrary")),
        name="post",
    )(y_a, y_b, w_out.astype(BF16), x, g1, norm_g.reshape(1, d), sc, sh, rw, rb)


def _sc_mesh():
    return plsc.VectorSubcoreMesh(core_axis_name="c", subcore_axis_name="s")


def _sc_worker():
    return lax.axis_index("c") * SC_SUBCORES + lax.axis_index("s")


def _dispatch_rows(h2, dest_c, n_rows):
    t, d = h2.shape
    per_worker = t // SC_WINDOW // SC_WORKERS

    def body(x_hbm, i_hbm, o_hbm, buf, idx):
        worker = _sc_worker()

        @pl.loop(0, per_worker)
        def _(j):
            ch = worker * per_worker + j
            pltpu.sync_copy(i_hbm.at[ch], idx)
            pltpu.sync_copy(x_hbm.at[pl.ds(ch * SC_WINDOW, SC_WINDOW)], buf)
            for k in range(TOP_K):
                pltpu.sync_copy(buf, o_hbm.at[idx.at[k]])

    return pl.kernel(
        body, out_type=jax.ShapeDtypeStruct((n_rows, d), h2.dtype), mesh=_sc_mesh(),
        scratch_types=[pltpu.VMEM((SC_WINDOW, d), h2.dtype), pltpu.VMEM((TOP_K, SC_WINDOW), jnp.int32)],
        name="dispatch_rows",
    )(h2, dest_c)


def _collect_rows(rows, dest_c, t):
    d = rows.shape[1]
    per_worker = t // SC_WINDOW // SC_WORKERS

    def body(r_hbm, i_hbm, o_hbm, buf, idx):
        worker = _sc_worker()

        @pl.loop(0, per_worker)
        def _(j):
            ch = worker * per_worker + j
            pltpu.sync_copy(i_hbm.at[ch], idx)
            for k in range(TOP_K):
                pltpu.sync_copy(r_hbm.at[idx.at[k]], buf)
                pltpu.sync_copy(buf, o_hbm.at[k, pl.ds(ch * SC_WINDOW, SC_WINDOW)])

    return pl.kernel(
        body, out_type=jax.ShapeDtypeStruct((TOP_K, t, d), rows.dtype), mesh=_sc_mesh(),
        scratch_types=[pltpu.VMEM((SC_WINDOW, d), rows.dtype), pltpu.VMEM((TOP_K, SC_WINDOW), jnp.int32)],
        name="collect_rows",
    )(rows, dest_c)


def _experts_kernel(layer, be_ref, nb_ref, slot_ref, next_ref, x_ref, w1_hbm, b1_ref, w2_hbm, b2_ref, o_ref,
                    w1f_ref, w2f_ref, w1b_ref, w2b_ref, sem):
    def weight_copies(expert, slot):
        return (pltpu.make_async_copy(w1_hbm.at[layer, expert], w1f_ref.at[slot], sem.at[slot, 0]),
                pltpu.make_async_copy(w2_hbm.at[layer, expert], w2f_ref.at[slot], sem.at[slot, 1]))

    def row_block(i, rs):
        e = be_ref[i]
        live = i < nb_ref[0]

        @pl.when(live & (i == 0))
        def _():
            for cp in weight_copies(e, 0):
                cp.start()

        @pl.when(live & ((i == 0) | (e != be_ref[jnp.maximum(i - 1, 0)])))
        def _():
            slot = slot_ref[i]
            for cp in weight_copies(e, slot):
                cp.wait()
            w1b_ref[...] = w1f_ref[slot].astype(BF16)
            w2b_ref[...] = w2f_ref[slot].astype(BF16)

            @pl.when(next_ref[i] >= 0)
            def _():
                for cp in weight_copies(next_ref[i], 1 - slot):
                    cp.start()

        @pl.when(live)
        def _():
            x = _unpack_halves(x_ref[rs, :]).astype(BF16)
            hdn = jnp.dot(x, w1b_ref[...], preferred_element_type=F32) + b1_ref[0, pl.ds(e, 1), :]
            glu = jnp.minimum(hdn[:, :D_EXPERT], SWIGLU_LIMIT)
            lin = jnp.clip(hdn[:, D_EXPERT:], -SWIGLU_LIMIT, SWIGLU_LIMIT)
            act = glu * jax.nn.sigmoid(SWIGLU_ALPHA * glu) * (lin + 1.0)
            out = jnp.dot(act.astype(BF16), w2b_ref[...], preferred_element_type=F32) + b2_ref[0, pl.ds(e, 1), :]
            o_ref[rs, :] = _pack_halves(out)

    for h in range(BLOCKS_PER_STEP):
        row_block(pl.program_id(0) * BLOCKS_PER_STEP + h, slice(h * ROW_BLOCK, (h + 1) * ROW_BLOCK))


def _experts(rows, block_exp, n_live, slot, next_exp, layer, w1, b1, w2, b2):
    n_rows, dp = rows.shape
    n_layers, n_e, d, f2 = w1.shape
    n_blocks = n_rows // ROW_BLOCK
    assert n_blocks % BLOCKS_PER_STEP == 0
    step_rows = BLOCKS_PER_STEP * ROW_BLOCK
    this_layer = lambda i, be, nb, sl, nx: (layer, 0, 0)
    grid_spec = pltpu.PrefetchScalarGridSpec(
        num_scalar_prefetch=4,
        grid=(n_blocks // BLOCKS_PER_STEP,),
        in_specs=[
            pl.BlockSpec((step_rows, dp), lambda i, be, nb, sl, nx: (i, 0)),
            pl.BlockSpec(memory_space=pl.ANY),
            pl.BlockSpec((1, n_e, f2), this_layer),
            pl.BlockSpec(memory_space=pl.ANY),
            pl.BlockSpec((1, n_e, d), this_layer),
        ],
        out_specs=pl.BlockSpec((step_rows, dp), lambda i, be, nb, sl, nx: (i, 0)),
        scratch_shapes=[pltpu.VMEM((2, d, f2), F32), pltpu.VMEM((2, f2 // 2, d), F32),
                        pltpu.VMEM((d, f2), BF16), pltpu.VMEM((f2 // 2, d), BF16),
                        pltpu.SemaphoreType.DMA((2, 2))],
    )
    return pl.pallas_call(
        functools.partial(_experts_kernel, layer),
        grid_spec=grid_spec,
        out_shape=jax.ShapeDtypeStruct((n_rows, dp), jnp.int32),
        compiler_params=_params(("arbitrary",)),
        name="experts",
    )(block_exp, n_live, slot, next_exp, rows, w1, b1, w2, b2)


def _moe_routed(h2, idx, rank, layer, w1, b1, w2, b2):
    t, d = h2.shape
    n_e = w1.shape[1]
    n_blocks = t * TOP_K // ROW_BLOCK + n_e
    onehot = (idx[:, :, None] == jnp.arange(n_e, dtype=jnp.int32)).astype(jnp.int32)
    counts = jnp.sum(onehot, axis=(0, 1))
    padded = (counts + ROW_BLOCK - 1) // ROW_BLOCK * ROW_BLOCK
    pad_end = jnp.cumsum(padded)
    dest = jnp.sum(onehot * (pad_end - padded), axis=-1) + rank
    block_row = jnp.arange(n_blocks, dtype=jnp.int32)[:, None] * ROW_BLOCK
    block_exp = jnp.minimum(jnp.sum((pad_end[None, :] <= block_row).astype(jnp.int32), axis=-1), n_e - 1)
    n_live = (pad_end[-1:] // ROW_BLOCK).astype(jnp.int32)
    experts = jnp.arange(n_e, dtype=jnp.int32)
    later = (experts[None, :] > experts[:, None]) & (counts[None, :] > 0)
    next_of = jnp.min(jnp.where(later, experts[None, :], n_e), axis=-1)
    next_of = jnp.where(next_of == n_e, -1, next_of)
    block_onehot = (block_exp[:, None] == experts[None, :]).astype(jnp.int32)
    next_exp = jnp.sum(block_onehot * next_of, axis=-1)
    run_of = jnp.cumsum((counts > 0).astype(jnp.int32)) - 1
    slot = jnp.sum(block_onehot * run_of, axis=-1) % 2
    dest_c = dest.reshape(TOP_K, t // SC_WINDOW, SC_WINDOW).transpose(1, 0, 2)
    rows = _dispatch_rows(h2, dest_c, n_blocks * ROW_BLOCK)
    out = _experts(rows, block_exp, n_live, slot, next_exp, layer, w1, b1, w2, b2)
    return _collect_rows(out, dest_c, t)


def _final_kernel(x_ref, rows_ref, gate_ref, g2_ref, fg_ref, o_ref):
    x = _moe_residual(x_ref, rows_ref, gate_ref, g2_ref)
    ms = jnp.mean(x * x, axis=-1, keepdims=True)
    o_ref[0] = x * lax.rsqrt(ms + NORM_EPS) * fg_ref[...]


def _final(x, res, final_g, tm=512):
    b, s, d = x.shape
    tm = min(tm, s)
    row = pl.BlockSpec((1, tm, d), lambda i, j: (i, j, 0))
    return pl.pallas_call(
        _final_kernel,
        grid=(b, s // tm),
        in_specs=[row] + _moe_residual_specs(tm, d) + [pl.BlockSpec((1, d), lambda i, j: (0, 0))],
        out_specs=row,
        out_shape=jax.ShapeDtypeStruct((b, s, d), F32),
        compiler_params=_params(("parallel", "parallel")),
        name="final",
    )(x, *res, final_g.reshape(1, d))


def _widen_w_in(w):
    gl0 = 2 * A_WIDTH + B_HEADS * DK + 6 * KV_GROUPS * DK
    per_group = HPG * 3
    gl_groups = [jnp.pad(w[:, gl0 + g * per_group:gl0 + (g + 1) * per_group], ((0, 0), (0, LANES - per_group)))
                 for g in range(KV_GROUPS)]
    wide = jnp.concatenate([w[:, :gl0]] + gl_groups, axis=1)
    assert gl0 == C_GL and wide.shape[1] == P_WIDE
    return wide.astype(BF16)


def _rope_tables(pos):
    half = DK // 2
    inv = ROPE_THETA ** (-jnp.arange(half, dtype=F32) / half)
    ang = pos.astype(F32)[..., None] * inv
    cos = jnp.cos(ang)
    sin = jnp.sin(ang)
    reps = LANES // DK
    return (jnp.concatenate([cos, cos] * reps, axis=-1),
            jnp.concatenate([-sin, sin] * reps, axis=-1))


def kernel(x, c, positions, ada_w, ada_b, norm1_g, norm2_g, w_in, w_out, sg_ln_g, sg_ln_b, sg_w, sg_b,
           cmp_pe_k, cmp_pe_v, cmp_w1_k, cmp_w2_k, cmp_w1_v, cmp_w2_v, router_w, router_b,
           exp_w1, exp_b1, exp_w2, exp_b2, final_g):
    b, s, d = x.shape
    n_layers = ada_w.shape[0]
    mod = _ada_mod(c, ada_w, ada_b)
    cos, sin = _rope_tables(positions)
    cmp_end = jnp.minimum(CMP_STRIDE * jnp.arange(s // CMP_STRIDE) + CMP_BLOCK - 1, s - 1)
    cos_c, sin_c = _rope_tables(positions[:, cmp_end])
    res = None
    for l in range(n_layers):
        sh1, sc1, g1, sh2, sc2, g2 = [m.reshape(b, 1, d) for m in jnp.split(mod[l], 6, axis=-1)]
        x, uv, qt, kc, vc, ks, vs_t, kw, vw_t, gl_t = _pre(x, res, norm1_g[l], sc1, sh1,
                                                          _widen_w_in(w_in[l]), cos, sin)
        y_a = _sgu(uv, sg_ln_g[l], sg_ln_b[l], sg_w[l], sg_b[l])
        kcmp, vcmp_t = _compress(kc, vc, cmp_pe_k[l], cmp_pe_v[l], cmp_w1_k[l], cmp_w2_k[l],
                                 cmp_w1_v[l], cmp_w2_v[l], cos_c, sin_c)
        oc_t, bias_t = _cmp_sel(qt, kcmp, vcmp_t)
        y_b = _attn(qt, ks, kw, vs_t, vw_t, bias_t, oc_t, gl_t)
        x, h2, idx, gate, rank = _post(y_a, y_b, w_out[l], x, g1, norm2_g[l], sc2, sh2,
                                       router_w[l], router_b[l])
        rows = _moe_routed(h2.reshape(b * s, d // 2), idx, rank, l, exp_w1, exp_b1, exp_w2, exp_b2)
        res = (rows.reshape(TOP_K, b, s, d // 2), gate.T.reshape(b, s, TOP_K), g2)
    return _final(x, res, final_g)
```

```python
import functools

import numpy as np
import jax
import jax.numpy as jnp
from jax import lax
from jax.experimental import pallas as pl
from jax.experimental.pallas import tpu as pltpu
from jax.experimental.pallas import tpu_sc as plsc

F32 = jnp.float32
BF16 = jnp.bfloat16
HIGHEST = lax.Precision.HIGHEST

D_MODEL = 1024
A_WIDTH = 512
A_HEADS = 8
CHUNK = 128
B_HEADS = 8
DK = 64
KV_GROUPS = 2
HPG = B_HEADS // KV_GROUPS
CMP_BLOCK = 32
CMP_STRIDE = 16
SEL_BLOCK = 64
SEL_TOP = 16
WINDOW = 512
ROPE_THETA = 10000.0
N_EXPERTS = 32
TOP_K = 4
D_EXPERT = 1024
SWIGLU_LIMIT = 7.0
SWIGLU_ALPHA = 1.702
NORM_EPS = 1e-6

LANES = 128
SUBLANES = 8
GW = HPG * DK
N_FORCED = 3
ONES_ROWS = 16
MASK_BIAS = -32768.0
Q_SCALE = DK ** -0.5 * 1.4426950408889634
VMEM_LIMIT = 56 * 1024 * 1024
ROW_BLOCK = 256
BLOCKS_PER_STEP = 4
SC_CORES = 2
SC_SUBCORES = 16
SC_WORKERS = SC_CORES * SC_SUBCORES
SC_WINDOW = 128
CMP_ROW_CHUNK = 128
SEL_ROW_CHUNK = 32

C_U, C_V, C_Q = 0, 512, 1024
C_KC, C_VC, C_KS, C_VS, C_KW, C_VW = 1536, 1664, 1792, 1920, 2048, 2176
C_GL = 2304
P_WIDE = 2560


def _params(sem):
    return pltpu.CompilerParams(dimension_semantics=sem, vmem_limit_bytes=VMEM_LIMIT)


def _split_bf16(x, parts):
    out = []
    for _ in range(parts):
        piece = x.astype(BF16)
        out.append(piece)
        x = x - piece.astype(F32)
    return out


def _split_bf16_bits(x, parts):
    out = []
    for _ in range(parts):
        piece = lax.bitcast_convert_type(lax.bitcast_convert_type(x, jnp.uint32) & jnp.uint32(0xFFFF0000), F32)
        out.append(piece.astype(BF16))
        x = x - piece
    return out


def _ada_kernel(c_ref, w_ref, b_ref, o_ref):
    c = c_ref[...]
    cond = c * jax.nn.sigmoid(c)
    o_ref[0] = jnp.dot(cond, w_ref[0], precision=HIGHEST, preferred_element_type=F32) + b_ref[0]


def _ada_mod(c, ada_w, ada_b):
    n_layers, d, d6 = ada_w.shape
    b = c.shape[0]
    rows = SUBLANES
    c_pad = jnp.zeros((rows, d), F32).at[:b].set(c)
    out = pl.pallas_call(
        _ada_kernel,
        grid=(n_layers, d6 // d),
        in_specs=[
            pl.BlockSpec((rows, d), lambda l, j: (0, 0)),
            pl.BlockSpec((1, d, d), lambda l, j: (l, 0, j)),
            pl.BlockSpec((1, 1, d), lambda l, j: (l, 0, j)),
        ],
        out_specs=pl.BlockSpec((1, rows, d), lambda l, j: (l, 0, j)),
        out_shape=jax.ShapeDtypeStruct((n_layers, rows, d6), F32),
        compiler_params=_params(("arbitrary", "arbitrary")),
        name="ada",
    )(c_pad, ada_w, ada_b.reshape(n_layers, 1, d6))
    return out[:, :b]


def _rope_slab(t, cos, sin_signed, lo):
    partner = jnp.where(lo, pltpu.roll(t, LANES - DK // 2, 1), pltpu.roll(t, DK // 2, 1))
    return t * cos + partner * sin_signed


def _rope(t, cos, sin_signed):
    lane = lax.broadcasted_iota(jnp.int32, (1, LANES), 1)
    lo = (lane % DK) < (DK // 2)
    slabs = [_rope_slab(t[:, s * LANES:(s + 1) * LANES], cos, sin_signed, lo)
             for s in range(t.shape[1] // LANES)]
    return slabs[0] if len(slabs) == 1 else jnp.concatenate(slabs, axis=1)


def _pack_halves(x):
    n = x.shape[1] // 2
    lo = lax.bitcast_convert_type(x[:, :n].astype(BF16).astype(F32), jnp.int32)
    hi = lax.bitcast_convert_type(x[:, n:].astype(BF16).astype(F32), jnp.int32)
    return lax.shift_right_logical(lo, jnp.int32(16)) | (hi & jnp.int32(-65536))


def _unpack_halves(p):
    lo = lax.bitcast_convert_type(lax.shift_left(p, jnp.int32(16)), F32)
    hi = lax.bitcast_convert_type(p & jnp.int32(-65536), F32)
    return jnp.concatenate([lo, hi], axis=1)


def _moe_residual(x_ref, rows_ref, gate_ref, g2_ref):
    gate = gate_ref[0]
    y = gate[:, 0:1] * _unpack_halves(rows_ref[0, 0])
    for k in range(1, TOP_K):
        y = y + gate[:, k:k + 1] * _unpack_halves(rows_ref[k, 0])
    return x_ref[0] + g2_ref[0] * y


def _moe_residual_specs(tm, d):
    return [pl.BlockSpec((TOP_K, 1, tm, d // 2), lambda i, j: (0, i, j, 0)),
            pl.BlockSpec((1, tm, TOP_K), lambda i, j: (i, j, 0)),
            pl.BlockSpec((1, 1, d), lambda i, j: (i, 0, 0))]


def _pre_kernel(has_res, *refs):
    if has_res:
        (x_ref, rows_ref, gate_ref, g2_ref, ng_ref, sc_ref, sh_ref, w_ref, cos_ref, sin_ref,
         xo_ref, uv_ref, qt_ref, kc_ref, vc_ref, ks_ref, vst_ref, kw_ref, vwt_ref, glt_ref) = refs
        x = _moe_residual(x_ref, rows_ref, gate_ref, g2_ref)
        xo_ref[0] = x
    else:
        (x_ref, ng_ref, sc_ref, sh_ref, w_ref, cos_ref, sin_ref,
         uv_ref, qt_ref, kc_ref, vc_ref, ks_ref, vst_ref, kw_ref, vwt_ref, glt_ref) = refs
        x = x_ref[0]
    ms = jnp.mean(x * x, axis=-1, keepdims=True)
    h = x * lax.rsqrt(ms + NORM_EPS) * ng_ref[...]
    h = h * (1.0 + sc_ref[0]) + sh_ref[0]
    proj = jnp.dot(h.astype(BF16), w_ref[...], preferred_element_type=F32)
    cos = cos_ref[0]
    sin = sin_ref[0]
    uv_ref[0] = proj[:, C_U:C_Q]
    qt_ref[0] = (_rope(proj[:, C_Q:C_KC], cos, sin) * Q_SCALE).T
    kc_ref[0] = proj[:, C_KC:C_VC]
    vc_ref[0] = proj[:, C_VC:C_KS]
    ks_ref[0] = _rope(proj[:, C_KS:C_VS], cos, sin).astype(BF16)
    vst_ref[0] = proj[:, C_VS:C_KW].T.astype(BF16)
    kw_ref[0] = _rope(proj[:, C_KW:C_VW], cos, sin).astype(BF16)
    vwt_ref[0] = proj[:, C_VW:C_GL].T.astype(BF16)
    glt_ref[0] = proj[:, C_GL:P_WIDE].T


def _pre(x, res, norm_g, sc, sh, w_wide, cos, sin, tm=512):
    b, s, d = x.shape
    tm = min(tm, s)
    row = lambda w: pl.BlockSpec((1, tm, w), lambda i, j: (i, j, 0))
    col = lambda w: pl.BlockSpec((1, w, tm), lambda i, j: (i, 0, j))
    vec = pl.BlockSpec((1, 1, d), lambda i, j: (i, 0, 0))
    in_specs = [row(d)]
    args = [x]
    if res is not None:
        in_specs += _moe_residual_specs(tm, d)
        args += list(res)
    in_specs += [pl.BlockSpec((1, d), lambda i, j: (0, 0)), vec, vec,
                 pl.BlockSpec((d, P_WIDE), lambda i, j: (0, 0)), row(LANES), row(LANES)]
    args += [norm_g.reshape(1, d), sc, sh, w_wide, cos, sin]
    kv = KV_GROUPS * DK
    outs = [(d, F32, False)] if res is not None else []
    outs += [(2 * A_WIDTH, F32, False), (B_HEADS * DK, F32, True), (kv, F32, False), (kv, F32, False),
             (kv, BF16, False), (kv, BF16, True), (kv, BF16, False), (kv, BF16, True),
             (KV_GROUPS * LANES, F32, True)]
    res_out = pl.pallas_call(
        functools.partial(_pre_kernel, res is not None),
        grid=(b, s // tm),
        in_specs=in_specs,
        out_specs=[col(w) if t else row(w) for w, _, t in outs],
        out_shape=[jax.ShapeDtypeStruct((b, w, s) if t else (b, s, w), dt) for w, dt, t in outs],
        compiler_params=_params(("parallel", "parallel")),
        name="pre",
    )(*args)
    if res is None:
        res_out = [x] + list(res_out)
    return res_out


def _sgu_kernel(uv_ref, lng_ref, lnb_ref, w_ref, bias_ref, o_ref):
    rows = uv_ref.shape[1]
    uv = uv_ref[0]
    gu = jax.nn.gelu(uv[:, :A_WIDTH])
    gv = jax.nn.gelu(uv[:, A_WIDTH:])
    mu = jnp.mean(gv, axis=-1, keepdims=True)
    var = jnp.mean(jnp.square(gv - mu), axis=-1, keepdims=True)
    vn = ((gv - mu) * lax.rsqrt(var + NORM_EPS) * lng_ref[...] + lnb_ref[...]).astype(BF16)
    r = lax.broadcasted_iota(jnp.int32, (CHUNK, CHUNK), 0)
    c = lax.broadcasted_iota(jnp.int32, (CHUNK, CHUNK), 1)
    causal = c <= r
    lane_lo = lax.broadcasted_iota(jnp.int32, (CHUNK, LANES), 1) < DK
    for p in range(A_HEADS // 2):
        w0 = jnp.where(causal, w_ref[2 * p], 0.0).astype(BF16)
        w1 = jnp.where(causal, w_ref[2 * p + 1], 0.0).astype(BF16)
        bias = bias_ref[:, p * LANES:(p + 1) * LANES]
        for ch in range(rows // CHUNK):
            rs = slice(ch * CHUNK, (ch + 1) * CHUNK)
            cs = slice(p * LANES, (p + 1) * LANES)
            vp = vn[rs, cs]
            m0 = jnp.dot(w0, vp, preferred_element_type=F32)
            m1 = jnp.dot(w1, vp, preferred_element_type=F32)
            mixed = jnp.where(lane_lo, m0, m1) + bias
            o_ref[0, rs, cs] = (gu[rs, cs] * mixed).astype(o_ref.dtype)


def _sgu(uv, ln_g, ln_b, w_s, b_s, tm=512):
    b, s, _ = uv.shape
    tm = min(tm, s)
    bias = jnp.repeat(b_s.T, A_WIDTH // A_HEADS, axis=1)
    return pl.pallas_call(
        _sgu_kernel,
        grid=(b, s // tm),
        in_specs=[
            pl.BlockSpec((1, tm, 2 * A_WIDTH), lambda i, j: (i, j, 0)),
            pl.BlockSpec((1, A_WIDTH), lambda i, j: (0, 0)),
            pl.BlockSpec((1, A_WIDTH), lambda i, j: (0, 0)),
            pl.BlockSpec((A_HEADS, CHUNK, CHUNK), lambda i, j: (0, 0, 0)),
            pl.BlockSpec((CHUNK, A_WIDTH), lambda i, j: (0, 0)),
        ],
        out_specs=pl.BlockSpec((1, tm, A_WIDTH), lambda i, j: (i, j, 0)),
        out_shape=jax.ShapeDtypeStruct((b, s, A_WIDTH), BF16),
        compiler_params=_params(("parallel", "parallel")),
        name="sgu",
    )(uv, ln_g.reshape(1, -1), ln_b.reshape(1, -1), w_s, bias)


def _compress_kernel(kr_ref, vr_ref, pek_ref, pev_ref, w1k_ref, w1v_ref, w2k_ref, w2v_ref,
                     cos_ref, sin_ref, ko_ref, vt_ref):
    nc = kr_ref.shape[1]

    def mlp(r, pe_ref, w1_ref, w2_ref):
        top = jnp.dot(r + pe_ref[0:1], w1_ref[0], precision=HIGHEST, preferred_element_type=F32)
        bot = jnp.dot(r + pe_ref[1:2], w1_ref[1], precision=HIGHEST, preferred_element_type=F32)
        pre = top + pltpu.roll(bot, nc - 1, 0)
        return jnp.dot(jax.nn.gelu(pre), w2_ref[...], precision=HIGHEST, preferred_element_type=F32)

    kc = _rope(mlp(kr_ref[0], pek_ref, w1k_ref, w2k_ref), cos_ref[0], sin_ref[0])
    vc = mlp(vr_ref[0], pev_ref, w1v_ref, w2v_ref)
    lo = lax.broadcasted_iota(jnp.int32, (nc, LANES), 1) < DK
    rolled = pltpu.roll(kc, DK, 1)
    for g in range(KV_GROUPS):
        dup = jnp.where(lo, kc, rolled) if g == 0 else jnp.where(lo, rolled, kc)
        hi, low = _split_bf16(dup, 2)
        ko_ref[0, g, :, 0:LANES] = hi
        ko_ref[0, g, :, LANES:2 * LANES] = jnp.where(lo, low, jnp.zeros_like(low))
    vt_ref[0] = vc.T.astype(BF16)


def _compress_weights(pe, w1, w2):
    half = CMP_BLOCK // 2
    eye = jnp.eye(KV_GROUPS, dtype=F32)
    w1r = w1.reshape(CMP_BLOCK, DK, DK)
    wfull = jnp.einsum('lde,gh->lgdhe', w1r, eye)
    w1s = wfull.reshape(2, half * KV_GROUPS * DK, KV_GROUPS * DK)
    pes = jnp.broadcast_to(pe.reshape(2, half, 1, DK), (2, half, KV_GROUPS, DK)).reshape(2, -1)
    w2bd = jnp.einsum('de,gh->gdhe', w2, eye).reshape(KV_GROUPS * DK, KV_GROUPS * DK)
    return pes, w1s, w2bd


def _compress(kc, vc, pe_k, pe_v, w1_k, w2_k, w1_v, w2_v, cos_c, sin_c):
    b, s, _ = kc.shape
    nc = s // CMP_STRIDE
    rw = CMP_STRIDE * LANES
    pek, w1ks, w2kb = _compress_weights(pe_k, w1_k, w2_k)
    pev, w1vs, w2vb = _compress_weights(pe_v, w1_v, w2_v)
    full = lambda shape: pl.BlockSpec(shape, lambda i: (0,) * len(shape))
    return pl.pallas_call(
        _compress_kernel,
        grid=(b,),
        in_specs=[
            pl.BlockSpec((1, nc, rw), lambda i: (i, 0, 0)),
            pl.BlockSpec((1, nc, rw), lambda i: (i, 0, 0)),
            full((2, rw)), full((2, rw)),
            full((2, rw, LANES)), full((2, rw, LANES)),
            full((LANES, LANES)), full((LANES, LANES)),
            pl.BlockSpec((1, nc, LANES), lambda i: (i, 0, 0)),
            pl.BlockSpec((1, nc, LANES), lambda i: (i, 0, 0)),
        ],
        out_specs=[pl.BlockSpec((1, KV_GROUPS, nc, 2 * LANES), lambda i: (i, 0, 0, 0)),
                   pl.BlockSpec((1, KV_GROUPS * DK, nc), lambda i: (i, 0, 0))],
        out_shape=[jax.ShapeDtypeStruct((b, KV_GROUPS, nc, 2 * LANES), BF16),
                   jax.ShapeDtypeStruct((b, KV_GROUPS * DK, nc), BF16)],
        compiler_params=_params(("parallel",)),
        name="compress",
    )(kc.reshape(b, nc, rw), vc.reshape(b, nc, rw), pek, pev, w1ks, w1vs, w2kb, w2vb, cos_c, sin_c)


def _heads_on_lanes(qt):
    return jnp.concatenate([qt[h * DK:(h + 1) * DK] for h in range(HPG)], axis=1)


def _cmp_sel_kernel(n_top, qt_ref, k_ref, vt_ref, map_ref, oc_ref, bias_ref, imp_ref):
    tq = qt_ref.shape[2]
    cols = HPG * tq
    nc = k_ref.shape[2]
    nsel = map_ref.shape[0]
    q0 = pl.program_id(2) * tq
    q_hi, q_lo = _split_bf16(_heads_on_lanes(qt_ref[0]), 2)
    q3t = jnp.concatenate([q_hi, q_lo, q_hi, jnp.zeros_like(q_hi)], axis=0)
    t_col = q0 + lax.broadcasted_iota(jnp.int32, (1, cols), 1) % tq

    def compressed_branch(rows):
        s = jnp.dot(k_ref[0, 0, 0:rows], q3t, preferred_element_type=F32)
        n_idx = lax.broadcasted_iota(jnp.int32, (rows, 1), 0)
        s = jnp.where((CMP_STRIDE * n_idx + CMP_BLOCK - 1) <= t_col, s, -jnp.inf)
        m = jnp.max(s, axis=0, keepdims=True)
        m = jnp.where(m == -jnp.inf, 0.0, m)
        e = jnp.exp2(s - m)
        d = jnp.sum(e, axis=0, keepdims=True)
        p = e / jnp.where(d > 0, d, 1.0)
        oc = jnp.dot(vt_ref[0, :, 0:rows], p.astype(BF16), preferred_element_type=F32)
        oc_ref[0] = jnp.concatenate([oc[:, h * tq:(h + 1) * tq] for h in range(HPG)], axis=0)
        psum = p[:, 0:tq]
        for h in range(1, HPG):
            psum = psum + p[:, h * tq:(h + 1) * tq]
        imp_ref[...] = jnp.dot(jnp.concatenate([map_ref[:, 0:rows]] * 3, axis=1),
                               jnp.concatenate(_split_bf16(psum, 3), axis=0), preferred_element_type=F32)

    visible = (q0 + tq - CMP_BLOCK) // CMP_STRIDE + 1
    n_chunks = -(-nc // CMP_ROW_CHUNK)
    for r in range(1, n_chunks + 1):
        lo = (r - 1) * CMP_ROW_CHUNK if r > 1 else -nc
        pl.when((visible > lo) & (visible <= r * CMP_ROW_CHUNK) if r < n_chunks else visible > lo)(
            functools.partial(compressed_branch, min(r * CMP_ROW_CHUNK, nc)))

    def select(rows):
        imp = imp_ref[0:rows]
        j = lax.broadcasted_iota(jnp.int32, (rows, tq), 0)
        cur = (q0 + lax.broadcasted_iota(jnp.int32, (rows, tq), 1)) // SEL_BLOCK
        valid = j <= cur
        forced = (j == 0) | (j == cur) | (j == cur - 1)
        keep = forced | (valid & (cur < n_top))
        vals = jnp.where(valid & jnp.logical_not(forced), imp, -jnp.inf)
        bias = jnp.where(keep, 0.0, MASK_BIAS)
        for _ in range(n_top - N_FORCED):
            mx = jnp.max(vals, axis=0, keepdims=True)
            first = jnp.min(jnp.where(vals == mx, j, rows), axis=0, keepdims=True)
            pick = (j == first) & (mx > -jnp.inf)
            bias = jnp.where(pick, 0.0, bias)
            vals = jnp.where(pick, -jnp.inf, vals)
        bias_ref[0, 0, 0:rows] = bias.astype(BF16)
        if rows < nsel:
            bias_ref[0, 0, rows:nsel] = jnp.full((nsel - rows, tq), MASK_BIAS, BF16)

    n_valid = (q0 + tq - 1) // SEL_BLOCK + 1
    n_sel_chunks = -(-nsel // SEL_ROW_CHUNK)
    for r in range(1, n_sel_chunks + 1):
        lo = (r - 1) * SEL_ROW_CHUNK if r > 1 else -nsel
        pl.when((n_valid > lo) & (n_valid <= r * SEL_ROW_CHUNK) if r < n_sel_chunks else n_valid > lo)(
            functools.partial(select, min(r * SEL_ROW_CHUNK, nsel)))


def _sel_map_t(s):
    nc = s // CMP_STRIDE
    n_cmp = (s - CMP_BLOCK) // CMP_STRIDE + 1
    n_sel = s // SEL_BLOCK
    cs = CMP_STRIDE * np.arange(n_cmp)[:, None]
    ce = cs + CMP_BLOCK
    ss = SEL_BLOCK * np.arange(n_sel)[None, :]
    se = ss + SEL_BLOCK
    ov = np.clip(np.minimum(ce, se) - np.maximum(cs, ss), 0, None) / CMP_STRIDE
    out = np.zeros((n_sel, nc), np.float32)
    out[:, :n_cmp] = ov.T
    return out


def _cmp_sel(qt, kcmp, vcmp_t, tq=256):
    b, _, s = qt.shape
    nc = s // CMP_STRIDE
    nsel = s // SEL_BLOCK
    n_top = min(SEL_TOP, nsel)
    sel_map = jnp.asarray(_sel_map_t(s), dtype=BF16)
    return pl.pallas_call(
        functools.partial(_cmp_sel_kernel, n_top),
        grid=(b, KV_GROUPS, s // tq),
        in_specs=[
            pl.BlockSpec((1, GW, tq), lambda i, g, j: (i, g, j)),
            pl.BlockSpec((1, 1, nc, 2 * LANES), lambda i, g, j: (i, g, 0, 0)),
            pl.BlockSpec((1, DK, nc), lambda i, g, j: (i, g, 0)),
            pl.BlockSpec((nsel, nc), lambda i, g, j: (0, 0)),
        ],
        out_specs=[
            pl.BlockSpec((1, GW, tq), lambda i, g, j: (i, g, j)),
            pl.BlockSpec((1, 1, nsel, tq), lambda i, g, j: (i, g, 0, j)),
        ],
        out_shape=[
            jax.ShapeDtypeStruct((b, B_HEADS * DK, s), F32),
            jax.ShapeDtypeStruct((b, KV_GROUPS, nsel, s), BF16),
        ],
        scratch_shapes=[pltpu.VMEM((nsel, tq), F32)],
        compiler_params=_params(("parallel", "parallel", "parallel")),
        name="cmp_sel",
    )(qt, kcmp, vcmp_t, sel_map)


def _attn_kernel(kt, qt_ref, ks_ref, kw_ref, vst_ref, vwt_ref, bias_ref, oh_ref, oc_ref, glt_ref,
                 o_ref, qa_ref, s0_ref, s1_ref, m_ref, acc_ref, ow_ref):
    tq = qt_ref.shape[2]
    cols = HPG * tq
    nsel = bias_ref.shape[2]
    g = pl.program_id(1)
    q0 = pl.program_id(2) * tq

    q4t = _heads_on_lanes(qt_ref[0]).astype(BF16)
    zero = jnp.zeros_like(q4t)
    qa_ref[0:DK] = jnp.where(g == 0, q4t, zero)
    qa_ref[DK:2 * DK] = jnp.where(g == 1, q4t, zero)
    qa_ref[LANES:LANES + nsel] = jnp.concatenate([bias_ref[0, 0]] * HPG, axis=1)
    qaug = qa_ref[...]
    t_col = q0 + lax.broadcasted_iota(jnp.int32, (1, cols), 1) % tq

    def fold(x, op):
        return op(x.reshape(x.shape[0] // SUBLANES, SUBLANES, cols), axis=0)

    def with_ones(vt):
        return jnp.concatenate([vt, jnp.ones((ONES_ROWS, vt.shape[1]), BF16)], axis=0)

    def stage(c, s_ref):
        k0 = pl.multiple_of(c * kt, kt)
        kaug = jnp.concatenate([ks_ref[0, pl.ds(k0, kt), :], oh_ref[pl.ds(k0, kt), :]], axis=1)
        s_ref[...] = jnp.dot(kaug, qaug, preferred_element_type=F32)
        r0 = pl.multiple_of(jnp.clip(q0 - k0, 0, kt - tq), tq)
        kpos = k0 + r0 + lax.broadcasted_iota(jnp.int32, (tq, 1), 0)
        s_ref[pl.ds(r0, tq), :] = jnp.where(kpos <= t_col, s_ref[pl.ds(r0, tq), :], -jnp.inf)

    def consume(c, s_ref):
        k0 = pl.multiple_of(c * kt, kt)
        sc = s_ref[...]
        m_old = m_ref[...]
        m_new = jnp.maximum(m_old, jnp.max(fold(sc, jnp.max), axis=0, keepdims=True))
        alpha = jnp.exp2(m_old - m_new)
        pr = jnp.exp2(sc - m_new)
        acc_ref[...] = alpha * acc_ref[...] + jnp.dot(with_ones(vst_ref[0, :, pl.ds(k0, kt)]), pr.astype(BF16),
                                                      preferred_element_type=F32)
        m_ref[...] = m_new

    m_ref[...] = jnp.full(m_ref.shape, -jnp.inf, F32)
    acc_ref[...] = jnp.zeros(acc_ref.shape, F32)
    n_full = q0 // kt
    n_pairs = n_full // 2
    odd = n_full % 2 == 1
    stage(0, s0_ref)

    wlen = WINDOW + tq
    w0 = pl.multiple_of(jnp.maximum(q0 - WINDOW, 0), tq)
    sw = jnp.dot(kw_ref[0, pl.ds(w0, wlen), :], qaug[0:LANES], preferred_element_type=F32)
    dlt = t_col - (w0 + lax.broadcasted_iota(jnp.int32, (wlen, 1), 0))
    sw = jnp.where(lax.bitcast_convert_type(dlt, jnp.uint32) < jnp.uint32(WINDOW), sw, -jnp.inf)
    pw = jnp.exp2(sw - jnp.max(sw, axis=0, keepdims=True))
    o_win = jnp.dot(with_ones(vwt_ref[0, :, pl.ds(w0, wlen)]), pw.astype(BF16), preferred_element_type=F32)
    ow_ref[...] = o_win[0:DK] / o_win[DK:DK + 1]

    def pair(i, carry):
        stage(2 * i + 1, s1_ref)
        consume(2 * i, s0_ref)
        stage(2 * i + 2, s0_ref)
        consume(2 * i + 1, s1_ref)
        return carry

    lax.fori_loop(0, n_pairs, pair, 0)

    @pl.when(odd)
    def _():
        stage(n_full, s1_ref)
        consume(n_full - 1, s0_ref)
        consume(n_full, s1_ref)

    @pl.when(jnp.logical_not(odd))
    def _():
        consume(n_full, s0_ref)

    o_sel = acc_ref[0:DK] / acc_ref[DK:DK + 1]

    o_win = ow_ref[...]
    gates = jax.nn.sigmoid(glt_ref[0])
    oc = _heads_on_lanes(oc_ref[0])
    for p in range(HPG // 2):
        halves = []
        for hh in (2 * p, 2 * p + 1):
            cs = slice(hh * tq, (hh + 1) * tq)
            halves.append(gates[3 * hh:3 * hh + 1] * oc[:, cs] + gates[3 * hh + 1:3 * hh + 2] * o_sel[:, cs]
                          + gates[3 * hh + 2:3 * hh + 3] * o_win[:, cs])
        o_ref[0, :, p * LANES:(p + 1) * LANES] = jnp.concatenate(halves, axis=0).T.astype(o_ref.dtype)


def _attn(qt, ks, kw, vs_t, vw_t, bias_t, oc_t, gl_t, tq=256, kt=512):
    b, _, s = qt.shape
    nsel = s // SEL_BLOCK
    kt = min(kt, s)
    onehot = jnp.asarray((np.arange(s)[:, None] // SEL_BLOCK == np.arange(nsel)[None, :]), dtype=BF16)
    k_spec = pl.BlockSpec((1, s, KV_GROUPS * DK), lambda i, g, j: (i, 0, 0))
    vt_spec = pl.BlockSpec((1, DK, s), lambda i, g, j: (i, g, 0))
    cols = HPG * tq
    return pl.pallas_call(
        functools.partial(_attn_kernel, kt),
        grid=(b, KV_GROUPS, s // tq),
        in_specs=[
            pl.BlockSpec((1, GW, tq), lambda i, g, j: (i, g, j)),
            k_spec, k_spec, vt_spec, vt_spec,
            pl.BlockSpec((1, 1, nsel, tq), lambda i, g, j: (i, g, 0, j)),
            pl.BlockSpec((s, nsel), lambda i, g, j: (0, 0)),
            pl.BlockSpec((1, GW, tq), lambda i, g, j: (i, g, j)),
            pl.BlockSpec((1, LANES, tq), lambda i, g, j: (i, g, j)),
        ],
        out_specs=pl.BlockSpec((1, tq, GW), lambda i, g, j: (i, j, g)),
        out_shape=jax.ShapeDtypeStruct((b, s, B_HEADS * DK), BF16),
        scratch_shapes=[pltpu.VMEM((LANES + nsel, cols), BF16),
                        pltpu.VMEM((kt, cols), F32), pltpu.VMEM((kt, cols), F32),
                        pltpu.VMEM((1, cols), F32), pltpu.VMEM((DK + ONES_ROWS, cols), F32),
                        pltpu.VMEM((DK, cols), F32)],
        compiler_params=_params(("parallel", "parallel", "arbitrary")),
        name="attn",
    )(qt, ks, kw, vs_t, vw_t, bias_t, onehot, oc_t, gl_t)


def _post_kernel(ya_ref, yb_ref, wo_ref, x_ref, g1_ref, ng_ref, sc_ref, sh_ref, rw_ref, rb_ref,
                 xo_ref, h_ref, idx_ref, gate_ref, rank_ref, cnt_ref):
    tm = x_ref.shape[1]

    @pl.when((pl.program_id(0) == 0) & (pl.program_id(1) == 0))
    def _():
        cnt_ref[...] = jnp.zeros(cnt_ref.shape, F32)

    mixed = jnp.dot(ya_ref[0], wo_ref[:A_WIDTH], preferred_element_type=F32)
    mixed = mixed + jnp.dot(yb_ref[0], wo_ref[A_WIDTH:], preferred_element_type=F32)
    x = x_ref[0] + g1_ref[0] * mixed
    xo_ref[0] = x
    ms = jnp.mean(x * x, axis=-1, keepdims=True)
    h = x * lax.rsqrt(ms + NORM_EPS) * ng_ref[...]
    h = h * (1.0 + sc_ref[0]) + sh_ref[0]
    h_ref[0] = _pack_halves(h)
    h_hi, h_mid = _split_bf16(h, 2)
    logits = jnp.dot(jnp.concatenate([h_hi, h_hi, h_mid], axis=1), rw_ref[...],
                     preferred_element_type=F32) + rb_ref[...]
    vals = logits.T[:N_EXPERTS]
    e_idx = lax.broadcasted_iota(jnp.int32, (N_EXPERTS, tm), 0)
    top = jnp.max(vals, axis=0, keepdims=True)
    routed_t = jnp.zeros_like(vals)
    firsts, exps = [], []
    for k in range(TOP_K):
        m = jnp.max(vals, axis=0, keepdims=True)
        first = jnp.min(jnp.where(vals == m, e_idx, N_EXPERTS), axis=0, keepdims=True)
        pick = e_idx == first
        routed_t = jnp.where(pick, 1.0, routed_t)
        firsts.append(first)
        exps.append(jnp.exp(m - top))
        vals = jnp.where(pick, -jnp.inf, vals)
    idx_i = jnp.concatenate(firsts, axis=0)
    idx_t = idx_i.astype(F32)
    slot_exp = jnp.concatenate(exps, axis=0)
    idx_ref[...] = idx_i
    gate_ref[...] = slot_exp / jnp.sum(slot_exp, axis=0, keepdims=True)
    r = lax.broadcasted_iota(jnp.int32, (tm, tm), 0)
    c = lax.broadcasted_iota(jnp.int32, (tm, tm), 1)
    before = jnp.where(r < c, 1.0, 0.0).astype(BF16)
    rank_t = cnt_ref[...] + jnp.dot(routed_t.astype(BF16), before, preferred_element_type=F32)
    cnt_ref[...] += jnp.sum(routed_t, axis=-1, keepdims=True)
    e_row = lax.broadcasted_iota(jnp.int32, (N_EXPERTS, tm), 0).astype(F32)
    ranks = [jnp.sum(jnp.where(e_row == idx_t[k:k + 1], rank_t, 0.0), axis=0, keepdims=True)
             for k in range(TOP_K)]
    rank_ref[...] = jnp.concatenate(ranks, axis=0).astype(jnp.int32)


def _post(y_a, y_b, w_out, x, g1, norm_g, sc, sh, router_w, router_b, tm=512):
    b, s, d = x.shape
    tm = min(tm, s)
    nt = s // tm
    w_hi, w_mid = _split_bf16_bits(jnp.zeros((d, LANES), F32).at[:, :N_EXPERTS].set(router_w), 2)
    rw = jnp.concatenate([w_hi, w_mid, w_hi], axis=0)
    rb = jnp.full((1, LANES), -1e30, F32).at[0, :N_EXPERTS].set(router_b)
    vec = pl.BlockSpec((1, 1, d), lambda i, j: (i, 0, 0))
    row = lambda w: pl.BlockSpec((1, tm, w), lambda i, j: (i, j, 0))
    return pl.pallas_call(
        _post_kernel,
        grid=(b, nt),
        in_specs=[
            row(A_WIDTH), row(B_HEADS * DK),
            pl.BlockSpec((d, d), lambda i, j: (0, 0)),
            row(d), vec,
            pl.BlockSpec((1, d), lambda i, j: (0, 0)), vec, vec,
            pl.BlockSpec((3 * d, LANES), lambda i, j: (0, 0)),
            pl.BlockSpec((1, LANES), lambda i, j: (0, 0)),
        ],
        out_specs=[row(d), row(d // 2)] + [pl.BlockSpec((TOP_K, tm), lambda i, j: (0, i * nt + j))] * 3,
        out_shape=[
            jax.ShapeDtypeStruct((b, s, d), F32),
            jax.ShapeDtypeStruct((b, s, d // 2), jnp.int32),
            jax.ShapeDtypeStruct((TOP_K, b * s), jnp.int32),
            jax.ShapeDtypeStruct((TOP_K, b * s), F32),
            jax.ShapeDtypeStruct((TOP_K, b * s), jnp.int32),
        ],
        scratch_shapes=[pltpu.VMEM((N_EXPERTS, 1), F32)],
        compiler_params=_params(("arbitrary", "arbitrary")),
        name="post",
    )(y_a, y_b, w_out.astype(BF16), x, g1, norm_g.reshape(1, d), sc, sh, rw, rb)


def _sc_mesh():
    return plsc.VectorSubcoreMesh(core_axis_name="c", subcore_axis_name="s")


def _sc_worker():
    return lax.axis_index("c") * SC_SUBCORES + lax.axis_index("s")


def _dispatch_rows(h2, dest_c, n_rows):
    t, d = h2.shape
    per_worker = t // SC_WINDOW // SC_WORKERS

    def body(x_hbm, i_hbm, o_hbm, buf, idx):
        worker = _sc_worker()

        @pl.loop(0, per_worker)
        def _(j):
            ch = worker * per_worker + j
            pltpu.sync_copy(i_hbm.at[ch], idx)
            pltpu.sync_copy(x_hbm.at[pl.ds(ch * SC_WINDOW, SC_WINDOW)], buf)
            for k in range(TOP_K):
                pltpu.sync_copy(buf, o_hbm.at[idx.at[k]])

    return pl.kernel(
        body, out_type=jax.ShapeDtypeStruct((n_rows, d), h2.dtype), mesh=_sc_mesh(),
        scratch_types=[pltpu.VMEM((SC_WINDOW, d), h2.dtype), pltpu.VMEM((TOP_K, SC_WINDOW), jnp.int32)],
        name="dispatch_rows",
    )(h2, dest_c)


def _collect_rows(rows, dest_c, t):
    d = rows.shape[1]
    per_worker = t // SC_WINDOW // SC_WORKERS

    def body(r_hbm, i_hbm, o_hbm, buf, idx):
        worker = _sc_worker()

        @pl.loop(0, per_worker)
        def _(j):
            ch = worker * per_worker + j
            pltpu.sync_copy(i_hbm.at[ch], idx)
            for k in range(TOP_K):
                pltpu.sync_copy(r_hbm.at[idx.at[k]], buf)
                pltpu.sync_copy(buf, o_hbm.at[k, pl.ds(ch * SC_WINDOW, SC_WINDOW)])

    return pl.kernel(
        body, out_type=jax.ShapeDtypeStruct((TOP_K, t, d), rows.dtype), mesh=_sc_mesh(),
        scratch_types=[pltpu.VMEM((SC_WINDOW, d), rows.dtype), pltpu.VMEM((TOP_K, SC_WINDOW), jnp.int32)],
        name="collect_rows",
    )(rows, dest_c)


def _experts_kernel(layer, be_ref, nb_ref, slot_ref, next_ref, x_ref, w1_hbm, b1_ref, w2_hbm, b2_ref, o_ref,
                    w1f_ref, w2f_ref, w1b_ref, w2b_ref, sem):
    def weight_copies(expert, slot):
        return (pltpu.make_async_copy(w1_hbm.at[layer, expert], w1f_ref.at[slot], sem.at[slot, 0]),
                pltpu.make_async_copy(w2_hbm.at[layer, expert], w2f_ref.at[slot], sem.at[slot, 1]))

    def row_block(i, rs):
        e = be_ref[i]
        live = i < nb_ref[0]

        @pl.when(live & (i == 0))
        def _():
            for cp in weight_copies(e, 0):
                cp.start()

        @pl.when(live & ((i == 0) | (e != be_ref[jnp.maximum(i - 1, 0)])))
        def _():
            slot = slot_ref[i]
            for cp in weight_copies(e, slot):
                cp.wait()
            w1b_ref[...] = w1f_ref[slot].astype(BF16)
            w2b_ref[...] = w2f_ref[slot].astype(BF16)

            @pl.when(next_ref[i] >= 0)
            def _():
                for cp in weight_copies(next_ref[i], 1 - slot):
                    cp.start()

        @pl.when(live)
        def _():
            x = _unpack_halves(x_ref[rs, :]).astype(BF16)
            hdn = jnp.dot(x, w1b_ref[...], preferred_element_type=F32) + b1_ref[0, pl.ds(e, 1), :]
            glu = jnp.minimum(hdn[:, :D_EXPERT], SWIGLU_LIMIT)
            lin = jnp.clip(hdn[:, D_EXPERT:], -SWIGLU_LIMIT, SWIGLU_LIMIT)
            act = glu * jax.nn.sigmoid(SWIGLU_ALPHA * glu) * (lin + 1.0)
            out = jnp.dot(act.astype(BF16), w2b_ref[...], preferred_element_type=F32) + b2_ref[0, pl.ds(e, 1), :]
            o_ref[rs, :] = _pack_halves(out)

    for h in range(BLOCKS_PER_STEP):
        row_block(pl.program_id(0) * BLOCKS_PER_STEP + h, slice(h * ROW_BLOCK, (h + 1) * ROW_BLOCK))


def _experts(rows, block_exp, n_live, slot, next_exp, layer, w1, b1, w2, b2):
    n_rows, dp = rows.shape
    n_layers, n_e, d, f2 = w1.shape
    n_blocks = n_rows // ROW_BLOCK
    assert n_blocks % BLOCKS_PER_STEP == 0
    step_rows = BLOCKS_PER_STEP * ROW_BLOCK
    this_layer = lambda i, be, nb, sl, nx: (layer, 0, 0)
    grid_spec = pltpu.PrefetchScalarGridSpec(
        num_scalar_prefetch=4,
        grid=(n_blocks // BLOCKS_PER_STEP,),
        in_specs=[
            pl.BlockSpec((step_rows, dp), lambda i, be, nb, sl, nx: (i, 0)),
            pl.BlockSpec(memory_space=pl.ANY),
            pl.BlockSpec((1, n_e, f2), this_layer),
            pl.BlockSpec(memory_space=pl.ANY),
            pl.BlockSpec((1, n_e, d), this_layer),
        ],
        out_specs=pl.BlockSpec((step_rows, dp), lambda i, be, nb, sl, nx: (i, 0)),
        scratch_shapes=[pltpu.VMEM((2, d, f2), F32), pltpu.VMEM((2, f2 // 2, d), F32),
                        pltpu.VMEM((d, f2), BF16), pltpu.VMEM((f2 // 2, d), BF16),
                        pltpu.SemaphoreType.DMA((2, 2))],
    )
    return pl.pallas_call(
        functools.partial(_experts_kernel, layer),
        grid_spec=grid_spec,
        out_shape=jax.ShapeDtypeStruct((n_rows, dp), jnp.int32),
        compiler_params=_params(("arbitrary",)),
        name="experts",
    )(block_exp, n_live, slot, next_exp, rows, w1, b1, w2, b2)


def _moe_routed(h2, idx, rank, layer, w1, b1, w2, b2):
    t, d = h2.shape
    n_e = w1.shape[1]
    n_blocks = t * TOP_K // ROW_BLOCK + n_e
    onehot = (idx[:, :, None] == jnp.arange(n_e, dtype=jnp.int32)).astype(jnp.int32)
    counts = jnp.sum(onehot, axis=(0, 1))
    padded = (counts + ROW_BLOCK - 1) // ROW_BLOCK * ROW_BLOCK
    pad_end = jnp.cumsum(padded)
    dest = jnp.sum(onehot * (pad_end - padded), axis=-1) + rank
    block_row = jnp.arange(n_blocks, dtype=jnp.int32)[:, None] * ROW_BLOCK
    block_exp = jnp.minimum(jnp.sum((pad_end[None, :] <= block_row).astype(jnp.int32), axis=-1), n_e - 1)
    n_live = (pad_end[-1:] // ROW_BLOCK).astype(jnp.int32)
    experts = jnp.arange(n_e, dtype=jnp.int32)
    later = (experts[None, :] > experts[:, None]) & (counts[None, :] > 0)
    next_of = jnp.min(jnp.where(later, experts[None, :], n_e), axis=-1)
    next_of = jnp.where(next_of == n_e, -1, next_of)
    block_onehot = (block_exp[:, None] == experts[None, :]).astype(jnp.int32)
    next_exp = jnp.sum(block_onehot * next_of, axis=-1)
    run_of = jnp.cumsum((counts > 0).astype(jnp.int32)) - 1
    slot = jnp.sum(block_onehot * run_of, axis=-1) % 2
    dest_c = dest.reshape(TOP_K, t // SC_WINDOW, SC_WINDOW).transpose(1, 0, 2)
    rows = _dispatch_rows(h2, dest_c, n_blocks * ROW_BLOCK)
    out = _experts(rows, block_exp, n_live, slot, next_exp, layer, w1, b1, w2, b2)
    return _collect_rows(out, dest_c, t)


def _final_kernel(x_ref, rows_ref, gate_ref, g2_ref, fg_ref, o_ref):
    x = _moe_residual(x_ref, rows_ref, gate_ref, g2_ref)
    ms = jnp.mean(x * x, axis=-1, keepdims=True)
    o_ref[0] = x * lax.rsqrt(ms + NORM_EPS) * fg_ref[...]


def _final(x, res, final_g, tm=512):
    b, s, d = x.shape
    tm = min(tm, s)
    row = pl.BlockSpec((1, tm, d), lambda i, j: (i, j, 0))
    return pl.pallas_call(
        _final_kernel,
        grid=(b, s // tm),
        in_specs=[row] + _moe_residual_specs(tm, d) + [pl.BlockSpec((1, d), lambda i, j: (0, 0))],
        out_specs=row,
        out_shape=jax.ShapeDtypeStruct((b, s, d), F32),
        compiler_params=_params(("parallel", "parallel")),
        name="final",
    )(x, *res, final_g.reshape(1, d))


def _widen_w_in(w):
    gl0 = 2 * A_WIDTH + B_HEADS * DK + 6 * KV_GROUPS * DK
    per_group = HPG * 3
    gl_groups = [jnp.pad(w[:, gl0 + g * per_group:gl0 + (g + 1) * per_group], ((0, 0), (0, LANES - per_group)))
                 for g in range(KV_GROUPS)]
    wide = jnp.concatenate([w[:, :gl0]] + gl_groups, axis=1)
    assert gl0 == C_GL and wide.shape[1] == P_WIDE
    return wide.astype(BF16)


def _rope_tables(pos):
    half = DK // 2
    inv = ROPE_THETA ** (-jnp.arange(half, dtype=F32) / half)
    ang = pos.astype(F32)[..., None] * inv
    cos = jnp.cos(ang)
    sin = jnp.sin(ang)
    reps = LANES // DK
    return (jnp.concatenate([cos, cos] * reps, axis=-1),
            jnp.concatenate([-sin, sin] * reps, axis=-1))


def kernel(x, c, positions, ada_w, ada_b, norm1_g, norm2_g, w_in, w_out, sg_ln_g, sg_ln_b, sg_w, sg_b,
           cmp_pe_k, cmp_pe_v, cmp_w1_k, cmp_w2_k, cmp_w1_v, cmp_w2_v, router_w, router_b,
           exp_w1, exp_b1, exp_w2, exp_b2, final_g):
    b, s, d = x.shape
    n_layers = ada_w.shape[0]
    mod = _ada_mod(c, ada_w, ada_b)
    cos, sin = _rope_tables(positions)
    cmp_end = jnp.minimum(CMP_STRIDE * jnp.arange(s // CMP_STRIDE) + CMP_BLOCK - 1, s - 1)
    cos_c, sin_c = _rope_tables(positions[:, cmp_end])
    res = None
    for l in range(n_layers):
        sh1, sc1, g1, sh2, sc2, g2 = [m.reshape(b, 1, d) for m in jnp.split(mod[l], 6, axis=-1)]
        x, uv, qt, kc, vc, ks, vs_t, kw, vw_t, gl_t = _pre(x, res, norm1_g[l], sc1, sh1,
                                                          _widen_w_in(w_in[l]), cos, sin)
        y_a = _sgu(uv, sg_ln_g[l], sg_ln_b[l], sg_w[l], sg_b[l])
        kcmp, vcmp_t = _compress(kc, vc, cmp_pe_k[l], cmp_pe_v[l], cmp_w1_k[l], cmp_w2_k[l],
                                 cmp_w1_v[l], cmp_w2_v[l], cos_c, sin_c)
        oc_t, bias_t = _cmp_sel(qt, kcmp, vcmp_t)
        y_b = _attn(qt, ks, kw, vs_t, vw_t, bias_t, oc_t, gl_t)
        x, h2, idx, gate, rank = _post(y_a, y_b, w_out[l], x, g1, norm2_g[l], sc2, sh2,
                                       router_w[l], router_b[l])
        rows = _moe_routed(h2.reshape(b * s, d // 2), idx, rank, l, exp_w1, exp_b1, exp_w2, exp_b2)
        res = (rows.reshape(TOP_K, b, s, d // 2), gate.T.reshape(b, s, TOP_K), g2)
    return _final(x, res, final_g)
```

```python
import functools

import numpy as np
import jax
import jax.numpy as jnp
from jax import lax
from jax.experimental import pallas as pl
from jax.experimental.pallas import tpu as pltpu
from jax.experimental.pallas import tpu_sc as plsc

F32 = jnp.float32
BF16 = jnp.bfloat16
HIGHEST = lax.Precision.HIGHEST

D_MODEL = 1024
A_WIDTH = 512
A_HEADS = 8
CHUNK = 128
B_HEADS = 8
DK = 64
KV_GROUPS = 2
HPG = B_HEADS // KV_GROUPS
CMP_BLOCK = 32
CMP_STRIDE = 16
SEL_BLOCK = 64
SEL_TOP = 16
WINDOW = 512
ROPE_THETA = 10000.0
N_EXPERTS = 32
TOP_K = 4
D_EXPERT = 1024
SWIGLU_LIMIT = 7.0
SWIGLU_ALPHA = 1.702
NORM_EPS = 1e-6

LANES = 128
SUBLANES = 8
GW = HPG * DK
N_FORCED = 3
ONES_ROWS = 16
MASK_BIAS = -32768.0
Q_SCALE = DK ** -0.5 * 1.4426950408889634
VMEM_LIMIT = 56 * 1024 * 1024
ROW_BLOCK = 256
BLOCKS_PER_STEP = 4
SC_CORES = 2
SC_SUBCORES = 16
SC_WORKERS = SC_CORES * SC_SUBCORES
SC_WINDOW = 128
CMP_ROW_CHUNK = 128
SEL_ROW_CHUNK = 32

C_U, C_V, C_Q = 0, 512, 1024
C_KC, C_VC, C_KS, C_VS, C_KW, C_VW = 1536, 1664, 1792, 1920, 2048, 2176
C_GL = 2304
P_WIDE = 2560


def _params(sem):
    return pltpu.CompilerParams(dimension_semantics=sem, vmem_limit_bytes=VMEM_LIMIT)


def _split_bf16(x, parts):
    out = []
    for _ in range(parts):
        piece = x.astype(BF16)
        out.append(piece)
        x = x - piece.astype(F32)
    return out


def _split_bf16_bits(x, parts):
    out = []
    for _ in range(parts):
        piece = lax.bitcast_convert_type(lax.bitcast_convert_type(x, jnp.uint32) & jnp.uint32(0xFFFF0000), F32)
        out.append(piece.astype(BF16))
        x = x - piece
    return out


def _ada_kernel(c_ref, w_ref, b_ref, o_ref):
    c = c_ref[...]
    cond = c * jax.nn.sigmoid(c)
    o_ref[0] = jnp.dot(cond, w_ref[0], precision=HIGHEST, preferred_element_type=F32) + b_ref[0]


def _ada_mod(c, ada_w, ada_b):
    n_layers, d, d6 = ada_w.shape
    b = c.shape[0]
    rows = SUBLANES
    c_pad = jnp.zeros((rows, d), F32).at[:b].set(c)
    out = pl.pallas_call(
        _ada_kernel,
        grid=(n_layers, d6 // d),
        in_specs=[
            pl.BlockSpec((rows, d), lambda l, j: (0, 0)),
            pl.BlockSpec((1, d, d), lambda l, j: (l, 0, j)),
            pl.BlockSpec((1, 1, d), lambda l, j: (l, 0, j)),
        ],
        out_specs=pl.BlockSpec((1, rows, d), lambda l, j: (l, 0, j)),
        out_shape=jax.ShapeDtypeStruct((n_layers, rows, d6), F32),
        compiler_params=_params(("arbitrary", "arbitrary")),
        name="ada",
    )(c_pad, ada_w, ada_b.reshape(n_layers, 1, d6))
    return out[:, :b]


def _rope_slab(t, cos, sin_signed, lo):
    partner = jnp.where(lo, pltpu.roll(t, LANES - DK // 2, 1), pltpu.roll(t, DK // 2, 1))
    return t * cos + partner * sin_signed


def _rope(t, cos, sin_signed):
    lane = lax.broadcasted_iota(jnp.int32, (1, LANES), 1)
    lo = (lane % DK) < (DK // 2)
    slabs = [_rope_slab(t[:, s * LANES:(s + 1) * LANES], cos, sin_signed, lo)
             for s in range(t.shape[1] // LANES)]
    return slabs[0] if len(slabs) == 1 else jnp.concatenate(slabs, axis=1)


def _pack_halves(x):
    n = x.shape[1] // 2
    lo = lax.bitcast_convert_type(x[:, :n].astype(BF16).astype(F32), jnp.int32)
    hi = lax.bitcast_convert_type(x[:, n:].astype(BF16).astype(F32), jnp.int32)
    return lax.shift_right_logical(lo, jnp.int32(16)) | (hi & jnp.int32(-65536))


def _unpack_halves(p):
    lo = lax.bitcast_convert_type(lax.shift_left(p, jnp.int32(16)), F32)
    hi = lax.bitcast_convert_type(p & jnp.int32(-65536), F32)
    return jnp.concatenate([lo, hi], axis=1)


def _moe_residual(x_ref, rows_ref, gate_ref, g2_ref):
    gate = gate_ref[0]
    y = gate[:, 0:1] * _unpack_halves(rows_ref[0, 0])
    for k in range(1, TOP_K):
        y = y + gate[:, k:k + 1] * _unpack_halves(rows_ref[k, 0])
    return x_ref[0] + g2_ref[0] * y


def _moe_residual_specs(tm, d):
    return [pl.BlockSpec((TOP_K, 1, tm, d // 2), lambda i, j: (0, i, j, 0)),
            pl.BlockSpec((1, tm, TOP_K), lambda i, j: (i, j, 0)),
            pl.BlockSpec((1, 1, d), lambda i, j: (i, 0, 0))]


def _pre_kernel(has_res, *refs):
    if has_res:
        (x_ref, rows_ref, gate_ref, g2_ref, ng_ref, sc_ref, sh_ref, w_ref, cos_ref, sin_ref,
         xo_ref, uv_ref, qt_ref, kc_ref, vc_ref, ks_ref, vst_ref, kw_ref, vwt_ref, glt_ref) = refs
        x = _moe_residual(x_ref, rows_ref, gate_ref, g2_ref)
        xo_ref[0] = x
    else:
        (x_ref, ng_ref, sc_ref, sh_ref, w_ref, cos_ref, sin_ref,
         uv_ref, qt_ref, kc_ref, vc_ref, ks_ref, vst_ref, kw_ref, vwt_ref, glt_ref) = refs
        x = x_ref[0]
    ms = jnp.mean(x * x, axis=-1, keepdims=True)
    h = x * lax.rsqrt(ms + NORM_EPS) * ng_ref[...]
    h = h * (1.0 + sc_ref[0]) + sh_ref[0]
    proj = jnp.dot(h.astype(BF16), w_ref[...], preferred_element_type=F32)
    cos = cos_ref[0]
    sin = sin_ref[0]
    uv_ref[0] = proj[:, C_U:C_Q]
    qt_ref[0] = (_rope(proj[:, C_Q:C_KC], cos, sin) * Q_SCALE).T
    kc_ref[0] = proj[:, C_KC:C_VC]
    vc_ref[0] = proj[:, C_VC:C_KS]
    ks_ref[0] = _rope(proj[:, C_KS:C_VS], cos, sin).astype(BF16)
    vst_ref[0] = proj[:, C_VS:C_KW].T.astype(BF16)
    kw_ref[0] = _rope(proj[:, C_KW:C_VW], cos, sin).astype(BF16)
    vwt_ref[0] = proj[:, C_VW:C_GL].T.astype(BF16)
    glt_ref[0] = proj[:, C_GL:P_WIDE].T


def _pre(x, res, norm_g, sc, sh, w_wide, cos, sin, tm=512):
    b, s, d = x.shape
    tm = min(tm, s)
    row = lambda w: pl.BlockSpec((1, tm, w), lambda i, j: (i, j, 0))
    col = lambda w: pl.BlockSpec((1, w, tm), lambda i, j: (i, 0, j))
    vec = pl.BlockSpec((1, 1, d), lambda i, j: (i, 0, 0))
    in_specs = [row(d)]
    args = [x]
    if res is not None:
        in_specs += _moe_residual_specs(tm, d)
        args += list(res)
    in_specs += [pl.BlockSpec((1, d), lambda i, j: (0, 0)), vec, vec,
                 pl.BlockSpec((d, P_WIDE), lambda i, j: (0, 0)), row(LANES), row(LANES)]
    args += [norm_g.reshape(1, d), sc, sh, w_wide, cos, sin]
    kv = KV_GROUPS * DK
    outs = [(d, F32, False)] if res is not None else []
    outs += [(2 * A_WIDTH, F32, False), (B_HEADS * DK, F32, True), (kv, F32, False), (kv, F32, False),
             (kv, BF16, False), (kv, BF16, True), (kv, BF16, False), (kv, BF16, True),
             (KV_GROUPS * LANES, F32, True)]
    res_out = pl.pallas_call(
        functools.partial(_pre_kernel, res is not None),
        grid=(b, s // tm),
        in_specs=in_specs,
        out_specs=[col(w) if t else row(w) for w, _, t in outs],
        out_shape=[jax.ShapeDtypeStruct((b, w, s) if t else (b, s, w), dt) for w, dt, t in outs],
        compiler_params=_params(("parallel", "parallel")),
        name="pre",
    )(*args)
    if res is None:
        res_out = [x] + list(res_out)
    return res_out


def _sgu_kernel(uv_ref, lng_ref, lnb_ref, w_ref, bias_ref, o_ref):
    rows = uv_ref.shape[1]
    uv = uv_ref[0]
    gu = jax.nn.gelu(uv[:, :A_WIDTH])
    gv = jax.nn.gelu(uv[:, A_WIDTH:])
    mu = jnp.mean(gv, axis=-1, keepdims=True)
    var = jnp.mean(jnp.square(gv - mu), axis=-1, keepdims=True)
    vn = ((gv - mu) * lax.rsqrt(var + NORM_EPS) * lng_ref[...] + lnb_ref[...]).astype(BF16)
    r = lax.broadcasted_iota(jnp.int32, (CHUNK, CHUNK), 0)
    c = lax.broadcasted_iota(jnp.int32, (CHUNK, CHUNK), 1)
    causal = c <= r
    lane_lo = lax.broadcasted_iota(jnp.int32, (CHUNK, LANES), 1) < DK
    for p in range(A_HEADS // 2):
        w0 = jnp.where(causal, w_ref[2 * p], 0.0).astype(BF16)
        w1 = jnp.where(causal, w_ref[2 * p + 1], 0.0).astype(BF16)
        bias = bias_ref[:, p * LANES:(p + 1) * LANES]
        for ch in range(rows // CHUNK):
            rs = slice(ch * CHUNK, (ch + 1) * CHUNK)
            cs = slice(p * LANES, (p + 1) * LANES)
            vp = vn[rs, cs]
            m0 = jnp.dot(w0, vp, preferred_element_type=F32)
            m1 = jnp.dot(w1, vp, preferred_element_type=F32)
            mixed = jnp.where(lane_lo, m0, m1) + bias
            o_ref[0, rs, cs] = (gu[rs, cs] * mixed).astype(o_ref.dtype)


def _sgu(uv, ln_g, ln_b, w_s, b_s, tm=512):
    b, s, _ = uv.shape
    tm = min(tm, s)
    bias = jnp.repeat(b_s.T, A_WIDTH // A_HEADS, axis=1)
    return pl.pallas_call(
        _sgu_kernel,
        grid=(b, s // tm),
        in_specs=[
            pl.BlockSpec((1, tm, 2 * A_WIDTH), lambda i, j: (i, j, 0)),
            pl.BlockSpec((1, A_WIDTH), lambda i, j: (0, 0)),
            pl.BlockSpec((1, A_WIDTH), lambda i, j: (0, 0)),
            pl.BlockSpec((A_HEADS, CHUNK, CHUNK), lambda i, j: (0, 0, 0)),
            pl.BlockSpec((CHUNK, A_WIDTH), lambda i, j: (0, 0)),
        ],
        out_specs=pl.BlockSpec((1, tm, A_WIDTH), lambda i, j: (i, j, 0)),
        out_shape=jax.ShapeDtypeStruct((b, s, A_WIDTH), BF16),
        compiler_params=_params(("parallel", "parallel")),
        name="sgu",
    )(uv, ln_g.reshape(1, -1), ln_b.reshape(1, -1), w_s, bias)


def _compress_kernel(kr_ref, vr_ref, pek_ref, pev_ref, w1k_ref, w1v_ref, w2k_ref, w2v_ref,
                     cos_ref, sin_ref, ko_ref, vt_ref):
    nc = kr_ref.shape[1] // CMP_STRIDE

    def row_groups(t_ref):
        return jnp.concatenate([t_ref[0, pl.ds(l, nc, stride=CMP_STRIDE), :] for l in range(CMP_STRIDE)], axis=1)

    def mlp(r, pe_ref, w1_ref, w2_ref):
        top = jnp.dot(r + pe_ref[0:1], w1_ref[0], precision=HIGHEST, preferred_element_type=F32)
        bot = jnp.dot(r + pe_ref[1:2], w1_ref[1], precision=HIGHEST, preferred_element_type=F32)
        pre = top + pltpu.roll(bot, nc - 1, 0)
        return jnp.dot(jax.nn.gelu(pre), w2_ref[...], precision=HIGHEST, preferred_element_type=F32)

    kc = _rope(mlp(row_groups(kr_ref), pek_ref, w1k_ref, w2k_ref), cos_ref[0], sin_ref[0])
    vc = mlp(row_groups(vr_ref), pev_ref, w1v_ref, w2v_ref)
    lo = lax.broadcasted_iota(jnp.int32, (nc, LANES), 1) < DK
    rolled = pltpu.roll(kc, DK, 1)
    for g in range(KV_GROUPS):
        dup = jnp.where(lo, kc, rolled) if g == 0 else jnp.where(lo, rolled, kc)
        hi, low = _split_bf16(dup, 2)
        ko_ref[0, g, :, 0:LANES] = hi
        ko_ref[0, g, :, LANES:2 * LANES] = jnp.where(lo, low, jnp.zeros_like(low))
    vt_ref[0] = vc.T.astype(BF16)


def _compress_weights(pe, w1, w2):
    half = CMP_BLOCK // 2
    eye = jnp.eye(KV_GROUPS, dtype=F32)
    w1r = w1.reshape(CMP_BLOCK, DK, DK)
    wfull = jnp.einsum('lde,gh->lgdhe', w1r, eye)
    w1s = wfull.reshape(2, half * KV_GROUPS * DK, KV_GROUPS * DK)
    pes = jnp.broadcast_to(pe.reshape(2, half, 1, DK), (2, half, KV_GROUPS, DK)).reshape(2, -1)
    w2bd = jnp.einsum('de,gh->gdhe', w2, eye).reshape(KV_GROUPS * DK, KV_GROUPS * DK)
    return pes, w1s, w2bd


def _compress(kc, vc, pe_k, pe_v, w1_k, w2_k, w1_v, w2_v, cos_c, sin_c):
    b, s, _ = kc.shape
    nc = s // CMP_STRIDE
    rw = CMP_STRIDE * LANES
    pek, w1ks, w2kb = _compress_weights(pe_k, w1_k, w2_k)
    pev, w1vs, w2vb = _compress_weights(pe_v, w1_v, w2_v)
    full = lambda shape: pl.BlockSpec(shape, lambda i: (0,) * len(shape))
    return pl.pallas_call(
        _compress_kernel,
        grid=(b,),
        in_specs=[
            pl.BlockSpec((1, s, LANES), lambda i: (i, 0, 0)),
            pl.BlockSpec((1, s, LANES), lambda i: (i, 0, 0)),
            full((2, rw)), full((2, rw)),
            full((2, rw, LANES)), full((2, rw, LANES)),
            full((LANES, LANES)), full((LANES, LANES)),
            pl.BlockSpec((1, nc, LANES), lambda i: (i, 0, 0)),
            pl.BlockSpec((1, nc, LANES), lambda i: (i, 0, 0)),
        ],
        out_specs=[pl.BlockSpec((1, KV_GROUPS, nc, 2 * LANES), lambda i: (i, 0, 0, 0)),
                   pl.BlockSpec((1, KV_GROUPS * DK, nc), lambda i: (i, 0, 0))],
        out_shape=[jax.ShapeDtypeStruct((b, KV_GROUPS, nc, 2 * LANES), BF16),
                   jax.ShapeDtypeStruct((b, KV_GROUPS * DK, nc), BF16)],
        compiler_params=_params(("parallel",)),
        name="compress",
    )(kc, vc, pek, pev, w1ks, w1vs, w2kb, w2vb, cos_c, sin_c)


def _heads_on_lanes(qt):
    return jnp.concatenate([qt[h * DK:(h + 1) * DK] for h in range(HPG)], axis=1)


def _cmp_sel_kernel(n_top, qt_ref, k_ref, vt_ref, map_ref, oc_ref, bias_ref, imp_ref):
    tq = qt_ref.shape[2]
    cols = HPG * tq
    nc = k_ref.shape[2]
    nsel = map_ref.shape[0]
    q0 = pl.program_id(2) * tq
    q_hi, q_lo = _split_bf16(_heads_on_lanes(qt_ref[0]), 2)
    q3t = jnp.concatenate([q_hi, q_lo, q_hi, jnp.zeros_like(q_hi)], axis=0)
    t_col = q0 + lax.broadcasted_iota(jnp.int32, (1, cols), 1) % tq

    def compressed_branch(rows):
        s = jnp.dot(k_ref[0, 0, 0:rows], q3t, preferred_element_type=F32)
        n_idx = lax.broadcasted_iota(jnp.int32, (rows, 1), 0)
        s = jnp.where((CMP_STRIDE * n_idx + CMP_BLOCK - 1) <= t_col, s, -jnp.inf)
        m = jnp.max(s, axis=0, keepdims=True)
        m = jnp.where(m == -jnp.inf, 0.0, m)
        e = jnp.exp2(s - m)
        d = jnp.sum(e, axis=0, keepdims=True)
        p = e / jnp.where(d > 0, d, 1.0)
        oc = jnp.dot(vt_ref[0, :, 0:rows], p.astype(BF16), preferred_element_type=F32)
        oc_ref[0] = jnp.concatenate([oc[:, h * tq:(h + 1) * tq] for h in range(HPG)], axis=0)
        psum = p[:, 0:tq]
        for h in range(1, HPG):
            psum = psum + p[:, h * tq:(h + 1) * tq]
        imp_ref[...] = jnp.dot(jnp.concatenate([map_ref[:, 0:rows]] * 3, axis=1),
                               jnp.concatenate(_split_bf16(psum, 3), axis=0), preferred_element_type=F32)

    visible = (q0 + tq - CMP_BLOCK) // CMP_STRIDE + 1
    n_chunks = -(-nc // CMP_ROW_CHUNK)
    for r in range(1, n_chunks + 1):
        lo = (r - 1) * CMP_ROW_CHUNK if r > 1 else -nc
        pl.when((visible > lo) & (visible <= r * CMP_ROW_CHUNK) if r < n_chunks else visible > lo)(
            functools.partial(compressed_branch, min(r * CMP_ROW_CHUNK, nc)))

    def select(rows):
        imp = imp_ref[0:rows]
        j = lax.broadcasted_iota(jnp.int32, (rows, tq), 0)
        cur = (q0 + lax.broadcasted_iota(jnp.int32, (rows, tq), 1)) // SEL_BLOCK
        valid = j <= cur
        forced = (j == 0) | (j == cur) | (j == cur - 1)
        keep = forced | (valid & (cur < n_top))
        vals = jnp.where(valid & jnp.logical_not(forced), imp, -jnp.inf)
        bias = jnp.where(keep, 0.0, MASK_BIAS)
        for _ in range(n_top - N_FORCED):
            mx = jnp.max(vals, axis=0, keepdims=True)
            first = jnp.min(jnp.where(vals == mx, j, rows), axis=0, keepdims=True)
            pick = (j == first) & (mx > -jnp.inf)
            bias = jnp.where(pick, 0.0, bias)
            vals = jnp.where(pick, -jnp.inf, vals)
        bias_ref[0, 0, 0:rows] = bias.astype(BF16)
        if rows < nsel:
            bias_ref[0, 0, rows:nsel] = jnp.full((nsel - rows, tq), MASK_BIAS, BF16)

    n_valid = (q0 + tq - 1) // SEL_BLOCK + 1
    n_sel_chunks = -(-nsel // SEL_ROW_CHUNK)
    for r in range(1, n_sel_chunks + 1):
        lo = (r - 1) * SEL_ROW_CHUNK if r > 1 else -nsel
        pl.when((n_valid > lo) & (n_valid <= r * SEL_ROW_CHUNK) if r < n_sel_chunks else n_valid > lo)(
            functools.partial(select, min(r * SEL_ROW_CHUNK, nsel)))


def _sel_map_t(s):
    nc = s // CMP_STRIDE
    n_cmp = (s - CMP_BLOCK) // CMP_STRIDE + 1
    n_sel = s // SEL_BLOCK
    cs = CMP_STRIDE * np.arange(n_cmp)[:, None]
    ce = cs + CMP_BLOCK
    ss = SEL_BLOCK * np.arange(n_sel)[None, :]
    se = ss + SEL_BLOCK
    ov = np.clip(np.minimum(ce, se) - np.maximum(cs, ss), 0, None) / CMP_STRIDE
    out = np.zeros((n_sel, nc), np.float32)
    out[:, :n_cmp] = ov.T
    return out


def _cmp_sel(qt, kcmp, vcmp_t, tq=256):
    b, _, s = qt.shape
    nc = s // CMP_STRIDE
    nsel = s // SEL_BLOCK
    n_top = min(SEL_TOP, nsel)
    sel_map = jnp.asarray(_sel_map_t(s), dtype=BF16)
    return pl.pallas_call(
        functools.partial(_cmp_sel_kernel, n_top),
        grid=(b, KV_GROUPS, s // tq),
        in_specs=[
            pl.BlockSpec((1, GW, tq), lambda i, g, j: (i, g, j)),
            pl.BlockSpec((1, 1, nc, 2 * LANES), lambda i, g, j: (i, g, 0, 0)),
            pl.BlockSpec((1, DK, nc), lambda i, g, j: (i, g, 0)),
            pl.BlockSpec((nsel, nc), lambda i, g, j: (0, 0)),
        ],
        out_specs=[
            pl.BlockSpec((1, GW, tq), lambda i, g, j: (i, g, j)),
            pl.BlockSpec((1, 1, nsel, tq), lambda i, g, j: (i, g, 0, j)),
        ],
        out_shape=[
            jax.ShapeDtypeStruct((b, B_HEADS * DK, s), F32),
            jax.ShapeDtypeStruct((b, KV_GROUPS, nsel, s), BF16),
        ],
        scratch_shapes=[pltpu.VMEM((nsel, tq), F32)],
        compiler_params=_params(("parallel", "parallel", "parallel")),
        name="cmp_sel",
    )(qt, kcmp, vcmp_t, sel_map)


def _attn_kernel(kt, qt_ref, ks_ref, kw_ref, vst_ref, vwt_ref, bias_ref, oh_ref, oc_ref, glt_ref,
                 o_ref, qa_ref, s0_ref, s1_ref, m_ref, acc_ref, ow_ref):
    tq = qt_ref.shape[2]
    cols = HPG * tq
    nsel = bias_ref.shape[2]
    g = pl.program_id(1)
    q0 = pl.program_id(2) * tq

    q4t = _heads_on_lanes(qt_ref[0]).astype(BF16)
    zero = jnp.zeros_like(q4t)
    qa_ref[0:DK] = jnp.where(g == 0, q4t, zero)
    qa_ref[DK:2 * DK] = jnp.where(g == 1, q4t, zero)
    qa_ref[LANES:LANES + nsel] = jnp.concatenate([bias_ref[0, 0]] * HPG, axis=1)
    qaug = qa_ref[...]
    t_col = q0 + lax.broadcasted_iota(jnp.int32, (1, cols), 1) % tq

    def fold(x, op):
        return op(x.reshape(x.shape[0] // SUBLANES, SUBLANES, cols), axis=0)

    def with_ones(vt):
        return jnp.concatenate([vt, jnp.ones((ONES_ROWS, vt.shape[1]), BF16)], axis=0)

    def stage(c, s_ref):
        k0 = pl.multiple_of(c * kt, kt)
        kaug = jnp.concatenate([ks_ref[0, pl.ds(k0, kt), :], oh_ref[pl.ds(k0, kt), :]], axis=1)
        s_ref[...] = jnp.dot(kaug, qaug, preferred_element_type=F32)
        r0 = pl.multiple_of(jnp.clip(q0 - k0, 0, kt - tq), tq)
        kpos = k0 + r0 + lax.broadcasted_iota(jnp.int32, (tq, 1), 0)
        s_ref[pl.ds(r0, tq), :] = jnp.where(kpos <= t_col, s_ref[pl.ds(r0, tq), :], -jnp.inf)

    def consume(c, s_ref):
        k0 = pl.multiple_of(c * kt, kt)
        sc = s_ref[...]
        m_old = m_ref[...]
        m_new = jnp.maximum(m_old, jnp.max(fold(sc, jnp.max), axis=0, keepdims=True))
        alpha = jnp.exp2(m_old - m_new)
        pr = jnp.exp2(sc - m_new)
        acc_ref[...] = alpha * acc_ref[...] + jnp.dot(with_ones(vst_ref[0, :, pl.ds(k0, kt)]), pr.astype(BF16),
                                                      preferred_element_type=F32)
        m_ref[...] = m_new

    m_ref[...] = jnp.full(m_ref.shape, -jnp.inf, F32)
    acc_ref[...] = jnp.zeros(acc_ref.shape, F32)
    n_full = q0 // kt
    n_pairs = n_full // 2
    odd = n_full % 2 == 1
    stage(0, s0_ref)

    wlen = WINDOW + tq
    w0 = pl.multiple_of(jnp.maximum(q0 - WINDOW, 0), tq)
    sw = jnp.dot(kw_ref[0, pl.ds(w0, wlen), :], qaug[0:LANES], preferred_element_type=F32)
    dlt = t_col - (w0 + lax.broadcasted_iota(jnp.int32, (wlen, 1), 0))
    sw = jnp.where(lax.bitcast_convert_type(dlt, jnp.uint32) < jnp.uint32(WINDOW), sw, -jnp.inf)
    pw = jnp.exp2(sw - jnp.max(sw, axis=0, keepdims=True))
    o_win = jnp.dot(with_ones(vwt_ref[0, :, pl.ds(w0, wlen)]), pw.astype(BF16), preferred_element_type=F32)
    ow_ref[...] = o_win[0:DK] / o_win[DK:DK + 1]

    def pair(i, carry):
        stage(2 * i + 1, s1_ref)
        consume(2 * i, s0_ref)
        stage(2 * i + 2, s0_ref)
        consume(2 * i + 1, s1_ref)
        return carry

    lax.fori_loop(0, n_pairs, pair, 0)

    @pl.when(odd)
    def _():
        stage(n_full, s1_ref)
        consume(n_full - 1, s0_ref)
        consume(n_full, s1_ref)

    @pl.when(jnp.logical_not(odd))
    def _():
        consume(n_full, s0_ref)

    o_sel = acc_ref[0:DK] / acc_ref[DK:DK + 1]

    o_win = ow_ref[...]
    gates = jax.nn.sigmoid(glt_ref[0])
    oc = _heads_on_lanes(oc_ref[0])
    for p in range(HPG // 2):
        halves = []
        for hh in (2 * p, 2 * p + 1):
            cs = slice(hh * tq, (hh + 1) * tq)
            halves.append(gates[3 * hh:3 * hh + 1] * oc[:, cs] + gates[3 * hh + 1:3 * hh + 2] * o_sel[:, cs]
                          + gates[3 * hh + 2:3 * hh + 3] * o_win[:, cs])
        o_ref[0, :, p * LANES:(p + 1) * LANES] = jnp.concatenate(halves, axis=0).T.astype(o_ref.dtype)


def _attn(qt, ks, kw, vs_t, vw_t, bias_t, oc_t, gl_t, tq=256, kt=512):
    b, _, s = qt.shape
    nsel = s // SEL_BLOCK
    kt = min(kt, s)
    onehot = jnp.asarray((np.arange(s)[:, None] // SEL_BLOCK == np.arange(nsel)[None, :]), dtype=BF16)
    k_spec = pl.BlockSpec((1, s, KV_GROUPS * DK), lambda i, g, j: (i, 0, 0))
    vt_spec = pl.BlockSpec((1, DK, s), lambda i, g, j: (i, g, 0))
    cols = HPG * tq
    return pl.pallas_call(
        functools.partial(_attn_kernel, kt),
        grid=(b, KV_GROUPS, s // tq),
        in_specs=[
            pl.BlockSpec((1, GW, tq), lambda i, g, j: (i, g, j)),
            k_spec, k_spec, vt_spec, vt_spec,
            pl.BlockSpec((1, 1, nsel, tq), lambda i, g, j: (i, g, 0, j)),
            pl.BlockSpec((s, nsel), lambda i, g, j: (0, 0)),
            pl.BlockSpec((1, GW, tq), lambda i, g, j: (i, g, j)),
            pl.BlockSpec((1, LANES, tq), lambda i, g, j: (i, g, j)),
        ],
        out_specs=pl.BlockSpec((1, tq, GW), lambda i, g, j: (i, j, g)),
        out_shape=jax.ShapeDtypeStruct((b, s, B_HEADS * DK), BF16),
        scratch_shapes=[pltpu.VMEM((LANES + nsel, cols), BF16),
                        pltpu.VMEM((kt, cols), F32), pltpu.VMEM((kt, cols), F32),
                        pltpu.VMEM((1, cols), F32), pltpu.VMEM((DK + ONES_ROWS, cols), F32),
                        pltpu.VMEM((DK, cols), F32)],
        compiler_params=_params(("parallel", "parallel", "arbitrary")),
        name="attn",
    )(qt, ks, kw, vs_t, vw_t, bias_t, onehot, oc_t, gl_t)


def _post_kernel(ya_ref, yb_ref, wo_ref, x_ref, g1_ref, ng_ref, sc_ref, sh_ref, rw_ref, rb_ref,
                 xo_ref, h_ref, idx_ref, gate_ref, rank_ref, cnt_ref):
    tm = x_ref.shape[1]

    @pl.when((pl.program_id(0) == 0) & (pl.program_id(1) == 0))
    def _():
        cnt_ref[...] = jnp.zeros(cnt_ref.shape, F32)

    mixed = jnp.dot(ya_ref[0], wo_ref[:A_WIDTH], preferred_element_type=F32)
    mixed = mixed + jnp.dot(yb_ref[0], wo_ref[A_WIDTH:], preferred_element_type=F32)
    x = x_ref[0] + g1_ref[0] * mixed
    xo_ref[0] = x
    ms = jnp.mean(x * x, axis=-1, keepdims=True)
    h = x * lax.rsqrt(ms + NORM_EPS) * ng_ref[...]
    h = h * (1.0 + sc_ref[0]) + sh_ref[0]
    h_ref[0] = _pack_halves(h)
    h_hi, h_mid = _split_bf16(h, 2)
    logits = jnp.dot(jnp.concatenate([h_hi, h_hi, h_mid], axis=1), rw_ref[...],
                     preferred_element_type=F32) + rb_ref[...]
    vals = logits.T[:N_EXPERTS]
    e_idx = lax.broadcasted_iota(jnp.int32, (N_EXPERTS, tm), 0)
    top = jnp.max(vals, axis=0, keepdims=True)
    routed_t = jnp.zeros_like(vals)
    firsts, exps = [], []
    for k in range(TOP_K):
        m = jnp.max(vals, axis=0, keepdims=True)
        first = jnp.min(jnp.where(vals == m, e_idx, N_EXPERTS), axis=0, keepdims=True)
        pick = e_idx == first
        routed_t = jnp.where(pick, 1.0, routed_t)
        firsts.append(first)
        exps.append(jnp.exp(m - top))
        vals = jnp.where(pick, -jnp.inf, vals)
    idx_i = jnp.concatenate(firsts, axis=0)
    idx_t = idx_i.astype(F32)
    slot_exp = jnp.concatenate(exps, axis=0)
    idx_ref[...] = idx_i
    gate_ref[...] = slot_exp / jnp.sum(slot_exp, axis=0, keepdims=True)
    r = lax.broadcasted_iota(jnp.int32, (tm, tm), 0)
    c = lax.broadcasted_iota(jnp.int32, (tm, tm), 1)
    before = jnp.where(r < c, 1.0, 0.0).astype(BF16)
    rank_t = cnt_ref[...] + jnp.dot(routed_t.astype(BF16), before, preferred_element_type=F32)
    cnt_ref[...] += jnp.sum(routed_t, axis=-1, keepdims=True)
    e_row = lax.broadcasted_iota(jnp.int32, (N_EXPERTS, tm), 0).astype(F32)
    ranks = [jnp.sum(jnp.where(e_row == idx_t[k:k + 1], rank_t, 0.0), axis=0, keepdims=True)
             for k in range(TOP_K)]
    rank_ref[...] = jnp.concatenate(ranks, axis=0).astype(jnp.int32)


def _post(y_a, y_b, w_out, x, g1, norm_g, sc, sh, router_w, router_b, tm=512):
    b, s, d = x.shape
    tm = min(tm, s)
    nt = s // tm
    w_hi, w_mid = _split_bf16_bits(jnp.zeros((d, LANES), F32).at[:, :N_EXPERTS].set(router_w), 2)
    rw = jnp.concatenate([w_hi, w_mid, w_hi], axis=0)
    rb = jnp.full((1, LANES), -1e30, F32).at[0, :N_EXPERTS].set(router_b)
    vec = pl.BlockSpec((1, 1, d), lambda i, j: (i, 0, 0))
    row = lambda w: pl.BlockSpec((1, tm, w), lambda i, j: (i, j, 0))
    return pl.pallas_call(
        _post_kernel,
        grid=(b, nt),
        in_specs=[
            row(A_WIDTH), row(B_HEADS * DK),
            pl.BlockSpec((d, d), lambda i, j: (0, 0)),
            row(d), vec,
            pl.BlockSpec((1, d), lambda i, j: (0, 0)), vec, vec,
            pl.BlockSpec((3 * d, LANES), lambda i, j: (0, 0)),
            pl.BlockSpec((1, LANES), lambda i, j: (0, 0)),
        ],
        out_specs=[row(d), row(d // 2)] + [pl.BlockSpec((TOP_K, tm), lambda i, j: (0, i * nt + j))] * 3,
        out_shape=[
            jax.ShapeDtypeStruct((b, s, d), F32),
            jax.ShapeDtypeStruct((b, s, d // 2), jnp.int32),
            jax.ShapeDtypeStruct((TOP_K, b * s), jnp.int32),
            jax.ShapeDtypeStruct((TOP_K, b * s), F32),
            jax.ShapeDtypeStruct((TOP_K, b * s), jnp.int32),
        ],
        scratch_shapes=[pltpu.VMEM((N_EXPERTS, 1), F32)],
        compiler_params=_params(("arbitrary", "arbitrary")),
        name="post",
    )(y_a, y_b, w_out.astype(BF16), x, g1, norm_g.reshape(1, d), sc, sh, rw, rb)


def _sc_mesh():
    return plsc.VectorSubcoreMesh(core_axis_name="c", subcore_axis_name="s")


def _sc_worker():
    return lax.axis_index("c") * SC_SUBCORES + lax.axis_index("s")


def _dispatch_rows(h2, dest_c, n_rows):
    t, d = h2.shape
    per_worker = t // SC_WINDOW // SC_WORKERS

    def body(x_hbm, i_hbm, o_hbm, buf, idx):
        worker = _sc_worker()

        @pl.loop(0, per_worker)
        def _(j):
            ch = worker * per_worker + j
            pltpu.sync_copy(i_hbm.at[ch], idx)
            pltpu.sync_copy(x_hbm.at[pl.ds(ch * SC_WINDOW, SC_WINDOW)], buf)
            for k in range(TOP_K):
                pltpu.sync_copy(buf, o_hbm.at[idx.at[k]])

    return pl.kernel(
        body, out_type=jax.ShapeDtypeStruct((n_rows, d), h2.dtype), mesh=_sc_mesh(),
        scratch_types=[pltpu.VMEM((SC_WINDOW, d), h2.dtype), pltpu.VMEM((TOP_K, SC_WINDOW), jnp.int32)],
        name="dispatch_rows",
    )(h2, dest_c)


def _collect_rows(rows, dest_c, t):
    d = rows.shape[1]
    per_worker = t // SC_WINDOW // SC_WORKERS

    def body(r_hbm, i_hbm, o_hbm, buf, idx):
        worker = _sc_worker()

        @pl.loop(0, per_worker)
        def _(j):
            ch = worker * per_worker + j
            pltpu.sync_copy(i_hbm.at[ch], idx)
            for k in range(TOP_K):
                pltpu.sync_copy(r_hbm.at[idx.at[k]], buf)
                pltpu.sync_copy(buf, o_hbm.at[k, pl.ds(ch * SC_WINDOW, SC_WINDOW)])

    return pl.kernel(
        body, out_type=jax.ShapeDtypeStruct((TOP_K, t, d), rows.dtype), mesh=_sc_mesh(),
        scratch_types=[pltpu.VMEM((SC_WINDOW, d), rows.dtype), pltpu.VMEM((TOP_K, SC_WINDOW), jnp.int32)],
        name="collect_rows",
    )(rows, dest_c)


def _experts_kernel(layer, be_ref, nb_ref, slot_ref, next_ref, x_ref, w1_hbm, b1_ref, w2_hbm, b2_ref, o_ref,
                    w1f_ref, w2f_ref, w1b_ref, w2b_ref, sem):
    def weight_copies(expert, slot):
        return (pltpu.make_async_copy(w1_hbm.at[layer, expert], w1f_ref.at[slot], sem.at[slot, 0]),
                pltpu.make_async_copy(w2_hbm.at[layer, expert], w2f_ref.at[slot], sem.at[slot, 1]))

    def row_block(i, rs):
        e = be_ref[i]
        live = i < nb_ref[0]

        @pl.when(live & (i == 0))
        def _():
            for cp in weight_copies(e, 0):
                cp.start()

        @pl.when(live & ((i == 0) | (e != be_ref[jnp.maximum(i - 1, 0)])))
        def _():
            slot = slot_ref[i]
            for cp in weight_copies(e, slot):
                cp.wait()
            w1b_ref[...] = w1f_ref[slot].astype(BF16)
            w2b_ref[...] = w2f_ref[slot].astype(BF16)

            @pl.when(next_ref[i] >= 0)
            def _():
                for cp in weight_copies(next_ref[i], 1 - slot):
                    cp.start()

        @pl.when(live)
        def _():
            x = _unpack_halves(x_ref[rs, :]).astype(BF16)
            hdn = jnp.dot(x, w1b_ref[...], preferred_element_type=F32) + b1_ref[0, pl.ds(e, 1), :]
            glu = jnp.minimum(hdn[:, :D_EXPERT], SWIGLU_LIMIT)
            lin = jnp.clip(hdn[:, D_EXPERT:], -SWIGLU_LIMIT, SWIGLU_LIMIT)
            act = glu * jax.nn.sigmoid(SWIGLU_ALPHA * glu) * (lin + 1.0)
            out = jnp.dot(act.astype(BF16), w2b_ref[...], preferred_element_type=F32) + b2_ref[0, pl.ds(e, 1), :]
            o_ref[rs, :] = _pack_halves(out)

    for h in range(BLOCKS_PER_STEP):
        row_block(pl.program_id(0) * BLOCKS_PER_STEP + h, slice(h * ROW_BLOCK, (h + 1) * ROW_BLOCK))


def _experts(rows, block_exp, n_live, slot, next_exp, layer, w1, b1, w2, b2):
    n_rows, dp = rows.shape
    n_layers, n_e, d, f2 = w1.shape
    n_blocks = n_rows // ROW_BLOCK
    assert n_blocks % BLOCKS_PER_STEP == 0
    step_rows = BLOCKS_PER_STEP * ROW_BLOCK
    this_layer = lambda i, be, nb, sl, nx: (layer, 0, 0)
    grid_spec = pltpu.PrefetchScalarGridSpec(
        num_scalar_prefetch=4,
        grid=(n_blocks // BLOCKS_PER_STEP,),
        in_specs=[
            pl.BlockSpec((step_rows, dp), lambda i, be, nb, sl, nx: (i, 0)),
            pl.BlockSpec(memory_space=pl.ANY),
            pl.BlockSpec((1, n_e, f2), this_layer),
            pl.BlockSpec(memory_space=pl.ANY),
            pl.BlockSpec((1, n_e, d), this_layer),
        ],
        out_specs=pl.BlockSpec((step_rows, dp), lambda i, be, nb, sl, nx: (i, 0)),
        scratch_shapes=[pltpu.VMEM((2, d, f2), F32), pltpu.VMEM((2, f2 // 2, d), F32),
                        pltpu.VMEM((d, f2), BF16), pltpu.VMEM((f2 // 2, d), BF16),
                        pltpu.SemaphoreType.DMA((2, 2))],
    )
    return pl.pallas_call(
        functools.partial(_experts_kernel, layer),
        grid_spec=grid_spec,
        out_shape=jax.ShapeDtypeStruct((n_rows, dp), jnp.int32),
        compiler_params=_params(("arbitrary",)),
        name="experts",
    )(block_exp, n_live, slot, next_exp, rows, w1, b1, w2, b2)


def _moe_routed(h2, idx, rank, layer, w1, b1, w2, b2):
    t, d = h2.shape
    n_e = w1.shape[1]
    n_blocks = t * TOP_K // ROW_BLOCK + n_e
    onehot = (idx[:, :, None] == jnp.arange(n_e, dtype=jnp.int32)).astype(jnp.int32)
    counts = jnp.sum(onehot, axis=(0, 1))
    padded = (counts + ROW_BLOCK - 1) // ROW_BLOCK * ROW_BLOCK
    pad_end = jnp.cumsum(padded)
    dest = jnp.sum(onehot * (pad_end - padded), axis=-1) + rank
    block_row = jnp.arange(n_blocks, dtype=jnp.int32)[:, None] * ROW_BLOCK
    block_exp = jnp.minimum(jnp.sum((pad_end[None, :] <= block_row).astype(jnp.int32), axis=-1), n_e - 1)
    n_live = (pad_end[-1:] // ROW_BLOCK).astype(jnp.int32)
    experts = jnp.arange(n_e, dtype=jnp.int32)
    later = (experts[None, :] > experts[:, None]) & (counts[None, :] > 0)
    next_of = jnp.min(jnp.where(later, experts[None, :], n_e), axis=-1)
    next_of = jnp.where(next_of == n_e, -1, next_of)
    block_onehot = (block_exp[:, None] == experts[None, :]).astype(jnp.int32)
    next_exp = jnp.sum(block_onehot * next_of, axis=-1)
    run_of = jnp.cumsum((counts > 0).astype(jnp.int32)) - 1
    slot = jnp.sum(block_onehot * run_of, axis=-1) % 2
    dest_c = dest.reshape(TOP_K, t // SC_WINDOW, SC_WINDOW).transpose(1, 0, 2)
    rows = _dispatch_rows(h2, dest_c, n_blocks * ROW_BLOCK)
    out = _experts(rows, block_exp, n_live, slot, next_exp, layer, w1, b1, w2, b2)
    return _collect_rows(out, dest_c, t)


def _final_kernel(x_ref, rows_ref, gate_ref, g2_ref, fg_ref, o_ref):
    x = _moe_residual(x_ref, rows_ref, gate_ref, g2_ref)
    ms = jnp.mean(x * x, axis=-1, keepdims=True)
    o_ref[0] = x * lax.rsqrt(ms + NORM_EPS) * fg_ref[...]


def _final(x, res, final_g, tm=512):
    b, s, d = x.shape
    tm = min(tm, s)
    row = pl.BlockSpec((1, tm, d), lambda i, j: (i, j, 0))
    return pl.pallas_call(
        _final_kernel,
        grid=(b, s // tm),
        in_specs=[row] + _moe_residual_specs(tm, d) + [pl.BlockSpec((1, d), lambda i, j: (0, 0))],
        out_specs=row,
        out_shape=jax.ShapeDtypeStruct((b, s, d), F32),
        compiler_params=_params(("parallel", "parallel")),
        name="final",
    )(x, *res, final_g.reshape(1, d))


def _widen_w_in(w):
    gl0 = 2 * A_WIDTH + B_HEADS * DK + 6 * KV_GROUPS * DK
    per_group = HPG * 3
    gl_groups = [jnp.pad(w[:, gl0 + g * per_group:gl0 + (g + 1) * per_group], ((0, 0), (0, LANES - per_group)))
                 for g in range(KV_GROUPS)]
    wide = jnp.concatenate([w[:, :gl0]] + gl_groups, axis=1)
    assert gl0 == C_GL and wide.shape[1] == P_WIDE
    return wide.astype(BF16)


def _rope_tables(pos):
    half = DK // 2
    inv = ROPE_THETA ** (-jnp.arange(half, dtype=F32) / half)
    ang = pos.astype(F32)[..., None] * inv
    cos = jnp.cos(ang)
    sin = jnp.sin(ang)
    reps = LANES // DK
    return (jnp.concatenate([cos, cos] * reps, axis=-1),
            jnp.concatenate([-sin, sin] * reps, axis=-1))


def kernel(x, c, positions, ada_w, ada_b, norm1_g, norm2_g, w_in, w_out, sg_ln_g, sg_ln_b, sg_w, sg_b,
           cmp_pe_k, cmp_pe_v, cmp_w1_k, cmp_w2_k, cmp_w1_v, cmp_w2_v, router_w, router_b,
           exp_w1, exp_b1, exp_w2, exp_b2, final_g):
    b, s, d = x.shape
    n_layers = ada_w.shape[0]
    mod = _ada_mod(c, ada_w, ada_b)
    cos, sin = _rope_tables(positions)
    cmp_end = jnp.minimum(CMP_STRIDE * jnp.arange(s // CMP_STRIDE) + CMP_BLOCK - 1, s - 1)
    cos_c, sin_c = _rope_tables(positions[:, cmp_end])
    res = None
    for l in range(n_layers):
        sh1, sc1, g1, sh2, sc2, g2 = [m.reshape(b, 1, d) for m in jnp.split(mod[l], 6, axis=-1)]
        x, uv, qt, kc, vc, ks, vs_t, kw, vw_t, gl_t = _pre(x, res, norm1_g[l], sc1, sh1,
                                                          _widen_w_in(w_in[l]), cos, sin)
        y_a = _sgu(uv, sg_ln_g[l], sg_ln_b[l], sg_w[l], sg_b[l])
        kcmp, vcmp_t = _compress(kc, vc, cmp_pe_k[l], cmp_pe_v[l], cmp_w1_k[l], cmp_w2_k[l],
                                 cmp_w1_v[l], cmp_w2_v[l], cos_c, sin_c)
        oc_t, bias_t = _cmp_sel(qt, kcmp, vcmp_t)
        y_b = _attn(qt, ks, kw, vs_t, vw_t, bias_t, oc_t, gl_t)
        x, h2, idx, gate, rank = _post(y_a, y_b, w_out[l], x, g1, norm2_g[l], sc2, sh2,
                                       router_w[l], router_b[l])
        rows = _moe_routed(h2.reshape(b * s, d // 2), idx, rank, l, exp_w1, exp_b1, exp_w2, exp_b2)
        res = (rows.reshape(TOP_K, b, s, d // 2), gate.T.reshape(b, s, TOP_K), g2)
    return _final(x, res, final_g)
```

```python
import functools

import numpy as np
import jax
import jax.numpy as jnp
from jax import lax
from jax.experimental import pallas as pl
from jax.experimental.pallas import tpu as pltpu
from jax.experimental.pallas import tpu_sc as plsc

F32 = jnp.float32
BF16 = jnp.bfloat16
HIGHEST = lax.Precision.HIGHEST

D_MODEL = 1024
A_WIDTH = 512
A_HEADS = 8
CHUNK = 128
B_HEADS = 8
DK = 64
KV_GROUPS = 2
HPG = B_HEADS // KV_GROUPS
CMP_BLOCK = 32
CMP_STRIDE = 16
SEL_BLOCK = 64
SEL_TOP = 16
WINDOW = 512
ROPE_THETA = 10000.0
N_EXPERTS = 32
TOP_K = 4
D_EXPERT = 1024
SWIGLU_LIMIT = 7.0
SWIGLU_ALPHA = 1.702
NORM_EPS = 1e-6

LANES = 128
SUBLANES = 8
GW = HPG * DK
N_FORCED = 3
ONES_ROWS = 16
MASK_BIAS = -32768.0
Q_SCALE = DK ** -0.5 * 1.4426950408889634
VMEM_LIMIT = 56 * 1024 * 1024
ROW_BLOCK = 256
BLOCKS_PER_STEP = 4
SC_CORES = 2
SC_SUBCORES = 16
SC_WORKERS = SC_CORES * SC_SUBCORES
SC_WINDOW = 128
CMP_ROW_CHUNK = 128
SEL_ROW_CHUNK = 32

C_U, C_V, C_Q = 0, 512, 1024
C_KC, C_VC, C_KS, C_VS, C_KW, C_VW = 1536, 1664, 1792, 1920, 2048, 2176
C_GL = 2304
P_WIDE = 2560


def _params(sem):
    return pltpu.CompilerParams(dimension_semantics=sem, vmem_limit_bytes=VMEM_LIMIT)


def _split_bf16(x, parts):
    out = []
    for _ in range(parts):
        piece = x.astype(BF16)
        out.append(piece)
        x = x - piece.astype(F32)
    return out


def _split_bf16_bits(x, parts):
    out = []
    for _ in range(parts):
        piece = lax.bitcast_convert_type(lax.bitcast_convert_type(x, jnp.uint32) & jnp.uint32(0xFFFF0000), F32)
        out.append(piece.astype(BF16))
        x = x - piece
    return out


def _ada_kernel(c_ref, w_ref, b_ref, o_ref):
    c = c_ref[...]
    cond = c * jax.nn.sigmoid(c)
    o_ref[0] = jnp.dot(cond, w_ref[0], precision=HIGHEST, preferred_element_type=F32) + b_ref[0]


def _ada_mod(c, ada_w, ada_b):
    n_layers, d, d6 = ada_w.shape
    b = c.shape[0]
    rows = SUBLANES
    c_pad = jnp.zeros((rows, d), F32).at[:b].set(c)
    out = pl.pallas_call(
        _ada_kernel,
        grid=(n_layers, d6 // d),
        in_specs=[
            pl.BlockSpec((rows, d), lambda l, j: (0, 0)),
            pl.BlockSpec((1, d, d), lambda l, j: (l, 0, j)),
            pl.BlockSpec((1, 1, d), lambda l, j: (l, 0, j)),
        ],
        out_specs=pl.BlockSpec((1, rows, d), lambda l, j: (l, 0, j)),
        out_shape=jax.ShapeDtypeStruct((n_layers, rows, d6), F32),
        compiler_params=_params(("arbitrary", "arbitrary")),
        name="ada",
    )(c_pad, ada_w, ada_b.reshape(n_layers, 1, d6))
    return out[:, :b]


def _rope_slab(t, cos, sin_signed, lo):
    partner = jnp.where(lo, pltpu.roll(t, LANES - DK // 2, 1), pltpu.roll(t, DK // 2, 1))
    return t * cos + partner * sin_signed


def _rope(t, cos, sin_signed):
    lane = lax.broadcasted_iota(jnp.int32, (1, LANES), 1)
    lo = (lane % DK) < (DK // 2)
    slabs = [_rope_slab(t[:, s * LANES:(s + 1) * LANES], cos, sin_signed, lo)
             for s in range(t.shape[1] // LANES)]
    return slabs[0] if len(slabs) == 1 else jnp.concatenate(slabs, axis=1)


def _pack_halves(x):
    n = x.shape[1] // 2
    lo = lax.bitcast_convert_type(x[:, :n].astype(BF16).astype(F32), jnp.int32)
    hi = lax.bitcast_convert_type(x[:, n:].astype(BF16).astype(F32), jnp.int32)
    return lax.shift_right_logical(lo, jnp.int32(16)) | (hi & jnp.int32(-65536))


def _unpack_halves(p):
    lo = lax.bitcast_convert_type(lax.shift_left(p, jnp.int32(16)), F32)
    hi = lax.bitcast_convert_type(p & jnp.int32(-65536), F32)
    return jnp.concatenate([lo, hi], axis=1)


def _moe_residual(x_ref, rows_ref, gate_ref, g2_ref):
    gate = gate_ref[0]
    y = gate[:, 0:1] * _unpack_halves(rows_ref[0, 0])
    for k in range(1, TOP_K):
        y = y + gate[:, k:k + 1] * _unpack_halves(rows_ref[k, 0])
    return x_ref[0] + g2_ref[0] * y


def _moe_residual_specs(tm, d):
    return [pl.BlockSpec((TOP_K, 1, tm, d // 2), lambda i, j: (0, i, j, 0)),
            pl.BlockSpec((1, tm, TOP_K), lambda i, j: (i, j, 0)),
            pl.BlockSpec((1, 1, d), lambda i, j: (i, 0, 0))]


def _pre_kernel(has_res, *refs):
    if has_res:
        (x_ref, rows_ref, gate_ref, g2_ref, ng_ref, sc_ref, sh_ref, w_ref, cos_ref, sin_ref,
         lng_ref, lnb_ref, sgw_ref, sgb_ref,
         xo_ref, ya_ref, qt_ref, kc_ref, vc_ref, ks_ref, vst_ref, kw_ref, vwt_ref, glt_ref) = refs
        x = _moe_residual(x_ref, rows_ref, gate_ref, g2_ref)
        xo_ref[0] = x
    else:
        (x_ref, ng_ref, sc_ref, sh_ref, w_ref, cos_ref, sin_ref, lng_ref, lnb_ref, sgw_ref, sgb_ref,
         ya_ref, qt_ref, kc_ref, vc_ref, ks_ref, vst_ref, kw_ref, vwt_ref, glt_ref) = refs
        x = x_ref[0]
    ms = jnp.mean(x * x, axis=-1, keepdims=True)
    h = x * lax.rsqrt(ms + NORM_EPS) * ng_ref[...]
    h = h * (1.0 + sc_ref[0]) + sh_ref[0]
    proj = jnp.dot(h.astype(BF16), w_ref[...], preferred_element_type=F32)
    cos = cos_ref[0]
    sin = sin_ref[0]
    _sgu_tile(proj[:, C_U:C_Q], lng_ref, lnb_ref, sgw_ref, sgb_ref, ya_ref)
    qt_ref[0] = (_rope(proj[:, C_Q:C_KC], cos, sin) * Q_SCALE).T
    kc_ref[0] = proj[:, C_KC:C_VC]
    vc_ref[0] = proj[:, C_VC:C_KS]
    ks_ref[0] = _rope(proj[:, C_KS:C_VS], cos, sin).astype(BF16)
    vst_ref[0] = proj[:, C_VS:C_KW].T.astype(BF16)
    kw_ref[0] = _rope(proj[:, C_KW:C_VW], cos, sin).astype(BF16)
    vwt_ref[0] = proj[:, C_VW:C_GL].T.astype(BF16)
    glt_ref[0] = proj[:, C_GL:P_WIDE].T


def _pre(x, res, norm_g, sc, sh, w_wide, cos, sin, sgu_params, tm=512):
    b, s, d = x.shape
    tm = min(tm, s)
    assert tm % CHUNK == 0
    ln_g, ln_b, w_s, b_s = sgu_params
    sg_bias = jnp.repeat(b_s.T, A_WIDTH // A_HEADS, axis=1)
    row = lambda w: pl.BlockSpec((1, tm, w), lambda i, j: (i, j, 0))
    col = lambda w: pl.BlockSpec((1, w, tm), lambda i, j: (i, 0, j))
    vec = pl.BlockSpec((1, 1, d), lambda i, j: (i, 0, 0))
    in_specs = [row(d)]
    args = [x]
    if res is not None:
        in_specs += _moe_residual_specs(tm, d)
        args += list(res)
    in_specs += [pl.BlockSpec((1, d), lambda i, j: (0, 0)), vec, vec,
                 pl.BlockSpec((d, P_WIDE), lambda i, j: (0, 0)), row(LANES), row(LANES),
                 pl.BlockSpec((1, A_WIDTH), lambda i, j: (0, 0)), pl.BlockSpec((1, A_WIDTH), lambda i, j: (0, 0)),
                 pl.BlockSpec((A_HEADS, CHUNK, CHUNK), lambda i, j: (0, 0, 0)),
                 pl.BlockSpec((CHUNK, A_WIDTH), lambda i, j: (0, 0))]
    args += [norm_g.reshape(1, d), sc, sh, w_wide, cos, sin,
             ln_g.reshape(1, -1), ln_b.reshape(1, -1), w_s, sg_bias]
    kv = KV_GROUPS * DK
    outs = [(d, F32, False)] if res is not None else []
    outs += [(A_WIDTH, BF16, False), (B_HEADS * DK, F32, True), (kv, F32, False), (kv, F32, False),
             (kv, BF16, False), (kv, BF16, True), (kv, BF16, False), (kv, BF16, True),
             (KV_GROUPS * LANES, F32, True)]
    res_out = pl.pallas_call(
        functools.partial(_pre_kernel, res is not None),
        grid=(b, s // tm),
        in_specs=in_specs,
        out_specs=[col(w) if t else row(w) for w, _, t in outs],
        out_shape=[jax.ShapeDtypeStruct((b, w, s) if t else (b, s, w), dt) for w, dt, t in outs],
        compiler_params=_params(("parallel", "parallel")),
        name="pre",
    )(*args)
    if res is None:
        res_out = [x] + list(res_out)
    return res_out


def _sgu_tile(uv, lng_ref, lnb_ref, w_ref, bias_ref, o_ref):
    rows = uv.shape[0]
    gu = jax.nn.gelu(uv[:, :A_WIDTH])
    gv = jax.nn.gelu(uv[:, A_WIDTH:])
    mu = jnp.mean(gv, axis=-1, keepdims=True)
    var = jnp.mean(jnp.square(gv - mu), axis=-1, keepdims=True)
    vn = ((gv - mu) * lax.rsqrt(var + NORM_EPS) * lng_ref[...] + lnb_ref[...]).astype(BF16)
    r = lax.broadcasted_iota(jnp.int32, (CHUNK, CHUNK), 0)
    c = lax.broadcasted_iota(jnp.int32, (CHUNK, CHUNK), 1)
    causal = c <= r
    lane_lo = lax.broadcasted_iota(jnp.int32, (CHUNK, LANES), 1) < DK
    for p in range(A_HEADS // 2):
        w0 = jnp.where(causal, w_ref[2 * p], 0.0).astype(BF16)
        w1 = jnp.where(causal, w_ref[2 * p + 1], 0.0).astype(BF16)
        bias = bias_ref[:, p * LANES:(p + 1) * LANES]
        for ch in range(rows // CHUNK):
            rs = slice(ch * CHUNK, (ch + 1) * CHUNK)
            cs = slice(p * LANES, (p + 1) * LANES)
            vp = vn[rs, cs]
            m0 = jnp.dot(w0, vp, preferred_element_type=F32)
            m1 = jnp.dot(w1, vp, preferred_element_type=F32)
            mixed = jnp.where(lane_lo, m0, m1) + bias
            o_ref[0, rs, cs] = (gu[rs, cs] * mixed).astype(o_ref.dtype)


def _compress_kernel(kr_ref, vr_ref, pek_ref, pev_ref, w1k_ref, w1v_ref, w2k_ref, w2v_ref,
                     cos_ref, sin_ref, ko_ref, vt_ref):
    nc = kr_ref.shape[1] // CMP_STRIDE

    def row_groups(t_ref):
        return jnp.concatenate([t_ref[0, pl.ds(l, nc, stride=CMP_STRIDE), :] for l in range(CMP_STRIDE)], axis=1)

    def mlp(r, pe_ref, w1_ref, w2_ref):
        top = jnp.dot(r + pe_ref[0:1], w1_ref[0], precision=HIGHEST, preferred_element_type=F32)
        bot = jnp.dot(r + pe_ref[1:2], w1_ref[1], precision=HIGHEST, preferred_element_type=F32)
        pre = top + pltpu.roll(bot, nc - 1, 0)
        return jnp.dot(jax.nn.gelu(pre), w2_ref[...], precision=HIGHEST, preferred_element_type=F32)

    kc = _rope(mlp(row_groups(kr_ref), pek_ref, w1k_ref, w2k_ref), cos_ref[0], sin_ref[0])
    vc = mlp(row_groups(vr_ref), pev_ref, w1v_ref, w2v_ref)
    lo = lax.broadcasted_iota(jnp.int32, (nc, LANES), 1) < DK
    rolled = pltpu.roll(kc, DK, 1)
    for g in range(KV_GROUPS):
        dup = jnp.where(lo, kc, rolled) if g == 0 else jnp.where(lo, rolled, kc)
        hi, low = _split_bf16(dup, 2)
        ko_ref[0, g, :, 0:LANES] = hi
        ko_ref[0, g, :, LANES:2 * LANES] = jnp.where(lo, low, jnp.zeros_like(low))
    vt_ref[0] = vc.T.astype(BF16)


def _compress_weights(pe, w1, w2):
    half = CMP_BLOCK // 2
    eye = jnp.eye(KV_GROUPS, dtype=F32)
    w1r = w1.reshape(CMP_BLOCK, DK, DK)
    wfull = jnp.einsum('lde,gh->lgdhe', w1r, eye)
    w1s = wfull.reshape(2, half * KV_GROUPS * DK, KV_GROUPS * DK)
    pes = jnp.broadcast_to(pe.reshape(2, half, 1, DK), (2, half, KV_GROUPS, DK)).reshape(2, -1)
    w2bd = jnp.einsum('de,gh->gdhe', w2, eye).reshape(KV_GROUPS * DK, KV_GROUPS * DK)
    return pes, w1s, w2bd


def _compress(kc, vc, pe_k, pe_v, w1_k, w2_k, w1_v, w2_v, cos_c, sin_c):
    b, s, _ = kc.shape
    nc = s // CMP_STRIDE
    rw = CMP_STRIDE * LANES
    pek, w1ks, w2kb = _compress_weights(pe_k, w1_k, w2_k)
    pev, w1vs, w2vb = _compress_weights(pe_v, w1_v, w2_v)
    full = lambda shape: pl.BlockSpec(shape, lambda i: (0,) * len(shape))
    return pl.pallas_call(
        _compress_kernel,
        grid=(b,),
        in_specs=[
            pl.BlockSpec((1, s, LANES), lambda i: (i, 0, 0)),
            pl.BlockSpec((1, s, LANES), lambda i: (i, 0, 0)),
            full((2, rw)), full((2, rw)),
            full((2, rw, LANES)), full((2, rw, LANES)),
            full((LANES, LANES)), full((LANES, LANES)),
            pl.BlockSpec((1, nc, LANES), lambda i: (i, 0, 0)),
            pl.BlockSpec((1, nc, LANES), lambda i: (i, 0, 0)),
        ],
        out_specs=[pl.BlockSpec((1, KV_GROUPS, nc, 2 * LANES), lambda i: (i, 0, 0, 0)),
                   pl.BlockSpec((1, KV_GROUPS * DK, nc), lambda i: (i, 0, 0))],
        out_shape=[jax.ShapeDtypeStruct((b, KV_GROUPS, nc, 2 * LANES), BF16),
                   jax.ShapeDtypeStruct((b, KV_GROUPS * DK, nc), BF16)],
        compiler_params=_params(("parallel",)),
        name="compress",
    )(kc, vc, pek, pev, w1ks, w1vs, w2kb, w2vb, cos_c, sin_c)


def _heads_on_lanes(qt):
    return jnp.concatenate([qt[h * DK:(h + 1) * DK] for h in range(HPG)], axis=1)


def _cmp_sel_kernel(n_top, qt_ref, k_ref, vt_ref, map_ref, oc_ref, bias_ref, imp_ref):
    tq = qt_ref.shape[2]
    cols = HPG * tq
    nc = k_ref.shape[2]
    nsel = map_ref.shape[0]
    q0 = pl.program_id(2) * tq
    q_hi, q_lo = _split_bf16(_heads_on_lanes(qt_ref[0]), 2)
    q3t = jnp.concatenate([q_hi, q_lo, q_hi, jnp.zeros_like(q_hi)], axis=0)
    t_col = q0 + lax.broadcasted_iota(jnp.int32, (1, cols), 1) % tq

    def compressed_branch(rows):
        s = jnp.dot(k_ref[0, 0, 0:rows], q3t, preferred_element_type=F32)
        n_idx = lax.broadcasted_iota(jnp.int32, (rows, 1), 0)
        s = jnp.where((CMP_STRIDE * n_idx + CMP_BLOCK - 1) <= t_col, s, -jnp.inf)
        m = jnp.max(s, axis=0, keepdims=True)
        m = jnp.where(m == -jnp.inf, 0.0, m)
        e = jnp.exp2(s - m)
        d = jnp.sum(e, axis=0, keepdims=True)
        p = e / jnp.where(d > 0, d, 1.0)
        oc = jnp.dot(vt_ref[0, :, 0:rows], p.astype(BF16), preferred_element_type=F32)
        oc_ref[0] = jnp.concatenate([oc[:, h * tq:(h + 1) * tq] for h in range(HPG)], axis=0)
        psum = p[:, 0:tq]
        for h in range(1, HPG):
            psum = psum + p[:, h * tq:(h + 1) * tq]
        imp_ref[...] = jnp.dot(jnp.concatenate([map_ref[:, 0:rows]] * 3, axis=1),
                               jnp.concatenate(_split_bf16(psum, 3), axis=0), preferred_element_type=F32)

    visible = (q0 + tq - CMP_BLOCK) // CMP_STRIDE + 1
    n_chunks = -(-nc // CMP_ROW_CHUNK)
    for r in range(1, n_chunks + 1):
        lo = (r - 1) * CMP_ROW_CHUNK if r > 1 else -nc
        pl.when((visible > lo) & (visible <= r * CMP_ROW_CHUNK) if r < n_chunks else visible > lo)(
            functools.partial(compressed_branch, min(r * CMP_ROW_CHUNK, nc)))

    def select(rows):
        imp = imp_ref[0:rows]
        j = lax.broadcasted_iota(jnp.int32, (rows, tq), 0)
        cur = (q0 + lax.broadcasted_iota(jnp.int32, (rows, tq), 1)) // SEL_BLOCK
        valid = j <= cur
        forced = (j == 0) | (j == cur) | (j == cur - 1)
        keep = forced | (valid & (cur < n_top))
        vals = jnp.where(valid & jnp.logical_not(forced), imp, -jnp.inf)
        bias = jnp.where(keep, 0.0, MASK_BIAS)
        for _ in range(n_top - N_FORCED):
            mx = jnp.max(vals, axis=0, keepdims=True)
            first = jnp.min(jnp.where(vals == mx, j, rows), axis=0, keepdims=True)
            pick = (j == first) & (mx > -jnp.inf)
            bias = jnp.where(pick, 0.0, bias)
            vals = jnp.where(pick, -jnp.inf, vals)
        bias_ref[0, 0, 0:rows] = bias.astype(BF16)
        if rows < nsel:
            bias_ref[0, 0, rows:nsel] = jnp.full((nsel - rows, tq), MASK_BIAS, BF16)

    n_valid = (q0 + tq - 1) // SEL_BLOCK + 1
    n_sel_chunks = -(-nsel // SEL_ROW_CHUNK)
    for r in range(1, n_sel_chunks + 1):
        lo = (r - 1) * SEL_ROW_CHUNK if r > 1 else -nsel
        pl.when((n_valid > lo) & (n_valid <= r * SEL_ROW_CHUNK) if r < n_sel_chunks else n_valid > lo)(
            functools.partial(select, min(r * SEL_ROW_CHUNK, nsel)))


def _sel_map_t(s):
    nc = s // CMP_STRIDE
    n_cmp = (s - CMP_BLOCK) // CMP_STRIDE + 1
    n_sel = s // SEL_BLOCK
    cs = CMP_STRIDE * np.arange(n_cmp)[:, None]
    ce = cs + CMP_BLOCK
    ss = SEL_BLOCK * np.arange(n_sel)[None, :]
    se = ss + SEL_BLOCK
    ov = np.clip(np.minimum(ce, se) - np.maximum(cs, ss), 0, None) / CMP_STRIDE
    out = np.zeros((n_sel, nc), np.float32)
    out[:, :n_cmp] = ov.T
    return out


def _cmp_sel(qt, kcmp, vcmp_t, tq=256):
    b, _, s = qt.shape
    nc = s // CMP_STRIDE
    nsel = s // SEL_BLOCK
    n_top = min(SEL_TOP, nsel)
    sel_map = jnp.asarray(_sel_map_t(s), dtype=BF16)
    return pl.pallas_call(
        functools.partial(_cmp_sel_kernel, n_top),
        grid=(b, KV_GROUPS, s // tq),
        in_specs=[
            pl.BlockSpec((1, GW, tq), lambda i, g, j: (i, g, j)),
            pl.BlockSpec((1, 1, nc, 2 * LANES), lambda i, g, j: (i, g, 0, 0)),
            pl.BlockSpec((1, DK, nc), lambda i, g, j: (i, g, 0)),
            pl.BlockSpec((nsel, nc), lambda i, g, j: (0, 0)),
        ],
        out_specs=[
            pl.BlockSpec((1, GW, tq), lambda i, g, j: (i, g, j)),
            pl.BlockSpec((1, 1, nsel, tq), lambda i, g, j: (i, g, 0, j)),
        ],
        out_shape=[
            jax.ShapeDtypeStruct((b, B_HEADS * DK, s), F32),
            jax.ShapeDtypeStruct((b, KV_GROUPS, nsel, s), BF16),
        ],
        scratch_shapes=[pltpu.VMEM((nsel, tq), F32)],
        compiler_params=_params(("parallel", "parallel", "parallel")),
        name="cmp_sel",
    )(qt, kcmp, vcmp_t, sel_map)


def _attn_kernel(kt, qt_ref, ks_ref, kw_ref, vst_ref, vwt_ref, bias_ref, oh_ref, oc_ref, glt_ref,
                 o_ref, qa_ref, s0_ref, s1_ref, m_ref, acc_ref, ow_ref):
    tq = qt_ref.shape[2]
    cols = HPG * tq
    nsel = bias_ref.shape[2]
    g = pl.program_id(1)
    q0 = pl.program_id(2) * tq

    q4t = _heads_on_lanes(qt_ref[0]).astype(BF16)
    zero = jnp.zeros_like(q4t)
    qa_ref[0:DK] = jnp.where(g == 0, q4t, zero)
    qa_ref[DK:2 * DK] = jnp.where(g == 1, q4t, zero)
    qa_ref[LANES:LANES + nsel] = jnp.concatenate([bias_ref[0, 0]] * HPG, axis=1)
    qaug = qa_ref[...]
    t_col = q0 + lax.broadcasted_iota(jnp.int32, (1, cols), 1) % tq

    def fold(x, op):
        return op(x.reshape(x.shape[0] // SUBLANES, SUBLANES, cols), axis=0)

    def with_ones(vt):
        return jnp.concatenate([vt, jnp.ones((ONES_ROWS, vt.shape[1]), BF16)], axis=0)

    def stage(c, s_ref):
        k0 = pl.multiple_of(c * kt, kt)
        kaug = jnp.concatenate([ks_ref[0, pl.ds(k0, kt), :], oh_ref[pl.ds(k0, kt), :]], axis=1)
        s_ref[...] = jnp.dot(kaug, qaug, preferred_element_type=F32)
        r0 = pl.multiple_of(jnp.clip(q0 - k0, 0, kt - tq), tq)
        kpos = k0 + r0 + lax.broadcasted_iota(jnp.int32, (tq, 1), 0)
        s_ref[pl.ds(r0, tq), :] = jnp.where(kpos <= t_col, s_ref[pl.ds(r0, tq), :], -jnp.inf)

    def consume(c, s_ref):
        k0 = pl.multiple_of(c * kt, kt)
        sc = s_ref[...]
        m_old = m_ref[...]
        m_new = jnp.maximum(m_old, jnp.max(fold(sc, jnp.max), axis=0, keepdims=True))
        alpha = jnp.exp2(m_old - m_new)
        pr = jnp.exp2(sc - m_new)
        acc_ref[...] = alpha * acc_ref[...] + jnp.dot(with_ones(vst_ref[0, :, pl.ds(k0, kt)]), pr.astype(BF16),
                                                      preferred_element_type=F32)
        m_ref[...] = m_new

    m_ref[...] = jnp.full(m_ref.shape, -jnp.inf, F32)
    acc_ref[...] = jnp.zeros(acc_ref.shape, F32)
    n_full = q0 // kt
    n_pairs = n_full // 2
    odd = n_full % 2 == 1
    stage(0, s0_ref)

    wlen = WINDOW + tq
    w0 = pl.multiple_of(jnp.maximum(q0 - WINDOW, 0), tq)
    sw = jnp.dot(kw_ref[0, pl.ds(w0, wlen), :], qaug[0:LANES], preferred_element_type=F32)
    dlt = t_col - (w0 + lax.broadcasted_iota(jnp.int32, (wlen, 1), 0))
    sw = jnp.where(lax.bitcast_convert_type(dlt, jnp.uint32) < jnp.uint32(WINDOW), sw, -jnp.inf)
    pw = jnp.exp2(sw - jnp.max(sw, axis=0, keepdims=True))
    o_win = jnp.dot(with_ones(vwt_ref[0, :, pl.ds(w0, wlen)]), pw.astype(BF16), preferred_element_type=F32)
    ow_ref[...] = o_win[0:DK] / o_win[DK:DK + 1]

    def pair(i, carry):
        stage(2 * i + 1, s1_ref)
        consume(2 * i, s0_ref)
        stage(2 * i + 2, s0_ref)
        consume(2 * i + 1, s1_ref)
        return carry

    lax.fori_loop(0, n_pairs, pair, 0)

    @pl.when(odd)
    def _():
        stage(n_full, s1_ref)
        consume(n_full - 1, s0_ref)
        consume(n_full, s1_ref)

    @pl.when(jnp.logical_not(odd))
    def _():
        consume(n_full, s0_ref)

    o_sel = acc_ref[0:DK] / acc_ref[DK:DK + 1]

    o_win = ow_ref[...]
    gates = jax.nn.sigmoid(glt_ref[0])
    oc = _heads_on_lanes(oc_ref[0])
    for p in range(HPG // 2):
        halves = []
        for hh in (2 * p, 2 * p + 1):
            cs = slice(hh * tq, (hh + 1) * tq)
            halves.append(gates[3 * hh:3 * hh + 1] * oc[:, cs] + gates[3 * hh + 1:3 * hh + 2] * o_sel[:, cs]
                          + gates[3 * hh + 2:3 * hh + 3] * o_win[:, cs])
        o_ref[0, :, p * LANES:(p + 1) * LANES] = jnp.concatenate(halves, axis=0).T.astype(o_ref.dtype)


def _attn(qt, ks, kw, vs_t, vw_t, bias_t, oc_t, gl_t, tq=256, kt=512):
    b, _, s = qt.shape
    nsel = s // SEL_BLOCK
    kt = min(kt, s)
    onehot = jnp.asarray((np.arange(s)[:, None] // SEL_BLOCK == np.arange(nsel)[None, :]), dtype=BF16)
    k_spec = pl.BlockSpec((1, s, KV_GROUPS * DK), lambda i, g, j: (i, 0, 0))
    vt_spec = pl.BlockSpec((1, DK, s), lambda i, g, j: (i, g, 0))
    cols = HPG * tq
    return pl.pallas_call(
        functools.partial(_attn_kernel, kt),
        grid=(b, KV_GROUPS, s // tq),
        in_specs=[
            pl.BlockSpec((1, GW, tq), lambda i, g, j: (i, g, j)),
            k_spec, k_spec, vt_spec, vt_spec,
            pl.BlockSpec((1, 1, nsel, tq), lambda i, g, j: (i, g, 0, j)),
            pl.BlockSpec((s, nsel), lambda i, g, j: (0, 0)),
            pl.BlockSpec((1, GW, tq), lambda i, g, j: (i, g, j)),
            pl.BlockSpec((1, LANES, tq), lambda i, g, j: (i, g, j)),
        ],
        out_specs=pl.BlockSpec((1, tq, GW), lambda i, g, j: (i, j, g)),
        out_shape=jax.ShapeDtypeStruct((b, s, B_HEADS * DK), BF16),
        scratch_shapes=[pltpu.VMEM((LANES + nsel, cols), BF16),
                        pltpu.VMEM((kt, cols), F32), pltpu.VMEM((kt, cols), F32),
                        pltpu.VMEM((1, cols), F32), pltpu.VMEM((DK + ONES_ROWS, cols), F32),
                        pltpu.VMEM((DK, cols), F32)],
        compiler_params=_params(("parallel", "parallel", "arbitrary")),
        name="attn",
    )(qt, ks, kw, vs_t, vw_t, bias_t, onehot, oc_t, gl_t)


def _post_kernel(ya_ref, yb_ref, wo_ref, x_ref, g1_ref, ng_ref, sc_ref, sh_ref, rw_ref, rb_ref,
                 xo_ref, h_ref, idx_ref, gate_ref, rank_ref, cnt_ref):
    tm = x_ref.shape[1]

    @pl.when((pl.program_id(0) == 0) & (pl.program_id(1) == 0))
    def _():
        cnt_ref[...] = jnp.zeros(cnt_ref.shape, F32)

    mixed = jnp.dot(ya_ref[0], wo_ref[:A_WIDTH], preferred_element_type=F32)
    mixed = mixed + jnp.dot(yb_ref[0], wo_ref[A_WIDTH:], preferred_element_type=F32)
    x = x_ref[0] + g1_ref[0] * mixed
    xo_ref[0] = x
    ms = jnp.mean(x * x, axis=-1, keepdims=True)
    h = x * lax.rsqrt(ms + NORM_EPS) * ng_ref[...]
    h = h * (1.0 + sc_ref[0]) + sh_ref[0]
    h_ref[0] = _pack_halves(h)
    h_hi, h_mid = _split_bf16(h, 2)
    logits = jnp.dot(jnp.concatenate([h_hi, h_hi, h_mid], axis=1), rw_ref[...],
                     preferred_element_type=F32) + rb_ref[...]
    vals = logits.T[:N_EXPERTS]
    e_idx = lax.broadcasted_iota(jnp.int32, (N_EXPERTS, tm), 0)
    top = jnp.max(vals, axis=0, keepdims=True)
    routed_t = jnp.zeros_like(vals)
    firsts, exps = [], []
    for k in range(TOP_K):
        m = jnp.max(vals, axis=0, keepdims=True)
        first = jnp.min(jnp.where(vals == m, e_idx, N_EXPERTS), axis=0, keepdims=True)
        pick = e_idx == first
        routed_t = jnp.where(pick, 1.0, routed_t)
        firsts.append(first)
        exps.append(jnp.exp(m - top))
        vals = jnp.where(pick, -jnp.inf, vals)
    idx_i = jnp.concatenate(firsts, axis=0)
    idx_t = idx_i.astype(F32)
    slot_exp = jnp.concatenate(exps, axis=0)
    idx_ref[...] = idx_i
    gate_ref[...] = slot_exp / jnp.sum(slot_exp, axis=0, keepdims=True)
    r = lax.broadcasted_iota(jnp.int32, (tm, tm), 0)
    c = lax.broadcasted_iota(jnp.int32, (tm, tm), 1)
    before = jnp.where(r < c, 1.0, 0.0).astype(BF16)
    rank_t = cnt_ref[...] + jnp.dot(routed_t.astype(BF16), before, preferred_element_type=F32)
    cnt_ref[...] += jnp.sum(routed_t, axis=-1, keepdims=True)
    e_row = lax.broadcasted_iota(jnp.int32, (N_EXPERTS, tm), 0).astype(F32)
    ranks = [jnp.sum(jnp.where(e_row == idx_t[k:k + 1], rank_t, 0.0), axis=0, keepdims=True)
             for k in range(TOP_K)]
    rank_ref[...] = jnp.concatenate(ranks, axis=0).astype(jnp.int32)


def _post(y_a, y_b, w_out, x, g1, norm_g, sc, sh, router_w, router_b, tm=512):
    b, s, d = x.shape
    tm = min(tm, s)
    nt = s // tm
    w_hi, w_mid = _split_bf16_bits(jnp.zeros((d, LANES), F32).at[:, :N_EXPERTS].set(router_w), 2)
    rw = jnp.concatenate([w_hi, w_mid, w_hi], axis=0)
    rb = jnp.full((1, LANES), -1e30, F32).at[0, :N_EXPERTS].set(router_b)
    vec = pl.BlockSpec((1, 1, d), lambda i, j: (i, 0, 0))
    row = lambda w: pl.BlockSpec((1, tm, w), lambda i, j: (i, j, 0))
    return pl.pallas_call(
        _post_kernel,
        grid=(b, nt),
        in_specs=[
            row(A_WIDTH), row(B_HEADS * DK),
            pl.BlockSpec((d, d), lambda i, j: (0, 0)),
            row(d), vec,
            pl.BlockSpec((1, d), lambda i, j: (0, 0)), vec, vec,
            pl.BlockSpec((3 * d, LANES), lambda i, j: (0, 0)),
            pl.BlockSpec((1, LANES), lambda i, j: (0, 0)),
        ],
        out_specs=[row(d), row(d // 2)] + [pl.BlockSpec((TOP_K, tm), lambda i, j: (0, i * nt + j))] * 3,
        out_shape=[
            jax.ShapeDtypeStruct((b, s, d), F32),
            jax.ShapeDtypeStruct((b, s, d // 2), jnp.int32),
            jax.ShapeDtypeStruct((TOP_K, b * s), jnp.int32),
            jax.ShapeDtypeStruct((TOP_K, b * s), F32),
            jax.ShapeDtypeStruct((TOP_K, b * s), jnp.int32),
        ],
        scratch_shapes=[pltpu.VMEM((N_EXPERTS, 1), F32)],
        compiler_params=_params(("arbitrary", "arbitrary")),
        name="post",
    )(y_a, y_b, w_out.astype(BF16), x, g1, norm_g.reshape(1, d), sc, sh, rw, rb)


def _sc_mesh():
    return plsc.VectorSubcoreMesh(core_axis_name="c", subcore_axis_name="s")


def _sc_worker():
    return lax.axis_index("c") * SC_SUBCORES + lax.axis_index("s")


def _dispatch_rows(h2, dest_c, n_rows):
    t, d = h2.shape
    per_worker = t // SC_WINDOW // SC_WORKERS

    def body(x_hbm, i_hbm, o_hbm, buf, idx):
        worker = _sc_worker()

        @pl.loop(0, per_worker)
        def _(j):
            ch = worker * per_worker + j
            pltpu.sync_copy(i_hbm.at[ch], idx)
            pltpu.sync_copy(x_hbm.at[pl.ds(ch * SC_WINDOW, SC_WINDOW)], buf)
            for k in range(TOP_K):
                pltpu.sync_copy(buf, o_hbm.at[idx.at[k]])

    return pl.kernel(
        body, out_type=jax.ShapeDtypeStruct((n_rows, d), h2.dtype), mesh=_sc_mesh(),
        scratch_types=[pltpu.VMEM((SC_WINDOW, d), h2.dtype), pltpu.VMEM((TOP_K, SC_WINDOW), jnp.int32)],
        name="dispatch_rows",
    )(h2, dest_c)


def _collect_rows(rows, dest_c, t):
    d = rows.shape[1]
    per_worker = t // SC_WINDOW // SC_WORKERS

    def body(r_hbm, i_hbm, o_hbm, buf, idx):
        worker = _sc_worker()

        @pl.loop(0, per_worker)
        def _(j):
            ch = worker * per_worker + j
            pltpu.sync_copy(i_hbm.at[ch], idx)
            for k in range(TOP_K):
                pltpu.sync_copy(r_hbm.at[idx.at[k]], buf)
                pltpu.sync_copy(buf, o_hbm.at[k, pl.ds(ch * SC_WINDOW, SC_WINDOW)])

    return pl.kernel(
        body, out_type=jax.ShapeDtypeStruct((TOP_K, t, d), rows.dtype), mesh=_sc_mesh(),
        scratch_types=[pltpu.VMEM((SC_WINDOW, d), rows.dtype), pltpu.VMEM((TOP_K, SC_WINDOW), jnp.int32)],
        name="collect_rows",
    )(rows, dest_c)


def _experts_kernel(layer, be_ref, nb_ref, slot_ref, next_ref, x_ref, w1_hbm, b1_ref, w2_hbm, b2_ref, o_ref,
                    w1f_ref, w2f_ref, w1b_ref, w2b_ref, sem):
    def weight_copies(expert, slot):
        return (pltpu.make_async_copy(w1_hbm.at[layer, expert], w1f_ref.at[slot], sem.at[slot, 0]),
                pltpu.make_async_copy(w2_hbm.at[layer, expert], w2f_ref.at[slot], sem.at[slot, 1]))

    def row_block(i, rs):
        e = be_ref[i]
        live = i < nb_ref[0]

        @pl.when(live & (i == 0))
        def _():
            for cp in weight_copies(e, 0):
                cp.start()

        @pl.when(live & ((i == 0) | (e != be_ref[jnp.maximum(i - 1, 0)])))
        def _():
            slot = slot_ref[i]
            for cp in weight_copies(e, slot):
                cp.wait()
            w1b_ref[...] = w1f_ref[slot].astype(BF16)
            w2b_ref[...] = w2f_ref[slot].astype(BF16)

            @pl.when(next_ref[i] >= 0)
            def _():
                for cp in weight_copies(next_ref[i], 1 - slot):
                    cp.start()

        @pl.when(live)
        def _():
            x = _unpack_halves(x_ref[rs, :]).astype(BF16)
            hdn = jnp.dot(x, w1b_ref[...], preferred_element_type=F32) + b1_ref[0, pl.ds(e, 1), :]
            glu = jnp.minimum(hdn[:, :D_EXPERT], SWIGLU_LIMIT)
            lin = jnp.clip(hdn[:, D_EXPERT:], -SWIGLU_LIMIT, SWIGLU_LIMIT)
            act = glu * jax.nn.sigmoid(SWIGLU_ALPHA * glu) * (lin + 1.0)
            out = jnp.dot(act.astype(BF16), w2b_ref[...], preferred_element_type=F32) + b2_ref[0, pl.ds(e, 1), :]
            o_ref[rs, :] = _pack_halves(out)

    for h in range(BLOCKS_PER_STEP):
        row_block(pl.program_id(0) * BLOCKS_PER_STEP + h, slice(h * ROW_BLOCK, (h + 1) * ROW_BLOCK))


def _experts(rows, block_exp, n_live, slot, next_exp, layer, w1, b1, w2, b2):
    n_rows, dp = rows.shape
    n_layers, n_e, d, f2 = w1.shape
    n_blocks = n_rows // ROW_BLOCK
    assert n_blocks % BLOCKS_PER_STEP == 0
    step_rows = BLOCKS_PER_STEP * ROW_BLOCK
    this_layer = lambda i, be, nb, sl, nx: (layer, 0, 0)
    grid_spec = pltpu.PrefetchScalarGridSpec(
        num_scalar_prefetch=4,
        grid=(n_blocks // BLOCKS_PER_STEP,),
        in_specs=[
            pl.BlockSpec((step_rows, dp), lambda i, be, nb, sl, nx: (i, 0)),
            pl.BlockSpec(memory_space=pl.ANY),
            pl.BlockSpec((1, n_e, f2), this_layer),
            pl.BlockSpec(memory_space=pl.ANY),
            pl.BlockSpec((1, n_e, d), this_layer),
        ],
        out_specs=pl.BlockSpec((step_rows, dp), lambda i, be, nb, sl, nx: (i, 0)),
        scratch_shapes=[pltpu.VMEM((2, d, f2), F32), pltpu.VMEM((2, f2 // 2, d), F32),
                        pltpu.VMEM((d, f2), BF16), pltpu.VMEM((f2 // 2, d), BF16),
                        pltpu.SemaphoreType.DMA((2, 2))],
    )
    return pl.pallas_call(
        functools.partial(_experts_kernel, layer),
        grid_spec=grid_spec,
        out_shape=jax.ShapeDtypeStruct((n_rows, dp), jnp.int32),
        compiler_params=_params(("arbitrary",)),
        name="experts",
    )(block_exp, n_live, slot, next_exp, rows, w1, b1, w2, b2)


def _moe_routed(h2, idx, rank, layer, w1, b1, w2, b2):
    t, d = h2.shape
    n_e = w1.shape[1]
    n_blocks = t * TOP_K // ROW_BLOCK + n_e
    onehot = (idx[:, :, None] == jnp.arange(n_e, dtype=jnp.int32)).astype(jnp.int32)
    counts = jnp.sum(onehot, axis=(0, 1))
    padded = (counts + ROW_BLOCK - 1) // ROW_BLOCK * ROW_BLOCK
    pad_end = jnp.cumsum(padded)
    dest = jnp.sum(onehot * (pad_end - padded), axis=-1) + rank
    block_row = jnp.arange(n_blocks, dtype=jnp.int32)[:, None] * ROW_BLOCK
    block_exp = jnp.minimum(jnp.sum((pad_end[None, :] <= block_row).astype(jnp.int32), axis=-1), n_e - 1)
    n_live = (pad_end[-1:] // ROW_BLOCK).astype(jnp.int32)
    experts = jnp.arange(n_e, dtype=jnp.int32)
    later = (experts[None, :] > experts[:, None]) & (counts[None, :] > 0)
    next_of = jnp.min(jnp.where(later, experts[None, :], n_e), axis=-1)
    next_of = jnp.where(next_of == n_e, -1, next_of)
    block_onehot = (block_exp[:, None] == experts[None, :]).astype(jnp.int32)
    next_exp = jnp.sum(block_onehot * next_of, axis=-1)
    run_of = jnp.cumsum((counts > 0).astype(jnp.int32)) - 1
    slot = jnp.sum(block_onehot * run_of, axis=-1) % 2
    dest_c = dest.reshape(TOP_K, t // SC_WINDOW, SC_WINDOW).transpose(1, 0, 2)
    rows = _dispatch_rows(h2, dest_c, n_blocks * ROW_BLOCK)
    out = _experts(rows, block_exp, n_live, slot, next_exp, layer, w1, b1, w2, b2)
    return _collect_rows(out, dest_c, t)


def _final_kernel(x_ref, rows_ref, gate_ref, g2_ref, fg_ref, o_ref):
    x = _moe_residual(x_ref, rows_ref, gate_ref, g2_ref)
    ms = jnp.mean(x * x, axis=-1, keepdims=True)
    o_ref[0] = x * lax.rsqrt(ms + NORM_EPS) * fg_ref[...]


def _final(x, res, final_g, tm=512):
    b, s, d = x.shape
    tm = min(tm, s)
    row = pl.BlockSpec((1, tm, d), lambda i, j: (i, j, 0))
    return pl.pallas_call(
        _final_kernel,
        grid=(b, s // tm),
        in_specs=[row] + _moe_residual_specs(tm, d) + [pl.BlockSpec((1, d), lambda i, j: (0, 0))],
        out_specs=row,
        out_shape=jax.ShapeDtypeStruct((b, s, d), F32),
        compiler_params=_params(("parallel", "parallel")),
        name="final",
    )(x, *res, final_g.reshape(1, d))


def _widen_w_in(w):
    gl0 = 2 * A_WIDTH + B_HEADS * DK + 6 * KV_GROUPS * DK
    per_group = HPG * 3
    gl_groups = [jnp.pad(w[:, gl0 + g * per_group:gl0 + (g + 1) * per_group], ((0, 0), (0, LANES - per_group)))
                 for g in range(KV_GROUPS)]
    wide = jnp.concatenate([w[:, :gl0]] + gl_groups, axis=1)
    assert gl0 == C_GL and wide.shape[1] == P_WIDE
    return wide.astype(BF16)


def _rope_tables(pos):
    half = DK // 2
    inv = ROPE_THETA ** (-jnp.arange(half, dtype=F32) / half)
    ang = pos.astype(F32)[..., None] * inv
    cos = jnp.cos(ang)
    sin = jnp.sin(ang)
    reps = LANES // DK
    return (jnp.concatenate([cos, cos] * reps, axis=-1),
            jnp.concatenate([-sin, sin] * reps, axis=-1))


def kernel(x, c, positions, ada_w, ada_b, norm1_g, norm2_g, w_in, w_out, sg_ln_g, sg_ln_b, sg_w, sg_b,
           cmp_pe_k, cmp_pe_v, cmp_w1_k, cmp_w2_k, cmp_w1_v, cmp_w2_v, router_w, router_b,
           exp_w1, exp_b1, exp_w2, exp_b2, final_g):
    b, s, d = x.shape
    n_layers = ada_w.shape[0]
    mod = _ada_mod(c, ada_w, ada_b)
    cos, sin = _rope_tables(positions)
    cmp_end = jnp.minimum(CMP_STRIDE * jnp.arange(s // CMP_STRIDE) + CMP_BLOCK - 1, s - 1)
    cos_c, sin_c = _rope_tables(positions[:, cmp_end])
    res = None
    for l in range(n_layers):
        sh1, sc1, g1, sh2, sc2, g2 = [m.reshape(b, 1, d) for m in jnp.split(mod[l], 6, axis=-1)]
        x, y_a, qt, kc, vc, ks, vs_t, kw, vw_t, gl_t = _pre(
            x, res, norm1_g[l], sc1, sh1, _widen_w_in(w_in[l]), cos, sin,
            (sg_ln_g[l], sg_ln_b[l], sg_w[l], sg_b[l]))
        kcmp, vcmp_t = _compress(kc, vc, cmp_pe_k[l], cmp_pe_v[l], cmp_w1_k[l], cmp_w2_k[l],
                                 cmp_w1_v[l], cmp_w2_v[l], cos_c, sin_c)
        oc_t, bias_t = _cmp_sel(qt, kcmp, vcmp_t)
        y_b = _attn(qt, ks, kw, vs_t, vw_t, bias_t, oc_t, gl_t)
        x, h2, idx, gate, rank = _post(y_a, y_b, w_out[l], x, g1, norm2_g[l], sc2, sh2,
                                       router_w[l], router_b[l])
        rows = _moe_routed(h2.reshape(b * s, d // 2), idx, rank, l, exp_w1, exp_b1, exp_w2, exp_b2)
        res = (rows.reshape(TOP_K, b, s, d // 2), gate.T.reshape(b, s, TOP_K), g2)
    return _final(x, res, final_g)
```

```python
import functools

import numpy as np
import jax
import jax.numpy as jnp
from jax import lax
from jax.experimental import pallas as pl
from jax.experimental.pallas import tpu as pltpu
from jax.experimental.pallas import tpu_sc as plsc

F32 = jnp.float32
BF16 = jnp.bfloat16
HIGHEST = lax.Precision.HIGHEST

A_WIDTH = 512
A_HEADS = 8
CHUNK = 128
B_HEADS = 8
DK = 64
KV_GROUPS = 2
HPG = B_HEADS // KV_GROUPS
CMP_BLOCK = 32
CMP_STRIDE = 16
SEL_BLOCK = 64
SEL_TOP = 16
WINDOW = 512
ROPE_THETA = 10000.0
N_EXPERTS = 32
TOP_K = 4
D_EXPERT = 1024
SWIGLU_LIMIT = 7.0
SWIGLU_ALPHA = 1.702
NORM_EPS = 1e-6

LANES = 128
SUBLANES = 8
GW = HPG * DK
N_FORCED = 3
ONES_ROWS = 16
MASK_BIAS = -32768.0
Q_SCALE = DK ** -0.5 * 1.4426950408889634
VMEM_LIMIT = 56 * 1024 * 1024
ROW_BLOCK = 256
BLOCKS_PER_STEP = 4
SC_CORES = 2
SC_SUBCORES = 16
SC_WORKERS = SC_CORES * SC_SUBCORES
SC_WINDOW = 128
CMP_ROW_CHUNK = 128
SEL_ROW_CHUNK = 32

C_U, C_V, C_Q = 0, 512, 1024
C_KC, C_VC, C_KS, C_VS, C_KW, C_VW = 1536, 1664, 1792, 1920, 2048, 2176
C_GL = 2304
P_WIDE = 2560


def _params(sem):
    return pltpu.CompilerParams(dimension_semantics=sem, vmem_limit_bytes=VMEM_LIMIT)


def _split_bf16(x, parts):
    out = []
    for _ in range(parts):
        piece = x.astype(BF16)
        out.append(piece)
        x = x - piece.astype(F32)
    return out


def _split_bf16_bits(x, parts):
    out = []
    for _ in range(parts):
        piece = lax.bitcast_convert_type(lax.bitcast_convert_type(x, jnp.uint32) & jnp.uint32(0xFFFF0000), F32)
        out.append(piece.astype(BF16))
        x = x - piece
    return out


def _ada_kernel(c_ref, w_ref, b_ref, o_ref):
    c = c_ref[...]
    cond = c * jax.nn.sigmoid(c)
    o_ref[0] = jnp.dot(cond, w_ref[0], precision=HIGHEST, preferred_element_type=F32) + b_ref[0]


def _ada_mod(c, ada_w, ada_b):
    n_layers, d, d6 = ada_w.shape
    b = c.shape[0]
    rows = SUBLANES
    c_pad = jnp.zeros((rows, d), F32).at[:b].set(c)
    out = pl.pallas_call(
        _ada_kernel,
        grid=(n_layers, d6 // d),
        in_specs=[
            pl.BlockSpec((rows, d), lambda l, j: (0, 0)),
            pl.BlockSpec((1, d, d), lambda l, j: (l, 0, j)),
            pl.BlockSpec((1, 1, d), lambda l, j: (l, 0, j)),
        ],
        out_specs=pl.BlockSpec((1, rows, d), lambda l, j: (l, 0, j)),
        out_shape=jax.ShapeDtypeStruct((n_layers, rows, d6), F32),
        compiler_params=_params(("arbitrary", "arbitrary")),
        name="ada",
    )(c_pad, ada_w, ada_b.reshape(n_layers, 1, d6))
    return out[:, :b]


def _rope_slab(t, cos, sin_signed, lo):
    partner = jnp.where(lo, pltpu.roll(t, LANES - DK // 2, 1), pltpu.roll(t, DK // 2, 1))
    return t * cos + partner * sin_signed


def _rope(t, cos, sin_signed):
    lane = lax.broadcasted_iota(jnp.int32, (1, LANES), 1)
    lo = (lane % DK) < (DK // 2)
    slabs = [_rope_slab(t[:, s * LANES:(s + 1) * LANES], cos, sin_signed, lo)
             for s in range(t.shape[1] // LANES)]
    return slabs[0] if len(slabs) == 1 else jnp.concatenate(slabs, axis=1)


def _pack_halves(x):
    n = x.shape[1] // 2
    lo = lax.bitcast_convert_type(x[:, :n].astype(BF16).astype(F32), jnp.int32)
    hi = lax.bitcast_convert_type(x[:, n:].astype(BF16).astype(F32), jnp.int32)
    return lax.shift_right_logical(lo, jnp.int32(16)) | (hi & jnp.int32(-65536))


def _unpack_halves(p):
    lo = lax.bitcast_convert_type(lax.shift_left(p, jnp.int32(16)), F32)
    hi = lax.bitcast_convert_type(p & jnp.int32(-65536), F32)
    return jnp.concatenate([lo, hi], axis=1)


def _moe_residual(x_ref, rows_ref, gate_ref, g2_ref):
    gate = gate_ref[0]
    y = gate[:, 0:1] * _unpack_halves(rows_ref[0, 0])
    for k in range(1, TOP_K):
        y = y + gate[:, k:k + 1] * _unpack_halves(rows_ref[k, 0])
    return x_ref[0] + g2_ref[0] * y


def _moe_residual_specs(tm, d):
    return [pl.BlockSpec((TOP_K, 1, tm, d // 2), lambda i, j: (0, i, j, 0)),
            pl.BlockSpec((1, tm, TOP_K), lambda i, j: (i, j, 0)),
            pl.BlockSpec((1, 1, d), lambda i, j: (i, 0, 0))]


def _pre_kernel(has_res, *refs):
    if has_res:
        (x_ref, rows_ref, gate_ref, g2_ref, ng_ref, sc_ref, sh_ref, w_ref, cos_ref, sin_ref,
         lng_ref, lnb_ref, sgw_ref, sgb_ref,
         xo_ref, ya_ref, qt_ref, kc_ref, vc_ref, ks_ref, vst_ref, kw_ref, vwt_ref, glt_ref) = refs
        x = _moe_residual(x_ref, rows_ref, gate_ref, g2_ref)
        xo_ref[0] = x
    else:
        (x_ref, ng_ref, sc_ref, sh_ref, w_ref, cos_ref, sin_ref, lng_ref, lnb_ref, sgw_ref, sgb_ref,
         ya_ref, qt_ref, kc_ref, vc_ref, ks_ref, vst_ref, kw_ref, vwt_ref, glt_ref) = refs
        x = x_ref[0]
    ms = jnp.mean(x * x, axis=-1, keepdims=True)
    h = x * lax.rsqrt(ms + NORM_EPS) * ng_ref[...]
    h = h * (1.0 + sc_ref[0]) + sh_ref[0]
    proj = jnp.dot(h.astype(BF16), w_ref[...], preferred_element_type=F32)
    cos = cos_ref[0]
    sin = sin_ref[0]
    _sgu_tile(proj[:, C_U:C_Q], lng_ref, lnb_ref, sgw_ref, sgb_ref, ya_ref)
    qt_ref[0] = (_rope(proj[:, C_Q:C_KC], cos, sin) * Q_SCALE).T
    kc_ref[0] = proj[:, C_KC:C_VC]
    vc_ref[0] = proj[:, C_VC:C_KS]
    ks_ref[0] = _rope(proj[:, C_KS:C_VS], cos, sin).astype(BF16)
    vst_ref[0] = proj[:, C_VS:C_KW].T.astype(BF16)
    kw_ref[0] = _rope(proj[:, C_KW:C_VW], cos, sin).astype(BF16)
    vwt_ref[0] = proj[:, C_VW:C_GL].T.astype(BF16)
    glt_ref[0] = proj[:, C_GL:P_WIDE].T


def _pre(x, res, norm_g, sc, sh, w_wide, cos, sin, sgu_params, tm=512):
    b, s, d = x.shape
    tm = min(tm, s)
    assert tm % CHUNK == 0
    ln_g, ln_b, w_s, b_s = sgu_params
    sg_bias = jnp.repeat(b_s.T, A_WIDTH // A_HEADS, axis=1)
    row = lambda w: pl.BlockSpec((1, tm, w), lambda i, j: (i, j, 0))
    col = lambda w: pl.BlockSpec((1, w, tm), lambda i, j: (i, 0, j))
    vec = pl.BlockSpec((1, 1, d), lambda i, j: (i, 0, 0))
    in_specs = [row(d)]
    args = [x]
    if res is not None:
        in_specs += _moe_residual_specs(tm, d)
        args += list(res)
    in_specs += [pl.BlockSpec((1, d), lambda i, j: (0, 0)), vec, vec,
                 pl.BlockSpec((d, P_WIDE), lambda i, j: (0, 0)), row(LANES), row(LANES),
                 pl.BlockSpec((1, A_WIDTH), lambda i, j: (0, 0)), pl.BlockSpec((1, A_WIDTH), lambda i, j: (0, 0)),
                 pl.BlockSpec((A_HEADS, CHUNK, CHUNK), lambda i, j: (0, 0, 0)),
                 pl.BlockSpec((CHUNK, A_WIDTH), lambda i, j: (0, 0))]
    args += [norm_g.reshape(1, d), sc, sh, w_wide, cos, sin,
             ln_g.reshape(1, -1), ln_b.reshape(1, -1), w_s, sg_bias]
    kv = KV_GROUPS * DK
    outs = [(d, F32, False)] if res is not None else []
    outs += [(A_WIDTH, BF16, False), (B_HEADS * DK, F32, True), (kv, F32, False), (kv, F32, False),
             (kv, BF16, False), (kv, BF16, True), (kv, BF16, False), (kv, BF16, True),
             (KV_GROUPS * LANES, F32, True)]
    res_out = pl.pallas_call(
        functools.partial(_pre_kernel, res is not None),
        grid=(b, s // tm),
        in_specs=in_specs,
        out_specs=[col(w) if t else row(w) for w, _, t in outs],
        out_shape=[jax.ShapeDtypeStruct((b, w, s) if t else (b, s, w), dt) for w, dt, t in outs],
        compiler_params=_params(("parallel", "parallel")),
        name="pre",
    )(*args)
    if res is None:
        res_out = [x] + list(res_out)
    return res_out


def _sgu_tile(uv, lng_ref, lnb_ref, w_ref, bias_ref, o_ref):
    rows = uv.shape[0]
    gu = jax.nn.gelu(uv[:, :A_WIDTH])
    gv = jax.nn.gelu(uv[:, A_WIDTH:])
    mu = jnp.mean(gv, axis=-1, keepdims=True)
    var = jnp.mean(jnp.square(gv - mu), axis=-1, keepdims=True)
    vn = ((gv - mu) * lax.rsqrt(var + NORM_EPS) * lng_ref[...] + lnb_ref[...]).astype(BF16)
    r = lax.broadcasted_iota(jnp.int32, (CHUNK, CHUNK), 0)
    c = lax.broadcasted_iota(jnp.int32, (CHUNK, CHUNK), 1)
    causal = c <= r
    lane_lo = lax.broadcasted_iota(jnp.int32, (CHUNK, LANES), 1) < DK
    for p in range(A_HEADS // 2):
        w0 = jnp.where(causal, w_ref[2 * p], 0.0).astype(BF16)
        w1 = jnp.where(causal, w_ref[2 * p + 1], 0.0).astype(BF16)
        bias = bias_ref[:, p * LANES:(p + 1) * LANES]
        for ch in range(rows // CHUNK):
            rs = slice(ch * CHUNK, (ch + 1) * CHUNK)
            cs = slice(p * LANES, (p + 1) * LANES)
            vp = vn[rs, cs]
            m0 = jnp.dot(w0, vp, preferred_element_type=F32)
            m1 = jnp.dot(w1, vp, preferred_element_type=F32)
            mixed = jnp.where(lane_lo, m0, m1) + bias
            o_ref[0, rs, cs] = (gu[rs, cs] * mixed).astype(o_ref.dtype)


def _compress_kernel(kr_ref, vr_ref, pek_ref, pev_ref, w1k_ref, w1v_ref, w2k_ref, w2v_ref,
                     cos_ref, sin_ref, ko_ref, vt_ref):
    nc = kr_ref.shape[1] // CMP_STRIDE

    def row_groups(t_ref):
        return jnp.concatenate([t_ref[0, pl.ds(l, nc, stride=CMP_STRIDE), :] for l in range(CMP_STRIDE)], axis=1)

    def mlp(r, pe_ref, w1_ref, w2_ref):
        top = jnp.dot(r + pe_ref[0:1], w1_ref[0], precision=HIGHEST, preferred_element_type=F32)
        bot = jnp.dot(r + pe_ref[1:2], w1_ref[1], precision=HIGHEST, preferred_element_type=F32)
        pre = top + pltpu.roll(bot, nc - 1, 0)
        return jnp.dot(jax.nn.gelu(pre), w2_ref[...], precision=HIGHEST, preferred_element_type=F32)

    kc = _rope(mlp(row_groups(kr_ref), pek_ref, w1k_ref, w2k_ref), cos_ref[0], sin_ref[0])
    vc = mlp(row_groups(vr_ref), pev_ref, w1v_ref, w2v_ref)
    lo = lax.broadcasted_iota(jnp.int32, (nc, LANES), 1) < DK
    rolled = pltpu.roll(kc, DK, 1)
    for g in range(KV_GROUPS):
        dup = jnp.where(lo, kc, rolled) if g == 0 else jnp.where(lo, rolled, kc)
        hi, low = _split_bf16(dup, 2)
        ko_ref[0, g, :, 0:LANES] = hi
        ko_ref[0, g, :, LANES:2 * LANES] = jnp.where(lo, low, jnp.zeros_like(low))
    vt_ref[0] = vc.T.astype(BF16)


def _compress_weights(pe, w1, w2):
    half = CMP_BLOCK // 2
    eye = jnp.eye(KV_GROUPS, dtype=F32)
    w1r = w1.reshape(CMP_BLOCK, DK, DK)
    wfull = jnp.einsum('lde,gh->lgdhe', w1r, eye)
    w1s = wfull.reshape(2, half * KV_GROUPS * DK, KV_GROUPS * DK)
    pes = jnp.broadcast_to(pe.reshape(2, half, 1, DK), (2, half, KV_GROUPS, DK)).reshape(2, -1)
    w2bd = jnp.einsum('de,gh->gdhe', w2, eye).reshape(KV_GROUPS * DK, KV_GROUPS * DK)
    return pes, w1s, w2bd


def _compress(kc, vc, pe_k, pe_v, w1_k, w2_k, w1_v, w2_v, cos_c, sin_c):
    b, s, _ = kc.shape
    nc = s // CMP_STRIDE
    rw = CMP_STRIDE * LANES
    pek, w1ks, w2kb = _compress_weights(pe_k, w1_k, w2_k)
    pev, w1vs, w2vb = _compress_weights(pe_v, w1_v, w2_v)
    full = lambda shape: pl.BlockSpec(shape, lambda i: (0,) * len(shape))
    return pl.pallas_call(
        _compress_kernel,
        grid=(b,),
        in_specs=[
            pl.BlockSpec((1, s, LANES), lambda i: (i, 0, 0)),
            pl.BlockSpec((1, s, LANES), lambda i: (i, 0, 0)),
            full((2, rw)), full((2, rw)),
            full((2, rw, LANES)), full((2, rw, LANES)),
            full((LANES, LANES)), full((LANES, LANES)),
            pl.BlockSpec((1, nc, LANES), lambda i: (i, 0, 0)),
            pl.BlockSpec((1, nc, LANES), lambda i: (i, 0, 0)),
        ],
        out_specs=[pl.BlockSpec((1, KV_GROUPS, nc, 2 * LANES), lambda i: (i, 0, 0, 0)),
                   pl.BlockSpec((1, KV_GROUPS * DK, nc), lambda i: (i, 0, 0))],
        out_shape=[jax.ShapeDtypeStruct((b, KV_GROUPS, nc, 2 * LANES), BF16),
                   jax.ShapeDtypeStruct((b, KV_GROUPS * DK, nc), BF16)],
        compiler_params=_params(("parallel",)),
        name="compress",
    )(kc, vc, pek, pev, w1ks, w1vs, w2kb, w2vb, cos_c, sin_c)


def _heads_on_lanes(qt):
    return jnp.concatenate([qt[h * DK:(h + 1) * DK] for h in range(HPG)], axis=1)


def _cmp_sel_kernel(n_top, qt_ref, k_ref, vt_ref, map_ref, oc_ref, bias_ref, imp_ref):
    tq = qt_ref.shape[2]
    cols = HPG * tq
    nc = k_ref.shape[2]
    nsel = map_ref.shape[0]
    q0 = pl.program_id(2) * tq
    q_hi, q_lo = _split_bf16(_heads_on_lanes(qt_ref[0]), 2)
    q3t = jnp.concatenate([q_hi, q_lo, q_hi, jnp.zeros_like(q_hi)], axis=0)
    t_col = q0 + lax.broadcasted_iota(jnp.int32, (1, cols), 1) % tq

    def compressed_branch(rows):
        s = jnp.dot(k_ref[0, 0, 0:rows], q3t, preferred_element_type=F32)
        n_idx = lax.broadcasted_iota(jnp.int32, (rows, 1), 0)
        s = jnp.where((CMP_STRIDE * n_idx + CMP_BLOCK - 1) <= t_col, s, -jnp.inf)
        m = jnp.max(s, axis=0, keepdims=True)
        m = jnp.where(m == -jnp.inf, 0.0, m)
        e = jnp.exp2(s - m)
        d = jnp.sum(e, axis=0, keepdims=True)
        p = e / jnp.where(d > 0, d, 1.0)
        oc = jnp.dot(vt_ref[0, :, 0:rows], p.astype(BF16), preferred_element_type=F32)
        oc_ref[0] = jnp.concatenate([oc[:, h * tq:(h + 1) * tq] for h in range(HPG)], axis=0)
        psum = p[:, 0:tq]
        for h in range(1, HPG):
            psum = psum + p[:, h * tq:(h + 1) * tq]
        imp_ref[...] = jnp.dot(jnp.concatenate([map_ref[:, 0:rows]] * 3, axis=1),
                               jnp.concatenate(_split_bf16(psum, 3), axis=0), preferred_element_type=F32)

    visible = (q0 + tq - CMP_BLOCK) // CMP_STRIDE + 1
    n_chunks = -(-nc // CMP_ROW_CHUNK)
    for r in range(1, n_chunks + 1):
        lo = (r - 1) * CMP_ROW_CHUNK if r > 1 else -nc
        pl.when((visible > lo) & (visible <= r * CMP_ROW_CHUNK) if r < n_chunks else visible > lo)(
            functools.partial(compressed_branch, min(r * CMP_ROW_CHUNK, nc)))

    def select(rows):
        imp = imp_ref[0:rows]
        j = lax.broadcasted_iota(jnp.int32, (rows, tq), 0)
        cur = (q0 + lax.broadcasted_iota(jnp.int32, (rows, tq), 1)) // SEL_BLOCK
        valid = j <= cur
        forced = (j == 0) | (j == cur) | (j == cur - 1)
        keep = forced | (valid & (cur < n_top))
        vals = jnp.where(valid & jnp.logical_not(forced), imp, -jnp.inf)
        bias = jnp.where(keep, 0.0, MASK_BIAS)
        for _ in range(n_top - N_FORCED):
            mx = jnp.max(vals, axis=0, keepdims=True)
            first = jnp.min(jnp.where(vals == mx, j, rows), axis=0, keepdims=True)
            pick = (j == first) & (mx > -jnp.inf)
            bias = jnp.where(pick, 0.0, bias)
            vals = jnp.where(pick, -jnp.inf, vals)
        bias_ref[0, 0, 0:rows] = bias.astype(BF16)
        if rows < nsel:
            bias_ref[0, 0, rows:nsel] = jnp.full((nsel - rows, tq), MASK_BIAS, BF16)

    n_valid = (q0 + tq - 1) // SEL_BLOCK + 1
    n_sel_chunks = -(-nsel // SEL_ROW_CHUNK)
    for r in range(1, n_sel_chunks + 1):
        lo = (r - 1) * SEL_ROW_CHUNK if r > 1 else -nsel
        pl.when((n_valid > lo) & (n_valid <= r * SEL_ROW_CHUNK) if r < n_sel_chunks else n_valid > lo)(
            functools.partial(select, min(r * SEL_ROW_CHUNK, nsel)))


def _sel_map_t(s):
    nc = s // CMP_STRIDE
    n_cmp = (s - CMP_BLOCK) // CMP_STRIDE + 1
    n_sel = s // SEL_BLOCK
    cs = CMP_STRIDE * np.arange(n_cmp)[:, None]
    ce = cs + CMP_BLOCK
    ss = SEL_BLOCK * np.arange(n_sel)[None, :]
    se = ss + SEL_BLOCK
    ov = np.clip(np.minimum(ce, se) - np.maximum(cs, ss), 0, None) / CMP_STRIDE
    out = np.zeros((n_sel, nc), np.float32)
    out[:, :n_cmp] = ov.T
    return out


def _cmp_sel(qt, kcmp, vcmp_t, tq=256):
    b, _, s = qt.shape
    nc = s // CMP_STRIDE
    nsel = s // SEL_BLOCK
    n_top = min(SEL_TOP, nsel)
    sel_map = jnp.asarray(_sel_map_t(s), dtype=BF16)
    return pl.pallas_call(
        functools.partial(_cmp_sel_kernel, n_top),
        grid=(b, KV_GROUPS, s // tq),
        in_specs=[
            pl.BlockSpec((1, GW, tq), lambda i, g, j: (i, g, j)),
            pl.BlockSpec((1, 1, nc, 2 * LANES), lambda i, g, j: (i, g, 0, 0)),
            pl.BlockSpec((1, DK, nc), lambda i, g, j: (i, g, 0)),
            pl.BlockSpec((nsel, nc), lambda i, g, j: (0, 0)),
        ],
        out_specs=[
            pl.BlockSpec((1, GW, tq), lambda i, g, j: (i, g, j)),
            pl.BlockSpec((1, 1, nsel, tq), lambda i, g, j: (i, g, 0, j)),
        ],
        out_shape=[
            jax.ShapeDtypeStruct((b, B_HEADS * DK, s), F32),
            jax.ShapeDtypeStruct((b, KV_GROUPS, nsel, s), BF16),
        ],
        scratch_shapes=[pltpu.VMEM((nsel, tq), F32)],
        compiler_params=_params(("parallel", "parallel", "parallel")),
        name="cmp_sel",
    )(qt, kcmp, vcmp_t, sel_map)


def _attn_kernel(kt, qt_ref, ks_ref, kw_ref, vst_ref, vwt_ref, bias_ref, oh_ref, oc_ref, glt_ref,
                 o_ref, qa_ref, s0_ref, s1_ref, m_ref, acc_ref, ow_ref):
    tq = qt_ref.shape[2]
    cols = HPG * tq
    nsel = bias_ref.shape[2]
    g = pl.program_id(1)
    q0 = pl.program_id(2) * tq

    q4t = _heads_on_lanes(qt_ref[0]).astype(BF16)
    zero = jnp.zeros_like(q4t)
    qa_ref[0:DK] = jnp.where(g == 0, q4t, zero)
    qa_ref[DK:2 * DK] = jnp.where(g == 1, q4t, zero)
    qa_ref[LANES:LANES + nsel] = jnp.concatenate([bias_ref[0, 0]] * HPG, axis=1)
    qaug = qa_ref[...]
    t_col = q0 + lax.broadcasted_iota(jnp.int32, (1, cols), 1) % tq

    def fold(x, op):
        return op(x.reshape(x.shape[0] // SUBLANES, SUBLANES, cols), axis=0)

    def with_ones(vt):
        return jnp.concatenate([vt, jnp.ones((ONES_ROWS, vt.shape[1]), BF16)], axis=0)

    def stage(c, s_ref):
        k0 = pl.multiple_of(c * kt, kt)
        kaug = jnp.concatenate([ks_ref[0, pl.ds(k0, kt), :], oh_ref[pl.ds(k0, kt), :]], axis=1)
        s_ref[...] = jnp.dot(kaug, qaug, preferred_element_type=F32)
        r0 = pl.multiple_of(jnp.clip(q0 - k0, 0, kt - tq), tq)
        kpos = k0 + r0 + lax.broadcasted_iota(jnp.int32, (tq, 1), 0)
        s_ref[pl.ds(r0, tq), :] = jnp.where(kpos <= t_col, s_ref[pl.ds(r0, tq), :], -jnp.inf)

    def consume(c, s_ref):
        k0 = pl.multiple_of(c * kt, kt)
        sc = s_ref[...]
        m_old = m_ref[...]
        m_new = jnp.maximum(m_old, jnp.max(fold(sc, jnp.max), axis=0, keepdims=True))
        alpha = jnp.exp2(m_old - m_new)
        pr = jnp.exp2(sc - m_new)
        acc_ref[...] = alpha * acc_ref[...] + jnp.dot(with_ones(vst_ref[0, :, pl.ds(k0, kt)]), pr.astype(BF16),
                                                      preferred_element_type=F32)
        m_ref[...] = m_new

    m_ref[...] = jnp.full(m_ref.shape, -jnp.inf, F32)
    acc_ref[...] = jnp.zeros(acc_ref.shape, F32)
    n_full = q0 // kt
    n_pairs = n_full // 2
    odd = n_full % 2 == 1

    def window(full_history):
        wlen = WINDOW + tq
        w0 = pl.multiple_of(jnp.maximum(q0 - WINDOW, 0), tq)
        sw = jnp.dot(kw_ref[0, pl.ds(w0, wlen), :], qaug[0:LANES], preferred_element_type=F32)
        if full_history:
            c_idx = t_col - q0
            r = lax.broadcasted_iota(jnp.int32, (tq, 1), 0)
            sw = jnp.concatenate([jnp.where(r > c_idx, sw[0:tq], -jnp.inf), sw[tq:WINDOW],
                                  jnp.where(r <= c_idx, sw[WINDOW:wlen], -jnp.inf)], axis=0)
        else:
            dlt = t_col - (w0 + lax.broadcasted_iota(jnp.int32, (wlen, 1), 0))
            sw = jnp.where(lax.bitcast_convert_type(dlt, jnp.uint32) < jnp.uint32(WINDOW), sw, -jnp.inf)
        pw = jnp.exp2(sw - jnp.max(sw, axis=0, keepdims=True))
        o_win = jnp.dot(with_ones(vwt_ref[0, :, pl.ds(w0, wlen)]), pw.astype(BF16), preferred_element_type=F32)
        ow_ref[...] = o_win[0:DK] / o_win[DK:DK + 1]

    @pl.when(q0 >= WINDOW)
    def _():
        stage(0, s0_ref)
        window(True)

    @pl.when(q0 < WINDOW)
    def _():
        stage(0, s0_ref)
        window(False)

    def pair(i, carry):
        stage(2 * i + 1, s1_ref)
        consume(2 * i, s0_ref)
        stage(2 * i + 2, s0_ref)
        consume(2 * i + 1, s1_ref)
        return carry

    lax.fori_loop(0, n_pairs, pair, 0)

    @pl.when(odd)
    def _():
        stage(n_full, s1_ref)
        consume(n_full - 1, s0_ref)
        consume(n_full, s1_ref)

    @pl.when(jnp.logical_not(odd))
    def _():
        consume(n_full, s0_ref)

    o_sel = acc_ref[0:DK] / acc_ref[DK:DK + 1]

    o_win = ow_ref[...]
    gates = jax.nn.sigmoid(glt_ref[0])
    oc = _heads_on_lanes(oc_ref[0])
    for p in range(HPG // 2):
        halves = []
        for hh in (2 * p, 2 * p + 1):
            cs = slice(hh * tq, (hh + 1) * tq)
            halves.append(gates[3 * hh:3 * hh + 1] * oc[:, cs] + gates[3 * hh + 1:3 * hh + 2] * o_sel[:, cs]
                          + gates[3 * hh + 2:3 * hh + 3] * o_win[:, cs])
        o_ref[0, :, p * LANES:(p + 1) * LANES] = jnp.concatenate(halves, axis=0).T.astype(o_ref.dtype)


def _attn(qt, ks, kw, vs_t, vw_t, bias_t, oc_t, gl_t, tq=256, kt=512):
    b, _, s = qt.shape
    nsel = s // SEL_BLOCK
    kt = min(kt, s)
    onehot = jnp.asarray((np.arange(s)[:, None] // SEL_BLOCK == np.arange(nsel)[None, :]), dtype=BF16)
    k_spec = pl.BlockSpec((1, s, KV_GROUPS * DK), lambda i, g, j: (i, 0, 0))
    vt_spec = pl.BlockSpec((1, DK, s), lambda i, g, j: (i, g, 0))
    cols = HPG * tq
    return pl.pallas_call(
        functools.partial(_attn_kernel, kt),
        grid=(b, KV_GROUPS, s // tq),
        in_specs=[
            pl.BlockSpec((1, GW, tq), lambda i, g, j: (i, g, j)),
            k_spec, k_spec, vt_spec, vt_spec,
            pl.BlockSpec((1, 1, nsel, tq), lambda i, g, j: (i, g, 0, j)),
            pl.BlockSpec((s, nsel), lambda i, g, j: (0, 0)),
            pl.BlockSpec((1, GW, tq), lambda i, g, j: (i, g, j)),
            pl.BlockSpec((1, LANES, tq), lambda i, g, j: (i, g, j)),
        ],
        out_specs=pl.BlockSpec((1, tq, GW), lambda i, g, j: (i, j, g)),
        out_shape=jax.ShapeDtypeStruct((b, s, B_HEADS * DK), BF16),
        scratch_shapes=[pltpu.VMEM((LANES + nsel, cols), BF16),
                        pltpu.VMEM((kt, cols), F32), pltpu.VMEM((kt, cols), F32),
                        pltpu.VMEM((1, cols), F32), pltpu.VMEM((DK + ONES_ROWS, cols), F32),
                        pltpu.VMEM((DK, cols), F32)],
        compiler_params=_params(("parallel", "parallel", "arbitrary")),
        name="attn",
    )(qt, ks, kw, vs_t, vw_t, bias_t, onehot, oc_t, gl_t)


def _post_kernel(ya_ref, yb_ref, wo_ref, x_ref, g1_ref, ng_ref, sc_ref, sh_ref, rw_ref, rb_ref,
                 xo_ref, h_ref, idx_ref, gate_ref, rank_ref, cnt_ref):
    tm = x_ref.shape[1]

    @pl.when((pl.program_id(0) == 0) & (pl.program_id(1) == 0))
    def _():
        cnt_ref[...] = jnp.zeros(cnt_ref.shape, F32)

    mixed = jnp.dot(ya_ref[0], wo_ref[:A_WIDTH], preferred_element_type=F32)
    mixed = mixed + jnp.dot(yb_ref[0], wo_ref[A_WIDTH:], preferred_element_type=F32)
    x = x_ref[0] + g1_ref[0] * mixed
    xo_ref[0] = x
    ms = jnp.mean(x * x, axis=-1, keepdims=True)
    h = x * lax.rsqrt(ms + NORM_EPS) * ng_ref[...]
    h = h * (1.0 + sc_ref[0]) + sh_ref[0]
    h_ref[0] = _pack_halves(h)
    h_hi, h_mid = _split_bf16(h, 2)
    logits = jnp.dot(jnp.concatenate([h_hi, h_hi, h_mid], axis=1), rw_ref[...],
                     preferred_element_type=F32) + rb_ref[...]
    vals = logits.T[:N_EXPERTS]
    e_idx = lax.broadcasted_iota(jnp.int32, (N_EXPERTS, tm), 0)
    top = jnp.max(vals, axis=0, keepdims=True)
    routed_t = jnp.zeros_like(vals)
    firsts, exps = [], []
    for k in range(TOP_K):
        m = jnp.max(vals, axis=0, keepdims=True)
        first = jnp.min(jnp.where(vals == m, e_idx, N_EXPERTS), axis=0, keepdims=True)
        pick = e_idx == first
        routed_t = jnp.where(pick, 1.0, routed_t)
        firsts.append(first)
        exps.append(jnp.exp(m - top))
        vals = jnp.where(pick, -jnp.inf, vals)
    idx_i = jnp.concatenate(firsts, axis=0)
    idx_t = idx_i.astype(F32)
    slot_exp = jnp.concatenate(exps, axis=0)
    idx_ref[...] = idx_i
    gate_ref[...] = slot_exp / jnp.sum(slot_exp, axis=0, keepdims=True)
    r = lax.broadcasted_iota(jnp.int32, (tm, tm), 0)
    c = lax.broadcasted_iota(jnp.int32, (tm, tm), 1)
    before = jnp.where(r < c, 1.0, 0.0).astype(BF16)
    rank_t = cnt_ref[...] + jnp.dot(routed_t.astype(BF16), before, preferred_element_type=F32)
    cnt_ref[...] += jnp.sum(routed_t, axis=-1, keepdims=True)
    e_row = lax.broadcasted_iota(jnp.int32, (N_EXPERTS, tm), 0).astype(F32)
    ranks = [jnp.sum(jnp.where(e_row == idx_t[k:k + 1], rank_t, 0.0), axis=0, keepdims=True)
             for k in range(TOP_K)]
    rank_ref[...] = jnp.concatenate(ranks, axis=0).astype(jnp.int32)


def _post(y_a, y_b, w_out, x, g1, norm_g, sc, sh, router_w, router_b, tm=512):
    b, s, d = x.shape
    tm = min(tm, s)
    nt = s // tm
    w_hi, w_mid = _split_bf16_bits(jnp.zeros((d, LANES), F32).at[:, :N_EXPERTS].set(router_w), 2)
    rw = jnp.concatenate([w_hi, w_mid, w_hi], axis=0)
    rb = jnp.full((1, LANES), -1e30, F32).at[0, :N_EXPERTS].set(router_b)
    vec = pl.BlockSpec((1, 1, d), lambda i, j: (i, 0, 0))
    row = lambda w: pl.BlockSpec((1, tm, w), lambda i, j: (i, j, 0))
    return pl.pallas_call(
        _post_kernel,
        grid=(b, nt),
        in_specs=[
            row(A_WIDTH), row(B_HEADS * DK),
            pl.BlockSpec((d, d), lambda i, j: (0, 0)),
            row(d), vec,
            pl.BlockSpec((1, d), lambda i, j: (0, 0)), vec, vec,
            pl.BlockSpec((3 * d, LANES), lambda i, j: (0, 0)),
            pl.BlockSpec((1, LANES), lambda i, j: (0, 0)),
        ],
        out_specs=[row(d), row(d // 2)] + [pl.BlockSpec((TOP_K, tm), lambda i, j: (0, i * nt + j))] * 3,
        out_shape=[
            jax.ShapeDtypeStruct((b, s, d), F32),
            jax.ShapeDtypeStruct((b, s, d // 2), jnp.int32),
            jax.ShapeDtypeStruct((TOP_K, b * s), jnp.int32),
            jax.ShapeDtypeStruct((TOP_K, b * s), F32),
            jax.ShapeDtypeStruct((TOP_K, b * s), jnp.int32),
        ],
        scratch_shapes=[pltpu.VMEM((N_EXPERTS, 1), F32)],
        compiler_params=_params(("arbitrary", "arbitrary")),
        name="post",
    )(y_a, y_b, w_out.astype(BF16), x, g1, norm_g.reshape(1, d), sc, sh, rw, rb)


def _sc_mesh():
    return plsc.VectorSubcoreMesh(core_axis_name="c", subcore_axis_name="s")


def _sc_worker():
    return lax.axis_index("c") * SC_SUBCORES + lax.axis_index("s")


def _dispatch_rows(h2, dest_c, n_rows):
    t, d = h2.shape
    per_worker = t // SC_WINDOW // SC_WORKERS

    def body(x_hbm, i_hbm, o_hbm, buf, idx):
        worker = _sc_worker()

        @pl.loop(0, per_worker)
        def _(j):
            ch = worker * per_worker + j
            pltpu.sync_copy(i_hbm.at[ch], idx)
            pltpu.sync_copy(x_hbm.at[pl.ds(ch * SC_WINDOW, SC_WINDOW)], buf)
            for k in range(TOP_K):
                pltpu.sync_copy(buf, o_hbm.at[idx.at[k]])

    return pl.kernel(
        body, out_type=jax.ShapeDtypeStruct((n_rows, d), h2.dtype), mesh=_sc_mesh(),
        scratch_types=[pltpu.VMEM((SC_WINDOW, d), h2.dtype), pltpu.VMEM((TOP_K, SC_WINDOW), jnp.int32)],
        name="dispatch_rows",
    )(h2, dest_c)


def _collect_rows(rows, dest_c, t):
    d = rows.shape[1]
    per_worker = t // SC_WINDOW // SC_WORKERS

    def body(r_hbm, i_hbm, o_hbm, buf, idx):
        worker = _sc_worker()

        @pl.loop(0, per_worker)
        def _(j):
            ch = worker * per_worker + j
            pltpu.sync_copy(i_hbm.at[ch], idx)
            for k in range(TOP_K):
                pltpu.sync_copy(r_hbm.at[idx.at[k]], buf)
                pltpu.sync_copy(buf, o_hbm.at[k, pl.ds(ch * SC_WINDOW, SC_WINDOW)])

    return pl.kernel(
        body, out_type=jax.ShapeDtypeStruct((TOP_K, t, d), rows.dtype), mesh=_sc_mesh(),
        scratch_types=[pltpu.VMEM((SC_WINDOW, d), rows.dtype), pltpu.VMEM((TOP_K, SC_WINDOW), jnp.int32)],
        name="collect_rows",
    )(rows, dest_c)


def _experts_kernel(layer, be_ref, nb_ref, slot_ref, next_ref, x_ref, w1_hbm, b1_ref, w2_hbm, b2_ref, o_ref,
                    w1f_ref, w2f_ref, w1b_ref, w2b_ref, sem):
    def weight_copies(expert, slot):
        return (pltpu.make_async_copy(w1_hbm.at[layer, expert], w1f_ref.at[slot], sem.at[slot, 0]),
                pltpu.make_async_copy(w2_hbm.at[layer, expert], w2f_ref.at[slot], sem.at[slot, 1]))

    def row_block(i, rs):
        e = be_ref[i]
        live = i < nb_ref[0]

        @pl.when(live & (i == 0))
        def _():
            for cp in weight_copies(e, 0):
                cp.start()

        @pl.when(live & ((i == 0) | (e != be_ref[jnp.maximum(i - 1, 0)])))
        def _():
            slot = slot_ref[i]
            for cp in weight_copies(e, slot):
                cp.wait()
            w1b_ref[...] = w1f_ref[slot].astype(BF16)
            w2b_ref[...] = w2f_ref[slot].astype(BF16)

            @pl.when(next_ref[i] >= 0)
            def _():
                for cp in weight_copies(next_ref[i], 1 - slot):
                    cp.start()

        @pl.when(live)
        def _():
            x = _unpack_halves(x_ref[rs, :]).astype(BF16)
            hdn = jnp.dot(x, w1b_ref[...], preferred_element_type=F32) + b1_ref[0, pl.ds(e, 1), :]
            glu = jnp.minimum(hdn[:, :D_EXPERT], SWIGLU_LIMIT)
            lin = jnp.clip(hdn[:, D_EXPERT:], -SWIGLU_LIMIT, SWIGLU_LIMIT)
            act = glu * jax.nn.sigmoid(SWIGLU_ALPHA * glu) * (lin + 1.0)
            out = jnp.dot(act.astype(BF16), w2b_ref[...], preferred_element_type=F32) + b2_ref[0, pl.ds(e, 1), :]
            o_ref[rs, :] = _pack_halves(out)

    for h in range(BLOCKS_PER_STEP):
        row_block(pl.program_id(0) * BLOCKS_PER_STEP + h, slice(h * ROW_BLOCK, (h + 1) * ROW_BLOCK))


def _experts(rows, block_exp, n_live, slot, next_exp, layer, w1, b1, w2, b2):
    n_rows, dp = rows.shape
    n_layers, n_e, d, f2 = w1.shape
    n_blocks = n_rows // ROW_BLOCK
    assert n_blocks % BLOCKS_PER_STEP == 0
    step_rows = BLOCKS_PER_STEP * ROW_BLOCK
    this_layer = lambda i, be, nb, sl, nx: (layer, 0, 0)
    grid_spec = pltpu.PrefetchScalarGridSpec(
        num_scalar_prefetch=4,
        grid=(n_blocks // BLOCKS_PER_STEP,),
        in_specs=[
            pl.BlockSpec((step_rows, dp), lambda i, be, nb, sl, nx: (i, 0)),
            pl.BlockSpec(memory_space=pl.ANY),
            pl.BlockSpec((1, n_e, f2), this_layer),
            pl.BlockSpec(memory_space=pl.ANY),
            pl.BlockSpec((1, n_e, d), this_layer),
        ],
        out_specs=pl.BlockSpec((step_rows, dp), lambda i, be, nb, sl, nx: (i, 0)),
        scratch_shapes=[pltpu.VMEM((2, d, f2), F32), pltpu.VMEM((2, f2 // 2, d), F32),
                        pltpu.VMEM((d, f2), BF16), pltpu.VMEM((f2 // 2, d), BF16),
                        pltpu.SemaphoreType.DMA((2, 2))],
    )
    return pl.pallas_call(
        functools.partial(_experts_kernel, layer),
        grid_spec=grid_spec,
        out_shape=jax.ShapeDtypeStruct((n_rows, dp), jnp.int32),
        compiler_params=_params(("arbitrary",)),
        name="experts",
    )(block_exp, n_live, slot, next_exp, rows, w1, b1, w2, b2)


def _moe_routed(h2, idx, rank, layer, w1, b1, w2, b2):
    t, d = h2.shape
    n_e = w1.shape[1]
    n_blocks = t * TOP_K // ROW_BLOCK + n_e
    onehot = (idx[:, :, None] == jnp.arange(n_e, dtype=jnp.int32)).astype(jnp.int32)
    counts = jnp.sum(onehot, axis=(0, 1))
    padded = (counts + ROW_BLOCK - 1) // ROW_BLOCK * ROW_BLOCK
    pad_end = jnp.cumsum(padded)
    dest = jnp.sum(onehot * (pad_end - padded), axis=-1) + rank
    block_row = jnp.arange(n_blocks, dtype=jnp.int32)[:, None] * ROW_BLOCK
    block_exp = jnp.minimum(jnp.sum((pad_end[None, :] <= block_row).astype(jnp.int32), axis=-1), n_e - 1)
    n_live = (pad_end[-1:] // ROW_BLOCK).astype(jnp.int32)
    experts = jnp.arange(n_e, dtype=jnp.int32)
    later = (experts[None, :] > experts[:, None]) & (counts[None, :] > 0)
    next_of = jnp.min(jnp.where(later, experts[None, :], n_e), axis=-1)
    next_of = jnp.where(next_of == n_e, -1, next_of)
    block_onehot = (block_exp[:, None] == experts[None, :]).astype(jnp.int32)
    next_exp = jnp.sum(block_onehot * next_of, axis=-1)
    run_of = jnp.cumsum((counts > 0).astype(jnp.int32)) - 1
    slot = jnp.sum(block_onehot * run_of, axis=-1) % 2
    dest_c = dest.reshape(TOP_K, t // SC_WINDOW, SC_WINDOW).transpose(1, 0, 2)
    rows = _dispatch_rows(h2, dest_c, n_blocks * ROW_BLOCK)
    out = _experts(rows, block_exp, n_live, slot, next_exp, layer, w1, b1, w2, b2)
    return _collect_rows(out, dest_c, t)


def _final_kernel(x_ref, rows_ref, gate_ref, g2_ref, fg_ref, o_ref):
    x = _moe_residual(x_ref, rows_ref, gate_ref, g2_ref)
    ms = jnp.mean(x * x, axis=-1, keepdims=True)
    o_ref[0] = x * lax.rsqrt(ms + NORM_EPS) * fg_ref[...]


def _final(x, res, final_g, tm=512):
    b, s, d = x.shape
    tm = min(tm, s)
    row = pl.BlockSpec((1, tm, d), lambda i, j: (i, j, 0))
    return pl.pallas_call(
        _final_kernel,
        grid=(b, s // tm),
        in_specs=[row] + _moe_residual_specs(tm, d) + [pl.BlockSpec((1, d), lambda i, j: (0, 0))],
        out_specs=row,
        out_shape=jax.ShapeDtypeStruct((b, s, d), F32),
        compiler_params=_params(("parallel", "parallel")),
        name="final",
    )(x, *res, final_g.reshape(1, d))


def _widen_w_in(w):
    gl0 = 2 * A_WIDTH + B_HEADS * DK + 6 * KV_GROUPS * DK
    per_group = HPG * 3
    gl_groups = [jnp.pad(w[:, gl0 + g * per_group:gl0 + (g + 1) * per_group], ((0, 0), (0, LANES - per_group)))
                 for g in range(KV_GROUPS)]
    wide = jnp.concatenate([w[:, :gl0]] + gl_groups, axis=1)
    assert gl0 == C_GL and wide.shape[1] == P_WIDE
    return wide.astype(BF16)


def _rope_tables(pos):
    half = DK // 2
    inv = ROPE_THETA ** (-jnp.arange(half, dtype=F32) / half)
    ang = pos.astype(F32)[..., None] * inv
    cos = jnp.cos(ang)
    sin = jnp.sin(ang)
    reps = LANES // DK
    return (jnp.concatenate([cos, cos] * reps, axis=-1),
            jnp.concatenate([-sin, sin] * reps, axis=-1))


def kernel(x, c, positions, ada_w, ada_b, norm1_g, norm2_g, w_in, w_out, sg_ln_g, sg_ln_b, sg_w, sg_b,
           cmp_pe_k, cmp_pe_v, cmp_w1_k, cmp_w2_k, cmp_w1_v, cmp_w2_v, router_w, router_b,
           exp_w1, exp_b1, exp_w2, exp_b2, final_g):
    b, s, d = x.shape
    n_layers = ada_w.shape[0]
    mod = _ada_mod(c, ada_w, ada_b)
    cos, sin = _rope_tables(positions)
    cmp_end = jnp.minimum(CMP_STRIDE * jnp.arange(s // CMP_STRIDE) + CMP_BLOCK - 1, s - 1)
    cos_c, sin_c = _rope_tables(positions[:, cmp_end])
    res = None
    for l in range(n_layers):
        sh1, sc1, g1, sh2, sc2, g2 = [m.reshape(b, 1, d) for m in jnp.split(mod[l], 6, axis=-1)]
        x, y_a, qt, kc, vc, ks, vs_t, kw, vw_t, gl_t = _pre(
            x, res, norm1_g[l], sc1, sh1, _widen_w_in(w_in[l]), cos, sin,
            (sg_ln_g[l], sg_ln_b[l], sg_w[l], sg_b[l]))
        kcmp, vcmp_t = _compress(kc, vc, cmp_pe_k[l], cmp_pe_v[l], cmp_w1_k[l], cmp_w2_k[l],
                                 cmp_w1_v[l], cmp_w2_v[l], cos_c, sin_c)
        oc_t, bias_t = _cmp_sel(qt, kcmp, vcmp_t)
        y_b = _attn(qt, ks, kw, vs_t, vw_t, bias_t, oc_t, gl_t)
        x, h2, idx, gate, rank = _post(y_a, y_b, w_out[l], x, g1, norm2_g[l], sc2, sh2,
                                       router_w[l], router_b[l])
        rows = _moe_routed(h2.reshape(b * s, d // 2), idx, rank, l, exp_w1, exp_b1, exp_w2, exp_b2)
        res = (rows.reshape(TOP_K, b, s, d // 2), gate.T.reshape(b, s, TOP_K), g2)
    return _final(x, res, final_g)
```

```python
import functools

import numpy as np
import jax
import jax.numpy as jnp
from jax import lax
from jax.experimental import pallas as pl
from jax.experimental.pallas import tpu as pltpu
from jax.experimental.pallas import tpu_sc as plsc

F32 = jnp.float32
BF16 = jnp.bfloat16
HIGHEST = lax.Precision.HIGHEST

A_WIDTH = 512
A_HEADS = 8
CHUNK = 128
B_HEADS = 8
DK = 64
KV_GROUPS = 2
HPG = B_HEADS // KV_GROUPS
CMP_BLOCK = 32
CMP_STRIDE = 16
SEL_BLOCK = 64
SEL_TOP = 16
WINDOW = 512
ROPE_THETA = 10000.0
N_EXPERTS = 32
TOP_K = 4
D_EXPERT = 1024
SWIGLU_LIMIT = 7.0
SWIGLU_ALPHA = 1.702
NORM_EPS = 1e-6

LANES = 128
SUBLANES = 8
GW = HPG * DK
N_FORCED = 3
ONES_ROWS = 16
MASK_BIAS = -32768.0
Q_SCALE = DK ** -0.5 * 1.4426950408889634
VMEM_LIMIT = 56 * 1024 * 1024
ROW_BLOCK = 256
BLOCKS_PER_STEP = 4
SC_CORES = 2
SC_SUBCORES = 16
SC_WORKERS = SC_CORES * SC_SUBCORES
SC_WINDOW = 128
CMP_ROW_CHUNK = 128
SEL_ROW_CHUNK = 32

C_U, C_V, C_Q = 0, 512, 1024
C_KC, C_VC, C_KS, C_VS, C_KW, C_VW = 1536, 1664, 1792, 1920, 2048, 2176
C_GL = 2304
P_WIDE = 2560


def _params(sem):
    return pltpu.CompilerParams(dimension_semantics=sem, vmem_limit_bytes=VMEM_LIMIT)


def _split_bf16(x, parts):
    out = []
    for _ in range(parts):
        piece = x.astype(BF16)
        out.append(piece)
        x = x - piece.astype(F32)
    return out


def _split_bf16_bits(x, parts):
    out = []
    for _ in range(parts):
        piece = lax.bitcast_convert_type(lax.bitcast_convert_type(x, jnp.uint32) & jnp.uint32(0xFFFF0000), F32)
        out.append(piece.astype(BF16))
        x = x - piece
    return out


def _ada_kernel(c_ref, w_ref, b_ref, o_ref):
    c = c_ref[...]
    cond = c * jax.nn.sigmoid(c)
    o_ref[0] = jnp.dot(cond, w_ref[0], precision=HIGHEST, preferred_element_type=F32) + b_ref[0]


def _ada_mod(c, ada_w, ada_b):
    n_layers, d, d6 = ada_w.shape
    b = c.shape[0]
    rows = SUBLANES
    c_pad = jnp.zeros((rows, d), F32).at[:b].set(c)
    out = pl.pallas_call(
        _ada_kernel,
        grid=(n_layers, d6 // d),
        in_specs=[
            pl.BlockSpec((rows, d), lambda l, j: (0, 0)),
            pl.BlockSpec((1, d, d), lambda l, j: (l, 0, j)),
            pl.BlockSpec((1, 1, d), lambda l, j: (l, 0, j)),
        ],
        out_specs=pl.BlockSpec((1, rows, d), lambda l, j: (l, 0, j)),
        out_shape=jax.ShapeDtypeStruct((n_layers, rows, d6), F32),
        compiler_params=_params(("arbitrary", "arbitrary")),
        name="ada",
    )(c_pad, ada_w, ada_b.reshape(n_layers, 1, d6))
    return out[:, :b]


def _rope_slab(t, cos, sin_signed, lo):
    partner = jnp.where(lo, pltpu.roll(t, LANES - DK // 2, 1), pltpu.roll(t, DK // 2, 1))
    return t * cos + partner * sin_signed


def _rope(t, cos, sin_signed):
    lane = lax.broadcasted_iota(jnp.int32, (1, LANES), 1)
    lo = (lane % DK) < (DK // 2)
    slabs = [_rope_slab(t[:, s * LANES:(s + 1) * LANES], cos, sin_signed, lo)
             for s in range(t.shape[1] // LANES)]
    return slabs[0] if len(slabs) == 1 else jnp.concatenate(slabs, axis=1)


def _pack_halves(x):
    n = x.shape[1] // 2
    lo = lax.bitcast_convert_type(x[:, :n].astype(BF16).astype(F32), jnp.int32)
    hi = lax.bitcast_convert_type(x[:, n:].astype(BF16).astype(F32), jnp.int32)
    return lax.shift_right_logical(lo, jnp.int32(16)) | (hi & jnp.int32(-65536))


def _unpack_halves(p):
    lo = lax.bitcast_convert_type(lax.shift_left(p, jnp.int32(16)), F32)
    hi = lax.bitcast_convert_type(p & jnp.int32(-65536), F32)
    return jnp.concatenate([lo, hi], axis=1)


def _moe_residual(x_ref, rows_ref, gate_ref, g2_ref):
    gate = gate_ref[0]
    y = gate[:, 0:1] * _unpack_halves(rows_ref[0, 0])
    for k in range(1, TOP_K):
        y = y + gate[:, k:k + 1] * _unpack_halves(rows_ref[k, 0])
    return x_ref[0] + g2_ref[0] * y


def _moe_residual_specs(tm, d):
    return [pl.BlockSpec((TOP_K, 1, tm, d // 2), lambda i, j: (0, i, j, 0)),
            pl.BlockSpec((1, tm, TOP_K), lambda i, j: (i, j, 0)),
            pl.BlockSpec((1, 1, d), lambda i, j: (i, 0, 0))]


def _pre_kernel(has_res, *refs):
    if has_res:
        (x_ref, rows_ref, gate_ref, g2_ref, ng_ref, sc_ref, sh_ref, w_ref, cos_ref, sin_ref,
         lng_ref, lnb_ref, sgw_ref, sgb_ref,
         xo_ref, ya_ref, qt_ref, kc_ref, vc_ref, ks_ref, vst_ref, kw_ref, vwt_ref, glt_ref) = refs
        x = _moe_residual(x_ref, rows_ref, gate_ref, g2_ref)
        xo_ref[0] = x
    else:
        (x_ref, ng_ref, sc_ref, sh_ref, w_ref, cos_ref, sin_ref, lng_ref, lnb_ref, sgw_ref, sgb_ref,
         ya_ref, qt_ref, kc_ref, vc_ref, ks_ref, vst_ref, kw_ref, vwt_ref, glt_ref) = refs
        x = x_ref[0]
    ms = jnp.mean(x * x, axis=-1, keepdims=True)
    h = x * lax.rsqrt(ms + NORM_EPS) * ng_ref[...]
    h = h * (1.0 + sc_ref[0]) + sh_ref[0]
    proj = jnp.dot(h.astype(BF16), w_ref[...], preferred_element_type=F32)
    cos = cos_ref[0]
    sin = sin_ref[0]
    _sgu_tile(proj[:, C_U:C_Q], lng_ref, lnb_ref, sgw_ref, sgb_ref, ya_ref)
    qt_ref[0] = (_rope(proj[:, C_Q:C_KC], cos, sin) * Q_SCALE).T
    kc_ref[0] = proj[:, C_KC:C_VC]
    vc_ref[0] = proj[:, C_VC:C_KS]
    ks_ref[0] = _rope(proj[:, C_KS:C_VS], cos, sin).astype(BF16)
    vst_ref[0] = proj[:, C_VS:C_KW].T.astype(BF16)
    kw_ref[0] = _rope(proj[:, C_KW:C_VW], cos, sin).astype(BF16)
    vwt_ref[0] = proj[:, C_VW:C_GL].T.astype(BF16)
    glt_ref[0] = proj[:, C_GL:P_WIDE].T


def _pre(x, res, norm_g, sc, sh, w_wide, cos, sin, sgu_params, tm=512):
    b, s, d = x.shape
    tm = min(tm, s)
    assert tm % CHUNK == 0
    ln_g, ln_b, w_s, b_s = sgu_params
    sg_bias = jnp.repeat(b_s.T, A_WIDTH // A_HEADS, axis=1)
    row = lambda w: pl.BlockSpec((1, tm, w), lambda i, j: (i, j, 0))
    col = lambda w: pl.BlockSpec((1, w, tm), lambda i, j: (i, 0, j))
    vec = pl.BlockSpec((1, 1, d), lambda i, j: (i, 0, 0))
    in_specs = [row(d)]
    args = [x]
    if res is not None:
        in_specs += _moe_residual_specs(tm, d)
        args += list(res)
    in_specs += [pl.BlockSpec((1, d), lambda i, j: (0, 0)), vec, vec,
                 pl.BlockSpec((d, P_WIDE), lambda i, j: (0, 0)), row(LANES), row(LANES),
                 pl.BlockSpec((1, A_WIDTH), lambda i, j: (0, 0)), pl.BlockSpec((1, A_WIDTH), lambda i, j: (0, 0)),
                 pl.BlockSpec((A_HEADS, CHUNK, CHUNK), lambda i, j: (0, 0, 0)),
                 pl.BlockSpec((CHUNK, A_WIDTH), lambda i, j: (0, 0))]
    args += [norm_g.reshape(1, d), sc, sh, w_wide, cos, sin,
             ln_g.reshape(1, -1), ln_b.reshape(1, -1), w_s, sg_bias]
    kv = KV_GROUPS * DK
    outs = [(d, F32, False)] if res is not None else []
    outs += [(A_WIDTH, BF16, False), (B_HEADS * DK, F32, True), (kv, F32, False), (kv, F32, False),
             (kv, BF16, False), (kv, BF16, True), (kv, BF16, False), (kv, BF16, True),
             (KV_GROUPS * LANES, F32, True)]
    res_out = pl.pallas_call(
        functools.partial(_pre_kernel, res is not None),
        grid=(b, s // tm),
        in_specs=in_specs,
        out_specs=[col(w) if t else row(w) for w, _, t in outs],
        out_shape=[jax.ShapeDtypeStruct((b, w, s) if t else (b, s, w), dt) for w, dt, t in outs],
        compiler_params=_params(("parallel", "parallel")),
        name="pre",
    )(*args)
    if res is None:
        res_out = [x] + list(res_out)
    return res_out


def _sgu_tile(uv, lng_ref, lnb_ref, w_ref, bias_ref, o_ref):
    rows = uv.shape[0]
    gu = jax.nn.gelu(uv[:, :A_WIDTH])
    gv = jax.nn.gelu(uv[:, A_WIDTH:])
    mu = jnp.mean(gv, axis=-1, keepdims=True)
    var = jnp.mean(jnp.square(gv - mu), axis=-1, keepdims=True)
    vn = ((gv - mu) * lax.rsqrt(var + NORM_EPS) * lng_ref[...] + lnb_ref[...]).astype(BF16)
    r = lax.broadcasted_iota(jnp.int32, (CHUNK, CHUNK), 0)
    c = lax.broadcasted_iota(jnp.int32, (CHUNK, CHUNK), 1)
    causal = c <= r
    lane_lo = lax.broadcasted_iota(jnp.int32, (CHUNK, LANES), 1) < DK
    for p in range(A_HEADS // 2):
        w0 = jnp.where(causal, w_ref[2 * p], 0.0).astype(BF16)
        w1 = jnp.where(causal, w_ref[2 * p + 1], 0.0).astype(BF16)
        bias = bias_ref[:, p * LANES:(p + 1) * LANES]
        for ch in range(rows // CHUNK):
            rs = slice(ch * CHUNK, (ch + 1) * CHUNK)
            cs = slice(p * LANES, (p + 1) * LANES)
            vp = vn[rs, cs]
            m0 = jnp.dot(w0, vp, preferred_element_type=F32)
            m1 = jnp.dot(w1, vp, preferred_element_type=F32)
            mixed = jnp.where(lane_lo, m0, m1) + bias
            o_ref[0, rs, cs] = (gu[rs, cs] * mixed).astype(o_ref.dtype)


def _compress_kernel(kr_ref, vr_ref, pek_ref, pev_ref, w1k_ref, w1v_ref, w2k_ref, w2v_ref,
                     cos_ref, sin_ref, ko_ref, vt_ref):
    nc = kr_ref.shape[1] // CMP_STRIDE

    def row_groups(t_ref):
        return jnp.concatenate([t_ref[0, pl.ds(l, nc, stride=CMP_STRIDE), :] for l in range(CMP_STRIDE)], axis=1)

    def mlp(r, pe_ref, w1_ref, w2_ref):
        top = jnp.dot(r + pe_ref[0:1], w1_ref[0], precision=HIGHEST, preferred_element_type=F32)
        bot = jnp.dot(r + pe_ref[1:2], w1_ref[1], precision=HIGHEST, preferred_element_type=F32)
        pre = top + pltpu.roll(bot, nc - 1, 0)
        return jnp.dot(jax.nn.gelu(pre), w2_ref[...], precision=HIGHEST, preferred_element_type=F32)

    kc = _rope(mlp(row_groups(kr_ref), pek_ref, w1k_ref, w2k_ref), cos_ref[0], sin_ref[0])
    vc = mlp(row_groups(vr_ref), pev_ref, w1v_ref, w2v_ref)
    lo = lax.broadcasted_iota(jnp.int32, (nc, LANES), 1) < DK
    rolled = pltpu.roll(kc, DK, 1)
    for g in range(KV_GROUPS):
        dup = jnp.where(lo, kc, rolled) if g == 0 else jnp.where(lo, rolled, kc)
        hi, low = _split_bf16(dup, 2)
        ko_ref[0, g, :, 0:LANES] = hi
        ko_ref[0, g, :, LANES:2 * LANES] = jnp.where(lo, low, jnp.zeros_like(low))
    vt_ref[0] = vc.T.astype(BF16)


def _compress_weights(pe, w1, w2):
    half = CMP_BLOCK // 2
    eye = jnp.eye(KV_GROUPS, dtype=F32)
    w1r = w1.reshape(CMP_BLOCK, DK, DK)
    wfull = jnp.einsum('lde,gh->lgdhe', w1r, eye)
    w1s = wfull.reshape(2, half * KV_GROUPS * DK, KV_GROUPS * DK)
    pes = jnp.broadcast_to(pe.reshape(2, half, 1, DK), (2, half, KV_GROUPS, DK)).reshape(2, -1)
    w2bd = jnp.einsum('de,gh->gdhe', w2, eye).reshape(KV_GROUPS * DK, KV_GROUPS * DK)
    return pes, w1s, w2bd


def _compress(kc, vc, pe_k, pe_v, w1_k, w2_k, w1_v, w2_v, cos_c, sin_c):
    b, s, _ = kc.shape
    nc = s // CMP_STRIDE
    rw = CMP_STRIDE * LANES
    pek, w1ks, w2kb = _compress_weights(pe_k, w1_k, w2_k)
    pev, w1vs, w2vb = _compress_weights(pe_v, w1_v, w2_v)
    full = lambda shape: pl.BlockSpec(shape, lambda i: (0,) * len(shape))
    return pl.pallas_call(
        _compress_kernel,
        grid=(b,),
        in_specs=[
            pl.BlockSpec((1, s, LANES), lambda i: (i, 0, 0)),
            pl.BlockSpec((1, s, LANES), lambda i: (i, 0, 0)),
            full((2, rw)), full((2, rw)),
            full((2, rw, LANES)), full((2, rw, LANES)),
            full((LANES, LANES)), full((LANES, LANES)),
            pl.BlockSpec((1, nc, LANES), lambda i: (i, 0, 0)),
            pl.BlockSpec((1, nc, LANES), lambda i: (i, 0, 0)),
        ],
        out_specs=[pl.BlockSpec((1, KV_GROUPS, nc, 2 * LANES), lambda i: (i, 0, 0, 0)),
                   pl.BlockSpec((1, KV_GROUPS * DK, nc), lambda i: (i, 0, 0))],
        out_shape=[jax.ShapeDtypeStruct((b, KV_GROUPS, nc, 2 * LANES), BF16),
                   jax.ShapeDtypeStruct((b, KV_GROUPS * DK, nc), BF16)],
        compiler_params=_params(("parallel",)),
        name="compress",
    )(kc, vc, pek, pev, w1ks, w1vs, w2kb, w2vb, cos_c, sin_c)


def _heads_on_lanes(qt):
    return jnp.concatenate([qt[h * DK:(h + 1) * DK] for h in range(HPG)], axis=1)


def _cmp_sel_kernel(n_top, qt_ref, k_ref, vt_ref, map_ref, oc_ref, bias_ref, imp_ref):
    tq = qt_ref.shape[2]
    cols = HPG * tq
    nc = k_ref.shape[2]
    nsel = map_ref.shape[0]
    q0 = pl.program_id(2) * tq
    q_hi, q_lo = _split_bf16(_heads_on_lanes(qt_ref[0]), 2)
    q3t = jnp.concatenate([q_hi, q_lo, q_hi, jnp.zeros_like(q_hi)], axis=0)
    t_col = q0 + lax.broadcasted_iota(jnp.int32, (1, cols), 1) % tq

    def compressed_branch(rows):
        s = jnp.dot(k_ref[0, 0, 0:rows], q3t, preferred_element_type=F32)
        n_idx = lax.broadcasted_iota(jnp.int32, (rows, 1), 0)
        s = jnp.where((CMP_STRIDE * n_idx + CMP_BLOCK - 1) <= t_col, s, -jnp.inf)
        m = jnp.max(s, axis=0, keepdims=True)
        m = jnp.where(m == -jnp.inf, 0.0, m)
        e = jnp.exp2(s - m)
        d = jnp.sum(e, axis=0, keepdims=True)
        p = e / jnp.where(d > 0, d, 1.0)
        oc = jnp.dot(vt_ref[0, :, 0:rows], p.astype(BF16), preferred_element_type=F32)
        oc_ref[0] = jnp.concatenate([oc[:, h * tq:(h + 1) * tq] for h in range(HPG)], axis=0)
        psum = p[:, 0:tq]
        for h in range(1, HPG):
            psum = psum + p[:, h * tq:(h + 1) * tq]
        imp_ref[...] = jnp.dot(jnp.concatenate([map_ref[:, 0:rows]] * 3, axis=1),
                               jnp.concatenate(_split_bf16(psum, 3), axis=0), preferred_element_type=F32)

    visible = (q0 + tq - CMP_BLOCK) // CMP_STRIDE + 1
    n_chunks = -(-nc // CMP_ROW_CHUNK)
    for r in range(1, n_chunks + 1):
        lo = (r - 1) * CMP_ROW_CHUNK if r > 1 else -nc
        pl.when((visible > lo) & (visible <= r * CMP_ROW_CHUNK) if r < n_chunks else visible > lo)(
            functools.partial(compressed_branch, min(r * CMP_ROW_CHUNK, nc)))

    def select(rows):
        imp = imp_ref[0:rows]
        j = lax.broadcasted_iota(jnp.int32, (rows, tq), 0)
        cur = (q0 + lax.broadcasted_iota(jnp.int32, (rows, tq), 1)) // SEL_BLOCK
        valid = j <= cur
        forced = (j == 0) | (j == cur) | (j == cur - 1)
        keep = forced | (valid & (cur < n_top))
        vals = jnp.where(valid & jnp.logical_not(forced), imp, -jnp.inf)
        bias = jnp.where(keep, 0.0, MASK_BIAS)
        for _ in range(n_top - N_FORCED):
            mx = jnp.max(vals, axis=0, keepdims=True)
            first = jnp.min(jnp.where(vals == mx, j, rows), axis=0, keepdims=True)
            pick = (j == first) & (mx > -jnp.inf)
            bias = jnp.where(pick, 0.0, bias)
            vals = jnp.where(pick, -jnp.inf, vals)
        bias_ref[0, 0, 0:rows] = bias.astype(BF16)
        if rows < nsel:
            bias_ref[0, 0, rows:nsel] = jnp.full((nsel - rows, tq), MASK_BIAS, BF16)

    n_valid = (q0 + tq - 1) // SEL_BLOCK + 1
    n_sel_chunks = -(-nsel // SEL_ROW_CHUNK)
    for r in range(1, n_sel_chunks + 1):
        lo = (r - 1) * SEL_ROW_CHUNK if r > 1 else -nsel
        pl.when((n_valid > lo) & (n_valid <= r * SEL_ROW_CHUNK) if r < n_sel_chunks else n_valid > lo)(
            functools.partial(select, min(r * SEL_ROW_CHUNK, nsel)))


def _sel_map_t(s):
    nc = s // CMP_STRIDE
    n_cmp = (s - CMP_BLOCK) // CMP_STRIDE + 1
    n_sel = s // SEL_BLOCK
    cs = CMP_STRIDE * np.arange(n_cmp)[:, None]
    ce = cs + CMP_BLOCK
    ss = SEL_BLOCK * np.arange(n_sel)[None, :]
    se = ss + SEL_BLOCK
    ov = np.clip(np.minimum(ce, se) - np.maximum(cs, ss), 0, None) / CMP_STRIDE
    out = np.zeros((n_sel, nc), np.float32)
    out[:, :n_cmp] = ov.T
    return out


def _cmp_sel(qt, kcmp, vcmp_t, tq=256):
    b, _, s = qt.shape
    nc = s // CMP_STRIDE
    nsel = s // SEL_BLOCK
    n_top = min(SEL_TOP, nsel)
    sel_map = jnp.asarray(_sel_map_t(s), dtype=BF16)
    return pl.pallas_call(
        functools.partial(_cmp_sel_kernel, n_top),
        grid=(b, KV_GROUPS, s // tq),
        in_specs=[
            pl.BlockSpec((1, GW, tq), lambda i, g, j: (i, g, j)),
            pl.BlockSpec((1, 1, nc, 2 * LANES), lambda i, g, j: (i, g, 0, 0)),
            pl.BlockSpec((1, DK, nc), lambda i, g, j: (i, g, 0)),
            pl.BlockSpec((nsel, nc), lambda i, g, j: (0, 0)),
        ],
        out_specs=[
            pl.BlockSpec((1, GW, tq), lambda i, g, j: (i, g, j)),
            pl.BlockSpec((1, 1, nsel, tq), lambda i, g, j: (i, g, 0, j)),
        ],
        out_shape=[
            jax.ShapeDtypeStruct((b, B_HEADS * DK, s), F32),
            jax.ShapeDtypeStruct((b, KV_GROUPS, nsel, s), BF16),
        ],
        scratch_shapes=[pltpu.VMEM((nsel, tq), F32)],
        compiler_params=_params(("parallel", "parallel", "parallel")),
        name="cmp_sel",
    )(qt, kcmp, vcmp_t, sel_map)


def _attn_kernel(kt, qt_ref, ks_ref, kw_ref, vst_ref, vwt_ref, bias_ref, oh_ref, oc_ref, glt_ref,
                 o_ref, qa_ref, s0_ref, s1_ref, m_ref, acc_ref, ow_ref):
    tq = qt_ref.shape[2]
    cols = HPG * tq
    nsel = bias_ref.shape[2]
    g = pl.program_id(1)
    q0 = pl.program_id(2) * tq

    q4t = _heads_on_lanes(qt_ref[0]).astype(BF16)
    zero = jnp.zeros_like(q4t)
    qa_ref[0:DK] = jnp.where(g == 0, q4t, zero)
    qa_ref[DK:2 * DK] = jnp.where(g == 1, q4t, zero)
    qa_ref[LANES:LANES + nsel] = jnp.concatenate([bias_ref[0, 0]] * HPG, axis=1)
    qaug = qa_ref[...]
    t_col = q0 + lax.broadcasted_iota(jnp.int32, (1, cols), 1) % tq

    def fold(x, op):
        return op(x.reshape(x.shape[0] // SUBLANES, SUBLANES, cols), axis=0)

    def with_ones(vt):
        return jnp.concatenate([vt, jnp.ones((ONES_ROWS, vt.shape[1]), BF16)], axis=0)

    def stage(c, s_ref, maybe_diagonal):
        k0 = pl.multiple_of(c * kt, kt)
        kaug = jnp.concatenate([ks_ref[0, pl.ds(k0, kt), :], oh_ref[pl.ds(k0, kt), :]], axis=1)
        s_ref[...] = jnp.dot(kaug, qaug, preferred_element_type=F32)
        if maybe_diagonal:
            r0 = pl.multiple_of(jnp.clip(q0 - k0, 0, kt - tq), tq)
            kpos = k0 + r0 + lax.broadcasted_iota(jnp.int32, (tq, 1), 0)
            s_ref[pl.ds(r0, tq), :] = jnp.where(kpos <= t_col, s_ref[pl.ds(r0, tq), :], -jnp.inf)

    def consume(c, s_ref):
        k0 = pl.multiple_of(c * kt, kt)
        sc = s_ref[...]
        m_old = m_ref[...]
        m_new = jnp.maximum(m_old, jnp.max(fold(sc, jnp.max), axis=0, keepdims=True))
        alpha = jnp.exp2(m_old - m_new)
        pr = jnp.exp2(sc - m_new)
        acc_ref[...] = alpha * acc_ref[...] + jnp.dot(with_ones(vst_ref[0, :, pl.ds(k0, kt)]), pr.astype(BF16),
                                                      preferred_element_type=F32)
        m_ref[...] = m_new

    m_ref[...] = jnp.full(m_ref.shape, -jnp.inf, F32)
    acc_ref[...] = jnp.zeros(acc_ref.shape, F32)
    n_full = q0 // kt
    n_pairs = n_full // 2
    odd = n_full % 2 == 1

    def window(full_history):
        wlen = WINDOW + tq
        w0 = pl.multiple_of(jnp.maximum(q0 - WINDOW, 0), tq)
        sw = jnp.dot(kw_ref[0, pl.ds(w0, wlen), :], qaug[0:LANES], preferred_element_type=F32)
        if full_history:
            c_idx = t_col - q0
            r = lax.broadcasted_iota(jnp.int32, (tq, 1), 0)
            sw = jnp.concatenate([jnp.where(r > c_idx, sw[0:tq], -jnp.inf), sw[tq:WINDOW],
                                  jnp.where(r <= c_idx, sw[WINDOW:wlen], -jnp.inf)], axis=0)
        else:
            dlt = t_col - (w0 + lax.broadcasted_iota(jnp.int32, (wlen, 1), 0))
            sw = jnp.where(lax.bitcast_convert_type(dlt, jnp.uint32) < jnp.uint32(WINDOW), sw, -jnp.inf)
        pw = jnp.exp2(sw - jnp.max(sw, axis=0, keepdims=True))
        o_win = jnp.dot(with_ones(vwt_ref[0, :, pl.ds(w0, wlen)]), pw.astype(BF16), preferred_element_type=F32)
        ow_ref[...] = o_win[0:DK] / o_win[DK:DK + 1]

    @pl.when(q0 >= WINDOW)
    def _():
        stage(0, s0_ref, kt > WINDOW)
        window(True)

    @pl.when(q0 < WINDOW)
    def _():
        stage(0, s0_ref, True)
        window(False)

    def pair(i, carry):
        stage(2 * i + 1, s1_ref, False)
        consume(2 * i, s0_ref)
        stage(2 * i + 2, s0_ref, True)
        consume(2 * i + 1, s1_ref)
        return carry

    lax.fori_loop(0, n_pairs, pair, 0)

    @pl.when(odd)
    def _():
        stage(n_full, s1_ref, True)
        consume(n_full - 1, s0_ref)
        consume(n_full, s1_ref)

    @pl.when(jnp.logical_not(odd))
    def _():
        consume(n_full, s0_ref)

    o_sel = acc_ref[0:DK] / acc_ref[DK:DK + 1]

    o_win = ow_ref[...]
    gates = jax.nn.sigmoid(glt_ref[0])
    oc = _heads_on_lanes(oc_ref[0])
    for p in range(HPG // 2):
        halves = []
        for hh in (2 * p, 2 * p + 1):
            cs = slice(hh * tq, (hh + 1) * tq)
            halves.append(gates[3 * hh:3 * hh + 1] * oc[:, cs] + gates[3 * hh + 1:3 * hh + 2] * o_sel[:, cs]
                          + gates[3 * hh + 2:3 * hh + 3] * o_win[:, cs])
        o_ref[0, :, p * LANES:(p + 1) * LANES] = jnp.concatenate(halves, axis=0).T.astype(o_ref.dtype)


def _attn(qt, ks, kw, vs_t, vw_t, bias_t, oc_t, gl_t, tq=256, kt=512):
    b, _, s = qt.shape
    nsel = s // SEL_BLOCK
    kt = min(kt, s)
    onehot = jnp.asarray((np.arange(s)[:, None] // SEL_BLOCK == np.arange(nsel)[None, :]), dtype=BF16)
    k_spec = pl.BlockSpec((1, s, KV_GROUPS * DK), lambda i, g, j: (i, 0, 0))
    vt_spec = pl.BlockSpec((1, DK, s), lambda i, g, j: (i, g, 0))
    cols = HPG * tq
    return pl.pallas_call(
        functools.partial(_attn_kernel, kt),
        grid=(b, KV_GROUPS, s // tq),
        in_specs=[
            pl.BlockSpec((1, GW, tq), lambda i, g, j: (i, g, j)),
            k_spec, k_spec, vt_spec, vt_spec,
            pl.BlockSpec((1, 1, nsel, tq), lambda i, g, j: (i, g, 0, j)),
            pl.BlockSpec((s, nsel), lambda i, g, j: (0, 0)),
            pl.BlockSpec((1, GW, tq), lambda i, g, j: (i, g, j)),
            pl.BlockSpec((1, LANES, tq), lambda i, g, j: (i, g, j)),
        ],
        out_specs=pl.BlockSpec((1, tq, GW), lambda i, g, j: (i, j, g)),
        out_shape=jax.ShapeDtypeStruct((b, s, B_HEADS * DK), BF16),
        scratch_shapes=[pltpu.VMEM((LANES + nsel, cols), BF16),
                        pltpu.VMEM((kt, cols), F32), pltpu.VMEM((kt, cols), F32),
                        pltpu.VMEM((1, cols), F32), pltpu.VMEM((DK + ONES_ROWS, cols), F32),
                        pltpu.VMEM((DK, cols), F32)],
        compiler_params=_params(("parallel", "parallel", "arbitrary")),
        name="attn",
    )(qt, ks, kw, vs_t, vw_t, bias_t, onehot, oc_t, gl_t)


def _post_kernel(ya_ref, yb_ref, wo_ref, x_ref, g1_ref, ng_ref, sc_ref, sh_ref, rw_ref, rb_ref,
                 xo_ref, h_ref, idx_ref, gate_ref, rank_ref, cnt_ref):
    tm = x_ref.shape[1]

    @pl.when((pl.program_id(0) == 0) & (pl.program_id(1) == 0))
    def _():
        cnt_ref[...] = jnp.zeros(cnt_ref.shape, F32)

    mixed = jnp.dot(ya_ref[0], wo_ref[:A_WIDTH], preferred_element_type=F32)
    mixed = mixed + jnp.dot(yb_ref[0], wo_ref[A_WIDTH:], preferred_element_type=F32)
    x = x_ref[0] + g1_ref[0] * mixed
    xo_ref[0] = x
    ms = jnp.mean(x * x, axis=-1, keepdims=True)
    h = x * lax.rsqrt(ms + NORM_EPS) * ng_ref[...]
    h = h * (1.0 + sc_ref[0]) + sh_ref[0]
    h_ref[0] = _pack_halves(h)
    h_hi, h_mid = _split_bf16(h, 2)
    logits = jnp.dot(jnp.concatenate([h_hi, h_hi, h_mid], axis=1), rw_ref[...],
                     preferred_element_type=F32) + rb_ref[...]
    vals = logits.T[:N_EXPERTS]
    e_idx = lax.broadcasted_iota(jnp.int32, (N_EXPERTS, tm), 0)
    top = jnp.max(vals, axis=0, keepdims=True)
    routed_t = jnp.zeros_like(vals)
    firsts, exps = [], []
    for k in range(TOP_K):
        m = jnp.max(vals, axis=0, keepdims=True)
        first = jnp.min(jnp.where(vals == m, e_idx, N_EXPERTS), axis=0, keepdims=True)
        pick = e_idx == first
        routed_t = jnp.where(pick, 1.0, routed_t)
        firsts.append(first)
        exps.append(jnp.exp(m - top))
        vals = jnp.where(pick, -jnp.inf, vals)
    idx_i = jnp.concatenate(firsts, axis=0)
    idx_t = idx_i.astype(F32)
    slot_exp = jnp.concatenate(exps, axis=0)
    idx_ref[...] = idx_i
    gate_ref[...] = slot_exp / jnp.sum(slot_exp, axis=0, keepdims=True)
    r = lax.broadcasted_iota(jnp.int32, (tm, tm), 0)
    c = lax.broadcasted_iota(jnp.int32, (tm, tm), 1)
    before = jnp.where(r < c, 1.0, 0.0).astype(BF16)
    rank_t = cnt_ref[...] + jnp.dot(routed_t.astype(BF16), before, preferred_element_type=F32)
    cnt_ref[...] += jnp.sum(routed_t, axis=-1, keepdims=True)
    e_row = lax.broadcasted_iota(jnp.int32, (N_EXPERTS, tm), 0).astype(F32)
    ranks = [jnp.sum(jnp.where(e_row == idx_t[k:k + 1], rank_t, 0.0), axis=0, keepdims=True)
             for k in range(TOP_K)]
    rank_ref[...] = jnp.concatenate(ranks, axis=0).astype(jnp.int32)


def _post(y_a, y_b, w_out, x, g1, norm_g, sc, sh, router_w, router_b, tm=512):
    b, s, d = x.shape
    tm = min(tm, s)
    nt = s // tm
    w_hi, w_mid = _split_bf16_bits(jnp.zeros((d, LANES), F32).at[:, :N_EXPERTS].set(router_w), 2)
    rw = jnp.concatenate([w_hi, w_mid, w_hi], axis=0)
    rb = jnp.full((1, LANES), -1e30, F32).at[0, :N_EXPERTS].set(router_b)
    vec = pl.BlockSpec((1, 1, d), lambda i, j: (i, 0, 0))
    row = lambda w: pl.BlockSpec((1, tm, w), lambda i, j: (i, j, 0))
    return pl.pallas_call(
        _post_kernel,
        grid=(b, nt),
        in_specs=[
            row(A_WIDTH), row(B_HEADS * DK),
            pl.BlockSpec((d, d), lambda i, j: (0, 0)),
            row(d), vec,
            pl.BlockSpec((1, d), lambda i, j: (0, 0)), vec, vec,
            pl.BlockSpec((3 * d, LANES), lambda i, j: (0, 0)),
            pl.BlockSpec((1, LANES), lambda i, j: (0, 0)),
        ],
        out_specs=[row(d), row(d // 2)] + [pl.BlockSpec((TOP_K, tm), lambda i, j: (0, i * nt + j))] * 3,
        out_shape=[
            jax.ShapeDtypeStruct((b, s, d), F32),
            jax.ShapeDtypeStruct((b, s, d // 2), jnp.int32),
            jax.ShapeDtypeStruct((TOP_K, b * s), jnp.int32),
            jax.ShapeDtypeStruct((TOP_K, b * s), F32),
            jax.ShapeDtypeStruct((TOP_K, b * s), jnp.int32),
        ],
        scratch_shapes=[pltpu.VMEM((N_EXPERTS, 1), F32)],
        compiler_params=_params(("arbitrary", "arbitrary")),
        name="post",
    )(y_a, y_b, w_out.astype(BF16), x, g1, norm_g.reshape(1, d), sc, sh, rw, rb)


def _sc_mesh():
    return plsc.VectorSubcoreMesh(core_axis_name="c", subcore_axis_name="s")


def _sc_worker():
    return lax.axis_index("c") * SC_SUBCORES + lax.axis_index("s")


def _dispatch_rows(h2, dest_c, n_rows):
    t, d = h2.shape
    per_worker = t // SC_WINDOW // SC_WORKERS

    def body(x_hbm, i_hbm, o_hbm, buf, idx):
        worker = _sc_worker()

        @pl.loop(0, per_worker)
        def _(j):
            ch = worker * per_worker + j
            pltpu.sync_copy(i_hbm.at[ch], idx)
            pltpu.sync_copy(x_hbm.at[pl.ds(ch * SC_WINDOW, SC_WINDOW)], buf)
            for k in range(TOP_K):
                pltpu.sync_copy(buf, o_hbm.at[idx.at[k]])

    return pl.kernel(
        body, out_type=jax.ShapeDtypeStruct((n_rows, d), h2.dtype), mesh=_sc_mesh(),
        scratch_types=[pltpu.VMEM((SC_WINDOW, d), h2.dtype), pltpu.VMEM((TOP_K, SC_WINDOW), jnp.int32)],
        name="dispatch_rows",
    )(h2, dest_c)


def _collect_rows(rows, dest_c, t):
    d = rows.shape[1]
    per_worker = t // SC_WINDOW // SC_WORKERS

    def body(r_hbm, i_hbm, o_hbm, buf, idx):
        worker = _sc_worker()

        @pl.loop(0, per_worker)
        def _(j):
            ch = worker * per_worker + j
            pltpu.sync_copy(i_hbm.at[ch], idx)
            for k in range(TOP_K):
                pltpu.sync_copy(r_hbm.at[idx.at[k]], buf)
                pltpu.sync_copy(buf, o_hbm.at[k, pl.ds(ch * SC_WINDOW, SC_WINDOW)])

    return pl.kernel(
        body, out_type=jax.ShapeDtypeStruct((TOP_K, t, d), rows.dtype), mesh=_sc_mesh(),
        scratch_types=[pltpu.VMEM((SC_WINDOW, d), rows.dtype), pltpu.VMEM((TOP_K, SC_WINDOW), jnp.int32)],
        name="collect_rows",
    )(rows, dest_c)


def _experts_kernel(layer, be_ref, nb_ref, slot_ref, next_ref, x_ref, w1_hbm, b1_ref, w2_hbm, b2_ref, o_ref,
                    w1f_ref, w2f_ref, w1b_ref, w2b_ref, sem):
    def weight_copies(expert, slot):
        return (pltpu.make_async_copy(w1_hbm.at[layer, expert], w1f_ref.at[slot], sem.at[slot, 0]),
                pltpu.make_async_copy(w2_hbm.at[layer, expert], w2f_ref.at[slot], sem.at[slot, 1]))

    def row_block(i, rs):
        e = be_ref[i]
        live = i < nb_ref[0]

        @pl.when(live & (i == 0))
        def _():
            for cp in weight_copies(e, 0):
                cp.start()

        @pl.when(live & ((i == 0) | (e != be_ref[jnp.maximum(i - 1, 0)])))
        def _():
            slot = slot_ref[i]
            for cp in weight_copies(e, slot):
                cp.wait()
            w1b_ref[...] = w1f_ref[slot].astype(BF16)
            w2b_ref[...] = w2f_ref[slot].astype(BF16)

            @pl.when(next_ref[i] >= 0)
            def _():
                for cp in weight_copies(next_ref[i], 1 - slot):
                    cp.start()

        @pl.when(live)
        def _():
            x = _unpack_halves(x_ref[rs, :]).astype(BF16)
            hdn = jnp.dot(x, w1b_ref[...], preferred_element_type=F32) + b1_ref[0, pl.ds(e, 1), :]
            glu = jnp.minimum(hdn[:, :D_EXPERT], SWIGLU_LIMIT)
            lin = jnp.clip(hdn[:, D_EXPERT:], -SWIGLU_LIMIT, SWIGLU_LIMIT)
            act = glu * jax.nn.sigmoid(SWIGLU_ALPHA * glu) * (lin + 1.0)
            out = jnp.dot(act.astype(BF16), w2b_ref[...], preferred_element_type=F32) + b2_ref[0, pl.ds(e, 1), :]
            o_ref[rs, :] = _pack_halves(out)

    for h in range(BLOCKS_PER_STEP):
        row_block(pl.program_id(0) * BLOCKS_PER_STEP + h, slice(h * ROW_BLOCK, (h + 1) * ROW_BLOCK))


def _experts(rows, block_exp, n_live, slot, next_exp, layer, w1, b1, w2, b2):
    n_rows, dp = rows.shape
    n_layers, n_e, d, f2 = w1.shape
    n_blocks = n_rows // ROW_BLOCK
    assert n_blocks % BLOCKS_PER_STEP == 0
    step_rows = BLOCKS_PER_STEP * ROW_BLOCK
    this_layer = lambda i, be, nb, sl, nx: (layer, 0, 0)
    grid_spec = pltpu.PrefetchScalarGridSpec(
        num_scalar_prefetch=4,
        grid=(n_blocks // BLOCKS_PER_STEP,),
        in_specs=[
            pl.BlockSpec((step_rows, dp), lambda i, be, nb, sl, nx: (i, 0)),
            pl.BlockSpec(memory_space=pl.ANY),
            pl.BlockSpec((1, n_e, f2), this_layer),
            pl.BlockSpec(memory_space=pl.ANY),
            pl.BlockSpec((1, n_e, d), this_layer),
        ],
        out_specs=pl.BlockSpec((step_rows, dp), lambda i, be, nb, sl, nx: (i, 0)),
        scratch_shapes=[pltpu.VMEM((2, d, f2), F32), pltpu.VMEM((2, f2 // 2, d), F32),
                        pltpu.VMEM((d, f2), BF16), pltpu.VMEM((f2 // 2, d), BF16),
                        pltpu.SemaphoreType.DMA((2, 2))],
    )
    return pl.pallas_call(
        functools.partial(_experts_kernel, layer),
        grid_spec=grid_spec,
        out_shape=jax.ShapeDtypeStruct((n_rows, dp), jnp.int32),
        compiler_params=_params(("arbitrary",)),
        name="experts",
    )(block_exp, n_live, slot, next_exp, rows, w1, b1, w2, b2)


def _moe_routed(h2, idx, rank, layer, w1, b1, w2, b2):
    t, d = h2.shape
    n_e = w1.shape[1]
    n_blocks = t * TOP_K // ROW_BLOCK + n_e
    onehot = (idx[:, :, None] == jnp.arange(n_e, dtype=jnp.int32)).astype(jnp.int32)
    counts = jnp.sum(onehot, axis=(0, 1))
    padded = (counts + ROW_BLOCK - 1) // ROW_BLOCK * ROW_BLOCK
    pad_end = jnp.cumsum(padded)
    dest = jnp.sum(onehot * (pad_end - padded), axis=-1) + rank
    block_row = jnp.arange(n_blocks, dtype=jnp.int32)[:, None] * ROW_BLOCK
    block_exp = jnp.minimum(jnp.sum((pad_end[None, :] <= block_row).astype(jnp.int32), axis=-1), n_e - 1)
    n_live = (pad_end[-1:] // ROW_BLOCK).astype(jnp.int32)
    experts = jnp.arange(n_e, dtype=jnp.int32)
    later = (experts[None, :] > experts[:, None]) & (counts[None, :] > 0)
    next_of = jnp.min(jnp.where(later, experts[None, :], n_e), axis=-1)
    next_of = jnp.where(next_of == n_e, -1, next_of)
    block_onehot = (block_exp[:, None] == experts[None, :]).astype(jnp.int32)
    next_exp = jnp.sum(block_onehot * next_of, axis=-1)
    run_of = jnp.cumsum((counts > 0).astype(jnp.int32)) - 1
    slot = jnp.sum(block_onehot * run_of, axis=-1) % 2
    dest_c = dest.reshape(TOP_K, t // SC_WINDOW, SC_WINDOW).transpose(1, 0, 2)
    rows = _dispatch_rows(h2, dest_c, n_blocks * ROW_BLOCK)
    out = _experts(rows, block_exp, n_live, slot, next_exp, layer, w1, b1, w2, b2)
    return _collect_rows(out, dest_c, t)


def _final_kernel(x_ref, rows_ref, gate_ref, g2_ref, fg_ref, o_ref):
    x = _moe_residual(x_ref, rows_ref, gate_ref, g2_ref)
    ms = jnp.mean(x * x, axis=-1, keepdims=True)
    o_ref[0] = x * lax.rsqrt(ms + NORM_EPS) * fg_ref[...]


def _final(x, res, final_g, tm=512):
    b, s, d = x.shape
    tm = min(tm, s)
    row = pl.BlockSpec((1, tm, d), lambda i, j: (i, j, 0))
    return pl.pallas_call(
        _final_kernel,
        grid=(b, s // tm),
        in_specs=[row] + _moe_residual_specs(tm, d) + [pl.BlockSpec((1, d), lambda i, j: (0, 0))],
        out_specs=row,
        out_shape=jax.ShapeDtypeStruct((b, s, d), F32),
        compiler_params=_params(("parallel", "parallel")),
        name="final",
    )(x, *res, final_g.reshape(1, d))


def _widen_w_in(w):
    gl0 = 2 * A_WIDTH + B_HEADS * DK + 6 * KV_GROUPS * DK
    per_group = HPG * 3
    gl_groups = [jnp.pad(w[:, gl0 + g * per_group:gl0 + (g + 1) * per_group], ((0, 0), (0, LANES - per_group)))
                 for g in range(KV_GROUPS)]
    wide = jnp.concatenate([w[:, :gl0]] + gl_groups, axis=1)
    assert gl0 == C_GL and wide.shape[1] == P_WIDE
    return wide.astype(BF16)


def _rope_tables(pos):
    half = DK // 2
    inv = ROPE_THETA ** (-jnp.arange(half, dtype=F32) / half)
    ang = pos.astype(F32)[..., None] * inv
    cos = jnp.cos(ang)
    sin = jnp.sin(ang)
    reps = LANES // DK
    return (jnp.concatenate([cos, cos] * reps, axis=-1),
            jnp.concatenate([-sin, sin] * reps, axis=-1))


def kernel(x, c, positions, ada_w, ada_b, norm1_g, norm2_g, w_in, w_out, sg_ln_g, sg_ln_b, sg_w, sg_b,
           cmp_pe_k, cmp_pe_v, cmp_w1_k, cmp_w2_k, cmp_w1_v, cmp_w2_v, router_w, router_b,
           exp_w1, exp_b1, exp_w2, exp_b2, final_g):
    b, s, d = x.shape
    n_layers = ada_w.shape[0]
    mod = _ada_mod(c, ada_w, ada_b)
    cos, sin = _rope_tables(positions)
    cmp_end = jnp.minimum(CMP_STRIDE * jnp.arange(s // CMP_STRIDE) + CMP_BLOCK - 1, s - 1)
    cos_c, sin_c = _rope_tables(positions[:, cmp_end])
    res = None
    for l in range(n_layers):
        sh1, sc1, g1, sh2, sc2, g2 = [m.reshape(b, 1, d) for m in jnp.split(mod[l], 6, axis=-1)]
        x, y_a, qt, kc, vc, ks, vs_t, kw, vw_t, gl_t = _pre(
            x, res, norm1_g[l], sc1, sh1, _widen_w_in(w_in[l]), cos, sin,
            (sg_ln_g[l], sg_ln_b[l], sg_w[l], sg_b[l]))
        kcmp, vcmp_t = _compress(kc, vc, cmp_pe_k[l], cmp_pe_v[l], cmp_w1_k[l], cmp_w2_k[l],
                                 cmp_w1_v[l], cmp_w2_v[l], cos_c, sin_c)
        oc_t, bias_t = _cmp_sel(qt, kcmp, vcmp_t)
        y_b = _attn(qt, ks, kw, vs_t, vw_t, bias_t, oc_t, gl_t)
        x, h2, idx, gate, rank = _post(y_a, y_b, w_out[l], x, g1, norm2_g[l], sc2, sh2,
                                       router_w[l], router_b[l])
        rows = _moe_routed(h2.reshape(b * s, d // 2), idx, rank, l, exp_w1, exp_b1, exp_w2, exp_b2)
        res = (rows.reshape(TOP_K, b, s, d // 2), gate.T.reshape(b, s, TOP_K), g2)
    return _final(x, res, final_g)
```

```python
import functools

import numpy as np
import jax
import jax.numpy as jnp
from jax import lax
from jax.experimental import pallas as pl
from jax.experimental.pallas import tpu as pltpu
from jax.experimental.pallas import tpu_sc as plsc

F32 = jnp.float32
BF16 = jnp.bfloat16
HIGHEST = lax.Precision.HIGHEST

A_WIDTH = 512
A_HEADS = 8
CHUNK = 128
B_HEADS = 8
DK = 64
KV_GROUPS = 2
HPG = B_HEADS // KV_GROUPS
CMP_BLOCK = 32
CMP_STRIDE = 16
SEL_BLOCK = 64
SEL_TOP = 16
WINDOW = 512
ROPE_THETA = 10000.0
N_EXPERTS = 32
TOP_K = 4
D_EXPERT = 1024
SWIGLU_LIMIT = 7.0
SWIGLU_ALPHA = 1.702
NORM_EPS = 1e-6

LANES = 128
SUBLANES = 8
GW = HPG * DK
N_FORCED = 3
ONES_ROWS = 16
MASK_BIAS = -32768.0
Q_SCALE = DK ** -0.5 * 1.4426950408889634
VMEM_LIMIT = 56 * 1024 * 1024
ROW_BLOCK = 256
BLOCKS_PER_STEP = 4
SC_CORES = 2
SC_SUBCORES = 16
SC_WORKERS = SC_CORES * SC_SUBCORES
SC_WINDOW = 128
CMP_ROW_CHUNK = 128
SEL_ROW_CHUNK = 32

C_U, C_V, C_Q = 0, 512, 1024
C_KC, C_VC, C_KS, C_VS, C_KW, C_VW = 1536, 1664, 1792, 1920, 2048, 2176
C_GL = 2304
P_WIDE = 2560


def _params(sem):
    return pltpu.CompilerParams(dimension_semantics=sem, vmem_limit_bytes=VMEM_LIMIT)


def _split_bf16(x, parts):
    out = []
    for _ in range(parts):
        piece = x.astype(BF16)
        out.append(piece)
        x = x - piece.astype(F32)
    return out


def _split_bf16_bits(x, parts):
    out = []
    for _ in range(parts):
        piece = lax.bitcast_convert_type(lax.bitcast_convert_type(x, jnp.uint32) & jnp.uint32(0xFFFF0000), F32)
        out.append(piece.astype(BF16))
        x = x - piece
    return out


def _ada_kernel(c_ref, w_ref, b_ref, o_ref):
    c = c_ref[...]
    cond = c * jax.nn.sigmoid(c)
    o_ref[0] = jnp.dot(cond, w_ref[0], precision=HIGHEST, preferred_element_type=F32) + b_ref[0]


def _ada_mod(c, ada_w, ada_b):
    n_layers, d, d6 = ada_w.shape
    b = c.shape[0]
    rows = SUBLANES
    c_pad = jnp.zeros((rows, d), F32).at[:b].set(c)
    out = pl.pallas_call(
        _ada_kernel,
        grid=(n_layers, d6 // d),
        in_specs=[
            pl.BlockSpec((rows, d), lambda l, j: (0, 0)),
            pl.BlockSpec((1, d, d), lambda l, j: (l, 0, j)),
            pl.BlockSpec((1, 1, d), lambda l, j: (l, 0, j)),
        ],
        out_specs=pl.BlockSpec((1, rows, d), lambda l, j: (l, 0, j)),
        out_shape=jax.ShapeDtypeStruct((n_layers, rows, d6), F32),
        compiler_params=_params(("arbitrary", "arbitrary")),
        name="ada",
    )(c_pad, ada_w, ada_b.reshape(n_layers, 1, d6))
    return out[:, :b]


def _rope_slab(t, cos, sin_signed, lo):
    partner = jnp.where(lo, pltpu.roll(t, LANES - DK // 2, 1), pltpu.roll(t, DK // 2, 1))
    return t * cos + partner * sin_signed


def _rope(t, cos, sin_signed):
    lane = lax.broadcasted_iota(jnp.int32, (1, LANES), 1)
    lo = (lane % DK) < (DK // 2)
    slabs = [_rope_slab(t[:, s * LANES:(s + 1) * LANES], cos, sin_signed, lo)
             for s in range(t.shape[1] // LANES)]
    return slabs[0] if len(slabs) == 1 else jnp.concatenate(slabs, axis=1)


def _pack_halves(x):
    n = x.shape[1] // 2
    lo = lax.bitcast_convert_type(x[:, :n].astype(BF16).astype(F32), jnp.int32)
    hi = lax.bitcast_convert_type(x[:, n:].astype(BF16).astype(F32), jnp.int32)
    return lax.shift_right_logical(lo, jnp.int32(16)) | (hi & jnp.int32(-65536))


def _unpack_halves(p):
    lo = lax.bitcast_convert_type(lax.shift_left(p, jnp.int32(16)), F32)
    hi = lax.bitcast_convert_type(p & jnp.int32(-65536), F32)
    return jnp.concatenate([lo, hi], axis=1)


def _moe_residual(x_ref, rows_ref, gate_ref, g2_ref):
    gate = gate_ref[0]
    y = gate[:, 0:1] * _unpack_halves(rows_ref[0, 0])
    for k in range(1, TOP_K):
        y = y + gate[:, k:k + 1] * _unpack_halves(rows_ref[k, 0])
    return x_ref[0] + g2_ref[0] * y


def _moe_residual_specs(tm, d):
    return [pl.BlockSpec((TOP_K, 1, tm, d // 2), lambda i, j: (0, i, j, 0)),
            pl.BlockSpec((1, tm, TOP_K), lambda i, j: (i, j, 0)),
            pl.BlockSpec((1, 1, d), lambda i, j: (i, 0, 0))]


def _pre_kernel(has_res, *refs):
    if has_res:
        (x_ref, rows_ref, gate_ref, g2_ref, ng_ref, sc_ref, sh_ref, w_ref, cos_ref, sin_ref,
         lng_ref, lnb_ref, sgw_ref, sgb_ref,
         xo_ref, ya_ref, qt_ref, kc_ref, vc_ref, ks_ref, vst_ref, kw_ref, vwt_ref, glt_ref) = refs
        x = _moe_residual(x_ref, rows_ref, gate_ref, g2_ref)
        xo_ref[0] = x
    else:
        (x_ref, ng_ref, sc_ref, sh_ref, w_ref, cos_ref, sin_ref, lng_ref, lnb_ref, sgw_ref, sgb_ref,
         ya_ref, qt_ref, kc_ref, vc_ref, ks_ref, vst_ref, kw_ref, vwt_ref, glt_ref) = refs
        x = x_ref[0]
    ms = jnp.mean(x * x, axis=-1, keepdims=True)
    h = x * lax.rsqrt(ms + NORM_EPS) * ng_ref[...]
    h = h * (1.0 + sc_ref[0]) + sh_ref[0]
    proj = jnp.dot(h.astype(BF16), w_ref[...], preferred_element_type=F32)
    cos = cos_ref[0]
    sin = sin_ref[0]
    _sgu_tile(proj[:, C_U:C_Q], lng_ref, lnb_ref, sgw_ref, sgb_ref, ya_ref)
    qt_ref[0] = (_rope(proj[:, C_Q:C_KC], cos, sin) * Q_SCALE).T
    kc_ref[0] = proj[:, C_KC:C_VC]
    vc_ref[0] = proj[:, C_VC:C_KS]
    ks_ref[0] = _rope(proj[:, C_KS:C_VS], cos, sin).astype(BF16)
    vst_ref[0] = proj[:, C_VS:C_KW].T.astype(BF16)
    kw_ref[0] = _rope(proj[:, C_KW:C_VW], cos, sin).astype(BF16)
    vwt_ref[0] = proj[:, C_VW:C_GL].T.astype(BF16)
    glt_ref[0] = proj[:, C_GL:P_WIDE].T


def _pre(x, res, norm_g, sc, sh, w_wide, cos, sin, sgu_params, tm=512):
    b, s, d = x.shape
    tm = min(tm, s)
    assert tm % CHUNK == 0
    ln_g, ln_b, w_s, b_s = sgu_params
    sg_bias = jnp.repeat(b_s.T, A_WIDTH // A_HEADS, axis=1)
    row = lambda w: pl.BlockSpec((1, tm, w), lambda i, j: (i, j, 0))
    col = lambda w: pl.BlockSpec((1, w, tm), lambda i, j: (i, 0, j))
    vec = pl.BlockSpec((1, 1, d), lambda i, j: (i, 0, 0))
    in_specs = [row(d)]
    args = [x]
    if res is not None:
        in_specs += _moe_residual_specs(tm, d)
        args += list(res)
    in_specs += [pl.BlockSpec((1, d), lambda i, j: (0, 0)), vec, vec,
                 pl.BlockSpec((d, P_WIDE), lambda i, j: (0, 0)), row(LANES), row(LANES),
                 pl.BlockSpec((1, A_WIDTH), lambda i, j: (0, 0)), pl.BlockSpec((1, A_WIDTH), lambda i, j: (0, 0)),
                 pl.BlockSpec((A_HEADS, CHUNK, CHUNK), lambda i, j: (0, 0, 0)),
                 pl.BlockSpec((CHUNK, A_WIDTH), lambda i, j: (0, 0))]
    args += [norm_g.reshape(1, d), sc, sh, w_wide, cos, sin,
             ln_g.reshape(1, -1), ln_b.reshape(1, -1), w_s, sg_bias]
    kv = KV_GROUPS * DK
    outs = [(d, F32, False)] if res is not None else []
    outs += [(A_WIDTH, BF16, False), (B_HEADS * DK, F32, True), (kv, F32, False), (kv, F32, False),
             (kv, BF16, False), (kv, BF16, True), (kv, BF16, False), (kv, BF16, True),
             (KV_GROUPS * LANES, F32, True)]
    res_out = pl.pallas_call(
        functools.partial(_pre_kernel, res is not None),
        grid=(b, s // tm),
        in_specs=in_specs,
        out_specs=[col(w) if t else row(w) for w, _, t in outs],
        out_shape=[jax.ShapeDtypeStruct((b, w, s) if t else (b, s, w), dt) for w, dt, t in outs],
        compiler_params=_params(("parallel", "parallel")),
        name="pre",
    )(*args)
    if res is None:
        res_out = [x] + list(res_out)
    return res_out


def _sgu_tile(uv, lng_ref, lnb_ref, w_ref, bias_ref, o_ref):
    rows = uv.shape[0]
    gu = jax.nn.gelu(uv[:, :A_WIDTH])
    gv = jax.nn.gelu(uv[:, A_WIDTH:])
    mu = jnp.mean(gv, axis=-1, keepdims=True)
    var = jnp.mean(jnp.square(gv - mu), axis=-1, keepdims=True)
    vn = ((gv - mu) * lax.rsqrt(var + NORM_EPS) * lng_ref[...] + lnb_ref[...]).astype(BF16)
    r = lax.broadcasted_iota(jnp.int32, (CHUNK, CHUNK), 0)
    c = lax.broadcasted_iota(jnp.int32, (CHUNK, CHUNK), 1)
    causal = c <= r
    lane_lo = lax.broadcasted_iota(jnp.int32, (CHUNK, LANES), 1) < DK
    for p in range(A_HEADS // 2):
        w0 = jnp.where(causal, w_ref[2 * p], 0.0).astype(BF16)
        w1 = jnp.where(causal, w_ref[2 * p + 1], 0.0).astype(BF16)
        bias = bias_ref[:, p * LANES:(p + 1) * LANES]
        for ch in range(rows // CHUNK):
            rs = slice(ch * CHUNK, (ch + 1) * CHUNK)
            cs = slice(p * LANES, (p + 1) * LANES)
            vp = vn[rs, cs]
            m0 = jnp.dot(w0, vp, preferred_element_type=F32)
            m1 = jnp.dot(w1, vp, preferred_element_type=F32)
            mixed = jnp.where(lane_lo, m0, m1) + bias
            o_ref[0, rs, cs] = (gu[rs, cs] * mixed).astype(o_ref.dtype)


def _compress_kernel(kr_ref, vr_ref, pek_ref, pev_ref, w1k_ref, w1v_ref, w2k_ref, w2v_ref,
                     cos_ref, sin_ref, ko_ref, vt_ref):
    nc = kr_ref.shape[1] // CMP_STRIDE

    def row_groups(t_ref):
        return jnp.concatenate([t_ref[0, pl.ds(l, nc, stride=CMP_STRIDE), :] for l in range(CMP_STRIDE)], axis=1)

    def mlp(r, pe_ref, w1_ref, w2_ref):
        top = jnp.dot(r + pe_ref[0:1], w1_ref[0], precision=HIGHEST, preferred_element_type=F32)
        bot = jnp.dot(r + pe_ref[1:2], w1_ref[1], precision=HIGHEST, preferred_element_type=F32)
        pre = top + pltpu.roll(bot, nc - 1, 0)
        return jnp.dot(jax.nn.gelu(pre), w2_ref[...], precision=HIGHEST, preferred_element_type=F32)

    kc = _rope(mlp(row_groups(kr_ref), pek_ref, w1k_ref, w2k_ref), cos_ref[0], sin_ref[0])
    vc = mlp(row_groups(vr_ref), pev_ref, w1v_ref, w2v_ref)
    lo = lax.broadcasted_iota(jnp.int32, (nc, LANES), 1) < DK
    rolled = pltpu.roll(kc, DK, 1)
    for g in range(KV_GROUPS):
        dup = jnp.where(lo, kc, rolled) if g == 0 else jnp.where(lo, rolled, kc)
        hi, low = _split_bf16(dup, 2)
        ko_ref[0, g, :, 0:LANES] = hi
        ko_ref[0, g, :, LANES:2 * LANES] = jnp.where(lo, low, jnp.zeros_like(low))
    vt_ref[0] = vc.T.astype(BF16)


def _compress_weights(pe, w1, w2):
    half = CMP_BLOCK // 2
    eye = jnp.eye(KV_GROUPS, dtype=F32)
    w1r = w1.reshape(CMP_BLOCK, DK, DK)
    wfull = jnp.einsum('lde,gh->lgdhe', w1r, eye)
    w1s = wfull.reshape(2, half * KV_GROUPS * DK, KV_GROUPS * DK)
    pes = jnp.broadcast_to(pe.reshape(2, half, 1, DK), (2, half, KV_GROUPS, DK)).reshape(2, -1)
    w2bd = jnp.einsum('de,gh->gdhe', w2, eye).reshape(KV_GROUPS * DK, KV_GROUPS * DK)
    return pes, w1s, w2bd


def _compress(kc, vc, pe_k, pe_v, w1_k, w2_k, w1_v, w2_v, cos_c, sin_c):
    b, s, _ = kc.shape
    nc = s // CMP_STRIDE
    rw = CMP_STRIDE * LANES
    pek, w1ks, w2kb = _compress_weights(pe_k, w1_k, w2_k)
    pev, w1vs, w2vb = _compress_weights(pe_v, w1_v, w2_v)
    full = lambda shape: pl.BlockSpec(shape, lambda i: (0,) * len(shape))
    return pl.pallas_call(
        _compress_kernel,
        grid=(b,),
        in_specs=[
            pl.BlockSpec((1, s, LANES), lambda i: (i, 0, 0)),
            pl.BlockSpec((1, s, LANES), lambda i: (i, 0, 0)),
            full((2, rw)), full((2, rw)),
            full((2, rw, LANES)), full((2, rw, LANES)),
            full((LANES, LANES)), full((LANES, LANES)),
            pl.BlockSpec((1, nc, LANES), lambda i: (i, 0, 0)),
            pl.BlockSpec((1, nc, LANES), lambda i: (i, 0, 0)),
        ],
        out_specs=[pl.BlockSpec((1, KV_GROUPS, nc, 2 * LANES), lambda i: (i, 0, 0, 0)),
                   pl.BlockSpec((1, KV_GROUPS * DK, nc), lambda i: (i, 0, 0))],
        out_shape=[jax.ShapeDtypeStruct((b, KV_GROUPS, nc, 2 * LANES), BF16),
                   jax.ShapeDtypeStruct((b, KV_GROUPS * DK, nc), BF16)],
        compiler_params=_params(("parallel",)),
        name="compress",
    )(kc, vc, pek, pev, w1ks, w1vs, w2kb, w2vb, cos_c, sin_c)


def _heads_on_lanes(qt):
    return jnp.concatenate([qt[h * DK:(h + 1) * DK] for h in range(HPG)], axis=1)


def _cmp_sel_kernel(n_top, qt_ref, k_ref, vt_ref, map_ref, oc_ref, bias_ref, imp_ref):
    tq = qt_ref.shape[2]
    cols = HPG * tq
    nc = k_ref.shape[2]
    nsel = map_ref.shape[0]
    q0 = pl.program_id(2) * tq
    q_hi, q_lo = _split_bf16(_heads_on_lanes(qt_ref[0]), 2)
    q3t = jnp.concatenate([q_hi, q_lo, q_hi, jnp.zeros_like(q_hi)], axis=0)
    t_col = q0 + lax.broadcasted_iota(jnp.int32, (1, cols), 1) % tq

    def compressed_branch(rows):
        s = jnp.dot(k_ref[0, 0, 0:rows], q3t, preferred_element_type=F32)
        n_idx = lax.broadcasted_iota(jnp.int32, (rows, 1), 0)
        s = jnp.where((CMP_STRIDE * n_idx + CMP_BLOCK - 1) <= t_col, s, -jnp.inf)
        m = jnp.max(s, axis=0, keepdims=True)
        m = jnp.where(m == -jnp.inf, 0.0, m)
        e = jnp.exp2(s - m)
        d = jnp.sum(e, axis=0, keepdims=True)
        p = e / jnp.where(d > 0, d, 1.0)
        oc = jnp.dot(vt_ref[0, :, 0:rows], p.astype(BF16), preferred_element_type=F32)
        oc_ref[0] = jnp.concatenate([oc[:, h * tq:(h + 1) * tq] for h in range(HPG)], axis=0)
        psum = p[:, 0:tq]
        for h in range(1, HPG):
            psum = psum + p[:, h * tq:(h + 1) * tq]
        imp_ref[...] = jnp.dot(jnp.concatenate([map_ref[:, 0:rows]] * 3, axis=1),
                               jnp.concatenate(_split_bf16(psum, 3), axis=0), preferred_element_type=F32)

    visible = (q0 + tq - CMP_BLOCK) // CMP_STRIDE + 1
    n_chunks = -(-nc // CMP_ROW_CHUNK)
    for r in range(1, n_chunks + 1):
        lo = (r - 1) * CMP_ROW_CHUNK if r > 1 else -nc
        pl.when((visible > lo) & (visible <= r * CMP_ROW_CHUNK) if r < n_chunks else visible > lo)(
            functools.partial(compressed_branch, min(r * CMP_ROW_CHUNK, nc)))

    def select(rows):
        imp = imp_ref[0:rows]
        j = lax.broadcasted_iota(jnp.int32, (rows, tq), 0)
        cur = (q0 + lax.broadcasted_iota(jnp.int32, (rows, tq), 1)) // SEL_BLOCK
        valid = j <= cur
        forced = (j == 0) | (j == cur) | (j == cur - 1)
        keep = forced | (valid & (cur < n_top))
        vals = jnp.where(valid & jnp.logical_not(forced), imp, -jnp.inf)
        bias = jnp.where(keep, 0.0, MASK_BIAS)
        for _ in range(n_top - N_FORCED):
            mx = jnp.max(vals, axis=0, keepdims=True)
            first = jnp.min(jnp.where(vals == mx, j, rows), axis=0, keepdims=True)
            pick = (j == first) & (mx > -jnp.inf)
            bias = jnp.where(pick, 0.0, bias)
            vals = jnp.where(pick, -jnp.inf, vals)
        bias_ref[0, 0, 0:rows] = bias.astype(BF16)
        if rows < nsel:
            bias_ref[0, 0, rows:nsel] = jnp.full((nsel - rows, tq), MASK_BIAS, BF16)

    n_valid = (q0 + tq - 1) // SEL_BLOCK + 1
    n_sel_chunks = -(-nsel // SEL_ROW_CHUNK)
    for r in range(1, n_sel_chunks + 1):
        lo = (r - 1) * SEL_ROW_CHUNK if r > 1 else -nsel
        pl.when((n_valid > lo) & (n_valid <= r * SEL_ROW_CHUNK) if r < n_sel_chunks else n_valid > lo)(
            functools.partial(select, min(r * SEL_ROW_CHUNK, nsel)))


def _sel_map_t(s):
    nc = s // CMP_STRIDE
    n_cmp = (s - CMP_BLOCK) // CMP_STRIDE + 1
    n_sel = s // SEL_BLOCK
    cs = CMP_STRIDE * np.arange(n_cmp)[:, None]
    ce = cs + CMP_BLOCK
    ss = SEL_BLOCK * np.arange(n_sel)[None, :]
    se = ss + SEL_BLOCK
    ov = np.clip(np.minimum(ce, se) - np.maximum(cs, ss), 0, None) / CMP_STRIDE
    out = np.zeros((n_sel, nc), np.float32)
    out[:, :n_cmp] = ov.T
    return out


def _cmp_sel(qt, kcmp, vcmp_t, tq=256):
    b, _, s = qt.shape
    nc = s // CMP_STRIDE
    nsel = s // SEL_BLOCK
    n_top = min(SEL_TOP, nsel)
    sel_map = jnp.asarray(_sel_map_t(s), dtype=BF16)
    return pl.pallas_call(
        functools.partial(_cmp_sel_kernel, n_top),
        grid=(b, KV_GROUPS, s // tq),
        in_specs=[
            pl.BlockSpec((1, GW, tq), lambda i, g, j: (i, g, j)),
            pl.BlockSpec((1, 1, nc, 2 * LANES), lambda i, g, j: (i, g, 0, 0)),
            pl.BlockSpec((1, DK, nc), lambda i, g, j: (i, g, 0)),
            pl.BlockSpec((nsel, nc), lambda i, g, j: (0, 0)),
        ],
        out_specs=[
            pl.BlockSpec((1, GW, tq), lambda i, g, j: (i, g, j)),
            pl.BlockSpec((1, 1, nsel, tq), lambda i, g, j: (i, g, 0, j)),
        ],
        out_shape=[
            jax.ShapeDtypeStruct((b, B_HEADS * DK, s), F32),
            jax.ShapeDtypeStruct((b, KV_GROUPS, nsel, s), BF16),
        ],
        scratch_shapes=[pltpu.VMEM((nsel, tq), F32)],
        compiler_params=_params(("parallel", "parallel", "parallel")),
        name="cmp_sel",
    )(qt, kcmp, vcmp_t, sel_map)


def _attn_kernel(kt, qt_ref, ks_ref, kw_ref, vst_ref, vwt_ref, bias_ref, oh_ref, oc_ref, glt_ref,
                 o_ref, qa_ref, s0_ref, s1_ref, m_ref, acc_ref, ow_ref):
    tq = qt_ref.shape[2]
    cols = HPG * tq
    nsel = bias_ref.shape[2]
    g = pl.program_id(1)
    q0 = pl.program_id(2) * tq

    q4t = _heads_on_lanes(qt_ref[0]).astype(BF16)
    zero = jnp.zeros_like(q4t)
    qa_ref[0:DK] = jnp.where(g == 0, q4t, zero)
    qa_ref[DK:2 * DK] = jnp.where(g == 1, q4t, zero)
    qa_ref[LANES:LANES + nsel] = jnp.concatenate([bias_ref[0, 0]] * HPG, axis=1)
    qaug = qa_ref[...]
    t_col = q0 + lax.broadcasted_iota(jnp.int32, (1, cols), 1) % tq

    def fold(x, op):
        return op(x.reshape(x.shape[0] // SUBLANES, SUBLANES, cols), axis=0)

    def with_ones(vt):
        return jnp.concatenate([vt, jnp.ones((ONES_ROWS, vt.shape[1]), BF16)], axis=0)

    def stage(c, s_ref, maybe_diagonal):
        k0 = pl.multiple_of(c * kt, kt)
        kaug = jnp.concatenate([ks_ref[0, pl.ds(k0, kt), :], oh_ref[pl.ds(k0, kt), :]], axis=1)
        s_ref[...] = jnp.dot(kaug, qaug, preferred_element_type=F32)
        if maybe_diagonal:
            r0 = pl.multiple_of(jnp.clip(q0 - k0, 0, kt - tq), tq)
            kpos = k0 + r0 + lax.broadcasted_iota(jnp.int32, (tq, 1), 0)
            s_ref[pl.ds(r0, tq), :] = jnp.where(kpos <= t_col, s_ref[pl.ds(r0, tq), :], -jnp.inf)

    def consume(c, s_ref):
        k0 = pl.multiple_of(c * kt, kt)
        sc = s_ref[...]
        m_old = m_ref[...]
        m_new = jnp.maximum(m_old, jnp.max(fold(sc, jnp.max), axis=0, keepdims=True))
        alpha = jnp.exp2(m_old - m_new)
        pr = jnp.exp2(sc - m_new)
        acc_ref[...] = alpha * acc_ref[...] + jnp.dot(with_ones(vst_ref[0, :, pl.ds(k0, kt)]), pr.astype(BF16),
                                                      preferred_element_type=F32)
        m_ref[...] = m_new

    m_ref[...] = jnp.full(m_ref.shape, -jnp.inf, F32)
    acc_ref[...] = jnp.zeros(acc_ref.shape, F32)
    n_full = q0 // kt
    n_pairs = n_full // 2
    odd = n_full % 2 == 1

    def window(full_history):
        wlen = WINDOW + tq
        w0 = pl.multiple_of(jnp.maximum(q0 - WINDOW, 0), tq)
        sw = jnp.dot(kw_ref[0, pl.ds(w0, wlen), :], qaug[0:LANES], preferred_element_type=F32)
        if full_history:
            c_idx = t_col - q0
            r = lax.broadcasted_iota(jnp.int32, (tq, 1), 0)
            sw = jnp.concatenate([jnp.where(r > c_idx, sw[0:tq], -jnp.inf), sw[tq:WINDOW],
                                  jnp.where(r <= c_idx, sw[WINDOW:wlen], -jnp.inf)], axis=0)
        else:
            dlt = t_col - (w0 + lax.broadcasted_iota(jnp.int32, (wlen, 1), 0))
            sw = jnp.where(lax.bitcast_convert_type(dlt, jnp.uint32) < jnp.uint32(WINDOW), sw, -jnp.inf)
        pw = jnp.exp2(sw - jnp.max(sw, axis=0, keepdims=True))
        o_win = jnp.dot(with_ones(vwt_ref[0, :, pl.ds(w0, wlen)]), pw.astype(BF16), preferred_element_type=F32)
        ow_ref[...] = o_win[0:DK] / o_win[DK:DK + 1]

    @pl.when(q0 >= WINDOW)
    def _():
        stage(0, s0_ref, kt > WINDOW)
        window(True)

    @pl.when(q0 < WINDOW)
    def _():
        stage(0, s0_ref, True)
        window(False)

    def pair(i, carry):
        stage(2 * i + 1, s1_ref, False)
        consume(2 * i, s0_ref)
        stage(2 * i + 2, s0_ref, True)
        consume(2 * i + 1, s1_ref)
        return carry

    lax.fori_loop(0, n_pairs, pair, 0)

    @pl.when(odd)
    def _():
        stage(n_full, s1_ref, True)
        consume(n_full - 1, s0_ref)
        consume(n_full, s1_ref)

    @pl.when(jnp.logical_not(odd))
    def _():
        consume(n_full, s0_ref)

    o_sel = acc_ref[0:DK] / acc_ref[DK:DK + 1]

    o_win = ow_ref[...]
    gates = jax.nn.sigmoid(glt_ref[0])
    oc = _heads_on_lanes(oc_ref[0])
    for p in range(HPG // 2):
        halves = []
        for hh in (2 * p, 2 * p + 1):
            cs = slice(hh * tq, (hh + 1) * tq)
            halves.append(gates[3 * hh:3 * hh + 1] * oc[:, cs] + gates[3 * hh + 1:3 * hh + 2] * o_sel[:, cs]
                          + gates[3 * hh + 2:3 * hh + 3] * o_win[:, cs])
        o_ref[0, :, p * LANES:(p + 1) * LANES] = jnp.concatenate(halves, axis=0).T.astype(o_ref.dtype)


def _attn(qt, ks, kw, vs_t, vw_t, bias_t, oc_t, gl_t, tq=256, kt=512):
    b, _, s = qt.shape
    nsel = s // SEL_BLOCK
    kt = min(kt, s)
    onehot = jnp.asarray((np.arange(s)[:, None] // SEL_BLOCK == np.arange(nsel)[None, :]), dtype=BF16)
    k_spec = pl.BlockSpec((1, s, KV_GROUPS * DK), lambda i, g, j: (i, 0, 0))
    vt_spec = pl.BlockSpec((1, DK, s), lambda i, g, j: (i, g, 0))
    cols = HPG * tq
    return pl.pallas_call(
        functools.partial(_attn_kernel, kt),
        grid=(b, KV_GROUPS, s // tq),
        in_specs=[
            pl.BlockSpec((1, GW, tq), lambda i, g, j: (i, g, j)),
            k_spec, k_spec, vt_spec, vt_spec,
            pl.BlockSpec((1, 1, nsel, tq), lambda i, g, j: (i, g, 0, j)),
            pl.BlockSpec((s, nsel), lambda i, g, j: (0, 0)),
            pl.BlockSpec((1, GW, tq), lambda i, g, j: (i, g, j)),
            pl.BlockSpec((1, LANES, tq), lambda i, g, j: (i, g, j)),
        ],
        out_specs=pl.BlockSpec((1, tq, GW), lambda i, g, j: (i, j, g)),
        out_shape=jax.ShapeDtypeStruct((b, s, B_HEADS * DK), BF16),
        scratch_shapes=[pltpu.VMEM((LANES + nsel, cols), BF16),
                        pltpu.VMEM((kt, cols), F32), pltpu.VMEM((kt, cols), F32),
                        pltpu.VMEM((1, cols), F32), pltpu.VMEM((DK + ONES_ROWS, cols), F32),
                        pltpu.VMEM((DK, cols), F32)],
        compiler_params=_params(("parallel", "parallel", "arbitrary")),
        name="attn",
    )(qt, ks, kw, vs_t, vw_t, bias_t, onehot, oc_t, gl_t)


def _post_kernel(ya_ref, yb_ref, wo_ref, x_ref, g1_ref, ng_ref, sc_ref, sh_ref, rw_ref, rb_ref,
                 xo_ref, h_ref, idx_ref, gate_ref, rank_ref, cnt_ref):
    tm = x_ref.shape[1]

    @pl.when((pl.program_id(0) == 0) & (pl.program_id(1) == 0))
    def _():
        cnt_ref[...] = jnp.zeros(cnt_ref.shape, F32)

    mixed = jnp.dot(ya_ref[0], wo_ref[:A_WIDTH], preferred_element_type=F32)
    mixed = mixed + jnp.dot(yb_ref[0], wo_ref[A_WIDTH:], preferred_element_type=F32)
    x = x_ref[0] + g1_ref[0] * mixed
    xo_ref[0] = x
    ms = jnp.mean(x * x, axis=-1, keepdims=True)
    h = x * lax.rsqrt(ms + NORM_EPS) * ng_ref[...]
    h = h * (1.0 + sc_ref[0]) + sh_ref[0]
    h_ref[0] = _pack_halves(h)
    h_hi, h_mid = _split_bf16(h, 2)
    logits = jnp.dot(jnp.concatenate([h_hi, h_hi, h_mid], axis=1), rw_ref[...],
                     preferred_element_type=F32) + rb_ref[...]
    vals = logits.T[:N_EXPERTS]
    e_idx = lax.broadcasted_iota(jnp.int32, (N_EXPERTS, tm), 0)
    top = jnp.max(vals, axis=0, keepdims=True)
    routed_t = jnp.zeros_like(vals)
    firsts, exps = [], []
    for k in range(TOP_K):
        m = jnp.max(vals, axis=0, keepdims=True)
        first = jnp.min(jnp.where(vals == m, e_idx, N_EXPERTS), axis=0, keepdims=True)
        pick = e_idx == first
        routed_t = jnp.where(pick, 1.0, routed_t)
        firsts.append(first)
        exps.append(jnp.exp(m - top))
        vals = jnp.where(pick, -jnp.inf, vals)
    idx_i = jnp.concatenate(firsts, axis=0)
    idx_t = idx_i.astype(F32)
    slot_exp = jnp.concatenate(exps, axis=0)
    idx_ref[...] = idx_i
    gate_ref[...] = slot_exp / jnp.sum(slot_exp, axis=0, keepdims=True)
    r = lax.broadcasted_iota(jnp.int32, (tm, tm), 0)
    c = lax.broadcasted_iota(jnp.int32, (tm, tm), 1)
    before = jnp.where(r < c, 1.0, 0.0).astype(BF16)
    rank_t = cnt_ref[...] + jnp.dot(routed_t.astype(BF16), before, preferred_element_type=F32)
    cnt_ref[...] += jnp.sum(routed_t, axis=-1, keepdims=True)
    e_row = lax.broadcasted_iota(jnp.int32, (N_EXPERTS, tm), 0).astype(F32)
    ranks = [jnp.sum(jnp.where(e_row == idx_t[k:k + 1], rank_t, 0.0), axis=0, keepdims=True)
             for k in range(TOP_K)]
    rank_ref[...] = jnp.concatenate(ranks, axis=0).astype(jnp.int32)


def _post(y_a, y_b, w_out, x, g1, norm_g, sc, sh, router_w, router_b, tm=512):
    b, s, d = x.shape
    tm = min(tm, s)
    nt = s // tm
    w_hi, w_mid = _split_bf16_bits(jnp.zeros((d, LANES), F32).at[:, :N_EXPERTS].set(router_w), 2)
    rw = jnp.concatenate([w_hi, w_mid, w_hi], axis=0)
    rb = jnp.full((1, LANES), -1e30, F32).at[0, :N_EXPERTS].set(router_b)
    vec = pl.BlockSpec((1, 1, d), lambda i, j: (i, 0, 0))
    row = lambda w: pl.BlockSpec((1, tm, w), lambda i, j: (i, j, 0))
    return pl.pallas_call(
        _post_kernel,
        grid=(b, nt),
        in_specs=[
            row(A_WIDTH), row(B_HEADS * DK),
            pl.BlockSpec((d, d), lambda i, j: (0, 0)),
            row(d), vec,
            pl.BlockSpec((1, d), lambda i, j: (0, 0)), vec, vec,
            pl.BlockSpec((3 * d, LANES), lambda i, j: (0, 0)),
            pl.BlockSpec((1, LANES), lambda i, j: (0, 0)),
        ],
        out_specs=[row(d), row(d // 2)] + [pl.BlockSpec((TOP_K, tm), lambda i, j: (0, i * nt + j))] * 3,
        out_shape=[
            jax.ShapeDtypeStruct((b, s, d), F32),
            jax.ShapeDtypeStruct((b, s, d // 2), jnp.int32),
            jax.ShapeDtypeStruct((TOP_K, b * s), jnp.int32),
            jax.ShapeDtypeStruct((TOP_K, b * s), F32),
            jax.ShapeDtypeStruct((TOP_K, b * s), jnp.int32),
        ],
        scratch_shapes=[pltpu.VMEM((N_EXPERTS, 1), F32)],
        compiler_params=_params(("arbitrary", "arbitrary")),
        name="post",
    )(y_a, y_b, w_out.astype(BF16), x, g1, norm_g.reshape(1, d), sc, sh, rw, rb)


def _sc_mesh():
    return plsc.VectorSubcoreMesh(core_axis_name="c", subcore_axis_name="s")


def _sc_worker():
    return lax.axis_index("c") * SC_SUBCORES + lax.axis_index("s")


def _dispatch_rows(h2, dest_c, n_rows):
    t, d = h2.shape
    per_worker = t // SC_WINDOW // SC_WORKERS

    def body(x_hbm, i_hbm, o_hbm, buf, idx, sem):
        worker = _sc_worker()

        @pl.loop(0, per_worker)
        def _(j):
            ch = worker * per_worker + j
            pltpu.sync_copy(i_hbm.at[ch], idx)
            pltpu.sync_copy(x_hbm.at[pl.ds(ch * SC_WINDOW, SC_WINDOW)], buf)
            copies = [pltpu.async_copy(buf, o_hbm.at[idx.at[k]], sem.at[k]) for k in range(TOP_K)]
            for cp in copies:
                cp.wait()

    return pl.kernel(
        body, out_type=jax.ShapeDtypeStruct((n_rows, d), h2.dtype), mesh=_sc_mesh(),
        scratch_types=[pltpu.VMEM((SC_WINDOW, d), h2.dtype), pltpu.VMEM((TOP_K, SC_WINDOW), jnp.int32),
                       pltpu.SemaphoreType.DMA((TOP_K,))],
        name="dispatch_rows",
    )(h2, dest_c)


def _collect_rows(rows, dest_c, t):
    d = rows.shape[1]
    per_worker = t // SC_WINDOW // SC_WORKERS

    def body(r_hbm, i_hbm, o_hbm, buf, idx):
        worker = _sc_worker()

        @pl.loop(0, per_worker)
        def _(j):
            ch = worker * per_worker + j
            pltpu.sync_copy(i_hbm.at[ch], idx)
            for k in range(TOP_K):
                pltpu.sync_copy(r_hbm.at[idx.at[k]], buf)
                pltpu.sync_copy(buf, o_hbm.at[k, pl.ds(ch * SC_WINDOW, SC_WINDOW)])

    return pl.kernel(
        body, out_type=jax.ShapeDtypeStruct((TOP_K, t, d), rows.dtype), mesh=_sc_mesh(),
        scratch_types=[pltpu.VMEM((SC_WINDOW, d), rows.dtype), pltpu.VMEM((TOP_K, SC_WINDOW), jnp.int32)],
        name="collect_rows",
    )(rows, dest_c)


def _experts_kernel(layer, be_ref, nb_ref, slot_ref, next_ref, x_ref, w1_hbm, b1_ref, w2_hbm, b2_ref, o_ref,
                    w1f_ref, w2f_ref, w1b_ref, w2b_ref, sem):
    def weight_copies(expert, slot):
        return (pltpu.make_async_copy(w1_hbm.at[layer, expert], w1f_ref.at[slot], sem.at[slot, 0]),
                pltpu.make_async_copy(w2_hbm.at[layer, expert], w2f_ref.at[slot], sem.at[slot, 1]))

    def row_block(i, rs):
        e = be_ref[i]
        live = i < nb_ref[0]

        @pl.when(live & (i == 0))
        def _():
            for cp in weight_copies(e, 0):
                cp.start()

        @pl.when(live & ((i == 0) | (e != be_ref[jnp.maximum(i - 1, 0)])))
        def _():
            slot = slot_ref[i]
            for cp in weight_copies(e, slot):
                cp.wait()
            w1b_ref[...] = w1f_ref[slot].astype(BF16)
            w2b_ref[...] = w2f_ref[slot].astype(BF16)

            @pl.when(next_ref[i] >= 0)
            def _():
                for cp in weight_copies(next_ref[i], 1 - slot):
                    cp.start()

        @pl.when(live)
        def _():
            x = _unpack_halves(x_ref[rs, :]).astype(BF16)
            hdn = jnp.dot(x, w1b_ref[...], preferred_element_type=F32) + b1_ref[0, pl.ds(e, 1), :]
            glu = jnp.minimum(hdn[:, :D_EXPERT], SWIGLU_LIMIT)
            lin = jnp.clip(hdn[:, D_EXPERT:], -SWIGLU_LIMIT, SWIGLU_LIMIT)
            act = glu * jax.nn.sigmoid(SWIGLU_ALPHA * glu) * (lin + 1.0)
            out = jnp.dot(act.astype(BF16), w2b_ref[...], preferred_element_type=F32) + b2_ref[0, pl.ds(e, 1), :]
            o_ref[rs, :] = _pack_halves(out)

    for h in range(BLOCKS_PER_STEP):
        row_block(pl.program_id(0) * BLOCKS_PER_STEP + h, slice(h * ROW_BLOCK, (h + 1) * ROW_BLOCK))


def _experts(rows, block_exp, n_live, slot, next_exp, layer, w1, b1, w2, b2):
    n_rows, dp = rows.shape
    n_layers, n_e, d, f2 = w1.shape
    n_blocks = n_rows // ROW_BLOCK
    assert n_blocks % BLOCKS_PER_STEP == 0
    step_rows = BLOCKS_PER_STEP * ROW_BLOCK
    this_layer = lambda i, be, nb, sl, nx: (layer, 0, 0)
    grid_spec = pltpu.PrefetchScalarGridSpec(
        num_scalar_prefetch=4,
        grid=(n_blocks // BLOCKS_PER_STEP,),
        in_specs=[
            pl.BlockSpec((step_rows, dp), lambda i, be, nb, sl, nx: (i, 0)),
            pl.BlockSpec(memory_space=pl.ANY),
            pl.BlockSpec((1, n_e, f2), this_layer),
            pl.BlockSpec(memory_space=pl.ANY),
            pl.BlockSpec((1, n_e, d), this_layer),
        ],
        out_specs=pl.BlockSpec((step_rows, dp), lambda i, be, nb, sl, nx: (i, 0)),
        scratch_shapes=[pltpu.VMEM((2, d, f2), F32), pltpu.VMEM((2, f2 // 2, d), F32),
                        pltpu.VMEM((d, f2), BF16), pltpu.VMEM((f2 // 2, d), BF16),
                        pltpu.SemaphoreType.DMA((2, 2))],
    )
    return pl.pallas_call(
        functools.partial(_experts_kernel, layer),
        grid_spec=grid_spec,
        out_shape=jax.ShapeDtypeStruct((n_rows, dp), jnp.int32),
        compiler_params=_params(("arbitrary",)),
        name="experts",
    )(block_exp, n_live, slot, next_exp, rows, w1, b1, w2, b2)


def _moe_routed(h2, idx, rank, layer, w1, b1, w2, b2):
    t, d = h2.shape
    n_e = w1.shape[1]
    n_blocks = t * TOP_K // ROW_BLOCK + n_e
    onehot = (idx[:, :, None] == jnp.arange(n_e, dtype=jnp.int32)).astype(jnp.int32)
    counts = jnp.sum(onehot, axis=(0, 1))
    padded = (counts + ROW_BLOCK - 1) // ROW_BLOCK * ROW_BLOCK
    pad_end = jnp.cumsum(padded)
    dest = jnp.sum(onehot * (pad_end - padded), axis=-1) + rank
    block_row = jnp.arange(n_blocks, dtype=jnp.int32)[:, None] * ROW_BLOCK
    block_exp = jnp.minimum(jnp.sum((pad_end[None, :] <= block_row).astype(jnp.int32), axis=-1), n_e - 1)
    n_live = (pad_end[-1:] // ROW_BLOCK).astype(jnp.int32)
    experts = jnp.arange(n_e, dtype=jnp.int32)
    later = (experts[None, :] > experts[:, None]) & (counts[None, :] > 0)
    next_of = jnp.min(jnp.where(later, experts[None, :], n_e), axis=-1)
    next_of = jnp.where(next_of == n_e, -1, next_of)
    block_onehot = (block_exp[:, None] == experts[None, :]).astype(jnp.int32)
    next_exp = jnp.sum(block_onehot * next_of, axis=-1)
    run_of = jnp.cumsum((counts > 0).astype(jnp.int32)) - 1
    slot = jnp.sum(block_onehot * run_of, axis=-1) % 2
    dest_c = dest.reshape(TOP_K, t // SC_WINDOW, SC_WINDOW).transpose(1, 0, 2)
    rows = _dispatch_rows(h2, dest_c, n_blocks * ROW_BLOCK)
    out = _experts(rows, block_exp, n_live, slot, next_exp, layer, w1, b1, w2, b2)
    return _collect_rows(out, dest_c, t)


def _final_kernel(x_ref, rows_ref, gate_ref, g2_ref, fg_ref, o_ref):
    x = _moe_residual(x_ref, rows_ref, gate_ref, g2_ref)
    ms = jnp.mean(x * x, axis=-1, keepdims=True)
    o_ref[0] = x * lax.rsqrt(ms + NORM_EPS) * fg_ref[...]


def _final(x, res, final_g, tm=512):
    b, s, d = x.shape
    tm = min(tm, s)
    row = pl.BlockSpec((1, tm, d), lambda i, j: (i, j, 0))
    return pl.pallas_call(
        _final_kernel,
        grid=(b, s // tm),
        in_specs=[row] + _moe_residual_specs(tm, d) + [pl.BlockSpec((1, d), lambda i, j: (0, 0))],
        out_specs=row,
        out_shape=jax.ShapeDtypeStruct((b, s, d), F32),
        compiler_params=_params(("parallel", "parallel")),
        name="final",
    )(x, *res, final_g.reshape(1, d))


def _widen_w_in(w):
    gl0 = 2 * A_WIDTH + B_HEADS * DK + 6 * KV_GROUPS * DK
    per_group = HPG * 3
    gl_groups = [jnp.pad(w[:, gl0 + g * per_group:gl0 + (g + 1) * per_group], ((0, 0), (0, LANES - per_group)))
                 for g in range(KV_GROUPS)]
    wide = jnp.concatenate([w[:, :gl0]] + gl_groups, axis=1)
    assert gl0 == C_GL and wide.shape[1] == P_WIDE
    return wide.astype(BF16)


def _rope_tables(pos):
    half = DK // 2
    inv = ROPE_THETA ** (-jnp.arange(half, dtype=F32) / half)
    ang = pos.astype(F32)[..., None] * inv
    cos = jnp.cos(ang)
    sin = jnp.sin(ang)
    reps = LANES // DK
    return (jnp.concatenate([cos, cos] * reps, axis=-1),
            jnp.concatenate([-sin, sin] * reps, axis=-1))


def kernel(x, c, positions, ada_w, ada_b, norm1_g, norm2_g, w_in, w_out, sg_ln_g, sg_ln_b, sg_w, sg_b,
           cmp_pe_k, cmp_pe_v, cmp_w1_k, cmp_w2_k, cmp_w1_v, cmp_w2_v, router_w, router_b,
           exp_w1, exp_b1, exp_w2, exp_b2, final_g):
    b, s, d = x.shape
    n_layers = ada_w.shape[0]
    mod = _ada_mod(c, ada_w, ada_b)
    cos, sin = _rope_tables(positions)
    cmp_end = jnp.minimum(CMP_STRIDE * jnp.arange(s // CMP_STRIDE) + CMP_BLOCK - 1, s - 1)
    cos_c, sin_c = _rope_tables(positions[:, cmp_end])
    res = None
    for l in range(n_layers):
        sh1, sc1, g1, sh2, sc2, g2 = [m.reshape(b, 1, d) for m in jnp.split(mod[l], 6, axis=-1)]
        x, y_a, qt, kc, vc, ks, vs_t, kw, vw_t, gl_t = _pre(
            x, res, norm1_g[l], sc1, sh1, _widen_w_in(w_in[l]), cos, sin,
            (sg_ln_g[l], sg_ln_b[l], sg_w[l], sg_b[l]))
        kcmp, vcmp_t = _compress(kc, vc, cmp_pe_k[l], cmp_pe_v[l], cmp_w1_k[l], cmp_w2_k[l],
                                 cmp_w1_v[l], cmp_w2_v[l], cos_c, sin_c)
        oc_t, bias_t = _cmp_sel(qt, kcmp, vcmp_t)
        y_b = _attn(qt, ks, kw, vs_t, vw_t, bias_t, oc_t, gl_t)
        x, h2, idx, gate, rank = _post(y_a, y_b, w_out[l], x, g1, norm2_g[l], sc2, sh2,
                                       router_w[l], router_b[l])
        rows = _moe_routed(h2.reshape(b * s, d // 2), idx, rank, l, exp_w1, exp_b1, exp_w2, exp_b2)
        res = (rows.reshape(TOP_K, b, s, d // 2), gate.T.reshape(b, s, TOP_K), g2)
    return _final(x, res, final_g)
```
